```python
import math
import jax, jax.numpy as jnp
from jax import lax
import numpy as np

D_MODEL = 1024
BATCH = 1
SEQ = 16384
DEPTH = 1
DEC_BATCH = 32
DEC_SEQ = 64
PAST_LEN = 2048

CHUNK = 64
GDN_HEADS = 8
GDN_DK = 128
GDN_DV = 128
CONV_W = 4
CONV_DIM = GDN_HEADS * (2 * GDN_DK + GDN_DV)
SWA_HEADS = 16
SWA_KV_HEADS = 2
SWA_GROUP = SWA_HEADS // SWA_KV_HEADS
SWA_HD = 64
WINDOW = 128
NUM_BUCKETS = 32
MAX_DISTANCE = 128
D_FF = 2816
N_MOD = 9
EPS = 1e-6
IN_SPLITS = (CONV_DIM, GDN_HEADS * GDN_DV, GDN_HEADS, GDN_HEADS,
             SWA_HEADS * SWA_HD, SWA_KV_HEADS * SWA_HD, SWA_KV_HEADS * SWA_HD,
             D_MODEL, D_MODEL)
IN_COLS = sum(IN_SPLITS)

kernel_name = 'hybrid_gdn_swa_macaron_adaln_step'


def rmsnorm(x, g):
    x32 = x.astype(jnp.float32)
    y = x32 * lax.rsqrt(jnp.mean(x32 * x32, axis=-1, keepdims=True) + EPS)
    return (y * g.astype(jnp.float32)).astype(x.dtype)


def l2norm(x):
    return x * lax.rsqrt(jnp.sum(x * x, axis=-1, keepdims=True) + EPS)


def swiglu(h, w_in, w_out):
    gate, up = jnp.split(h @ w_in, 2, axis=-1)
    return (jax.nn.silu(gate) * up) @ w_out


def t5_bucket(rel):
    half = NUM_BUCKETS // 2
    max_exact = half // 2
    n = jnp.abs(rel)
    large = max_exact + (jnp.log(jnp.maximum(n, 1).astype(jnp.float32) / max_exact)
                         / math.log(MAX_DISTANCE / max_exact) * (half - max_exact)).astype(jnp.int32)
    large = jnp.minimum(large, half - 1)
    return jnp.where(rel > 0, half, 0) + jnp.where(n < max_exact, n, large)


def gated_delta_chunked(q, k, v, g, beta, s0, C):
    B, L, H, DK = k.shape
    DV = v.shape[-1]
    N = L // C

    def chunks(t):
        return t.reshape(B, N, C, H, t.shape[-1]).transpose(1, 0, 3, 2, 4)

    qc, kc, vc = chunks(q), chunks(k), chunks(v)
    gc = jnp.cumsum(chunks(g[..., None])[..., 0], axis=-1)
    bc = chunks(beta[..., None])
    causal = jnp.tril(jnp.ones((C, C), dtype=bool))
    strict = jnp.tril(jnp.ones((C, C), dtype=bool), k=-1)
    diff = gc[..., :, None] - gc[..., None, :]
    decay = jnp.where(causal, jnp.exp(jnp.where(causal, diff, 0.0)), 0.0)
    kb = kc * bc
    a_mat = jnp.where(strict, jnp.einsum('nbhid,nbhjd->nbhij', kb, kc) * decay, 0.0) + jnp.eye(C, dtype=jnp.float32)
    rhs = jnp.concatenate([vc * bc, kb * jnp.exp(gc)[..., None]], axis=-1)
    sol = lax.linalg.triangular_solve(a_mat, rhs, left_side=True, lower=True)
    u, w = sol[..., :DV], sol[..., DV:]
    intra = jnp.where(causal, jnp.einsum('nbhid,nbhjd->nbhij', qc, kc) * decay, 0.0)

    def step(S, xs):
        q_i, k_i, u_i, w_i, g_i, a_i = xs
        v_new = u_i - jnp.einsum('bhcd,bhde->bhce', w_i, S)
        o = (jnp.einsum('bhcd,bhde->bhce', q_i * jnp.exp(g_i)[..., None], S)
             + jnp.einsum('bhij,bhje->bhie', a_i, v_new))
        g_last = g_i[..., -1:]
        S = (S * jnp.exp(g_last)[..., None]
             + jnp.einsum('bhcd,bhce->bhde', k_i * jnp.exp(g_last - g_i)[..., None], v_new))
        return S, o

    s_final, o = lax.scan(step, s0, (qc, kc, u, w, gc, intra))
    return o.transpose(1, 0, 3, 2, 4).reshape(B, L, H, DV), s_final


def gated_deltanet(qkv_pre, z, b_raw, a_raw, conv_hist, s0, conv_w, a_log, dt_bias, norm_w):
    B, L, _ = qkv_pre.shape
    xp = jnp.concatenate([conv_hist.astype(qkv_pre.dtype), qkv_pre], axis=1)
    conv = lax.conv_general_dilated(xp, conv_w[:, None, :].astype(xp.dtype), window_strides=(1,),
                                    padding='VALID', dimension_numbers=('NWC', 'WIO', 'NWC'),
                                    feature_group_count=CONV_DIM)
    qkv = jax.nn.silu(conv.astype(jnp.float32))
    q, k, v = jnp.split(qkv, [GDN_HEADS * GDN_DK, 2 * GDN_HEADS * GDN_DK], axis=-1)
    q = l2norm(q.reshape(B, L, GDN_HEADS, GDN_DK)) * GDN_DK ** -0.5
    k = l2norm(k.reshape(B, L, GDN_HEADS, GDN_DK))
    v = v.reshape(B, L, GDN_HEADS, GDN_DV)
    beta = jax.nn.sigmoid(b_raw.astype(jnp.float32))
    g = -jnp.exp(a_log.astype(jnp.float32)) * jax.nn.softplus(a_raw.astype(jnp.float32) + dt_bias.astype(jnp.float32))
    o, s_new = gated_delta_chunked(q, k, v, g, beta, s0.astype(jnp.float32), min(CHUNK, L))
    zg = jax.nn.silu(z.astype(jnp.float32).reshape(B, L, GDN_HEADS, GDN_DV))
    o = o * lax.rsqrt(jnp.mean(o * o, axis=-1, keepdims=True) + EPS) * norm_w.astype(jnp.float32) * zg
    return o.reshape(B, L, GDN_HEADS * GDN_DV).astype(qkv_pre.dtype), xp[:, L:], s_new.astype(s0.dtype)


def sliding_window_attention(q, k, v, k_hist, v_hist, n_hist_valid, rel_bias, sinks):
    B, L = q.shape[:2]
    win = k_hist.shape[1]
    C = min(CHUNK, L)
    N = L // C
    k = k.reshape(B, L, SWA_KV_HEADS, SWA_HD)
    v = v.reshape(B, L, SWA_KV_HEADS, SWA_HD)
    kf = jnp.concatenate([k_hist.astype(k.dtype), k], axis=1)
    vf = jnp.concatenate([v_hist.astype(v.dtype), v], axis=1)
    idx = (jnp.arange(N) * C)[:, None] + jnp.arange(win + C)[None, :]
    kb = kf[:, idx]
    vb = vf[:, idx]
    qb = q.reshape(B, N, C, SWA_KV_HEADS, SWA_GROUP, SWA_HD)
    logits = jnp.einsum('bnikgd,bnjkd->bnkgij', qb, kb).astype(jnp.float32) * SWA_HD ** -0.5
    rel = jnp.arange(win + C)[None, :] - win - jnp.arange(C)[:, None]
    bias = rel_bias.astype(jnp.float32)[t5_bucket(rel)]
    bias = bias.transpose(2, 0, 1).reshape(SWA_KV_HEADS, SWA_GROUP, C, win + C)
    valid = (idx >= win - n_hist_valid)[None, :, None, None, None, :]
    logits = jnp.where(valid, logits + bias, -jnp.inf)
    sink = sinks.astype(jnp.float32).reshape(SWA_KV_HEADS, SWA_GROUP, 1, 1)
    m = jnp.maximum(jnp.max(logits, axis=-1, keepdims=True), sink)
    p = jnp.exp(logits - m)
    probs = p / (jnp.sum(p, axis=-1, keepdims=True) + jnp.exp(sink - m))
    out = jnp.einsum('bnkgij,bnjkd->bnikgd', probs.astype(v.dtype), vb)
    return out.reshape(B, L, SWA_HEADS * SWA_HD), kf[:, -win:], vf[:, -win:]


def trunk(x, c, conv_hist, s_hist, k_hist, v_hist, n_hist_valid, p):
    B, L, D = x.shape
    splits = np.cumsum(IN_SPLITS)[:-1].tolist()
    new_conv, new_s, new_k, new_v = [], [], [], []
    for l in range(DEPTH):
        mod = (jax.nn.silu(c) @ p['w_ada'][l] + p['b_ada'][l]).reshape(B, N_MOD, 1, D)
        sh1, sc1, ga1, sh2, sc2, ga2, sh3, sc3, ga3 = [mod[:, i] for i in range(N_MOD)]
        h = rmsnorm(x, p['norm_ffn1'][l]) * (1 + sc1) + sh1
        x = x + 0.5 * ga1 * swiglu(h, p['w_ffn1_in'][l], p['w_ffn1_out'][l])
        h = rmsnorm(x, p['norm_mix'][l]) * (1 + sc2) + sh2
        qkv_a, z_a, b_a, a_a, q_b, k_b, v_b, gate_a, gate_b = jnp.split(h @ p['w_in'][l], splits, axis=-1)
        o_a, conv_new, s_new = gated_deltanet(qkv_a, z_a, b_a, a_a, conv_hist[l], s_hist[l], p['gdn_conv_w'][l],
                                              p['gdn_a_log'][l], p['gdn_dt_bias'][l], p['gdn_norm_w'][l])
        o_b, k_new, v_new = sliding_window_attention(q_b, k_b, v_b, k_hist[l], v_hist[l], n_hist_valid,
                                                     p['rel_bias'], p['swa_sinks'][l])
        merged = (jax.nn.sigmoid(gate_a) * (o_a @ p['w_branch_a'][l])
                  + jax.nn.sigmoid(gate_b) * (o_b @ p['w_branch_b'][l]))
        x = x + ga2 * (merged @ p['w_out'][l])
        h = rmsnorm(x, p['norm_ffn2'][l]) * (1 + sc3) + sh3
        x = x + 0.5 * ga3 * swiglu(h, p['w_ffn2_in'][l], p['w_ffn2_out'][l])
        new_conv.append(conv_new)
        new_s.append(s_new)
        new_k.append(k_new)
        new_v.append(v_new)
    y = rmsnorm(x, p['norm_final'])
    return y, jnp.stack(new_conv), jnp.stack(new_s), jnp.stack(new_k), jnp.stack(new_v)


def setup_inputs(seed: int = 0) -> dict:
    key = jax.random.key(seed)
    ks = iter(jax.random.split(key, 40))

    def nrm(shape, scale):
        return scale * jax.random.normal(next(ks), shape, jnp.float32)

    def gain(shape):
        return 1.0 + nrm(shape, 0.02)

    win_rows = min(WINDOW, PAST_LEN)
    dt = jnp.exp(jax.random.uniform(next(ks), (DEPTH, GDN_HEADS), jnp.float32, math.log(1e-3), math.log(1e-1)))
    a_init = jax.random.uniform(next(ks), (DEPTH, GDN_HEADS), jnp.float32, 1.0, 16.0)
    return {
        'x_prompt': nrm((BATCH, SEQ, D_MODEL), 1.0),
        'x_sample': nrm((DEC_BATCH, DEC_SEQ, D_MODEL), 1.0),
        'state_gdn_conv': nrm((DEPTH, DEC_BATCH, CONV_W - 1, CONV_DIM), 1.0),
        'state_gdn_s': nrm((DEPTH, DEC_BATCH, GDN_HEADS, GDN_DK, GDN_DV), 0.1),
        'cache_swa_k': nrm((DEPTH, DEC_BATCH, win_rows, SWA_KV_HEADS, SWA_HD), 1.0),
        'cache_swa_v': nrm((DEPTH, DEC_BATCH, win_rows, SWA_KV_HEADS, SWA_HD), 1.0),
        'c_prompt': nrm((BATCH, D_MODEL), 1.0),
        'c_sample': nrm((DEC_BATCH, D_MODEL), 1.0),
        'norm_ffn1': gain((DEPTH, D_MODEL)),
        'w_ffn1_in': nrm((DEPTH, D_MODEL, 2 * D_FF), D_MODEL ** -0.5),
        'w_ffn1_out': nrm((DEPTH, D_FF, D_MODEL), D_FF ** -0.5),
        'norm_mix': gain((DEPTH, D_MODEL)),
        'w_in': nrm((DEPTH, D_MODEL, IN_COLS), D_MODEL ** -0.5),
        'gdn_conv_w': nrm((DEPTH, CONV_W, CONV_DIM), 0.5),
        'gdn_a_log': jnp.log(a_init),
        'gdn_dt_bias': dt + jnp.log(-jnp.expm1(-dt)),
        'gdn_norm_w': gain((DEPTH, GDN_DV)),
        'swa_sinks': nrm((DEPTH, SWA_HEADS), 1.0),
        'rel_bias': nrm((NUM_BUCKETS, SWA_HEADS), 0.5),
        'w_branch_a': nrm((DEPTH, GDN_HEADS * GDN_DV, D_MODEL), (GDN_HEADS * GDN_DV) ** -0.5),
        'w_branch_b': nrm((DEPTH, SWA_HEADS * SWA_HD, D_MODEL), (SWA_HEADS * SWA_HD) ** -0.5),
        'w_out': nrm((DEPTH, D_MODEL, D_MODEL), D_MODEL ** -0.5),
        'norm_ffn2': gain((DEPTH, D_MODEL)),
        'w_ffn2_in': nrm((DEPTH, D_MODEL, 2 * D_FF), D_MODEL ** -0.5),
        'w_ffn2_out': nrm((DEPTH, D_FF, D_MODEL), D_FF ** -0.5),
        'w_ada': nrm((DEPTH, D_MODEL, N_MOD * D_MODEL), 0.5 * D_MODEL ** -0.5),
        'b_ada': nrm((DEPTH, N_MOD * D_MODEL), 0.02),
        'norm_final': gain((D_MODEL,)),
    }


def reference(x_prompt, x_sample, state_gdn_conv, state_gdn_s, cache_swa_k, cache_swa_v, c_prompt, c_sample,
              norm_ffn1, w_ffn1_in, w_ffn1_out, norm_mix, w_in, gdn_conv_w, gdn_a_log, gdn_dt_bias, gdn_norm_w,
              swa_sinks, rel_bias, w_branch_a, w_branch_b, w_out, norm_ffn2, w_ffn2_in, w_ffn2_out,
              w_ada, b_ada, norm_final):
    params = {
        'norm_ffn1': norm_ffn1, 'w_ffn1_in': w_ffn1_in, 'w_ffn1_out': w_ffn1_out,
        'norm_mix': norm_mix, 'w_in': w_in, 'gdn_conv_w': gdn_conv_w, 'gdn_a_log': gdn_a_log,
        'gdn_dt_bias': gdn_dt_bias, 'gdn_norm_w': gdn_norm_w, 'swa_sinks': swa_sinks, 'rel_bias': rel_bias,
        'w_branch_a': w_branch_a, 'w_branch_b': w_branch_b, 'w_out': w_out,
        'norm_ffn2': norm_ffn2, 'w_ffn2_in': w_ffn2_in, 'w_ffn2_out': w_ffn2_out,
        'w_ada': w_ada, 'b_ada': b_ada, 'norm_final': norm_final,
    }
    bp = x_prompt.shape[0]
    dt = x_prompt.dtype
    zero_conv = jnp.zeros((DEPTH, bp, CONV_W - 1, CONV_DIM), dt)
    zero_s = jnp.zeros((DEPTH, bp, GDN_HEADS, GDN_DK, GDN_DV), state_gdn_s.dtype)
    zero_kv = jnp.zeros((DEPTH, bp, WINDOW, SWA_KV_HEADS, SWA_HD), dt)
    y_prompt, p_conv, p_s, p_k, p_v = trunk(x_prompt, c_prompt, zero_conv, zero_s, zero_kv, zero_kv, 0, params)
    y_sample, s_conv, s_s, s_k, s_v = trunk(x_sample, c_sample, state_gdn_conv, state_gdn_s, cache_swa_k,
                                            cache_swa_v, cache_swa_k.shape[2], params)
    return (y_prompt, y_sample, p_conv, p_s, p_k, p_v, s_conv, s_s, s_k, s_v)
```

```python
import functools
import math

import jax
import jax.numpy as jnp
from jax import lax
from jax.experimental import pallas as pl
from jax.experimental.pallas import tpu as pltpu

F32 = jnp.float32
BF16 = jnp.bfloat16

D_MODEL = 1024
CHUNK = 64
GDN_HEADS = 8
GDN_DK = 128
GDN_DV = 128
CONV_W = 4
CONV_DIM = GDN_HEADS * (2 * GDN_DK + GDN_DV)
SWA_HEADS = 16
SWA_KV_HEADS = 2
SWA_GROUP = SWA_HEADS // SWA_KV_HEADS
SWA_HD = 64
WINDOW = 128
NUM_BUCKETS = 32
MAX_DISTANCE = 128
D_FF = 2816
N_MOD = 9
EPS = 1e-6

LANES = 128
SUBLANES = 8
VMEM_LIMIT = 56 * 1024 * 1024

FF_CHUNK = D_FF // 2
HIST_ROWS = SUBLANES
SOLVE_PASSES = 3


def _dot(a, b):
    return jnp.dot(a, b, preferred_element_type=F32)


def _dot_nt(a, b):
    return lax.dot_general(a, b, (((1,), (1,)), ((), ())), preferred_element_type=F32)


def _dot_tn(a, b):
    return lax.dot_general(a, b, (((0,), (0,)), ((), ())), preferred_element_type=F32)


def _split(a):
    hi = a.astype(BF16)
    lo = (a - hi.astype(F32)).astype(BF16)
    return hi, lo


def _dot_split(a_parts, b_parts):
    ah, al = a_parts
    bh, bl = b_parts
    out = _dot(ah, bh)
    if SOLVE_PASSES >= 3:
        out = out + _dot(al, bh) + _dot(ah, bl)
    return out


def _cparams(n_grid):
    return pltpu.CompilerParams(dimension_semantics=("arbitrary",) * n_grid, vmem_limit_bytes=VMEM_LIMIT)


def _const_spec(shape):
    nd = len(shape)
    return pl.BlockSpec(shape, lambda i: (0,) * nd, pipeline_mode=pl.Buffered(1))


def _mod_kernel(c_ref, w_ref, b_ref, o_ref):
    c = c_ref[...]
    a = (c * jax.nn.sigmoid(c)).astype(BF16)
    o_ref[...] = _dot(a, w_ref[...].astype(BF16)) + b_ref[...]


def _modulation(c_pad, w_ada, b_ada):
    rows = c_pad.shape[0]
    return pl.pallas_call(
        _mod_kernel,
        grid=(N_MOD,),
        in_specs=[
            pl.BlockSpec((rows, D_MODEL), lambda j: (0, 0)),
            pl.BlockSpec((D_MODEL, D_MODEL), lambda j: (0, j)),
            pl.BlockSpec((1, D_MODEL), lambda j: (0, j)),
        ],
        out_specs=pl.BlockSpec((rows, D_MODEL), lambda j: (0, j)),
        out_shape=jax.ShapeDtypeStruct((rows, N_MOD * D_MODEL), F32),
        compiler_params=_cparams(1),
        name="adaln_mod",
    )(c_pad, w_ada, b_ada)


def _mod_row(mod_ref, idx, spt, j):
    if mod_ref.shape[1] == 1:
        return mod_ref[idx, 0:1, :]
    return mod_ref[idx, pl.ds(pl.program_id(0) * spt + j, 1), :]


def _norm_mod_store(h_ref, x_ref, g_ref, mod_ref, sub, spt):
    rows = x_ref.shape[0] // spt
    g = g_ref[...]
    for j in range(spt):
        xs = x_ref[j * rows:(j + 1) * rows, :]
        ms = jnp.mean(xs * xs, axis=-1, keepdims=True)
        y = xs * lax.rsqrt(ms + EPS) * g
        sh = _mod_row(mod_ref, 3 * sub, spt, j)
        sc = _mod_row(mod_ref, 3 * sub + 1, spt, j)
        h_ref[j * rows:(j + 1) * rows, :] = (y * (1.0 + sc) + sh).astype(h_ref.dtype)


def _tile_rows(n_rows, seq_rows, target):
    if seq_rows >= target:
        assert seq_rows % target == 0
        return target, 1
    assert target % seq_rows == 0
    tm = min(target, n_rows)
    assert n_rows % tm == 0
    return tm, tm // seq_rows


def _mod_spec(n_seq_total):
    return pl.BlockSpec((N_MOD, n_seq_total, D_MODEL), lambda i: (0, 0, 0))


def _ffn_kernel(x_ref, mod_ref, g_ref, w1_ref, w2_ref, gf_ref, o_ref, h_ref, *, sub, final, spt):
    _norm_mod_store(h_ref, x_ref, g_ref, mod_ref, sub, spt)
    h = h_ref[...]
    acc = None
    for c in range(D_FF // FF_CHUNK):
        c0 = c * FF_CHUNK
        gate = _dot(h, w1_ref[:, c0:c0 + FF_CHUNK])
        up = _dot(h, w1_ref[:, D_FF + c0:D_FF + c0 + FF_CHUNK])
        a = (gate * jax.nn.sigmoid(gate) * up).astype(BF16)
        part = _dot(a, w2_ref[c0:c0 + FF_CHUNK, :])
        acc = part if acc is None else acc + part
    rows = x_ref.shape[0] // spt
    for j in range(spt):
        sl = slice(j * rows, (j + 1) * rows)
        ga = _mod_row(mod_ref, 3 * sub + 2, spt, j)
        xn = x_ref[sl, :] + 0.5 * ga * acc[sl, :]
        if final:
            ms = jnp.mean(xn * xn, axis=-1, keepdims=True)
            xn = xn * lax.rsqrt(ms + EPS) * gf_ref[...]
        o_ref[sl, :] = xn


def _ffn(x, mod, seq_rows, g, w1, w2, gf, *, sub, final):
    n = x.shape[0]
    tm, spt = _tile_rows(n, seq_rows, 512)
    kern = functools.partial(_ffn_kernel, sub=sub, final=final, spt=spt)
    return pl.pallas_call(
        kern,
        grid=(n // tm,),
        in_specs=[
            pl.BlockSpec((tm, D_MODEL), lambda i: (i, 0)),
            _mod_spec(mod.shape[1]),
            _const_spec((1, D_MODEL)),
            _const_spec((D_MODEL, 2 * D_FF)),
            _const_spec((D_FF, D_MODEL)),
            _const_spec((1, D_MODEL)),
        ],
        out_specs=pl.BlockSpec((tm, D_MODEL), lambda i: (i, 0)),
        out_shape=jax.ShapeDtypeStruct((n, D_MODEL), F32),
        scratch_shapes=[pltpu.VMEM((tm, D_MODEL), BF16)],
        compiler_params=_cparams(1),
        name="ffn_final" if final else "ffn",
    )(x, mod, g, w1, w2, gf)


PROJ_GROUPS = (("qkv", CONV_DIM), ("z", GDN_HEADS * GDN_DV), ("qb", SWA_HEADS * SWA_HD),
               ("kv", 2 * SWA_KV_HEADS * SWA_HD), ("gates", 2 * D_MODEL), ("ba", LANES))
PROJ_COLS = sum(w for _, w in PROJ_GROUPS)


def _proj_kernel(x_ref, mod_ref, g_ref, w_ref, *refs, spt):
    out_refs, h_ref = refs[:-1], refs[-1]
    _norm_mod_store(h_ref, x_ref, g_ref, mod_ref, 1, spt)
    h = h_ref[...]
    c0 = 0
    for (_, width), o_ref in zip(PROJ_GROUPS, out_refs):
        o_ref[...] = _dot(h, w_ref[:, c0:c0 + width]).astype(o_ref.dtype)
        c0 += width


def _proj(x, mod, seq_rows, g, w_packed):
    n = x.shape[0]
    tm, spt = _tile_rows(n, seq_rows, 256)
    return pl.pallas_call(
        functools.partial(_proj_kernel, spt=spt),
        grid=(n // tm,),
        in_specs=[
            pl.BlockSpec((tm, D_MODEL), lambda i: (i, 0)),
            _mod_spec(mod.shape[1]),
            _const_spec((1, D_MODEL)),
            _const_spec((D_MODEL, PROJ_COLS)),
        ],
        out_specs=[pl.BlockSpec((tm, w), lambda i: (i, 0)) for _, w in PROJ_GROUPS],
        out_shape=[jax.ShapeDtypeStruct((n, w), F32) for _, w in PROJ_GROUPS],
        scratch_shapes=[pltpu.VMEM((tm, D_MODEL), BF16)],
        compiler_params=_cparams(1),
        name="in_proj",
    )(x, mod, g, w_packed)


def _unit_lower_inverse(neg_l, eye):
    p = _split(neg_l)
    t = eye + neg_l
    n_sq = int(math.log2(CHUNK)) - 1
    for _ in range(n_sq):
        pp = _dot_split(p, p)
        p = _split(pp)
        t = t + _dot_split(_split(t), p)
    return t


def _gdn_kernel(qkv_ref, z_ref, ba_ref, hist_ref, s0_ref, cw_ref, par_ref, nw_ref,
                o_ref, sout_ref, xp_ref, s_ref, *, n_chunks, carry):
    pid = pl.program_id(0)
    rows = n_chunks * CHUNK

    def _load_state():
        xp_ref[0:HIST_ROWS, :] = hist_ref[0]
        s_ref[...] = s0_ref[0]

    if carry:
        pl.when(pid == 0)(_load_state)
    else:
        _load_state()

    xp_ref[HIST_ROWS:HIST_ROWS + rows, :] = qkv_ref[...]

    ri = lax.broadcasted_iota(jnp.int32, (CHUNK, CHUNK), 0)
    ci = lax.broadcasted_iota(jnp.int32, (CHUNK, CHUNK), 1)
    causal = ri >= ci
    strict = ri > ci
    eye = (ri == ci).astype(F32)
    tri = causal.astype(BF16)

    a_coef = -jnp.exp(par_ref[0:1, :])
    dt_bias = par_ref[1:2, :]
    norm_w = nw_ref[...]

    def conv_tile(r0, c0):
        acc = None
        for j in range(CONV_W):
            start = HIST_ROWS - (CONV_W - 1) + j + r0
            term = cw_ref[j:j + 1, c0:c0 + LANES] * xp_ref[start:start + CHUNK, c0:c0 + LANES]
            acc = term if acc is None else acc + term
        return acc * jax.nn.sigmoid(acc)

    for c in range(n_chunks):
        r0 = c * CHUNK
        ba = ba_ref[r0:r0 + CHUNK, :]
        beta_all = jax.nn.sigmoid(ba)
        g_all = a_coef * jax.nn.softplus(ba + dt_bias)
        g_hi = g_all.astype(BF16)
        g_r1 = g_all - g_hi.astype(F32)
        g_mid = g_r1.astype(BF16)
        g_lo = (g_r1 - g_mid.astype(F32)).astype(BF16)
        gc_all = _dot(tri, g_hi) + _dot(tri, g_mid) + _dot(tri, g_lo)

        for h in range(GDN_HEADS):
            q = conv_tile(r0, h * GDN_DK)
            k = conv_tile(r0, GDN_HEADS * GDN_DK + h * GDN_DK)
            v = conv_tile(r0, 2 * GDN_HEADS * GDN_DK + h * GDN_DV)
            q = q * lax.rsqrt(jnp.sum(q * q, axis=-1, keepdims=True) + EPS) * (GDN_DK ** -0.5)
            k = k * lax.rsqrt(jnp.sum(k * k, axis=-1, keepdims=True) + EPS)
            beta = beta_all[:, h:h + 1]
            gcol = gc_all[:, GDN_HEADS + h:GDN_HEADS + h + 1]
            grow = jnp.sum(gcol * eye, axis=0, keepdims=True)
            diff = gcol - grow
            decay = jnp.where(causal, jnp.exp(jnp.where(causal, diff, 0.0)), 0.0)
            kb = k * beta
            k16 = k.astype(BF16)
            kk = _dot_nt(kb.astype(BF16), k16)
            qk = _dot_nt(q.astype(BF16), k16)
            neg_l = jnp.where(strict, -(kk * decay), 0.0)
            intra = jnp.where(causal, qk * decay, 0.0)
            t_inv = _split(_unit_lower_inverse(neg_l, eye))
            egc = jnp.exp(gcol)
            u = _dot_split(t_inv, _split(v * beta))
            w = _dot_split(t_inv, _split(kb * egc))
            s_old = s_ref[h]
            s16 = s_old.astype(BF16)
            v_new = u - _dot(w.astype(BF16), s16)
            vn16 = v_new.astype(BF16)
            o = _dot((q * egc).astype(BF16), s16) + _dot(intra.astype(BF16), vn16)
            g_last = gcol[CHUNK - 1:CHUNK, :]
            kd = (k * jnp.exp(g_last - gcol)).astype(BF16)
            s_ref[h] = s_old * jnp.exp(g_last) + _dot_tn(kd, vn16)
            zz = z_ref[r0:r0 + CHUNK, h * GDN_DV:(h + 1) * GDN_DV]
            on = o * lax.rsqrt(jnp.mean(o * o, axis=-1, keepdims=True) + EPS) * norm_w * (zz * jax.nn.sigmoid(zz))
            o_ref[r0:r0 + CHUNK, h * GDN_DV:(h + 1) * GDN_DV] = on

    def _store_state():
        sout_ref[0] = s_ref[...]

    if carry:
        xp_ref[0:HIST_ROWS, :] = xp_ref[rows:rows + HIST_ROWS, :]
        pl.when(pid == pl.num_programs(0) - 1)(_store_state)
    else:
        _store_state()


def _gdn(qkv, z, ba, hist8, s0, conv_w, par, norm_w, seq_rows):
    n = qkv.shape[0]
    n_seq = n // seq_rows
    carry = n_seq == 1
    n_chunks = 1
    rows = n_chunks * CHUNK
    if carry:
        seq_map = lambda i: (0, 0, 0)
        s_map = lambda i: (0, 0, 0, 0)
    else:
        assert seq_rows == rows
        seq_map = lambda i: (i, 0, 0)
        s_map = lambda i: (i, 0, 0, 0)
    kern = functools.partial(_gdn_kernel, n_chunks=n_chunks, carry=carry)
    return pl.pallas_call(
        kern,
        grid=(n // rows,),
        in_specs=[
            pl.BlockSpec((rows, CONV_DIM), lambda i: (i, 0)),
            pl.BlockSpec((rows, GDN_HEADS * GDN_DV), lambda i: (i, 0)),
            pl.BlockSpec((rows, LANES), lambda i: (i, 0)),
            pl.BlockSpec((1, HIST_ROWS, CONV_DIM), seq_map),
            pl.BlockSpec((1, GDN_HEADS, GDN_DK, GDN_DV), s_map),
            pl.BlockSpec((CONV_W, CONV_DIM), lambda i: (0, 0)),
            pl.BlockSpec((SUBLANES, LANES), lambda i: (0, 0)),
            pl.BlockSpec((1, GDN_DV), lambda i: (0, 0)),
        ],
        out_specs=[
            pl.BlockSpec((rows, GDN_HEADS * GDN_DV), lambda i: (i, 0)),
            pl.BlockSpec((1, GDN_HEADS, GDN_DK, GDN_DV), s_map),
        ],
        out_shape=[
            jax.ShapeDtypeStruct((n, GDN_HEADS * GDN_DV), F32),
            jax.ShapeDtypeStruct((n_seq, GDN_HEADS, GDN_DK, GDN_DV), F32),
        ],
        scratch_shapes=[
            pltpu.VMEM((HIST_ROWS + rows, CONV_DIM), F32),
            pltpu.VMEM((GDN_HEADS, GDN_DK, GDN_DV), F32),
        ],
        compiler_params=_cparams(1),
        name="gdn",
    )(qkv, z, ba, hist8, s0, conv_w, par, norm_w)


SWA_KEYS = WINDOW + CHUNK
KV_COLS = 2 * SWA_KV_HEADS * SWA_HD


def _bias_kernel(bucket_ref, rb_ref, o_ref):
    bucket = bucket_ref[...]
    h = pl.program_id(0)
    acc = jnp.zeros((CHUNK, SWA_KEYS), F32)
    for b in range(NUM_BUCKETS):
        acc = jnp.where(bucket == b, rb_ref[b * SWA_HEADS + h], acc)
    o_ref[0] = acc


def _bias_table(bucket, rel_bias_flat):
    return pl.pallas_call(
        _bias_kernel,
        grid=(SWA_HEADS,),
        in_specs=[
            pl.BlockSpec((CHUNK, SWA_KEYS), lambda h: (0, 0)),
            pl.BlockSpec(memory_space=pltpu.SMEM),
        ],
        out_specs=pl.BlockSpec((1, CHUNK, SWA_KEYS), lambda h: (h, 0, 0)),
        out_shape=jax.ShapeDtypeStruct((SWA_HEADS, CHUNK, SWA_KEYS), F32),
        compiler_params=_cparams(1),
        name="swa_bias",
    )(bucket, rel_bias_flat)


def _swa_kernel(q_ref, kvc_ref, kvp_ref, bias_ref, sink_ref, o_ref, kf_ref, *, n_chunks, first_tile_masked):
    rows = n_chunks * CHUNK
    kf_ref[0:WINDOW, :] = kvp_ref[...]
    kf_ref[WINDOW:WINDOW + rows, :] = kvc_ref[...]
    key_pos = lax.broadcasted_iota(jnp.int32, (CHUNK, SWA_KEYS), 1)
    tile_start = pl.program_id(0) * rows
    for c in range(n_chunks):
        r0 = c * CHUNK
        if first_tile_masked:
            valid = key_pos + (tile_start + r0 - WINDOW) >= 0
        for kh in range(SWA_KV_HEADS):
            keys = kf_ref[r0:r0 + SWA_KEYS, kh * SWA_HD:(kh + 1) * SWA_HD].astype(BF16)
            vals = kf_ref[r0:r0 + SWA_KEYS, (SWA_KV_HEADS + kh) * SWA_HD:(SWA_KV_HEADS + kh + 1) * SWA_HD].astype(BF16)
            for g in range(SWA_GROUP):
                hd = kh * SWA_GROUP + g
                q = q_ref[r0:r0 + CHUNK, hd * SWA_HD:(hd + 1) * SWA_HD].astype(BF16)
                logits = _dot_nt(q, keys) * (SWA_HD ** -0.5) + bias_ref[hd]
                if first_tile_masked:
                    logits = jnp.where(valid, logits, -jnp.inf)
                sink = sink_ref[hd]
                m = jnp.maximum(jnp.max(logits, axis=-1, keepdims=True), sink)
                p = jnp.exp(logits - m)
                denom = jnp.sum(p, axis=-1, keepdims=True) + jnp.exp(sink - m)
                probs = (p / denom).astype(BF16)
                o_ref[r0:r0 + CHUNK, hd * SWA_HD:(hd + 1) * SWA_HD] = _dot(probs, vals)


def _swa(qb, kv, kv_hist, bias, sinks, seq_rows):
    n = qb.shape[0]
    if kv_hist is None:
        n_chunks = 2
        rows = n_chunks * CHUNK
        assert rows % WINDOW == 0
        per = rows // WINDOW
        prev_arr = kv
        prev_map = lambda i: (jnp.maximum(i * per - 1, 0), 0)
        masked = True
    else:
        n_chunks = 1
        rows = CHUNK
        assert seq_rows == rows
        prev_arr = kv_hist
        prev_map = lambda i: (i, 0)
        masked = False
    kern = functools.partial(_swa_kernel, n_chunks=n_chunks, first_tile_masked=masked)
    return pl.pallas_call(
        kern,
        grid=(n // rows,),
        in_specs=[
            pl.BlockSpec((rows, SWA_HEADS * SWA_HD), lambda i: (i, 0)),
            pl.BlockSpec((rows, KV_COLS), lambda i: (i, 0)),
            pl.BlockSpec((WINDOW, KV_COLS), prev_map),
            pl.BlockSpec((SWA_HEADS, CHUNK, SWA_KEYS), lambda i: (0, 0, 0)),
            pl.BlockSpec(memory_space=pltpu.SMEM),
        ],
        out_specs=pl.BlockSpec((rows, SWA_HEADS * SWA_HD), lambda i: (i, 0)),
        out_shape=jax.ShapeDtypeStruct((n, SWA_HEADS * SWA_HD), F32),
        scratch_shapes=[pltpu.VMEM((WINDOW + rows, KV_COLS), F32)],
        compiler_params=_cparams(1),
        name="swa",
    )(qb, kv, prev_arr, bias, sinks)


def _merge_kernel(x_ref, oa_ref, ob_ref, gates_ref, mod_ref, wa_ref, wb_ref, wo_ref, o_ref, *, spt):
    ya = _dot(oa_ref[...].astype(BF16), wa_ref[...])
    yb = _dot(ob_ref[...].astype(BF16), wb_ref[...])
    merged = (jax.nn.sigmoid(gates_ref[:, 0:D_MODEL]) * ya
              + jax.nn.sigmoid(gates_ref[:, D_MODEL:2 * D_MODEL]) * yb)
    y = _dot(merged.astype(BF16), wo_ref[...])
    rows = x_ref.shape[0] // spt
    for j in range(spt):
        sl = slice(j * rows, (j + 1) * rows)
        o_ref[sl, :] = x_ref[sl, :] + _mod_row(mod_ref, 5, spt, j) * y[sl, :]


def _merge(x, oa, ob, gates, mod, seq_rows, wa, wb, wo):
    n = x.shape[0]
    tm, spt = _tile_rows(n, seq_rows, 512)
    row_spec = lambda w: pl.BlockSpec((tm, w), lambda i: (i, 0))
    return pl.pallas_call(
        functools.partial(_merge_kernel, spt=spt),
        grid=(n // tm,),
        in_specs=[
            row_spec(D_MODEL), row_spec(D_MODEL), row_spec(D_MODEL), row_spec(2 * D_MODEL),
            _mod_spec(mod.shape[1]),
            _const_spec((D_MODEL, D_MODEL)), _const_spec((D_MODEL, D_MODEL)), _const_spec((D_MODEL, D_MODEL)),
        ],
        out_specs=row_spec(D_MODEL),
        out_shape=jax.ShapeDtypeStruct((n, D_MODEL), F32),
        compiler_params=_cparams(1),
        name="merge",
    )(x, oa, ob, gates, mod, wa, wb, wo)


def _t5_bucket(rel):
    half = NUM_BUCKETS // 2
    max_exact = half // 2
    n = jnp.abs(rel)
    large = max_exact + (jnp.log(jnp.maximum(n, 1).astype(jnp.float32) / max_exact)
                         / math.log(MAX_DISTANCE / max_exact) * (half - max_exact)).astype(jnp.int32)
    large = jnp.minimum(large, half - 1)
    return jnp.where(rel > 0, half, 0) + jnp.where(n < max_exact, n, large)


def _trunk(x, mod, conv_hist, s_hist, k_hist, v_hist, w):
    bsz, seq, _ = x.shape
    n = bsz * seq
    x2 = x.reshape(n, D_MODEL)
    x1 = _ffn(x2, mod, seq, w["norm_ffn1"], w["ffn1_in"], w["ffn1_out"], w["norm_final"], sub=0, final=False)
    qkv, z, qb, kv, gates, ba = _proj(x1, mod, seq, w["norm_mix"], w["w_in"])

    if conv_hist is None:
        hist8 = jnp.zeros((bsz, HIST_ROWS, CONV_DIM), F32)
        s0 = jnp.zeros((bsz, GDN_HEADS, GDN_DK, GDN_DV), F32)
        kv_hist = None
    else:
        hist8 = jnp.concatenate([jnp.zeros((bsz, HIST_ROWS - (CONV_W - 1), CONV_DIM), F32), conv_hist], axis=1)
        s0 = s_hist
        kv_hist = jnp.concatenate([k_hist.reshape(bsz, WINDOW, SWA_KV_HEADS * SWA_HD),
                                   v_hist.reshape(bsz, WINDOW, SWA_KV_HEADS * SWA_HD)], axis=-1)
        kv_hist = kv_hist.reshape(bsz * WINDOW, KV_COLS)
    oa, s_new = _gdn(qkv, z, ba, hist8, s0, w["conv_w"], w["gdn_par"], w["gdn_norm_w"], seq)
    ob = _swa(qb, kv, kv_hist, w["bias"], w["sinks"], seq)
    x3 = _merge(x1, oa, ob, gates, mod, seq, w["w_a"], w["w_b"], w["w_out"])
    y = _ffn(x3, mod, seq, w["norm_ffn2"], w["ffn2_in"], w["ffn2_out"], w["norm_final"], sub=2, final=True)

    conv_new = qkv.reshape(bsz, seq, CONV_DIM)[:, seq - (CONV_W - 1):]
    half = SWA_KV_HEADS * SWA_HD
    kv3 = kv.reshape(bsz, seq, KV_COLS)
    if kv_hist is None:
        k_new = kv3[:, seq - WINDOW:, :half]
        v_new = kv3[:, seq - WINDOW:, half:]
    else:
        keep = WINDOW - seq
        k_new = jnp.concatenate([k_hist.reshape(bsz, WINDOW, half)[:, WINDOW - keep:], kv3[:, :, :half]], axis=1)
        v_new = jnp.concatenate([v_hist.reshape(bsz, WINDOW, half)[:, WINDOW - keep:], kv3[:, :, half:]], axis=1)
    k_new = k_new.reshape(bsz, WINDOW, SWA_KV_HEADS, SWA_HD)
    v_new = v_new.reshape(bsz, WINDOW, SWA_KV_HEADS, SWA_HD)
    return (y.reshape(bsz, seq, D_MODEL), conv_new[None], s_new[None], k_new[None], v_new[None])


def kernel(x_prompt, x_sample, state_gdn_conv, state_gdn_s, cache_swa_k, cache_swa_v, c_prompt, c_sample,
           norm_ffn1, w_ffn1_in, w_ffn1_out, norm_mix, w_in, gdn_conv_w, gdn_a_log, gdn_dt_bias, gdn_norm_w,
           swa_sinks, rel_bias, w_branch_a, w_branch_b, w_out, norm_ffn2, w_ffn2_in, w_ffn2_out,
           w_ada, b_ada, norm_final):
    bp = x_prompt.shape[0]
    bs = x_sample.shape[0]
    assert bp == 1 and x_sample.shape[1] == CHUNK and cache_swa_k.shape[2] == WINDOW

    n_c = bp + bs
    pad = -n_c % SUBLANES
    c_all = jnp.concatenate([c_prompt, c_sample, jnp.zeros((pad, D_MODEL), F32)], axis=0)
    mod_all = _modulation(c_all, w_ada[0], b_ada[0][None, :])
    mod_all = mod_all.reshape(n_c + pad, N_MOD, D_MODEL).transpose(1, 0, 2)
    mod_p = mod_all[:, :bp]
    mod_s = mod_all[:, bp:n_c]

    wi = w_in[0]
    o_z = CONV_DIM
    o_b = o_z + GDN_HEADS * GDN_DV
    o_qb = o_b + 2 * GDN_HEADS
    o_kv = o_qb + SWA_HEADS * SWA_HD
    o_g = o_kv + KV_COLS
    w_packed = jnp.concatenate([wi[:, :o_b], wi[:, o_qb:], wi[:, o_b:o_qb],
                                jnp.zeros((D_MODEL, LANES - 2 * GDN_HEADS), F32)], axis=1).astype(BF16)
    assert w_packed.shape[1] == PROJ_COLS and o_g + 2 * D_MODEL == wi.shape[1]
    par = jnp.zeros((SUBLANES, LANES), F32)
    par = par.at[0, GDN_HEADS:2 * GDN_HEADS].set(gdn_a_log[0]).at[1, GDN_HEADS:2 * GDN_HEADS].set(gdn_dt_bias[0])

    rel = jnp.arange(SWA_KEYS)[None, :] - WINDOW - jnp.arange(CHUNK)[:, None]
    bias = _bias_table(_t5_bucket(rel).astype(jnp.int32), rel_bias.reshape(-1))

    w = {
        "norm_ffn1": norm_ffn1, "ffn1_in": w_ffn1_in[0].astype(BF16), "ffn1_out": w_ffn1_out[0].astype(BF16),
        "norm_mix": norm_mix, "w_in": w_packed, "conv_w": gdn_conv_w[0], "gdn_par": par,
        "gdn_norm_w": gdn_norm_w, "bias": bias, "sinks": swa_sinks[0],
        "w_a": w_branch_a[0].astype(BF16), "w_b": w_branch_b[0].astype(BF16), "w_out": w_out[0].astype(BF16),
        "norm_ffn2": norm_ffn2, "ffn2_in": w_ffn2_in[0].astype(BF16), "ffn2_out": w_ffn2_out[0].astype(BF16),
        "norm_final": norm_final[None, :],
    }
    y_p, p_conv, p_s, p_k, p_v = _trunk(x_prompt, mod_p, None, None, None, None, w)
    y_s, s_conv, s_s, s_k, s_v = _trunk(x_sample, mod_s, state_gdn_conv[0], state_gdn_s[0],
                                        cache_swa_k[0], cache_swa_v[0], w)
    return (y_p, y_s, p_conv, p_s, p_k, p_v, s_conv, s_s, s_k, s_v)
```

```python
import functools
import math

import jax
import jax.numpy as jnp
from jax import lax
from jax.experimental import pallas as pl
from jax.experimental.pallas import tpu as pltpu

F32 = jnp.float32
BF16 = jnp.bfloat16

D_MODEL = 1024
CHUNK = 64
GDN_HEADS = 8
GDN_DK = 128
GDN_DV = 128
CONV_W = 4
CONV_DIM = GDN_HEADS * (2 * GDN_DK + GDN_DV)
SWA_HEADS = 16
SWA_KV_HEADS = 2
SWA_GROUP = SWA_HEADS // SWA_KV_HEADS
SWA_HD = 64
WINDOW = 128
NUM_BUCKETS = 32
MAX_DISTANCE = 128
D_FF = 2816
N_MOD = 9
EPS = 1e-6

LANES = 128
SUBLANES = 8
VMEM_LIMIT = 56 * 1024 * 1024

FF_CHUNK = D_FF // 2
HIST_ROWS = SUBLANES
INV_PASSES = 1
SOLVE_PASSES = 1
GDN_CHUNKS_PER_STEP = 2
SWA_CHUNKS_PER_STEP = 2


def _dot(a, b):
    return jnp.dot(a, b, preferred_element_type=F32)


def _dot_nt(a, b):
    return lax.dot_general(a, b, (((1,), (1,)), ((), ())), preferred_element_type=F32)


def _dot_tn(a, b):
    return lax.dot_general(a, b, (((0,), (0,)), ((), ())), preferred_element_type=F32)


def _split(a, passes):
    hi = a.astype(BF16)
    if passes == 1:
        return hi, None
    return hi, (a - hi.astype(F32)).astype(BF16)


def _dot_split(a_parts, b_parts, passes):
    ah, al = a_parts
    bh, bl = b_parts
    out = _dot(ah, bh)
    if passes == 3:
        out = out + _dot(al, bh) + _dot(ah, bl)
    return out


def _cparams(n_grid):
    return pltpu.CompilerParams(dimension_semantics=("arbitrary",) * n_grid, vmem_limit_bytes=VMEM_LIMIT)


def _const_spec(shape):
    nd = len(shape)
    return pl.BlockSpec(shape, lambda i: (0,) * nd, pipeline_mode=pl.Buffered(1))


def _mod_kernel(c_ref, w_ref, b_ref, o_ref):
    c = c_ref[...]
    a = (c * jax.nn.sigmoid(c)).astype(BF16)
    o_ref[...] = _dot(a, w_ref[...].astype(BF16)) + b_ref[...]


def _modulation(c_pad, w_ada, b_ada):
    rows = c_pad.shape[0]
    return pl.pallas_call(
        _mod_kernel,
        grid=(N_MOD,),
        in_specs=[
            pl.BlockSpec((rows, D_MODEL), lambda j: (0, 0)),
            pl.BlockSpec((D_MODEL, D_MODEL), lambda j: (0, j)),
            pl.BlockSpec((1, D_MODEL), lambda j: (0, j)),
        ],
        out_specs=pl.BlockSpec((rows, D_MODEL), lambda j: (0, j)),
        out_shape=jax.ShapeDtypeStruct((rows, N_MOD * D_MODEL), F32),
        compiler_params=_cparams(1),
        name="adaln_mod",
    )(c_pad, w_ada, b_ada)


def _mod_row(mod_ref, idx, spt, j):
    if mod_ref.shape[1] == 1:
        return mod_ref[idx, 0:1, :]
    return mod_ref[idx, pl.ds(pl.program_id(0) * spt + j, 1), :]


def _norm_mod_store(h_ref, x_ref, g_ref, mod_ref, sub, spt):
    rows = x_ref.shape[0] // spt
    g = g_ref[...]
    for j in range(spt):
        xs = x_ref[j * rows:(j + 1) * rows, :]
        ms = jnp.mean(xs * xs, axis=-1, keepdims=True)
        y = xs * lax.rsqrt(ms + EPS) * g
        sh = _mod_row(mod_ref, 3 * sub, spt, j)
        sc = _mod_row(mod_ref, 3 * sub + 1, spt, j)
        h_ref[j * rows:(j + 1) * rows, :] = (y * (1.0 + sc) + sh).astype(h_ref.dtype)


def _tile_rows(n_rows, seq_rows, target):
    if seq_rows >= target:
        assert seq_rows % target == 0
        return target, 1
    assert target % seq_rows == 0
    tm = min(target, n_rows)
    assert n_rows % tm == 0
    return tm, tm // seq_rows


def _mod_spec(n_seq_total):
    return pl.BlockSpec((N_MOD, n_seq_total, D_MODEL), lambda i: (0, 0, 0))


def _ffn_kernel(x_ref, mod_ref, g_ref, w1_ref, w2_ref, gf_ref, o_ref, h_ref, *, sub, final, spt):
    _norm_mod_store(h_ref, x_ref, g_ref, mod_ref, sub, spt)
    h = h_ref[...]
    acc = None
    for c in range(D_FF // FF_CHUNK):
        c0 = c * FF_CHUNK
        gate = _dot(h, w1_ref[:, c0:c0 + FF_CHUNK])
        up = _dot(h, w1_ref[:, D_FF + c0:D_FF + c0 + FF_CHUNK])
        a = (gate * jax.nn.sigmoid(gate) * up).astype(BF16)
        part = _dot(a, w2_ref[c0:c0 + FF_CHUNK, :])
        acc = part if acc is None else acc + part
    rows = x_ref.shape[0] // spt
    for j in range(spt):
        sl = slice(j * rows, (j + 1) * rows)
        ga = _mod_row(mod_ref, 3 * sub + 2, spt, j)
        xn = x_ref[sl, :] + 0.5 * ga * acc[sl, :]
        if final:
            ms = jnp.mean(xn * xn, axis=-1, keepdims=True)
            xn = xn * lax.rsqrt(ms + EPS) * gf_ref[...]
        o_ref[sl, :] = xn


def _ffn(x, mod, seq_rows, g, w1, w2, gf, *, sub, final):
    n = x.shape[0]
    tm, spt = _tile_rows(n, seq_rows, 512)
    kern = functools.partial(_ffn_kernel, sub=sub, final=final, spt=spt)
    return pl.pallas_call(
        kern,
        grid=(n // tm,),
        in_specs=[
            pl.BlockSpec((tm, D_MODEL), lambda i: (i, 0)),
            _mod_spec(mod.shape[1]),
            _const_spec((1, D_MODEL)),
            _const_spec((D_MODEL, 2 * D_FF)),
            _const_spec((D_FF, D_MODEL)),
            _const_spec((1, D_MODEL)),
        ],
        out_specs=pl.BlockSpec((tm, D_MODEL), lambda i: (i, 0)),
        out_shape=jax.ShapeDtypeStruct((n, D_MODEL), F32),
        scratch_shapes=[pltpu.VMEM((tm, D_MODEL), BF16)],
        compiler_params=_cparams(1),
        name="ffn_final" if final else "ffn",
    )(x, mod, g, w1, w2, gf)


PROJ_GROUPS = (("qkv", CONV_DIM), ("z", GDN_HEADS * GDN_DV), ("qb", SWA_HEADS * SWA_HD),
               ("kv", 2 * SWA_KV_HEADS * SWA_HD), ("gates", 2 * D_MODEL), ("ba", LANES))
PROJ_COLS = sum(w for _, w in PROJ_GROUPS)


def _proj_kernel(x_ref, mod_ref, g_ref, w_ref, *refs, spt):
    out_refs, h_ref = refs[:-1], refs[-1]
    _norm_mod_store(h_ref, x_ref, g_ref, mod_ref, 1, spt)
    h = h_ref[...]
    c0 = 0
    for (_, width), o_ref in zip(PROJ_GROUPS, out_refs):
        o_ref[...] = _dot(h, w_ref[:, c0:c0 + width]).astype(o_ref.dtype)
        c0 += width


def _proj(x, mod, seq_rows, g, w_packed):
    n = x.shape[0]
    tm, spt = _tile_rows(n, seq_rows, 256)
    return pl.pallas_call(
        functools.partial(_proj_kernel, spt=spt),
        grid=(n // tm,),
        in_specs=[
            pl.BlockSpec((tm, D_MODEL), lambda i: (i, 0)),
            _mod_spec(mod.shape[1]),
            _const_spec((1, D_MODEL)),
            _const_spec((D_MODEL, PROJ_COLS)),
        ],
        out_specs=[pl.BlockSpec((tm, w), lambda i: (i, 0)) for _, w in PROJ_GROUPS],
        out_shape=[jax.ShapeDtypeStruct((n, w), F32) for _, w in PROJ_GROUPS],
        scratch_shapes=[pltpu.VMEM((tm, D_MODEL), BF16)],
        compiler_params=_cparams(1),
        name="in_proj",
    )(x, mod, g, w_packed)


def _unit_lower_inverses(ls, eye, level_masks):
    ts = [eye - jnp.where(level_masks[0], l, 0.0) for l in ls]
    for mask in level_masks[1:]:
        lk = [_split(jnp.where(mask, l, 0.0), INV_PASSES) for l in ls]
        tp = [_split(t, INV_PASSES) for t in ts]
        m1 = [_dot_split(a, b, INV_PASSES) for a, b in zip(lk, tp)]
        m2 = [_dot_split(a, _split(b, INV_PASSES), INV_PASSES) for a, b in zip(tp, m1)]
        ts = [t - m for t, m in zip(ts, m2)]
    return ts


def _gdn_kernel(qkv_ref, z_ref, ba_ref, hist_ref, s0_ref, cw_ref, par_ref, nw_ref,
                o_ref, sout_ref, xp_ref, s_ref, *, n_chunks, carry):
    pid = pl.program_id(0)
    seg = CHUNK if carry else HIST_ROWS + CHUNK

    def _load_state():
        xp_ref[0:HIST_ROWS, :] = hist_ref[0]
        s_ref[0] = s0_ref[0]

    if carry:
        pl.when(pid == 0)(_load_state)
        xp_ref[HIST_ROWS:HIST_ROWS + n_chunks * CHUNK, :] = qkv_ref[...]
    else:
        for c in range(n_chunks):
            xp_ref[c * seg:c * seg + HIST_ROWS, :] = hist_ref[c]
            xp_ref[c * seg + HIST_ROWS:(c + 1) * seg, :] = qkv_ref[c * CHUNK:(c + 1) * CHUNK, :]
            s_ref[c] = s0_ref[c]

    ri = lax.broadcasted_iota(jnp.int32, (CHUNK, CHUNK), 0)
    ci = lax.broadcasted_iota(jnp.int32, (CHUNK, CHUNK), 1)
    causal = ri >= ci
    strict = ri > ci
    eye = (ri == ci).astype(F32)
    tri = causal.astype(BF16)
    level_masks = []
    for lvl in range(int(math.log2(CHUNK))):
        same_block = (ri >> (lvl + 1)) == (ci >> (lvl + 1))
        level_masks.append(same_block & (((ri >> lvl) & 1) == 1) & (((ci >> lvl) & 1) == 0))

    a_coef = -jnp.exp(par_ref[0:1, :])
    dt_bias = par_ref[1:2, :]
    norm_w = nw_ref[...]

    def conv_tile(c, c0):
        win = xp_ref[c * seg:c * seg + HIST_ROWS + CHUNK, c0:c0 + LANES]
        acc = cw_ref[CONV_W - 1:CONV_W, c0:c0 + LANES] * win[HIST_ROWS:, :]
        for s in range(1, CONV_W):
            tap = pltpu.roll(win, s, 0)[HIST_ROWS:, :]
            acc = acc + cw_ref[CONV_W - 1 - s:CONV_W - s, c0:c0 + LANES] * tap
        return acc * jax.nn.sigmoid(acc)

    items = [(c, h) for c in range(n_chunks) for h in range(GDN_HEADS)]
    beta_all, gc_all = [], []
    for c in range(n_chunks):
        ba = ba_ref[c * CHUNK:(c + 1) * CHUNK, :]
        beta_all.append(jax.nn.sigmoid(ba))
        g_all = a_coef * jax.nn.softplus(ba + dt_bias)
        g_hi = g_all.astype(BF16)
        g_r1 = g_all - g_hi.astype(F32)
        g_mid = g_r1.astype(BF16)
        g_lo = (g_r1 - g_mid.astype(F32)).astype(BF16)
        gc_all.append(_dot(tri, g_hi) + _dot(tri, g_mid) + _dot(tri, g_lo))

    q = [conv_tile(c, h * GDN_DK) for c, h in items]
    k = [conv_tile(c, GDN_HEADS * GDN_DK + h * GDN_DK) for c, h in items]
    v = [conv_tile(c, 2 * GDN_HEADS * GDN_DK + h * GDN_DV) for c, h in items]
    q = [x * lax.rsqrt(jnp.sum(x * x, axis=-1, keepdims=True) + EPS) * (GDN_DK ** -0.5) for x in q]
    k = [x * lax.rsqrt(jnp.sum(x * x, axis=-1, keepdims=True) + EPS) for x in k]
    beta = [beta_all[c][:, h:h + 1] for c, h in items]
    gcol = [gc_all[c][:, GDN_HEADS + h:GDN_HEADS + h + 1] for c, h in items]
    grow = [jnp.sum(g * eye, axis=0, keepdims=True) for g in gcol]
    decay = [jnp.where(causal, jnp.exp(jnp.where(causal, gc - gr, 0.0)), 0.0) for gc, gr in zip(gcol, grow)]
    kb = [x * b for x, b in zip(k, beta)]
    k16 = [x.astype(BF16) for x in k]
    kk = [_dot_nt(a.astype(BF16), b) for a, b in zip(kb, k16)]
    qk = [_dot_nt(a.astype(BF16), b) for a, b in zip(q, k16)]
    ls = [jnp.where(strict, a * d, 0.0) for a, d in zip(kk, decay)]
    intra16 = [jnp.where(causal, a * d, 0.0).astype(BF16) for a, d in zip(qk, decay)]
    ts = _unit_lower_inverses(ls, eye, level_masks)
    ys = [_split(t - eye, SOLVE_PASSES) for t in ts]
    egc = [jnp.exp(g) for g in gcol]
    vb = [x * b for x, b in zip(v, beta)]
    kbe = [x * e for x, e in zip(kb, egc)]
    u = [x + _dot_split(y, _split(x, SOLVE_PASSES), SOLVE_PASSES) for x, y in zip(vb, ys)]
    w16 = [(x + _dot_split(y, _split(x, SOLVE_PASSES), SOLVE_PASSES)).astype(BF16) for x, y in zip(kbe, ys)]
    qg16 = [(x * e).astype(BF16) for x, e in zip(q, egc)]
    g_last = [g[CHUNK - 1:CHUNK, :] for g in gcol]
    kd16 = [(x * jnp.exp(gl - g)).astype(BF16) for x, gl, g in zip(k, g_last, gcol)]
    eg_last = [jnp.exp(gl) for gl in g_last]

    o = [None] * len(items)
    if carry:
        state = [s_ref[0, h] for h in range(GDN_HEADS)]
    for c in range(n_chunks):
        idx = [c * GDN_HEADS + h for h in range(GDN_HEADS)]
        if not carry:
            state = [s_ref[c, h] for h in range(GDN_HEADS)]
        s16 = [s.astype(BF16) for s in state]
        v_new = [u[i] - _dot(w16[i], s) for i, s in zip(idx, s16)]
        vn16 = [x.astype(BF16) for x in v_new]
        for i, s, vn in zip(idx, s16, vn16):
            o[i] = _dot(qg16[i], s) + _dot(intra16[i], vn)
        state = [s * eg_last[i] + _dot_tn(kd16[i], vn) for i, s, vn in zip(idx, state, vn16)]
        if not carry:
            for h in range(GDN_HEADS):
                sout_ref[c, h] = state[h]

    for (c, h), oo in zip(items, o):
        zz = z_ref[c * CHUNK:(c + 1) * CHUNK, h * GDN_DV:(h + 1) * GDN_DV]
        on = oo * lax.rsqrt(jnp.mean(oo * oo, axis=-1, keepdims=True) + EPS) * norm_w * (zz * jax.nn.sigmoid(zz))
        o_ref[c * CHUNK:(c + 1) * CHUNK, h * GDN_DV:(h + 1) * GDN_DV] = on

    if carry:
        for h in range(GDN_HEADS):
            s_ref[0, h] = state[h]
        xp_ref[0:HIST_ROWS, :] = xp_ref[n_chunks * CHUNK:n_chunks * CHUNK + HIST_ROWS, :]

        def _store_state():
            sout_ref[0] = s_ref[0]

        pl.when(pid == pl.num_programs(0) - 1)(_store_state)


def _gdn(qkv, z, ba, hist8, s0, conv_w, par, norm_w, seq_rows):
    n = qkv.shape[0]
    n_seq = n // seq_rows
    carry = n_seq == 1
    n_chunks = GDN_CHUNKS_PER_STEP
    rows = n_chunks * CHUNK
    assert n % rows == 0
    if carry:
        n_state = 1
        seq_map = lambda i: (0, 0, 0)
        s_map = lambda i: (0, 0, 0, 0)
        xp_rows = HIST_ROWS + rows
    else:
        assert seq_rows == CHUNK
        n_state = n_chunks
        seq_map = lambda i: (i, 0, 0)
        s_map = lambda i: (i, 0, 0, 0)
        xp_rows = n_chunks * (HIST_ROWS + CHUNK)
    kern = functools.partial(_gdn_kernel, n_chunks=n_chunks, carry=carry)
    return pl.pallas_call(
        kern,
        grid=(n // rows,),
        in_specs=[
            pl.BlockSpec((rows, CONV_DIM), lambda i: (i, 0)),
            pl.BlockSpec((rows, GDN_HEADS * GDN_DV), lambda i: (i, 0)),
            pl.BlockSpec((rows, LANES), lambda i: (i, 0)),
            pl.BlockSpec((n_state, HIST_ROWS, CONV_DIM), seq_map),
            pl.BlockSpec((n_state, GDN_HEADS, GDN_DK, GDN_DV), s_map),
            pl.BlockSpec((CONV_W, CONV_DIM), lambda i: (0, 0)),
            pl.BlockSpec((SUBLANES, LANES), lambda i: (0, 0)),
            pl.BlockSpec((1, GDN_DV), lambda i: (0, 0)),
        ],
        out_specs=[
            pl.BlockSpec((rows, GDN_HEADS * GDN_DV), lambda i: (i, 0)),
            pl.BlockSpec((n_state, GDN_HEADS, GDN_DK, GDN_DV), s_map),
        ],
        out_shape=[
            jax.ShapeDtypeStruct((n, GDN_HEADS * GDN_DV), F32),
            jax.ShapeDtypeStruct((n_seq, GDN_HEADS, GDN_DK, GDN_DV), F32),
        ],
        scratch_shapes=[
            pltpu.VMEM((xp_rows, CONV_DIM), F32),
            pltpu.VMEM((n_state, GDN_HEADS, GDN_DK, GDN_DV), F32),
        ],
        compiler_params=_cparams(1),
        name="gdn",
    )(qkv, z, ba, hist8, s0, conv_w, par, norm_w)


SWA_KEYS = WINDOW + CHUNK
KV_COLS = 2 * SWA_KV_HEADS * SWA_HD


def _bias_kernel(bucket_ref, rb_ref, o_ref):
    bucket = bucket_ref[...]
    h = pl.program_id(0)
    acc = jnp.zeros((CHUNK, SWA_KEYS), F32)
    for b in range(NUM_BUCKETS):
        acc = jnp.where(bucket == b, rb_ref[b * SWA_HEADS + h], acc)
    o_ref[0] = acc


def _bias_table(bucket, rel_bias_flat):
    return pl.pallas_call(
        _bias_kernel,
        grid=(SWA_HEADS,),
        in_specs=[
            pl.BlockSpec((CHUNK, SWA_KEYS), lambda h: (0, 0)),
            pl.BlockSpec(memory_space=pltpu.SMEM),
        ],
        out_specs=pl.BlockSpec((1, CHUNK, SWA_KEYS), lambda h: (h, 0, 0)),
        out_shape=jax.ShapeDtypeStruct((SWA_HEADS, CHUNK, SWA_KEYS), F32),
        compiler_params=_cparams(1),
        name="swa_bias",
    )(bucket, rel_bias_flat)


def _swa_kernel(q_ref, kvc_ref, kvp_ref, bias_ref, sink_ref, o_ref, kf_ref, *, n_chunks, one_sequence):
    if one_sequence:
        kf_ref[0:WINDOW, :] = kvp_ref[...]
        kf_ref[WINDOW:WINDOW + n_chunks * CHUNK, :] = kvc_ref[...]
        key_start = [c * CHUNK for c in range(n_chunks)]
        key_pos = lax.broadcasted_iota(jnp.int32, (CHUNK, SWA_KEYS), 1)
        tile_start = pl.program_id(0) * (n_chunks * CHUNK)
        valid = [key_pos + (tile_start + c * CHUNK - WINDOW) >= 0 for c in range(n_chunks)]
    else:
        for c in range(n_chunks):
            kf_ref[c * SWA_KEYS:c * SWA_KEYS + WINDOW, :] = kvp_ref[c * WINDOW:(c + 1) * WINDOW, :]
            kf_ref[c * SWA_KEYS + WINDOW:(c + 1) * SWA_KEYS, :] = kvc_ref[c * CHUNK:(c + 1) * CHUNK, :]
        key_start = [c * SWA_KEYS for c in range(n_chunks)]

    pairs = [(c, kh) for c in range(n_chunks) for kh in range(SWA_KV_HEADS)]
    keys = {(c, kh): kf_ref[key_start[c]:key_start[c] + SWA_KEYS, kh * SWA_HD:(kh + 1) * SWA_HD].astype(BF16)
            for c, kh in pairs}
    vals = {(c, kh): kf_ref[key_start[c]:key_start[c] + SWA_KEYS,
                            (SWA_KV_HEADS + kh) * SWA_HD:(SWA_KV_HEADS + kh + 1) * SWA_HD].astype(BF16)
            for c, kh in pairs}
    items = [(c, hd) for c in range(n_chunks) for hd in range(SWA_HEADS)]
    q = [q_ref[c * CHUNK:(c + 1) * CHUNK, hd * SWA_HD:(hd + 1) * SWA_HD].astype(BF16) for c, hd in items]
    logits = [_dot_nt(x, keys[(c, hd // SWA_GROUP)]) * (SWA_HD ** -0.5) + bias_ref[hd]
              for x, (c, hd) in zip(q, items)]
    if one_sequence:
        logits = [jnp.where(valid[c], x, -jnp.inf) for x, (c, _) in zip(logits, items)]
    sink = [sink_ref[hd] for _, hd in items]
    m = [jnp.maximum(jnp.max(x, axis=-1, keepdims=True), s) for x, s in zip(logits, sink)]
    p = [jnp.exp(x - mm) for x, mm in zip(logits, m)]
    denom = [jnp.sum(x, axis=-1, keepdims=True) + jnp.exp(s - mm) for x, s, mm in zip(p, sink, m)]
    probs = [(x / d).astype(BF16) for x, d in zip(p, denom)]
    out = [_dot(x, vals[(c, hd // SWA_GROUP)]) for x, (c, hd) in zip(probs, items)]
    for x, (c, hd) in zip(out, items):
        o_ref[c * CHUNK:(c + 1) * CHUNK, hd * SWA_HD:(hd + 1) * SWA_HD] = x


def _swa(qb, kv, kv_hist, bias, sinks, seq_rows):
    n = qb.shape[0]
    n_chunks = SWA_CHUNKS_PER_STEP
    rows = n_chunks * CHUNK
    assert n % rows == 0
    one_sequence = kv_hist is None
    if one_sequence:
        assert rows % WINDOW == 0
        per = rows // WINDOW
        prev_arr = kv
        prev_spec = pl.BlockSpec((WINDOW, KV_COLS), lambda i: (jnp.maximum(i * per - 1, 0), 0))
        kf_rows = WINDOW + rows
    else:
        assert seq_rows == CHUNK
        prev_arr = kv_hist
        prev_spec = pl.BlockSpec((n_chunks * WINDOW, KV_COLS), lambda i: (i, 0))
        kf_rows = n_chunks * SWA_KEYS
    kern = functools.partial(_swa_kernel, n_chunks=n_chunks, one_sequence=one_sequence)
    return pl.pallas_call(
        kern,
        grid=(n // rows,),
        in_specs=[
            pl.BlockSpec((rows, SWA_HEADS * SWA_HD), lambda i: (i, 0)),
            pl.BlockSpec((rows, KV_COLS), lambda i: (i, 0)),
            prev_spec,
            pl.BlockSpec((SWA_HEADS, CHUNK, SWA_KEYS), lambda i: (0, 0, 0)),
            pl.BlockSpec(memory_space=pltpu.SMEM),
        ],
        out_specs=pl.BlockSpec((rows, SWA_HEADS * SWA_HD), lambda i: (i, 0)),
        out_shape=jax.ShapeDtypeStruct((n, SWA_HEADS * SWA_HD), F32),
        scratch_shapes=[pltpu.VMEM((kf_rows, KV_COLS), F32)],
        compiler_params=_cparams(1),
        name="swa",
    )(qb, kv, prev_arr, bias, sinks)


def _merge_kernel(x_ref, oa_ref, ob_ref, gates_ref, mod_ref, wa_ref, wb_ref, wo_ref, o_ref, *, spt):
    ya = _dot(oa_ref[...].astype(BF16), wa_ref[...])
    yb = _dot(ob_ref[...].astype(BF16), wb_ref[...])
    merged = (jax.nn.sigmoid(gates_ref[:, 0:D_MODEL]) * ya
              + jax.nn.sigmoid(gates_ref[:, D_MODEL:2 * D_MODEL]) * yb)
    y = _dot(merged.astype(BF16), wo_ref[...])
    rows = x_ref.shape[0] // spt
    for j in range(spt):
        sl = slice(j * rows, (j + 1) * rows)
        o_ref[sl, :] = x_ref[sl, :] + _mod_row(mod_ref, 5, spt, j) * y[sl, :]


def _merge(x, oa, ob, gates, mod, seq_rows, wa, wb, wo):
    n = x.shape[0]
    tm, spt = _tile_rows(n, seq_rows, 512)
    row_spec = lambda w: pl.BlockSpec((tm, w), lambda i: (i, 0))
    return pl.pallas_call(
        functools.partial(_merge_kernel, spt=spt),
        grid=(n // tm,),
        in_specs=[
            row_spec(D_MODEL), row_spec(D_MODEL), row_spec(D_MODEL), row_spec(2 * D_MODEL),
            _mod_spec(mod.shape[1]),
            _const_spec((D_MODEL, D_MODEL)), _const_spec((D_MODEL, D_MODEL)), _const_spec((D_MODEL, D_MODEL)),
        ],
        out_specs=row_spec(D_MODEL),
        out_shape=jax.ShapeDtypeStruct((n, D_MODEL), F32),
        compiler_params=_cparams(1),
        name="merge",
    )(x, oa, ob, gates, mod, wa, wb, wo)


def _t5_bucket(rel):
    half = NUM_BUCKETS // 2
    max_exact = half // 2
    n = jnp.abs(rel)
    large = max_exact + (jnp.log(jnp.maximum(n, 1).astype(jnp.float32) / max_exact)
                         / math.log(MAX_DISTANCE / max_exact) * (half - max_exact)).astype(jnp.int32)
    large = jnp.minimum(large, half - 1)
    return jnp.where(rel > 0, half, 0) + jnp.where(n < max_exact, n, large)


def _trunk(x, mod, conv_hist, s_hist, k_hist, v_hist, w):
    bsz, seq, _ = x.shape
    n = bsz * seq
    x2 = x.reshape(n, D_MODEL)
    x1 = _ffn(x2, mod, seq, w["norm_ffn1"], w["ffn1_in"], w["ffn1_out"], w["norm_final"], sub=0, final=False)
    qkv, z, qb, kv, gates, ba = _proj(x1, mod, seq, w["norm_mix"], w["w_in"])

    if conv_hist is None:
        hist8 = jnp.zeros((bsz, HIST_ROWS, CONV_DIM), F32)
        s0 = jnp.zeros((bsz, GDN_HEADS, GDN_DK, GDN_DV), F32)
        kv_hist = None
    else:
        hist8 = jnp.concatenate([jnp.zeros((bsz, HIST_ROWS - (CONV_W - 1), CONV_DIM), F32), conv_hist], axis=1)
        s0 = s_hist
        kv_hist = jnp.concatenate([k_hist.reshape(bsz, WINDOW, SWA_KV_HEADS * SWA_HD),
                                   v_hist.reshape(bsz, WINDOW, SWA_KV_HEADS * SWA_HD)], axis=-1)
        kv_hist = kv_hist.reshape(bsz * WINDOW, KV_COLS)
    oa, s_new = _gdn(qkv, z, ba, hist8, s0, w["conv_w"], w["gdn_par"], w["gdn_norm_w"], seq)
    ob = _swa(qb, kv, kv_hist, w["bias"], w["sinks"], seq)
    x3 = _merge(x1, oa, ob, gates, mod, seq, w["w_a"], w["w_b"], w["w_out"])
    y = _ffn(x3, mod, seq, w["norm_ffn2"], w["ffn2_in"], w["ffn2_out"], w["norm_final"], sub=2, final=True)

    conv_new = qkv.reshape(bsz, seq, CONV_DIM)[:, seq - (CONV_W - 1):]
    half = SWA_KV_HEADS * SWA_HD
    kv3 = kv.reshape(bsz, seq, KV_COLS)
    if kv_hist is None:
        k_new = kv3[:, seq - WINDOW:, :half]
        v_new = kv3[:, seq - WINDOW:, half:]
    else:
        keep = WINDOW - seq
        k_new = jnp.concatenate([k_hist.reshape(bsz, WINDOW, half)[:, WINDOW - keep:], kv3[:, :, :half]], axis=1)
        v_new = jnp.concatenate([v_hist.reshape(bsz, WINDOW, half)[:, WINDOW - keep:], kv3[:, :, half:]], axis=1)
    k_new = k_new.reshape(bsz, WINDOW, SWA_KV_HEADS, SWA_HD)
    v_new = v_new.reshape(bsz, WINDOW, SWA_KV_HEADS, SWA_HD)
    return (y.reshape(bsz, seq, D_MODEL), conv_new[None], s_new[None], k_new[None], v_new[None])


def kernel(x_prompt, x_sample, state_gdn_conv, state_gdn_s, cache_swa_k, cache_swa_v, c_prompt, c_sample,
           norm_ffn1, w_ffn1_in, w_ffn1_out, norm_mix, w_in, gdn_conv_w, gdn_a_log, gdn_dt_bias, gdn_norm_w,
           swa_sinks, rel_bias, w_branch_a, w_branch_b, w_out, norm_ffn2, w_ffn2_in, w_ffn2_out,
           w_ada, b_ada, norm_final):
    bp = x_prompt.shape[0]
    bs = x_sample.shape[0]
    assert bp == 1 and x_sample.shape[1] == CHUNK and cache_swa_k.shape[2] == WINDOW

    n_c = bp + bs
    pad = -n_c % SUBLANES
    c_all = jnp.concatenate([c_prompt, c_sample, jnp.zeros((pad, D_MODEL), F32)], axis=0)
    mod_all = _modulation(c_all, w_ada[0], b_ada[0][None, :])
    mod_all = mod_all.reshape(n_c + pad, N_MOD, D_MODEL).transpose(1, 0, 2)
    mod_p = mod_all[:, :bp]
    mod_s = mod_all[:, bp:n_c]

    wi = w_in[0]
    o_z = CONV_DIM
    o_b = o_z + GDN_HEADS * GDN_DV
    o_qb = o_b + 2 * GDN_HEADS
    o_kv = o_qb + SWA_HEADS * SWA_HD
    o_g = o_kv + KV_COLS
    w_packed = jnp.concatenate([wi[:, :o_b], wi[:, o_qb:], wi[:, o_b:o_qb],
                                jnp.zeros((D_MODEL, LANES - 2 * GDN_HEADS), F32)], axis=1).astype(BF16)
    assert w_packed.shape[1] == PROJ_COLS and o_g + 2 * D_MODEL == wi.shape[1]
    par = jnp.zeros((SUBLANES, LANES), F32)
    par = par.at[0, GDN_HEADS:2 * GDN_HEADS].set(gdn_a_log[0]).at[1, GDN_HEADS:2 * GDN_HEADS].set(gdn_dt_bias[0])

    rel = jnp.arange(SWA_KEYS)[None, :] - WINDOW - jnp.arange(CHUNK)[:, None]
    bias = _bias_table(_t5_bucket(rel).astype(jnp.int32), rel_bias.reshape(-1))

    w = {
        "norm_ffn1": norm_ffn1, "ffn1_in": w_ffn1_in[0].astype(BF16), "ffn1_out": w_ffn1_out[0].astype(BF16),
        "norm_mix": norm_mix, "w_in": w_packed, "conv_w": gdn_conv_w[0], "gdn_par": par,
        "gdn_norm_w": gdn_norm_w, "bias": bias, "sinks": swa_sinks[0],
        "w_a": w_branch_a[0].astype(BF16), "w_b": w_branch_b[0].astype(BF16), "w_out": w_out[0].astype(BF16),
        "norm_ffn2": norm_ffn2, "ffn2_in": w_ffn2_in[0].astype(BF16), "ffn2_out": w_ffn2_out[0].astype(BF16),
        "norm_final": norm_final[None, :],
    }
    y_p, p_conv, p_s, p_k, p_v = _trunk(x_prompt, mod_p, None, None, None, None, w)
    y_s, s_conv, s_s, s_k, s_v = _trunk(x_sample, mod_s, state_gdn_conv[0], state_gdn_s[0],
                                        cache_swa_k[0], cache_swa_v[0], w)
    return (y_p, y_s, p_conv, p_s, p_k, p_v, s_conv, s_s, s_k, s_v)
```

```python
import functools
import math

import jax
import jax.numpy as jnp
from jax import lax
from jax.experimental import pallas as pl
from jax.experimental.pallas import tpu as pltpu

F32 = jnp.float32
BF16 = jnp.bfloat16

D_MODEL = 1024
CHUNK = 64
GDN_HEADS = 8
GDN_DK = 128
GDN_DV = 128
CONV_W = 4
CONV_DIM = GDN_HEADS * (2 * GDN_DK + GDN_DV)
SWA_HEADS = 16
SWA_KV_HEADS = 2
SWA_GROUP = SWA_HEADS // SWA_KV_HEADS
SWA_HD = 64
WINDOW = 128
NUM_BUCKETS = 32
MAX_DISTANCE = 128
D_FF = 2816
N_MOD = 9
EPS = 1e-6

LANES = 128
SUBLANES = 8
VMEM_LIMIT = 56 * 1024 * 1024

FF_CHUNK = D_FF // 2
HIST_ROWS = SUBLANES
INV_PASSES = 1
SOLVE_PASSES = 1
GDN_CHUNKS_PER_STEP = 4
SWA_CHUNKS_PER_STEP = 4


def _dot(a, b):
    return jnp.dot(a, b, preferred_element_type=F32)


def _dot_nt(a, b):
    return lax.dot_general(a, b, (((1,), (1,)), ((), ())), preferred_element_type=F32)


def _dot_tn(a, b):
    return lax.dot_general(a, b, (((0,), (0,)), ((), ())), preferred_element_type=F32)


def _split(a, passes):
    hi = a.astype(BF16)
    if passes == 1:
        return hi, None
    return hi, (a - hi.astype(F32)).astype(BF16)


def _dot_split(a_parts, b_parts, passes):
    ah, al = a_parts
    bh, bl = b_parts
    out = _dot(ah, bh)
    if passes == 3:
        out = out + _dot(al, bh) + _dot(ah, bl)
    return out


def _cparams(n_grid):
    return pltpu.CompilerParams(dimension_semantics=("arbitrary",) * n_grid, vmem_limit_bytes=VMEM_LIMIT)


def _const_spec(shape):
    nd = len(shape)
    return pl.BlockSpec(shape, lambda i: (0,) * nd, pipeline_mode=pl.Buffered(1))


def _mod_kernel(c_ref, w_ref, b_ref, o_ref):
    c = c_ref[...]
    a = (c * jax.nn.sigmoid(c)).astype(BF16)
    o_ref[...] = _dot(a, w_ref[...].astype(BF16)) + b_ref[...]


def _modulation(c_pad, w_ada, b_ada):
    rows = c_pad.shape[0]
    return pl.pallas_call(
        _mod_kernel,
        grid=(N_MOD,),
        in_specs=[
            pl.BlockSpec((rows, D_MODEL), lambda j: (0, 0)),
            pl.BlockSpec((D_MODEL, D_MODEL), lambda j: (0, j)),
            pl.BlockSpec((1, D_MODEL), lambda j: (0, j)),
        ],
        out_specs=pl.BlockSpec((rows, D_MODEL), lambda j: (0, j)),
        out_shape=jax.ShapeDtypeStruct((rows, N_MOD * D_MODEL), F32),
        compiler_params=_cparams(1),
        name="adaln_mod",
    )(c_pad, w_ada, b_ada)


def _mod_row(mod_ref, idx, spt, j):
    if mod_ref.shape[1] == 1:
        return mod_ref[idx, 0:1, :]
    return mod_ref[idx, pl.ds(pl.program_id(0) * spt + j, 1), :]


def _norm_mod_store(h_ref, x_ref, g_ref, mod_ref, sub, spt):
    rows = x_ref.shape[0] // spt
    g = g_ref[...]
    for j in range(spt):
        xs = x_ref[j * rows:(j + 1) * rows, :]
        ms = jnp.mean(xs * xs, axis=-1, keepdims=True)
        y = xs * lax.rsqrt(ms + EPS) * g
        sh = _mod_row(mod_ref, 3 * sub, spt, j)
        sc = _mod_row(mod_ref, 3 * sub + 1, spt, j)
        h_ref[j * rows:(j + 1) * rows, :] = (y * (1.0 + sc) + sh).astype(h_ref.dtype)


def _tile_rows(n_rows, seq_rows, target):
    if seq_rows >= target:
        assert seq_rows % target == 0
        return target, 1
    assert target % seq_rows == 0
    tm = min(target, n_rows)
    assert n_rows % tm == 0
    return tm, tm // seq_rows


def _mod_spec(n_seq_total):
    return pl.BlockSpec((N_MOD, n_seq_total, D_MODEL), lambda i: (0, 0, 0))


def _ffn_kernel(x_ref, mod_ref, g_ref, w1_ref, w2_ref, gf_ref, o_ref, h_ref, *, sub, final, spt):
    _norm_mod_store(h_ref, x_ref, g_ref, mod_ref, sub, spt)
    h = h_ref[...]
    acc = None
    for c in range(D_FF // FF_CHUNK):
        c0 = c * FF_CHUNK
        gate = _dot(h, w1_ref[:, c0:c0 + FF_CHUNK])
        up = _dot(h, w1_ref[:, D_FF + c0:D_FF + c0 + FF_CHUNK])
        a = (gate * jax.nn.sigmoid(gate) * up).astype(BF16)
        part = _dot(a, w2_ref[c0:c0 + FF_CHUNK, :])
        acc = part if acc is None else acc + part
    rows = x_ref.shape[0] // spt
    for j in range(spt):
        sl = slice(j * rows, (j + 1) * rows)
        ga = _mod_row(mod_ref, 3 * sub + 2, spt, j)
        xn = x_ref[sl, :] + 0.5 * ga * acc[sl, :]
        if final:
            ms = jnp.mean(xn * xn, axis=-1, keepdims=True)
            xn = xn * lax.rsqrt(ms + EPS) * gf_ref[...]
        o_ref[sl, :] = xn


def _ffn(x, mod, seq_rows, g, w1, w2, gf, *, sub, final):
    n = x.shape[0]
    tm, spt = _tile_rows(n, seq_rows, 512)
    kern = functools.partial(_ffn_kernel, sub=sub, final=final, spt=spt)
    return pl.pallas_call(
        kern,
        grid=(n // tm,),
        in_specs=[
            pl.BlockSpec((tm, D_MODEL), lambda i: (i, 0)),
            _mod_spec(mod.shape[1]),
            _const_spec((1, D_MODEL)),
            _const_spec((D_MODEL, 2 * D_FF)),
            _const_spec((D_FF, D_MODEL)),
            _const_spec((1, D_MODEL)),
        ],
        out_specs=pl.BlockSpec((tm, D_MODEL), lambda i: (i, 0)),
        out_shape=jax.ShapeDtypeStruct((n, D_MODEL), F32),
        scratch_shapes=[pltpu.VMEM((tm, D_MODEL), BF16)],
        compiler_params=_cparams(1),
        name="ffn_final" if final else "ffn",
    )(x, mod, g, w1, w2, gf)


PROJ_GROUPS = (("qkv", CONV_DIM), ("z", GDN_HEADS * GDN_DV), ("qb", SWA_HEADS * SWA_HD),
               ("kv", 2 * SWA_KV_HEADS * SWA_HD), ("gates", 2 * D_MODEL), ("ba", LANES))
PROJ_COLS = sum(w for _, w in PROJ_GROUPS)


PROJ_OFFSETS = {}
_off = 0
for _name, _width in PROJ_GROUPS:
    PROJ_OFFSETS[_name] = (_off, _width)
    _off += _width
PROJ_ROWS = 256


def _proj_kernel(x_ref, mod_ref, g_ref, w_ref, hist_ref, cw_ref, par_ref,
                 qkv_ref, z_ref, qb_ref, kv_ref, gates_ref, gate_ref, tail_ref,
                 h_ref, xp_ref, *, spt, carry):
    pid = pl.program_id(0)
    tm = x_ref.shape[0]
    n_chunks = tm // CHUNK
    seg = CHUNK if carry else HIST_ROWS + CHUNK

    _norm_mod_store(h_ref, x_ref, g_ref, mod_ref, 1, spt)
    h = h_ref[...]

    def group(name, lo=0, width=None):
        c0, full = PROJ_OFFSETS[name]
        width = full if width is None else width
        return _dot(h, w_ref[:, c0 + lo:c0 + lo + width])

    if carry:
        def _load_hist():
            xp_ref[0:HIST_ROWS, :] = hist_ref[0]

        pl.when(pid == 0)(_load_hist)
    else:
        for c in range(n_chunks):
            xp_ref[c * seg:c * seg + HIST_ROWS, :] = hist_ref[c]

    part = GDN_HEADS * GDN_DK

    def project_part(p):
        raw = group("qkv", p * part, part)
        cols = slice(p * part, (p + 1) * part)
        if carry:
            xp_ref[HIST_ROWS:HIST_ROWS + tm, cols] = raw
        else:
            for c in range(n_chunks):
                xp_ref[c * seg + HIST_ROWS:(c + 1) * seg, cols] = raw[c * CHUNK:(c + 1) * CHUNK, :]

    def conv_part(p):
        for c in range(n_chunks):
            for t in range(p * GDN_HEADS, (p + 1) * GDN_HEADS):
                c0 = t * LANES
                win = xp_ref[c * seg:c * seg + HIST_ROWS + CHUNK, c0:c0 + LANES]
                acc = cw_ref[CONV_W - 1:CONV_W, c0:c0 + LANES] * win[HIST_ROWS:, :]
                for s in range(1, CONV_W):
                    tap = pltpu.roll(win, s, 0)[HIST_ROWS:, :]
                    acc = acc + cw_ref[CONV_W - 1 - s:CONV_W - s, c0:c0 + LANES] * tap
                y = acc * jax.nn.sigmoid(acc)
                if p < 2:
                    y = y * lax.rsqrt(jnp.sum(y * y, axis=-1, keepdims=True) + EPS)
                    if p == 0:
                        y = y * (GDN_DK ** -0.5)
                qkv_ref[c * CHUNK:(c + 1) * CHUNK, c0:c0 + LANES] = y

    def gate_math():
        ba = group("ba")
        ri = lax.broadcasted_iota(jnp.int32, (CHUNK, CHUNK), 0)
        ci = lax.broadcasted_iota(jnp.int32, (CHUNK, CHUNK), 1)
        tri = (ri >= ci).astype(BF16)
        lane = lax.broadcasted_iota(jnp.int32, (CHUNK, LANES), 1)
        a_coef = -jnp.exp(par_ref[0:1, :])
        dt_bias = par_ref[1:2, :]
        for c in range(n_chunks):
            bac = ba[c * CHUNK:(c + 1) * CHUNK, :]
            g_all = a_coef * jax.nn.softplus(bac + dt_bias)
            g_hi = g_all.astype(BF16)
            g_r1 = g_all - g_hi.astype(F32)
            g_mid = g_r1.astype(BF16)
            g_lo = (g_r1 - g_mid.astype(F32)).astype(BF16)
            gc = _dot(tri, g_hi) + _dot(tri, g_mid) + _dot(tri, g_lo)
            gate_ref[c * CHUNK:(c + 1) * CHUNK, :] = jnp.where(lane < GDN_HEADS, jax.nn.sigmoid(bac), gc)

    project_part(0)
    project_part(1)
    conv_part(0)
    project_part(2)
    conv_part(1)
    z_ref[...] = group("z")
    gate_math()
    qb_ref[...] = group("qb")
    conv_part(2)
    kv_ref[...] = group("kv")
    gates_ref[...] = group("gates")

    if carry:
        def _store_tail():
            tail_ref[0] = xp_ref[tm:tm + HIST_ROWS, :]

        pl.when(pid == pl.num_programs(0) - 1)(_store_tail)
        xp_ref[0:HIST_ROWS, :] = xp_ref[tm:tm + HIST_ROWS, :]
    else:
        for c in range(n_chunks):
            tail_ref[c] = xp_ref[c * seg + CHUNK:c * seg + CHUNK + HIST_ROWS, :]


def _proj(x, mod, seq_rows, g, w_packed, hist8, conv_w, par):
    n = x.shape[0]
    n_seq = n // seq_rows
    carry = n_seq == 1
    tm, spt = _tile_rows(n, seq_rows, PROJ_ROWS)
    if carry:
        n_hist = 1
        hist_map = lambda i: (0, 0, 0)
        xp_rows = HIST_ROWS + tm
    else:
        assert seq_rows == CHUNK
        n_hist = tm // CHUNK
        hist_map = lambda i: (i, 0, 0)
        xp_rows = n_hist * (HIST_ROWS + CHUNK)
    widths = [CONV_DIM, GDN_HEADS * GDN_DV, SWA_HEADS * SWA_HD, KV_COLS, 2 * D_MODEL, LANES]
    return pl.pallas_call(
        functools.partial(_proj_kernel, spt=spt, carry=carry),
        grid=(n // tm,),
        in_specs=[
            pl.BlockSpec((tm, D_MODEL), lambda i: (i, 0)),
            _mod_spec(mod.shape[1]),
            _const_spec((1, D_MODEL)),
            _const_spec((D_MODEL, PROJ_COLS)),
            pl.BlockSpec((n_hist, HIST_ROWS, CONV_DIM), hist_map),
            _const_spec((CONV_W, CONV_DIM)),
            _const_spec((SUBLANES, LANES)),
        ],
        out_specs=[pl.BlockSpec((tm, w), lambda i: (i, 0)) for w in widths]
        + [pl.BlockSpec((n_hist, HIST_ROWS, CONV_DIM), hist_map)],
        out_shape=[jax.ShapeDtypeStruct((n, w), F32) for w in widths]
        + [jax.ShapeDtypeStruct((n_seq, HIST_ROWS, CONV_DIM), F32)],
        scratch_shapes=[pltpu.VMEM((tm, D_MODEL), BF16), pltpu.VMEM((xp_rows, CONV_DIM), F32)],
        compiler_params=_cparams(1),
        name="in_proj",
    )(x, mod, g, w_packed, hist8, conv_w, par)


def _unit_lower_inverses(ls, eye, level_masks):
    ts = [eye - jnp.where(level_masks[0], l, 0.0) for l in ls]
    for mask in level_masks[1:]:
        lk = [_split(jnp.where(mask, l, 0.0), INV_PASSES) for l in ls]
        tp = [_split(t, INV_PASSES) for t in ts]
        m1 = [_dot_split(a, b, INV_PASSES) for a, b in zip(lk, tp)]
        m2 = [_dot_split(a, _split(b, INV_PASSES), INV_PASSES) for a, b in zip(tp, m1)]
        ts = [t - m for t, m in zip(ts, m2)]
    return ts


def _gdn_kernel(qkv_ref, gate_ref, s0_ref, o_ref, sout_ref, s_ref, *, n_chunks, carry):
    pid = pl.program_id(0)

    if carry:
        def _load_state():
            s_ref[0] = s0_ref[0]

        pl.when(pid == 0)(_load_state)
    else:
        for c in range(n_chunks):
            s_ref[c] = s0_ref[c]

    ri = lax.broadcasted_iota(jnp.int32, (CHUNK, CHUNK), 0)
    ci = lax.broadcasted_iota(jnp.int32, (CHUNK, CHUNK), 1)
    causal = ri >= ci
    strict = ri > ci
    eye = (ri == ci).astype(F32)
    level_masks = []
    for lvl in range(int(math.log2(CHUNK))):
        same_block = (ri >> (lvl + 1)) == (ci >> (lvl + 1))
        level_masks.append(same_block & (((ri >> lvl) & 1) == 1) & (((ci >> lvl) & 1) == 0))

    items = [(c, h) for c in range(n_chunks) for h in range(GDN_HEADS)]
    gate = [gate_ref[c * CHUNK:(c + 1) * CHUNK, :] for c in range(n_chunks)]
    egc_all = [jnp.exp(g) for g in gate]
    g_last_all = [g[CHUNK - 1:CHUNK, :] for g in gate]
    kdf_all = [jnp.exp(gl - g) for gl, g in zip(g_last_all, gate)]
    eg_last_all = [jnp.exp(gl) for gl in g_last_all]

    def tile(c, t):
        return qkv_ref[c * CHUNK:(c + 1) * CHUNK, t * LANES:(t + 1) * LANES]

    def col(arrs, c, h):
        return arrs[c][:, GDN_HEADS + h:GDN_HEADS + h + 1]

    q = [tile(c, h) for c, h in items]
    k = [tile(c, GDN_HEADS + h) for c, h in items]
    v = [tile(c, 2 * GDN_HEADS + h) for c, h in items]
    beta = [gate[c][:, h:h + 1] for c, h in items]
    gcol = [col(gate, c, h) for c, h in items]
    grow = [jnp.sum(g * eye, axis=0, keepdims=True) for g in gcol]
    decay = [jnp.where(causal, jnp.exp(jnp.where(causal, gc - gr, 0.0)), 0.0) for gc, gr in zip(gcol, grow)]
    kb = [x * b for x, b in zip(k, beta)]
    k16 = [x.astype(BF16) for x in k]
    kk = [_dot_nt(a.astype(BF16), b) for a, b in zip(kb, k16)]
    qk = [_dot_nt(a.astype(BF16), b) for a, b in zip(q, k16)]
    ls = [jnp.where(strict, a * d, 0.0) for a, d in zip(kk, decay)]
    intra16 = [jnp.where(causal, a * d, 0.0).astype(BF16) for a, d in zip(qk, decay)]
    ts = _unit_lower_inverses(ls, eye, level_masks)
    ys = [_split(t - eye, SOLVE_PASSES) for t in ts]
    egc = [col(egc_all, c, h) for c, h in items]
    vb = [x * b for x, b in zip(v, beta)]
    kbe = [x * e for x, e in zip(kb, egc)]
    u = [x + _dot_split(y, _split(x, SOLVE_PASSES), SOLVE_PASSES) for x, y in zip(vb, ys)]
    w16 = [(x + _dot_split(y, _split(x, SOLVE_PASSES), SOLVE_PASSES)).astype(BF16) for x, y in zip(kbe, ys)]
    qg16 = [(x * e).astype(BF16) for x, e in zip(q, egc)]
    kd16 = [(x * col(kdf_all, c, h)).astype(BF16) for x, (c, h) in zip(k, items)]
    eg_last = [col(eg_last_all, c, h) for c, h in items]

    o = [None] * len(items)
    if carry:
        state = [s_ref[0, h] for h in range(GDN_HEADS)]
    for c in range(n_chunks):
        idx = [c * GDN_HEADS + h for h in range(GDN_HEADS)]
        if not carry:
            state = [s_ref[c, h] for h in range(GDN_HEADS)]
        s16 = [s.astype(BF16) for s in state]
        v_new = [u[i] - _dot(w16[i], s) for i, s in zip(idx, s16)]
        vn16 = [x.astype(BF16) for x in v_new]
        for i, s, vn in zip(idx, s16, vn16):
            o[i] = _dot(qg16[i], s) + _dot(intra16[i], vn)
        state = [s * eg_last[i] + _dot_tn(kd16[i], vn) for i, s, vn in zip(idx, state, vn16)]
        if not carry:
            for h in range(GDN_HEADS):
                sout_ref[c, h] = state[h]

    for (c, h), oo in zip(items, o):
        o_ref[c * CHUNK:(c + 1) * CHUNK, h * GDN_DV:(h + 1) * GDN_DV] = oo

    if carry:
        for h in range(GDN_HEADS):
            s_ref[0, h] = state[h]

        def _store_state():
            sout_ref[0] = s_ref[0]

        pl.when(pid == pl.num_programs(0) - 1)(_store_state)


def _gdn(qkv, gate, s0, seq_rows):
    n = qkv.shape[0]
    n_seq = n // seq_rows
    carry = n_seq == 1
    n_chunks = GDN_CHUNKS_PER_STEP
    rows = n_chunks * CHUNK
    assert n % rows == 0
    if carry:
        n_state = 1
        s_map = lambda i: (0, 0, 0, 0)
    else:
        assert seq_rows == CHUNK
        n_state = n_chunks
        s_map = lambda i: (i, 0, 0, 0)
    kern = functools.partial(_gdn_kernel, n_chunks=n_chunks, carry=carry)
    return pl.pallas_call(
        kern,
        grid=(n // rows,),
        in_specs=[
            pl.BlockSpec((rows, CONV_DIM), lambda i: (i, 0)),
            pl.BlockSpec((rows, LANES), lambda i: (i, 0)),
            pl.BlockSpec((n_state, GDN_HEADS, GDN_DK, GDN_DV), s_map),
        ],
        out_specs=[
            pl.BlockSpec((rows, GDN_HEADS * GDN_DV), lambda i: (i, 0)),
            pl.BlockSpec((n_state, GDN_HEADS, GDN_DK, GDN_DV), s_map),
        ],
        out_shape=[
            jax.ShapeDtypeStruct((n, GDN_HEADS * GDN_DV), F32),
            jax.ShapeDtypeStruct((n_seq, GDN_HEADS, GDN_DK, GDN_DV), F32),
        ],
        scratch_shapes=[pltpu.VMEM((n_state, GDN_HEADS, GDN_DK, GDN_DV), F32)],
        compiler_params=_cparams(1),
        name="gdn",
    )(qkv, gate, s0)


SWA_KEYS = WINDOW + CHUNK
KV_COLS = 2 * SWA_KV_HEADS * SWA_HD


def _bias_kernel(bucket_ref, rb_ref, o_ref):
    bucket = bucket_ref[...]
    h = pl.program_id(0)
    acc = jnp.zeros((CHUNK, SWA_KEYS), F32)
    for b in range(NUM_BUCKETS):
        acc = jnp.where(bucket == b, rb_ref[b * SWA_HEADS + h], acc)
    o_ref[0] = acc


def _bias_table(bucket, rel_bias_flat):
    return pl.pallas_call(
        _bias_kernel,
        grid=(SWA_HEADS,),
        in_specs=[
            pl.BlockSpec((CHUNK, SWA_KEYS), lambda h: (0, 0)),
            pl.BlockSpec(memory_space=pltpu.SMEM),
        ],
        out_specs=pl.BlockSpec((1, CHUNK, SWA_KEYS), lambda h: (h, 0, 0)),
        out_shape=jax.ShapeDtypeStruct((SWA_HEADS, CHUNK, SWA_KEYS), F32),
        compiler_params=_cparams(1),
        name="swa_bias",
    )(bucket, rel_bias_flat)


def _swa_kernel(q_ref, kvc_ref, kvp_ref, bias_ref, sink_ref, o_ref, kf_ref, *, n_chunks, one_sequence):
    if one_sequence:
        kf_ref[0:WINDOW, :] = kvp_ref[...]
        kf_ref[WINDOW:WINDOW + n_chunks * CHUNK, :] = kvc_ref[...]
        key_start = [c * CHUNK for c in range(n_chunks)]
        key_pos = lax.broadcasted_iota(jnp.int32, (CHUNK, SWA_KEYS), 1)
        tile_start = pl.program_id(0) * (n_chunks * CHUNK)
        valid = [key_pos + (tile_start + c * CHUNK - WINDOW) >= 0 for c in range(n_chunks)]
    else:
        for c in range(n_chunks):
            kf_ref[c * SWA_KEYS:c * SWA_KEYS + WINDOW, :] = kvp_ref[c * WINDOW:(c + 1) * WINDOW, :]
            kf_ref[c * SWA_KEYS + WINDOW:(c + 1) * SWA_KEYS, :] = kvc_ref[c * CHUNK:(c + 1) * CHUNK, :]
        key_start = [c * SWA_KEYS for c in range(n_chunks)]

    pairs = [(c, kh) for c in range(n_chunks) for kh in range(SWA_KV_HEADS)]
    keys = {(c, kh): kf_ref[key_start[c]:key_start[c] + SWA_KEYS, kh * SWA_HD:(kh + 1) * SWA_HD].astype(BF16)
            for c, kh in pairs}
    vals = {(c, kh): kf_ref[key_start[c]:key_start[c] + SWA_KEYS,
                            (SWA_KV_HEADS + kh) * SWA_HD:(SWA_KV_HEADS + kh + 1) * SWA_HD].astype(BF16)
            for c, kh in pairs}
    items = [(c, hd) for c in range(n_chunks) for hd in range(SWA_HEADS)]
    q = [q_ref[c * CHUNK:(c + 1) * CHUNK, hd * SWA_HD:(hd + 1) * SWA_HD].astype(BF16) for c, hd in items]
    logits = [_dot_nt(x, keys[(c, hd // SWA_GROUP)]) * (SWA_HD ** -0.5) + bias_ref[hd]
              for x, (c, hd) in zip(q, items)]
    if one_sequence:
        logits = [jnp.where(valid[c], x, -jnp.inf) for x, (c, _) in zip(logits, items)]
    sink = [sink_ref[hd] for _, hd in items]
    m = [jnp.maximum(jnp.max(x, axis=-1, keepdims=True), s) for x, s in zip(logits, sink)]
    p = [jnp.exp(x - mm) for x, mm in zip(logits, m)]
    denom = [jnp.sum(x, axis=-1, keepdims=True) + jnp.exp(s - mm) for x, s, mm in zip(p, sink, m)]
    probs = [(x / d).astype(BF16) for x, d in zip(p, denom)]
    out = [_dot(x, vals[(c, hd // SWA_GROUP)]) for x, (c, hd) in zip(probs, items)]
    for x, (c, hd) in zip(out, items):
        o_ref[c * CHUNK:(c + 1) * CHUNK, hd * SWA_HD:(hd + 1) * SWA_HD] = x


def _swa(qb, kv, kv_hist, bias, sinks, seq_rows):
    n = qb.shape[0]
    n_chunks = SWA_CHUNKS_PER_STEP
    rows = n_chunks * CHUNK
    assert n % rows == 0
    one_sequence = kv_hist is None
    if one_sequence:
        assert rows % WINDOW == 0
        per = rows // WINDOW
        prev_arr = kv
        prev_spec = pl.BlockSpec((WINDOW, KV_COLS), lambda i: (jnp.maximum(i * per - 1, 0), 0))
        kf_rows = WINDOW + rows
    else:
        assert seq_rows == CHUNK
        prev_arr = kv_hist
        prev_spec = pl.BlockSpec((n_chunks * WINDOW, KV_COLS), lambda i: (i, 0))
        kf_rows = n_chunks * SWA_KEYS
    kern = functools.partial(_swa_kernel, n_chunks=n_chunks, one_sequence=one_sequence)
    return pl.pallas_call(
        kern,
        grid=(n // rows,),
        in_specs=[
            pl.BlockSpec((rows, SWA_HEADS * SWA_HD), lambda i: (i, 0)),
            pl.BlockSpec((rows, KV_COLS), lambda i: (i, 0)),
            prev_spec,
            pl.BlockSpec((SWA_HEADS, CHUNK, SWA_KEYS), lambda i: (0, 0, 0)),
            pl.BlockSpec(memory_space=pltpu.SMEM),
        ],
        out_specs=pl.BlockSpec((rows, SWA_HEADS * SWA_HD), lambda i: (i, 0)),
        out_shape=jax.ShapeDtypeStruct((n, SWA_HEADS * SWA_HD), F32),
        scratch_shapes=[pltpu.VMEM((kf_rows, KV_COLS), F32)],
        compiler_params=_cparams(1),
        name="swa",
    )(qb, kv, prev_arr, bias, sinks)


def _merge_kernel(x_ref, oa_ref, z_ref, ob_ref, gates_ref, mod_ref, nw_ref, wa_ref, wb_ref, wo_ref, o_ref, oa16_ref,
                  *, spt):
    for h in range(GDN_HEADS):
        cols = slice(h * GDN_DV, (h + 1) * GDN_DV)
        oo = oa_ref[:, cols]
        zz = z_ref[:, cols]
        on = oo * lax.rsqrt(jnp.mean(oo * oo, axis=-1, keepdims=True) + EPS) * nw_ref[...] * (zz * jax.nn.sigmoid(zz))
        oa16_ref[:, cols] = on.astype(BF16)
    ya = _dot(oa16_ref[...], wa_ref[...])
    yb = _dot(ob_ref[...].astype(BF16), wb_ref[...])
    merged = (jax.nn.sigmoid(gates_ref[:, 0:D_MODEL]) * ya
              + jax.nn.sigmoid(gates_ref[:, D_MODEL:2 * D_MODEL]) * yb)
    y = _dot(merged.astype(BF16), wo_ref[...])
    rows = x_ref.shape[0] // spt
    for j in range(spt):
        sl = slice(j * rows, (j + 1) * rows)
        o_ref[sl, :] = x_ref[sl, :] + _mod_row(mod_ref, 5, spt, j) * y[sl, :]


def _merge(x, oa, z, ob, gates, mod, seq_rows, norm_w, wa, wb, wo):
    n = x.shape[0]
    tm, spt = _tile_rows(n, seq_rows, 512)
    row_spec = lambda w: pl.BlockSpec((tm, w), lambda i: (i, 0))
    return pl.pallas_call(
        functools.partial(_merge_kernel, spt=spt),
        grid=(n // tm,),
        in_specs=[
            row_spec(D_MODEL), row_spec(D_MODEL), row_spec(D_MODEL), row_spec(D_MODEL), row_spec(2 * D_MODEL),
            _mod_spec(mod.shape[1]),
            _const_spec((1, GDN_DV)),
            _const_spec((D_MODEL, D_MODEL)), _const_spec((D_MODEL, D_MODEL)), _const_spec((D_MODEL, D_MODEL)),
        ],
        out_specs=row_spec(D_MODEL),
        out_shape=jax.ShapeDtypeStruct((n, D_MODEL), F32),
        scratch_shapes=[pltpu.VMEM((tm, GDN_HEADS * GDN_DV), BF16)],
        compiler_params=_cparams(1),
        name="merge",
    )(x, oa, z, ob, gates, mod, norm_w, wa, wb, wo)


def _t5_bucket(rel):
    half = NUM_BUCKETS // 2
    max_exact = half // 2
    n = jnp.abs(rel)
    large = max_exact + (jnp.log(jnp.maximum(n, 1).astype(jnp.float32) / max_exact)
                         / math.log(MAX_DISTANCE / max_exact) * (half - max_exact)).astype(jnp.int32)
    large = jnp.minimum(large, half - 1)
    return jnp.where(rel > 0, half, 0) + jnp.where(n < max_exact, n, large)


def _trunk(x, mod, conv_hist, s_hist, k_hist, v_hist, w):
    bsz, seq, _ = x.shape
    n = bsz * seq
    x2 = x.reshape(n, D_MODEL)
    x1 = _ffn(x2, mod, seq, w["norm_ffn1"], w["ffn1_in"], w["ffn1_out"], w["norm_final"], sub=0, final=False)
    if conv_hist is None:
        hist8 = jnp.zeros((bsz, HIST_ROWS, CONV_DIM), F32)
        s0 = jnp.zeros((bsz, GDN_HEADS, GDN_DK, GDN_DV), F32)
        kv_hist = None
    else:
        hist8 = jnp.concatenate([jnp.zeros((bsz, HIST_ROWS - (CONV_W - 1), CONV_DIM), F32), conv_hist], axis=1)
        s0 = s_hist
        kv_hist = jnp.concatenate([k_hist.reshape(bsz, WINDOW, SWA_KV_HEADS * SWA_HD),
                                   v_hist.reshape(bsz, WINDOW, SWA_KV_HEADS * SWA_HD)], axis=-1)
        kv_hist = kv_hist.reshape(bsz * WINDOW, KV_COLS)
    qkv, z, qb, kv, gates, gate, tail = _proj(x1, mod, seq, w["norm_mix"], w["w_in"], hist8, w["conv_w"], w["gdn_par"])
    oa, s_new = _gdn(qkv, gate, s0, seq)
    ob = _swa(qb, kv, kv_hist, w["bias"], w["sinks"], seq)
    x3 = _merge(x1, oa, z, ob, gates, mod, seq, w["gdn_norm_w"], w["w_a"], w["w_b"], w["w_out"])
    y = _ffn(x3, mod, seq, w["norm_ffn2"], w["ffn2_in"], w["ffn2_out"], w["norm_final"], sub=2, final=True)

    conv_new = tail[:, HIST_ROWS - (CONV_W - 1):]
    half = SWA_KV_HEADS * SWA_HD
    kv3 = kv.reshape(bsz, seq, KV_COLS)
    if kv_hist is None:
        k_new = kv3[:, seq - WINDOW:, :half]
        v_new = kv3[:, seq - WINDOW:, half:]
    else:
        keep = WINDOW - seq
        k_new = jnp.concatenate([k_hist.reshape(bsz, WINDOW, half)[:, WINDOW - keep:], kv3[:, :, :half]], axis=1)
        v_new = jnp.concatenate([v_hist.reshape(bsz, WINDOW, half)[:, WINDOW - keep:], kv3[:, :, half:]], axis=1)
    k_new = k_new.reshape(bsz, WINDOW, SWA_KV_HEADS, SWA_HD)
    v_new = v_new.reshape(bsz, WINDOW, SWA_KV_HEADS, SWA_HD)
    return (y.reshape(bsz, seq, D_MODEL), conv_new[None], s_new[None], k_new[None], v_new[None])


def kernel(x_prompt, x_sample, state_gdn_conv, state_gdn_s, cache_swa_k, cache_swa_v, c_prompt, c_sample,
           norm_ffn1, w_ffn1_in, w_ffn1_out, norm_mix, w_in, gdn_conv_w, gdn_a_log, gdn_dt_bias, gdn_norm_w,
           swa_sinks, rel_bias, w_branch_a, w_branch_b, w_out, norm_ffn2, w_ffn2_in, w_ffn2_out,
           w_ada, b_ada, norm_final):
    bp = x_prompt.shape[0]
    bs = x_sample.shape[0]
    assert bp == 1 and x_sample.shape[1] == CHUNK and cache_swa_k.shape[2] == WINDOW

    n_c = bp + bs
    pad = -n_c % SUBLANES
    c_all = jnp.concatenate([c_prompt, c_sample, jnp.zeros((pad, D_MODEL), F32)], axis=0)
    mod_all = _modulation(c_all, w_ada[0], b_ada[0][None, :])
    mod_all = mod_all.reshape(n_c + pad, N_MOD, D_MODEL).transpose(1, 0, 2)
    mod_p = mod_all[:, :bp]
    mod_s = mod_all[:, bp:n_c]

    wi = w_in[0]
    o_z = CONV_DIM
    o_b = o_z + GDN_HEADS * GDN_DV
    o_qb = o_b + 2 * GDN_HEADS
    o_kv = o_qb + SWA_HEADS * SWA_HD
    o_g = o_kv + KV_COLS
    w_packed = jnp.concatenate([wi[:, :o_b], wi[:, o_qb:], wi[:, o_b:o_qb],
                                jnp.zeros((D_MODEL, LANES - 2 * GDN_HEADS), F32)], axis=1).astype(BF16)
    assert w_packed.shape[1] == PROJ_COLS and o_g + 2 * D_MODEL == wi.shape[1]
    par = jnp.zeros((SUBLANES, LANES), F32)
    par = par.at[0, GDN_HEADS:2 * GDN_HEADS].set(gdn_a_log[0]).at[1, GDN_HEADS:2 * GDN_HEADS].set(gdn_dt_bias[0])

    rel = jnp.arange(SWA_KEYS)[None, :] - WINDOW - jnp.arange(CHUNK)[:, None]
    bias = _bias_table(_t5_bucket(rel).astype(jnp.int32), rel_bias.reshape(-1))

    w = {
        "norm_ffn1": norm_ffn1, "ffn1_in": w_ffn1_in[0].astype(BF16), "ffn1_out": w_ffn1_out[0].astype(BF16),
        "norm_mix": norm_mix, "w_in": w_packed, "conv_w": gdn_conv_w[0], "gdn_par": par,
        "gdn_norm_w": gdn_norm_w, "bias": bias, "sinks": swa_sinks[0],
        "w_a": w_branch_a[0].astype(BF16), "w_b": w_branch_b[0].astype(BF16), "w_out": w_out[0].astype(BF16),
        "norm_ffn2": norm_ffn2, "ffn2_in": w_ffn2_in[0].astype(BF16), "ffn2_out": w_ffn2_out[0].astype(BF16),
        "norm_final": norm_final[None, :],
    }
    y_p, p_conv, p_s, p_k, p_v = _trunk(x_prompt, mod_p, None, None, None, None, w)
    y_s, s_conv, s_s, s_k, s_v = _trunk(x_sample, mod_s, state_gdn_conv[0], state_gdn_s[0],
                                        cache_swa_k[0], cache_swa_v[0], w)
    return (y_p, y_s, p_conv, p_s, p_k, p_v, s_conv, s_s, s_k, s_v)
```

```python
import functools
import math

import jax
import jax.numpy as jnp
from jax import lax
from jax.experimental import pallas as pl
from jax.experimental.pallas import tpu as pltpu

F32 = jnp.float32
BF16 = jnp.bfloat16

D_MODEL = 1024
CHUNK = 64
GDN_HEADS = 8
GDN_DK = 128
GDN_DV = 128
CONV_W = 4
CONV_DIM = GDN_HEADS * (2 * GDN_DK + GDN_DV)
SWA_HEADS = 16
SWA_KV_HEADS = 2
SWA_GROUP = SWA_HEADS // SWA_KV_HEADS
SWA_HD = 64
WINDOW = 128
NUM_BUCKETS = 32
MAX_DISTANCE = 128
D_FF = 2816
N_MOD = 9
EPS = 1e-6

LANES = 128
SUBLANES = 8
VMEM_LIMIT = 56 * 1024 * 1024

FF_CHUNK = D_FF // 2
HIST_ROWS = SUBLANES
INV_PASSES = 1
SOLVE_PASSES = 1
GDN_CHUNKS_PER_STEP = 4
SWA_CHUNKS_PER_STEP = 4


def _dot(a, b):
    return jnp.dot(a, b, preferred_element_type=F32)


def _dot_nt(a, b):
    return lax.dot_general(a, b, (((1,), (1,)), ((), ())), preferred_element_type=F32)


def _dot_tn(a, b):
    return lax.dot_general(a, b, (((0,), (0,)), ((), ())), preferred_element_type=F32)


def _split(a, passes):
    hi = a.astype(BF16)
    if passes == 1:
        return hi, None
    return hi, (a - hi.astype(F32)).astype(BF16)


def _dot_split(a_parts, b_parts, passes):
    ah, al = a_parts
    bh, bl = b_parts
    out = _dot(ah, bh)
    if passes == 3:
        out = out + _dot(al, bh) + _dot(ah, bl)
    return out


def _cparams(n_grid):
    return pltpu.CompilerParams(dimension_semantics=("arbitrary",) * n_grid, vmem_limit_bytes=VMEM_LIMIT)


def _const_spec(shape):
    nd = len(shape)
    return pl.BlockSpec(shape, lambda i: (0,) * nd, pipeline_mode=pl.Buffered(1))


def _mod_kernel(c_ref, w_ref, b_ref, o_ref):
    c = c_ref[...]
    a = (c * jax.nn.sigmoid(c)).astype(BF16)
    o_ref[0] = _dot(a, w_ref[...].astype(BF16)) + b_ref[...]


def _modulation(c_pad, w_ada, b_ada):
    rows = c_pad.shape[0]
    return pl.pallas_call(
        _mod_kernel,
        grid=(N_MOD,),
        in_specs=[
            pl.BlockSpec((rows, D_MODEL), lambda j: (0, 0)),
            pl.BlockSpec((D_MODEL, D_MODEL), lambda j: (0, j)),
            pl.BlockSpec((1, D_MODEL), lambda j: (0, j)),
        ],
        out_specs=pl.BlockSpec((1, rows, D_MODEL), lambda j: (j, 0, 0)),
        out_shape=jax.ShapeDtypeStruct((N_MOD, rows, D_MODEL), F32),
        compiler_params=_cparams(1),
        name="adaln_mod",
    )(c_pad, w_ada, b_ada)


def _mod_row(mod_ref, idx, spt, j):
    if mod_ref.shape[1] == 1:
        return mod_ref[idx, 0:1, :]
    return mod_ref[idx, pl.ds(pl.program_id(0) * spt + j, 1), :]


def _norm_mod_store(h_ref, x_ref, g_ref, mod_ref, sub, spt):
    rows = x_ref.shape[0] // spt
    g = g_ref[...]
    for j in range(spt):
        xs = x_ref[j * rows:(j + 1) * rows, :]
        ms = jnp.mean(xs * xs, axis=-1, keepdims=True)
        y = xs * lax.rsqrt(ms + EPS) * g
        sh = _mod_row(mod_ref, 3 * sub, spt, j)
        sc = _mod_row(mod_ref, 3 * sub + 1, spt, j)
        h_ref[j * rows:(j + 1) * rows, :] = (y * (1.0 + sc) + sh).astype(h_ref.dtype)


def _tile_rows(n_rows, seq_rows, target):
    if seq_rows >= target:
        assert seq_rows % target == 0
        return target, 1
    assert target % seq_rows == 0
    tm = min(target, n_rows)
    assert n_rows % tm == 0
    return tm, tm // seq_rows


def _mod_spec(n_seq_total):
    return pl.BlockSpec((N_MOD, n_seq_total, D_MODEL), lambda i: (0, 0, 0))


def _ffn_kernel(x_ref, mod_ref, g_ref, w1_ref, w2_ref, gf_ref, o_ref, h_ref, *, sub, final, spt):
    _norm_mod_store(h_ref, x_ref, g_ref, mod_ref, sub, spt)
    h = h_ref[...]
    acc = None
    for c in range(D_FF // FF_CHUNK):
        c0 = c * FF_CHUNK
        gate = _dot(h, w1_ref[:, c0:c0 + FF_CHUNK])
        up = _dot(h, w1_ref[:, D_FF + c0:D_FF + c0 + FF_CHUNK])
        a = (gate * jax.nn.sigmoid(gate) * up).astype(BF16)
        part = _dot(a, w2_ref[c0:c0 + FF_CHUNK, :])
        acc = part if acc is None else acc + part
    rows = x_ref.shape[0] // spt
    for j in range(spt):
        sl = slice(j * rows, (j + 1) * rows)
        ga = _mod_row(mod_ref, 3 * sub + 2, spt, j)
        xn = x_ref[sl, :] + 0.5 * ga * acc[sl, :]
        if final:
            ms = jnp.mean(xn * xn, axis=-1, keepdims=True)
            xn = xn * lax.rsqrt(ms + EPS) * gf_ref[...]
        o_ref[sl, :] = xn


def _ffn(x, mod, seq_rows, g, w1, w2, gf, *, sub, final):
    n = x.shape[0]
    tm, spt = _tile_rows(n, seq_rows, 512)
    kern = functools.partial(_ffn_kernel, sub=sub, final=final, spt=spt)
    return pl.pallas_call(
        kern,
        grid=(n // tm,),
        in_specs=[
            pl.BlockSpec((tm, D_MODEL), lambda i: (i, 0)),
            _mod_spec(mod.shape[1]),
            _const_spec((1, D_MODEL)),
            _const_spec((D_MODEL, 2 * D_FF)),
            _const_spec((D_FF, D_MODEL)),
            _const_spec((1, D_MODEL)),
        ],
        out_specs=pl.BlockSpec((tm, D_MODEL), lambda i: (i, 0)),
        out_shape=jax.ShapeDtypeStruct((n, D_MODEL), F32),
        scratch_shapes=[pltpu.VMEM((tm, D_MODEL), BF16)],
        compiler_params=_cparams(1),
        name="ffn_final" if final else "ffn",
    )(x, mod, g, w1, w2, gf)


PROJ_GROUPS = (("qkv", CONV_DIM), ("z", GDN_HEADS * GDN_DV), ("qb", SWA_HEADS * SWA_HD),
               ("kv", 2 * SWA_KV_HEADS * SWA_HD), ("gates", 2 * D_MODEL), ("ba", LANES))
PROJ_COLS = sum(w for _, w in PROJ_GROUPS)


assert math.log2(SWA_HD) % 2 == 0
W_IN_ALIGNED = CONV_DIM + GDN_HEADS * GDN_DV
W_IN_SMALL = 2 * GDN_HEADS
W_IN_REST = PROJ_COLS - W_IN_ALIGNED - LANES


def _pack_w_in_kernel(lo_ref, hi_ref, o_ref):
    o_ref[:, 0:W_IN_ALIGNED] = lo_ref[...].astype(BF16)
    n_qb = SWA_HEADS * SWA_HD
    o_ref[:, W_IN_ALIGNED:W_IN_ALIGNED + n_qb] = (hi_ref[:, W_IN_SMALL:W_IN_SMALL + n_qb] * (SWA_HD ** -0.5)).astype(BF16)
    o_ref[:, W_IN_ALIGNED + n_qb:W_IN_ALIGNED + W_IN_REST] = (
        hi_ref[:, W_IN_SMALL + n_qb:W_IN_SMALL + W_IN_REST].astype(BF16))
    lane = lax.broadcasted_iota(jnp.int32, (lo_ref.shape[0], LANES), 1)
    small = jnp.where(lane < W_IN_SMALL, hi_ref[:, 0:LANES], 0.0)
    o_ref[:, W_IN_ALIGNED + W_IN_REST:PROJ_COLS] = small.astype(BF16)


def _pack_w_in(w_in):
    assert w_in.shape == (D_MODEL, W_IN_ALIGNED + W_IN_SMALL + W_IN_REST)
    rows = 256
    return pl.pallas_call(
        _pack_w_in_kernel,
        grid=(D_MODEL // rows,),
        in_specs=[
            pl.BlockSpec((rows, W_IN_ALIGNED), lambda i: (i, 0)),
            pl.BlockSpec((rows, W_IN_ALIGNED), lambda i: (i, 1)),
        ],
        out_specs=pl.BlockSpec((rows, PROJ_COLS), lambda i: (i, 0)),
        out_shape=jax.ShapeDtypeStruct((D_MODEL, PROJ_COLS), BF16),
        compiler_params=_cparams(1),
        name="pack_w_in",
    )(w_in, w_in)


PROJ_OFFSETS = {}
_off = 0
for _name, _width in PROJ_GROUPS:
    PROJ_OFFSETS[_name] = (_off, _width)
    _off += _width
PROJ_ROWS = 256


def _proj_kernel(x_ref, mod_ref, g_ref, w_ref, hist_ref, cw_ref, par_ref,
                 qkv_ref, z_ref, qb_ref, kv_ref, gates_ref, gate_ref, tail_ref,
                 h_ref, xp_ref, *, spt, carry):
    pid = pl.program_id(0)
    tm = x_ref.shape[0]
    n_chunks = tm // CHUNK
    seg = CHUNK if carry else HIST_ROWS + CHUNK

    _norm_mod_store(h_ref, x_ref, g_ref, mod_ref, 1, spt)
    h = h_ref[...]

    def group(name, lo=0, width=None):
        c0, full = PROJ_OFFSETS[name]
        width = full if width is None else width
        return _dot(h, w_ref[:, c0 + lo:c0 + lo + width])

    if carry:
        def _load_hist():
            xp_ref[0:HIST_ROWS, :] = hist_ref[0]

        pl.when(pid == 0)(_load_hist)
    else:
        for c in range(n_chunks):
            xp_ref[c * seg:c * seg + HIST_ROWS, :] = hist_ref[c]

    part = GDN_HEADS * GDN_DK

    def project_part(p):
        raw = group("qkv", p * part, part)
        cols = slice(p * part, (p + 1) * part)
        if carry:
            xp_ref[HIST_ROWS:HIST_ROWS + tm, cols] = raw
        else:
            for c in range(n_chunks):
                xp_ref[c * seg + HIST_ROWS:(c + 1) * seg, cols] = raw[c * CHUNK:(c + 1) * CHUNK, :]

    def conv_part(p):
        for c in range(n_chunks):
            for t in range(p * GDN_HEADS, (p + 1) * GDN_HEADS):
                c0 = t * LANES
                win = xp_ref[c * seg:c * seg + HIST_ROWS + CHUNK, c0:c0 + LANES]
                acc = cw_ref[CONV_W - 1:CONV_W, c0:c0 + LANES] * win[HIST_ROWS:, :]
                for s in range(1, CONV_W):
                    tap = pltpu.roll(win, s, 0)[HIST_ROWS:, :]
                    acc = acc + cw_ref[CONV_W - 1 - s:CONV_W - s, c0:c0 + LANES] * tap
                y = acc * jax.nn.sigmoid(acc)
                if p < 2:
                    y = y * lax.rsqrt(jnp.sum(y * y, axis=-1, keepdims=True) + EPS)
                    if p == 0:
                        y = y * (GDN_DK ** -0.5)
                qkv_ref[c * CHUNK:(c + 1) * CHUNK, c0:c0 + LANES] = y

    def gate_math():
        ba = group("ba")
        ri = lax.broadcasted_iota(jnp.int32, (CHUNK, CHUNK), 0)
        ci = lax.broadcasted_iota(jnp.int32, (CHUNK, CHUNK), 1)
        tri = (ri >= ci).astype(BF16)
        lane = lax.broadcasted_iota(jnp.int32, (CHUNK, LANES), 1)
        a_coef = -jnp.exp(par_ref[0:1, :])
        dt_bias = par_ref[1:2, :]
        for c in range(n_chunks):
            bac = ba[c * CHUNK:(c + 1) * CHUNK, :]
            g_all = a_coef * jax.nn.softplus(bac + dt_bias)
            g_hi = g_all.astype(BF16)
            g_r1 = g_all - g_hi.astype(F32)
            g_mid = g_r1.astype(BF16)
            g_lo = (g_r1 - g_mid.astype(F32)).astype(BF16)
            gc = _dot(tri, g_hi) + _dot(tri, g_mid) + _dot(tri, g_lo)
            gate_ref[c * CHUNK:(c + 1) * CHUNK, :] = jnp.where(lane < GDN_HEADS, jax.nn.sigmoid(bac), gc)

    project_part(0)
    project_part(1)
    conv_part(0)
    project_part(2)
    conv_part(1)
    z_ref[...] = group("z")
    gate_math()
    qb_ref[...] = group("qb")
    conv_part(2)
    kv_ref[...] = group("kv")
    gates_ref[...] = group("gates")

    if carry:
        def _store_tail():
            tail_ref[0] = xp_ref[tm:tm + HIST_ROWS, :]

        pl.when(pid == pl.num_programs(0) - 1)(_store_tail)
        xp_ref[0:HIST_ROWS, :] = xp_ref[tm:tm + HIST_ROWS, :]
    else:
        for c in range(n_chunks):
            tail_ref[c] = xp_ref[c * seg + CHUNK:c * seg + CHUNK + HIST_ROWS, :]


def _proj(x, mod, seq_rows, g, w_packed, hist8, conv_w, par):
    n = x.shape[0]
    n_seq = n // seq_rows
    carry = n_seq == 1
    tm, spt = _tile_rows(n, seq_rows, PROJ_ROWS)
    if carry:
        n_hist = 1
        hist_map = lambda i: (0, 0, 0)
        xp_rows = HIST_ROWS + tm
    else:
        assert seq_rows == CHUNK
        n_hist = tm // CHUNK
        hist_map = lambda i: (i, 0, 0)
        xp_rows = n_hist * (HIST_ROWS + CHUNK)
    widths = [CONV_DIM, GDN_HEADS * GDN_DV, SWA_HEADS * SWA_HD, KV_COLS, 2 * D_MODEL, LANES]
    return pl.pallas_call(
        functools.partial(_proj_kernel, spt=spt, carry=carry),
        grid=(n // tm,),
        in_specs=[
            pl.BlockSpec((tm, D_MODEL), lambda i: (i, 0)),
            _mod_spec(mod.shape[1]),
            _const_spec((1, D_MODEL)),
            _const_spec((D_MODEL, PROJ_COLS)),
            pl.BlockSpec((n_hist, HIST_ROWS, CONV_DIM), hist_map),
            _const_spec((CONV_W, CONV_DIM)),
            _const_spec((SUBLANES, LANES)),
        ],
        out_specs=[pl.BlockSpec((tm, w), lambda i: (i, 0)) for w in widths]
        + [pl.BlockSpec((n_hist, HIST_ROWS, CONV_DIM), hist_map)],
        out_shape=[jax.ShapeDtypeStruct((n, w), F32) for w in widths]
        + [jax.ShapeDtypeStruct((n_seq, HIST_ROWS, CONV_DIM), F32)],
        scratch_shapes=[pltpu.VMEM((tm, D_MODEL), BF16), pltpu.VMEM((xp_rows, CONV_DIM), F32)],
        compiler_params=_cparams(1),
        name="in_proj",
    )(x, mod, g, w_packed, hist8, conv_w, par)


def _unit_lower_inverses(ls, eye, level_masks):
    ts = [eye - jnp.where(level_masks[0], l, 0.0) for l in ls]
    for mask in level_masks[1:]:
        lk = [_split(jnp.where(mask, l, 0.0), INV_PASSES) for l in ls]
        tp = [_split(t, INV_PASSES) for t in ts]
        m1 = [_dot_split(a, b, INV_PASSES) for a, b in zip(lk, tp)]
        m2 = [_dot_split(a, _split(b, INV_PASSES), INV_PASSES) for a, b in zip(tp, m1)]
        ts = [t - m for t, m in zip(ts, m2)]
    return ts


def _gdn_kernel(qkv_ref, gate_ref, s0_ref, o_ref, sout_ref, s_ref, *, n_chunks, carry):
    pid = pl.program_id(0)

    if carry:
        def _load_state():
            s_ref[0] = s0_ref[0]

        pl.when(pid == 0)(_load_state)
    else:
        for c in range(n_chunks):
            s_ref[c] = s0_ref[c]

    ri = lax.broadcasted_iota(jnp.int32, (CHUNK, CHUNK), 0)
    ci = lax.broadcasted_iota(jnp.int32, (CHUNK, CHUNK), 1)
    causal = ri >= ci
    strict = ri > ci
    eye = (ri == ci).astype(F32)
    level_masks = []
    for lvl in range(int(math.log2(CHUNK))):
        same_block = (ri >> (lvl + 1)) == (ci >> (lvl + 1))
        level_masks.append(same_block & (((ri >> lvl) & 1) == 1) & (((ci >> lvl) & 1) == 0))

    items = [(c, h) for c in range(n_chunks) for h in range(GDN_HEADS)]
    gate = [gate_ref[c * CHUNK:(c + 1) * CHUNK, :] for c in range(n_chunks)]
    egc_all = [jnp.exp(g) for g in gate]
    g_last_all = [g[CHUNK - 1:CHUNK, :] for g in gate]
    kdf_all = [jnp.exp(gl - g) for gl, g in zip(g_last_all, gate)]
    eg_last_all = [jnp.exp(gl) for gl in g_last_all]

    def tile(c, t):
        return qkv_ref[c * CHUNK:(c + 1) * CHUNK, t * LANES:(t + 1) * LANES]

    def col(arrs, c, h):
        return arrs[c][:, GDN_HEADS + h:GDN_HEADS + h + 1]

    q = [tile(c, h) for c, h in items]
    k = [tile(c, GDN_HEADS + h) for c, h in items]
    v = [tile(c, 2 * GDN_HEADS + h) for c, h in items]
    beta = [gate[c][:, h:h + 1] for c, h in items]
    gcol = [col(gate, c, h) for c, h in items]
    grow = [jnp.sum(g * eye, axis=0, keepdims=True) for g in gcol]
    decay = [jnp.where(causal, jnp.exp(jnp.where(causal, gc - gr, 0.0)), 0.0) for gc, gr in zip(gcol, grow)]
    kb = [x * b for x, b in zip(k, beta)]
    k16 = [x.astype(BF16) for x in k]
    kk = [_dot_nt(a.astype(BF16), b) for a, b in zip(kb, k16)]
    qk = [_dot_nt(a.astype(BF16), b) for a, b in zip(q, k16)]
    ls = [jnp.where(strict, a * d, 0.0) for a, d in zip(kk, decay)]
    intra16 = [jnp.where(causal, a * d, 0.0).astype(BF16) for a, d in zip(qk, decay)]
    ts = _unit_lower_inverses(ls, eye, level_masks)
    ys = [_split(t - eye, SOLVE_PASSES) for t in ts]
    egc = [col(egc_all, c, h) for c, h in items]
    vb = [x * b for x, b in zip(v, beta)]
    kbe = [x * e for x, e in zip(kb, egc)]
    u = [x + _dot_split(y, _split(x, SOLVE_PASSES), SOLVE_PASSES) for x, y in zip(vb, ys)]
    w16 = [(x + _dot_split(y, _split(x, SOLVE_PASSES), SOLVE_PASSES)).astype(BF16) for x, y in zip(kbe, ys)]
    qg16 = [(x * e).astype(BF16) for x, e in zip(q, egc)]
    kd16 = [(x * col(kdf_all, c, h)).astype(BF16) for x, (c, h) in zip(k, items)]
    eg_last = [col(eg_last_all, c, h) for c, h in items]

    o = [None] * len(items)
    if carry:
        state = [s_ref[0, h] for h in range(GDN_HEADS)]
    for c in range(n_chunks):
        idx = [c * GDN_HEADS + h for h in range(GDN_HEADS)]
        if not carry:
            state = [s_ref[c, h] for h in range(GDN_HEADS)]
        s16 = [s.astype(BF16) for s in state]
        v_new = [u[i] - _dot(w16[i], s) for i, s in zip(idx, s16)]
        vn16 = [x.astype(BF16) for x in v_new]
        for i, s, vn in zip(idx, s16, vn16):
            o[i] = _dot(qg16[i], s) + _dot(intra16[i], vn)
        state = [s * eg_last[i] + _dot_tn(kd16[i], vn) for i, s, vn in zip(idx, state, vn16)]
        if not carry:
            for h in range(GDN_HEADS):
                sout_ref[c, h] = state[h]

    for (c, h), oo in zip(items, o):
        o_ref[c * CHUNK:(c + 1) * CHUNK, h * GDN_DV:(h + 1) * GDN_DV] = oo

    if carry:
        for h in range(GDN_HEADS):
            s_ref[0, h] = state[h]

        def _store_state():
            sout_ref[0] = s_ref[0]

        pl.when(pid == pl.num_programs(0) - 1)(_store_state)


def _gdn(qkv, gate, s0, seq_rows):
    n = qkv.shape[0]
    n_seq = n // seq_rows
    carry = n_seq == 1
    n_chunks = GDN_CHUNKS_PER_STEP
    rows = n_chunks * CHUNK
    assert n % rows == 0
    if carry:
        n_state = 1
        s_map = lambda i: (0, 0, 0, 0)
    else:
        assert seq_rows == CHUNK
        n_state = n_chunks
        s_map = lambda i: (i, 0, 0, 0)
    kern = functools.partial(_gdn_kernel, n_chunks=n_chunks, carry=carry)
    return pl.pallas_call(
        kern,
        grid=(n // rows,),
        in_specs=[
            pl.BlockSpec((rows, CONV_DIM), lambda i: (i, 0)),
            pl.BlockSpec((rows, LANES), lambda i: (i, 0)),
            pl.BlockSpec((n_state, GDN_HEADS, GDN_DK, GDN_DV), s_map),
        ],
        out_specs=[
            pl.BlockSpec((rows, GDN_HEADS * GDN_DV), lambda i: (i, 0)),
            pl.BlockSpec((n_state, GDN_HEADS, GDN_DK, GDN_DV), s_map),
        ],
        out_shape=[
            jax.ShapeDtypeStruct((n, GDN_HEADS * GDN_DV), F32),
            jax.ShapeDtypeStruct((n_seq, GDN_HEADS, GDN_DK, GDN_DV), F32),
        ],
        scratch_shapes=[pltpu.VMEM((n_state, GDN_HEADS, GDN_DK, GDN_DV), F32)],
        compiler_params=_cparams(1),
        name="gdn",
    )(qkv, gate, s0)


SWA_KEYS = WINDOW + CHUNK
KV_COLS = 2 * SWA_KV_HEADS * SWA_HD


def _bias_kernel(bucket_ref, rb_ref, o_ref):
    bucket = bucket_ref[...]
    h = pl.program_id(0)
    acc = jnp.zeros((CHUNK, SWA_KEYS), F32)
    for b in range(NUM_BUCKETS):
        acc = jnp.where(bucket == b, rb_ref[b * SWA_HEADS + h], acc)
    o_ref[0] = acc


def _bias_table(bucket, rel_bias_flat):
    return pl.pallas_call(
        _bias_kernel,
        grid=(SWA_HEADS,),
        in_specs=[
            pl.BlockSpec((CHUNK, SWA_KEYS), lambda h: (0, 0)),
            pl.BlockSpec(memory_space=pltpu.SMEM),
        ],
        out_specs=pl.BlockSpec((1, CHUNK, SWA_KEYS), lambda h: (h, 0, 0)),
        out_shape=jax.ShapeDtypeStruct((SWA_HEADS, CHUNK, SWA_KEYS), F32),
        compiler_params=_cparams(1),
        name="swa_bias",
    )(bucket, rel_bias_flat)


def _swa_kernel(q_ref, kvc_ref, kvp_ref, bias_ref, sink_ref, o_ref, kf_ref, *, n_chunks, one_sequence):
    if one_sequence:
        kf_ref[0:WINDOW, :] = kvp_ref[...]
        kf_ref[WINDOW:WINDOW + n_chunks * CHUNK, :] = kvc_ref[...]
        key_start = [c * CHUNK for c in range(n_chunks)]
        key_pos = lax.broadcasted_iota(jnp.int32, (CHUNK, SWA_KEYS), 1)
        tile_start = pl.program_id(0) * (n_chunks * CHUNK)
        valid = [key_pos + (tile_start + c * CHUNK - WINDOW) >= 0 for c in range(n_chunks)]
    else:
        for c in range(n_chunks):
            kf_ref[c * SWA_KEYS:c * SWA_KEYS + WINDOW, :] = kvp_ref[c * WINDOW:(c + 1) * WINDOW, :]
            kf_ref[c * SWA_KEYS + WINDOW:(c + 1) * SWA_KEYS, :] = kvc_ref[c * CHUNK:(c + 1) * CHUNK, :]
        key_start = [c * SWA_KEYS for c in range(n_chunks)]

    pairs = [(c, kh) for c in range(n_chunks) for kh in range(SWA_KV_HEADS)]
    keys = {(c, kh): kf_ref[key_start[c]:key_start[c] + SWA_KEYS, kh * SWA_HD:(kh + 1) * SWA_HD].astype(BF16)
            for c, kh in pairs}
    vals = {(c, kh): kf_ref[key_start[c]:key_start[c] + SWA_KEYS,
                            (SWA_KV_HEADS + kh) * SWA_HD:(SWA_KV_HEADS + kh + 1) * SWA_HD].astype(BF16)
            for c, kh in pairs}
    items = [(c, hd) for c in range(n_chunks) for hd in range(SWA_HEADS)]
    q = [q_ref[c * CHUNK:(c + 1) * CHUNK, hd * SWA_HD:(hd + 1) * SWA_HD].astype(BF16) for c, hd in items]
    logits = [_dot_nt(x, keys[(c, hd // SWA_GROUP)]) + bias_ref[hd] for x, (c, hd) in zip(q, items)]
    if one_sequence:
        logits = [jnp.where(valid[c], x, -jnp.inf) for x, (c, _) in zip(logits, items)]
    sink = [sink_ref[hd] for _, hd in items]
    m = [jnp.maximum(jnp.max(x, axis=-1, keepdims=True), s) for x, s in zip(logits, sink)]
    p = [jnp.exp(x - mm) for x, mm in zip(logits, m)]
    denom = [jnp.sum(x, axis=-1, keepdims=True) + jnp.exp(s - mm) for x, s, mm in zip(p, sink, m)]
    out = [_dot(x.astype(BF16), vals[(c, hd // SWA_GROUP)]) / d for x, d, (c, hd) in zip(p, denom, items)]
    for x, (c, hd) in zip(out, items):
        o_ref[c * CHUNK:(c + 1) * CHUNK, hd * SWA_HD:(hd + 1) * SWA_HD] = x


def _swa(qb, kv, kv_hist, bias, sinks, seq_rows):
    n = qb.shape[0]
    n_chunks = SWA_CHUNKS_PER_STEP
    rows = n_chunks * CHUNK
    assert n % rows == 0
    one_sequence = kv_hist is None
    if one_sequence:
        assert rows % WINDOW == 0
        per = rows // WINDOW
        prev_arr = kv
        prev_spec = pl.BlockSpec((WINDOW, KV_COLS), lambda i: (jnp.maximum(i * per - 1, 0), 0))
        kf_rows = WINDOW + rows
    else:
        assert seq_rows == CHUNK
        prev_arr = kv_hist
        prev_spec = pl.BlockSpec((n_chunks * WINDOW, KV_COLS), lambda i: (i, 0))
        kf_rows = n_chunks * SWA_KEYS
    kern = functools.partial(_swa_kernel, n_chunks=n_chunks, one_sequence=one_sequence)
    return pl.pallas_call(
        kern,
        grid=(n // rows,),
        in_specs=[
            pl.BlockSpec((rows, SWA_HEADS * SWA_HD), lambda i: (i, 0)),
            pl.BlockSpec((rows, KV_COLS), lambda i: (i, 0)),
            prev_spec,
            pl.BlockSpec((SWA_HEADS, CHUNK, SWA_KEYS), lambda i: (0, 0, 0)),
            pl.BlockSpec(memory_space=pltpu.SMEM),
        ],
        out_specs=pl.BlockSpec((rows, SWA_HEADS * SWA_HD), lambda i: (i, 0)),
        out_shape=jax.ShapeDtypeStruct((n, SWA_HEADS * SWA_HD), F32),
        scratch_shapes=[pltpu.VMEM((kf_rows, KV_COLS), F32)],
        compiler_params=_cparams(1),
        name="swa",
    )(qb, kv, prev_arr, bias, sinks)


def _merge_kernel(x_ref, oa_ref, z_ref, ob_ref, gates_ref, mod_ref, nw_ref, wa_ref, wb_ref, wo_ref, o_ref, oa16_ref,
                  *, spt):
    for h in range(GDN_HEADS):
        cols = slice(h * GDN_DV, (h + 1) * GDN_DV)
        oo = oa_ref[:, cols]
        zz = z_ref[:, cols]
        on = oo * lax.rsqrt(jnp.mean(oo * oo, axis=-1, keepdims=True) + EPS) * nw_ref[...] * (zz * jax.nn.sigmoid(zz))
        oa16_ref[:, cols] = on.astype(BF16)
    ya = _dot(oa16_ref[...], wa_ref[...])
    yb = _dot(ob_ref[...].astype(BF16), wb_ref[...])
    merged = (jax.nn.sigmoid(gates_ref[:, 0:D_MODEL]) * ya
              + jax.nn.sigmoid(gates_ref[:, D_MODEL:2 * D_MODEL]) * yb)
    y = _dot(merged.astype(BF16), wo_ref[...])
    rows = x_ref.shape[0] // spt
    for j in range(spt):
        sl = slice(j * rows, (j + 1) * rows)
        o_ref[sl, :] = x_ref[sl, :] + _mod_row(mod_ref, 5, spt, j) * y[sl, :]


def _merge(x, oa, z, ob, gates, mod, seq_rows, norm_w, wa, wb, wo):
    n = x.shape[0]
    tm, spt = _tile_rows(n, seq_rows, 512)
    row_spec = lambda w: pl.BlockSpec((tm, w), lambda i: (i, 0))
    return pl.pallas_call(
        functools.partial(_merge_kernel, spt=spt),
        grid=(n // tm,),
        in_specs=[
            row_spec(D_MODEL), row_spec(D_MODEL), row_spec(D_MODEL), row_spec(D_MODEL), row_spec(2 * D_MODEL),
            _mod_spec(mod.shape[1]),
            _const_spec((1, GDN_DV)),
            _const_spec((D_MODEL, D_MODEL)), _const_spec((D_MODEL, D_MODEL)), _const_spec((D_MODEL, D_MODEL)),
        ],
        out_specs=row_spec(D_MODEL),
        out_shape=jax.ShapeDtypeStruct((n, D_MODEL), F32),
        scratch_shapes=[pltpu.VMEM((tm, GDN_HEADS * GDN_DV), BF16)],
        compiler_params=_cparams(1),
        name="merge",
    )(x, oa, z, ob, gates, mod, norm_w, wa, wb, wo)


def _t5_bucket(rel):
    half = NUM_BUCKETS // 2
    max_exact = half // 2
    n = jnp.abs(rel)
    large = max_exact + (jnp.log(jnp.maximum(n, 1).astype(jnp.float32) / max_exact)
                         / math.log(MAX_DISTANCE / max_exact) * (half - max_exact)).astype(jnp.int32)
    large = jnp.minimum(large, half - 1)
    return jnp.where(rel > 0, half, 0) + jnp.where(n < max_exact, n, large)


def _trunk(x, mod, conv_hist, s_hist, k_hist, v_hist, w):
    bsz, seq, _ = x.shape
    n = bsz * seq
    x2 = x.reshape(n, D_MODEL)
    x1 = _ffn(x2, mod, seq, w["norm_ffn1"], w["ffn1_in"], w["ffn1_out"], w["norm_final"], sub=0, final=False)
    if conv_hist is None:
        hist8 = jnp.zeros((bsz, HIST_ROWS, CONV_DIM), F32)
        s0 = jnp.zeros((bsz, GDN_HEADS, GDN_DK, GDN_DV), F32)
        kv_hist = None
    else:
        hist8 = jnp.concatenate([jnp.zeros((bsz, HIST_ROWS - (CONV_W - 1), CONV_DIM), F32), conv_hist], axis=1)
        s0 = s_hist
        kv_hist = jnp.concatenate([k_hist.reshape(bsz, WINDOW, SWA_KV_HEADS * SWA_HD),
                                   v_hist.reshape(bsz, WINDOW, SWA_KV_HEADS * SWA_HD)], axis=-1)
        kv_hist = kv_hist.reshape(bsz * WINDOW, KV_COLS)
    qkv, z, qb, kv, gates, gate, tail = _proj(x1, mod, seq, w["norm_mix"], w["w_in"], hist8, w["conv_w"], w["gdn_par"])
    oa, s_new = _gdn(qkv, gate, s0, seq)
    ob = _swa(qb, kv, kv_hist, w["bias"], w["sinks"], seq)
    x3 = _merge(x1, oa, z, ob, gates, mod, seq, w["gdn_norm_w"], w["w_a"], w["w_b"], w["w_out"])
    y = _ffn(x3, mod, seq, w["norm_ffn2"], w["ffn2_in"], w["ffn2_out"], w["norm_final"], sub=2, final=True)

    conv_new = tail[:, HIST_ROWS - (CONV_W - 1):]
    half = SWA_KV_HEADS * SWA_HD
    kv3 = kv.reshape(bsz, seq, KV_COLS)
    if kv_hist is None:
        k_new = kv3[:, seq - WINDOW:, :half]
        v_new = kv3[:, seq - WINDOW:, half:]
    else:
        keep = WINDOW - seq
        k_new = jnp.concatenate([k_hist.reshape(bsz, WINDOW, half)[:, WINDOW - keep:], kv3[:, :, :half]], axis=1)
        v_new = jnp.concatenate([v_hist.reshape(bsz, WINDOW, half)[:, WINDOW - keep:], kv3[:, :, half:]], axis=1)
    k_new = k_new.reshape(bsz, WINDOW, SWA_KV_HEADS, SWA_HD)
    v_new = v_new.reshape(bsz, WINDOW, SWA_KV_HEADS, SWA_HD)
    return (y.reshape(bsz, seq, D_MODEL), conv_new[None], s_new[None], k_new[None], v_new[None])


def kernel(x_prompt, x_sample, state_gdn_conv, state_gdn_s, cache_swa_k, cache_swa_v, c_prompt, c_sample,
           norm_ffn1, w_ffn1_in, w_ffn1_out, norm_mix, w_in, gdn_conv_w, gdn_a_log, gdn_dt_bias, gdn_norm_w,
           swa_sinks, rel_bias, w_branch_a, w_branch_b, w_out, norm_ffn2, w_ffn2_in, w_ffn2_out,
           w_ada, b_ada, norm_final):
    bp = x_prompt.shape[0]
    bs = x_sample.shape[0]
    assert bp == 1 and x_sample.shape[1] == CHUNK and cache_swa_k.shape[2] == WINDOW

    n_c = bp + bs
    pad = -n_c % SUBLANES
    c_all = jnp.concatenate([c_prompt, c_sample, jnp.zeros((pad, D_MODEL), F32)], axis=0)
    mod_all = _modulation(c_all, w_ada[0], b_ada[0][None, :])
    mod_p = mod_all[:, :bp]
    mod_s = mod_all[:, bp:n_c]

    w_packed = _pack_w_in(w_in[0])
    par = jnp.zeros((SUBLANES, LANES), F32)
    par = par.at[0, GDN_HEADS:2 * GDN_HEADS].set(gdn_a_log[0]).at[1, GDN_HEADS:2 * GDN_HEADS].set(gdn_dt_bias[0])

    rel = jnp.arange(SWA_KEYS)[None, :] - WINDOW - jnp.arange(CHUNK)[:, None]
    bias = _bias_table(_t5_bucket(rel).astype(jnp.int32), rel_bias.reshape(-1))

    w = {
        "norm_ffn1": norm_ffn1, "ffn1_in": w_ffn1_in[0].astype(BF16), "ffn1_out": w_ffn1_out[0].astype(BF16),
        "norm_mix": norm_mix, "w_in": w_packed, "conv_w": gdn_conv_w[0], "gdn_par": par,
        "gdn_norm_w": gdn_norm_w, "bias": bias, "sinks": swa_sinks[0],
        "w_a": w_branch_a[0].astype(BF16), "w_b": w_branch_b[0].astype(BF16), "w_out": w_out[0].astype(BF16),
        "norm_ffn2": norm_ffn2, "ffn2_in": w_ffn2_in[0].astype(BF16), "ffn2_out": w_ffn2_out[0].astype(BF16),
        "norm_final": norm_final[None, :],
    }
    y_p, p_conv, p_s, p_k, p_v = _trunk(x_prompt, mod_p, None, None, None, None, w)
    y_s, s_conv, s_s, s_k, s_v = _trunk(x_sample, mod_s, state_gdn_conv[0], state_gdn_s[0],
                                        cache_swa_k[0], cache_swa_v[0], w)
    return (y_p, y_s, p_conv, p_s, p_k, p_v, s_conv, s_s, s_k, s_v)
```

```python
import functools
import math

import jax
import jax.numpy as jnp
from jax import lax
from jax.experimental import pallas as pl
from jax.experimental.pallas import tpu as pltpu

F32 = jnp.float32
BF16 = jnp.bfloat16

D_MODEL = 1024
CHUNK = 64
GDN_HEADS = 8
GDN_DK = 128
GDN_DV = 128
CONV_W = 4
CONV_DIM = GDN_HEADS * (2 * GDN_DK + GDN_DV)
SWA_HEADS = 16
SWA_KV_HEADS = 2
SWA_GROUP = SWA_HEADS // SWA_KV_HEADS
SWA_HD = 64
WINDOW = 128
NUM_BUCKETS = 32
MAX_DISTANCE = 128
D_FF = 2816
N_MOD = 9
EPS = 1e-6

LANES = 128
SUBLANES = 8
VMEM_LIMIT = 56 * 1024 * 1024

FF_CHUNK = D_FF // 2
HIST_ROWS = SUBLANES
INV_PASSES = 1
SOLVE_PASSES = 1
GDN_CHUNKS_PER_STEP = 4
SWA_CHUNKS_PER_STEP = 4


def _dot(a, b):
    return jnp.dot(a, b, preferred_element_type=F32)


def _dot_nt(a, b):
    return lax.dot_general(a, b, (((1,), (1,)), ((), ())), preferred_element_type=F32)


def _dot_tn(a, b):
    return lax.dot_general(a, b, (((0,), (0,)), ((), ())), preferred_element_type=F32)


def _split(a, passes):
    hi = a.astype(BF16)
    if passes == 1:
        return hi, None
    return hi, (a - hi.astype(F32)).astype(BF16)


def _dot_split(a_parts, b_parts, passes):
    ah, al = a_parts
    bh, bl = b_parts
    out = _dot(ah, bh)
    if passes == 3:
        out = out + _dot(al, bh) + _dot(ah, bl)
    return out


def _cparams(n_grid):
    return pltpu.CompilerParams(dimension_semantics=("arbitrary",) * n_grid, vmem_limit_bytes=VMEM_LIMIT)


def _const_spec(shape):
    nd = len(shape)
    return pl.BlockSpec(shape, lambda i: (0,) * nd, pipeline_mode=pl.Buffered(1))


def _mod_kernel(c_ref, w_ref, b_ref, o_ref):
    c = c_ref[...]
    a = (c * jax.nn.sigmoid(c)).astype(BF16)
    o_ref[0] = _dot(a, w_ref[...].astype(BF16)) + b_ref[...]


def _modulation(c_pad, w_ada, b_ada):
    rows = c_pad.shape[0]
    return pl.pallas_call(
        _mod_kernel,
        grid=(N_MOD,),
        in_specs=[
            pl.BlockSpec((rows, D_MODEL), lambda j: (0, 0)),
            pl.BlockSpec((D_MODEL, D_MODEL), lambda j: (0, j)),
            pl.BlockSpec((1, D_MODEL), lambda j: (0, j)),
        ],
        out_specs=pl.BlockSpec((1, rows, D_MODEL), lambda j: (j, 0, 0)),
        out_shape=jax.ShapeDtypeStruct((N_MOD, rows, D_MODEL), F32),
        compiler_params=_cparams(1),
        name="adaln_mod",
    )(c_pad, w_ada, b_ada)


def _mod_row(mod_ref, idx, spt, j):
    if mod_ref.shape[1] == 1:
        return mod_ref[idx, 0:1, :]
    return mod_ref[idx, pl.ds(pl.program_id(0) * spt + j, 1), :]


def _norm_mod_store(h_ref, x_ref, g_ref, mod_ref, sub, spt):
    rows = x_ref.shape[0] // spt
    g = g_ref[...]
    for j in range(spt):
        xs = x_ref[j * rows:(j + 1) * rows, :]
        ms = jnp.mean(xs * xs, axis=-1, keepdims=True)
        y = xs * lax.rsqrt(ms + EPS) * g
        sh = _mod_row(mod_ref, 3 * sub, spt, j)
        sc = _mod_row(mod_ref, 3 * sub + 1, spt, j)
        h_ref[j * rows:(j + 1) * rows, :] = (y * (1.0 + sc) + sh).astype(h_ref.dtype)


def _tile_rows(n_rows, seq_rows, target):
    if seq_rows >= target:
        assert seq_rows % target == 0
        return target, 1
    assert target % seq_rows == 0
    tm = min(target, n_rows)
    assert n_rows % tm == 0
    return tm, tm // seq_rows


def _mod_spec(n_seq_total):
    return pl.BlockSpec((N_MOD, n_seq_total, D_MODEL), lambda i: (0, 0, 0))


def _ffn_kernel(x_ref, mod_ref, g_ref, w1_ref, w2_ref, gf_ref, o_ref, h_ref, *, sub, final, spt):
    _norm_mod_store(h_ref, x_ref, g_ref, mod_ref, sub, spt)
    h = h_ref[...]
    acc = None
    for c in range(D_FF // FF_CHUNK):
        c0 = c * FF_CHUNK
        gate = _dot(h, w1_ref[:, c0:c0 + FF_CHUNK])
        up = _dot(h, w1_ref[:, D_FF + c0:D_FF + c0 + FF_CHUNK])
        a = (gate * jax.nn.sigmoid(gate) * up).astype(BF16)
        part = _dot(a, w2_ref[c0:c0 + FF_CHUNK, :])
        acc = part if acc is None else acc + part
    rows = x_ref.shape[0] // spt
    for j in range(spt):
        sl = slice(j * rows, (j + 1) * rows)
        ga = _mod_row(mod_ref, 3 * sub + 2, spt, j)
        xn = x_ref[sl, :] + 0.5 * ga * acc[sl, :]
        if final:
            ms = jnp.mean(xn * xn, axis=-1, keepdims=True)
            xn = xn * lax.rsqrt(ms + EPS) * gf_ref[...]
        o_ref[sl, :] = xn


def _ffn(x, mod, seq_rows, g, w1, w2, gf, *, sub, final):
    n = x.shape[0]
    tm, spt = _tile_rows(n, seq_rows, 512)
    kern = functools.partial(_ffn_kernel, sub=sub, final=final, spt=spt)
    return pl.pallas_call(
        kern,
        grid=(n // tm,),
        in_specs=[
            pl.BlockSpec((tm, D_MODEL), lambda i: (i, 0)),
            _mod_spec(mod.shape[1]),
            _const_spec((1, D_MODEL)),
            _const_spec((D_MODEL, 2 * D_FF)),
            _const_spec((D_FF, D_MODEL)),
            _const_spec((1, D_MODEL)),
        ],
        out_specs=pl.BlockSpec((tm, D_MODEL), lambda i: (i, 0)),
        out_shape=jax.ShapeDtypeStruct((n, D_MODEL), F32),
        scratch_shapes=[pltpu.VMEM((tm, D_MODEL), BF16)],
        compiler_params=_cparams(1),
        name="ffn_final" if final else "ffn",
    )(x, mod, g, w1, w2, gf)


PROJ_GROUPS = (("qkv", CONV_DIM), ("z", GDN_HEADS * GDN_DV), ("qb", SWA_HEADS * SWA_HD),
               ("kv", 2 * SWA_KV_HEADS * SWA_HD), ("gates", 2 * D_MODEL), ("ba", LANES))
PROJ_COLS = sum(w for _, w in PROJ_GROUPS)


assert math.log2(SWA_HD) % 2 == 0
W_IN_ALIGNED = CONV_DIM + GDN_HEADS * GDN_DV
W_IN_SMALL = 2 * GDN_HEADS
W_IN_REST = PROJ_COLS - W_IN_ALIGNED - LANES


def _pack_w_in_kernel(lo_ref, hi_ref, o_ref):
    o_ref[:, 0:W_IN_ALIGNED] = lo_ref[...].astype(BF16)
    n_qb = SWA_HEADS * SWA_HD
    o_ref[:, W_IN_ALIGNED:W_IN_ALIGNED + n_qb] = (hi_ref[:, W_IN_SMALL:W_IN_SMALL + n_qb] * (SWA_HD ** -0.5)).astype(BF16)
    o_ref[:, W_IN_ALIGNED + n_qb:W_IN_ALIGNED + W_IN_REST] = (
        hi_ref[:, W_IN_SMALL + n_qb:W_IN_SMALL + W_IN_REST].astype(BF16))
    lane = lax.broadcasted_iota(jnp.int32, (lo_ref.shape[0], LANES), 1)
    small = jnp.where(lane < W_IN_SMALL, hi_ref[:, 0:LANES], 0.0)
    o_ref[:, W_IN_ALIGNED + W_IN_REST:PROJ_COLS] = small.astype(BF16)


def _pack_w_in(w_in):
    assert w_in.shape == (1, D_MODEL, W_IN_ALIGNED + W_IN_SMALL + W_IN_REST)
    rows = 256
    return pl.pallas_call(
        _pack_w_in_kernel,
        grid=(D_MODEL // rows,),
        in_specs=[
            pl.BlockSpec((None, rows, W_IN_ALIGNED), lambda i: (0, i, 0)),
            pl.BlockSpec((None, rows, W_IN_ALIGNED), lambda i: (0, i, 1)),
        ],
        out_specs=pl.BlockSpec((rows, PROJ_COLS), lambda i: (i, 0)),
        out_shape=jax.ShapeDtypeStruct((D_MODEL, PROJ_COLS), BF16),
        compiler_params=_cparams(1),
        name="pack_w_in",
    )(w_in, w_in)


PROJ_OFFSETS = {}
_off = 0
for _name, _width in PROJ_GROUPS:
    PROJ_OFFSETS[_name] = (_off, _width)
    _off += _width
PROJ_ROWS = 256


def _proj_kernel(x_ref, mod_ref, g_ref, w_ref, hist_ref, cw_ref, par_ref,
                 qkv_ref, z_ref, qb_ref, kv_ref, gates_ref, gate_ref, tail_ref,
                 h_ref, xp_ref, *, spt, carry):
    pid = pl.program_id(0)
    tm = x_ref.shape[0]
    n_chunks = tm // CHUNK
    seg = CHUNK if carry else HIST_ROWS + CHUNK

    _norm_mod_store(h_ref, x_ref, g_ref, mod_ref, 1, spt)
    h = h_ref[...]

    def group(name, lo=0, width=None):
        c0, full = PROJ_OFFSETS[name]
        width = full if width is None else width
        return _dot(h, w_ref[:, c0 + lo:c0 + lo + width])

    if carry:
        def _load_hist():
            xp_ref[0:HIST_ROWS, :] = hist_ref[0]

        pl.when(pid == 0)(_load_hist)
    else:
        for c in range(n_chunks):
            xp_ref[c * seg:c * seg + HIST_ROWS, :] = hist_ref[c]

    part = GDN_HEADS * GDN_DK

    def project_part(p):
        raw = group("qkv", p * part, part)
        cols = slice(p * part, (p + 1) * part)
        if carry:
            xp_ref[HIST_ROWS:HIST_ROWS + tm, cols] = raw
        else:
            for c in range(n_chunks):
                xp_ref[c * seg + HIST_ROWS:(c + 1) * seg, cols] = raw[c * CHUNK:(c + 1) * CHUNK, :]

    def conv_part(p):
        for c in range(n_chunks):
            for t in range(p * GDN_HEADS, (p + 1) * GDN_HEADS):
                c0 = t * LANES
                win = xp_ref[c * seg:c * seg + HIST_ROWS + CHUNK, c0:c0 + LANES]
                acc = cw_ref[CONV_W - 1:CONV_W, c0:c0 + LANES] * win[HIST_ROWS:, :]
                for s in range(1, CONV_W):
                    tap = pltpu.roll(win, s, 0)[HIST_ROWS:, :]
                    acc = acc + cw_ref[CONV_W - 1 - s:CONV_W - s, c0:c0 + LANES] * tap
                y = acc * jax.nn.sigmoid(acc)
                if p < 2:
                    y = y * lax.rsqrt(jnp.sum(y * y, axis=-1, keepdims=True) + EPS)
                    if p == 0:
                        y = y * (GDN_DK ** -0.5)
                qkv_ref[c * CHUNK:(c + 1) * CHUNK, c0:c0 + LANES] = y

    def gate_math():
        ba = group("ba")
        ri = lax.broadcasted_iota(jnp.int32, (CHUNK, CHUNK), 0)
        ci = lax.broadcasted_iota(jnp.int32, (CHUNK, CHUNK), 1)
        tri = (ri >= ci).astype(BF16)
        lane = lax.broadcasted_iota(jnp.int32, (CHUNK, LANES), 1)
        a_coef = -jnp.exp(par_ref[0:1, :])
        dt_bias = par_ref[1:2, :]
        for c in range(n_chunks):
            bac = ba[c * CHUNK:(c + 1) * CHUNK, :]
            g_all = a_coef * jax.nn.softplus(bac + dt_bias)
            g_hi = g_all.astype(BF16)
            g_r1 = g_all - g_hi.astype(F32)
            g_mid = g_r1.astype(BF16)
            g_lo = (g_r1 - g_mid.astype(F32)).astype(BF16)
            gc = _dot(tri, g_hi) + _dot(tri, g_mid) + _dot(tri, g_lo)
            gate_ref[c * CHUNK:(c + 1) * CHUNK, :] = jnp.where(lane < GDN_HEADS, jax.nn.sigmoid(bac), gc)

    project_part(0)
    project_part(1)
    conv_part(0)
    project_part(2)
    conv_part(1)
    z_ref[...] = group("z")
    gate_math()
    qb_ref[...] = group("qb")
    conv_part(2)
    kv_ref[...] = group("kv")
    gates_ref[...] = group("gates")

    if carry:
        def _store_tail():
            tail_ref[0] = xp_ref[tm:tm + HIST_ROWS, :]

        pl.when(pid == pl.num_programs(0) - 1)(_store_tail)
        xp_ref[0:HIST_ROWS, :] = xp_ref[tm:tm + HIST_ROWS, :]
    else:
        for c in range(n_chunks):
            tail_ref[c] = xp_ref[c * seg + CHUNK:c * seg + CHUNK + HIST_ROWS, :]


def _proj(x, mod, seq_rows, g, w_packed, hist8, conv_w, par):
    n = x.shape[0]
    n_seq = n // seq_rows
    carry = n_seq == 1
    tm, spt = _tile_rows(n, seq_rows, PROJ_ROWS)
    if carry:
        n_hist = 1
        hist_map = lambda i: (0, 0, 0)
        xp_rows = HIST_ROWS + tm
    else:
        assert seq_rows == CHUNK
        n_hist = tm // CHUNK
        hist_map = lambda i: (i, 0, 0)
        xp_rows = n_hist * (HIST_ROWS + CHUNK)
    widths = [CONV_DIM, GDN_HEADS * GDN_DV, SWA_HEADS * SWA_HD, KV_COLS, 2 * D_MODEL, LANES]
    return pl.pallas_call(
        functools.partial(_proj_kernel, spt=spt, carry=carry),
        grid=(n // tm,),
        in_specs=[
            pl.BlockSpec((tm, D_MODEL), lambda i: (i, 0)),
            _mod_spec(mod.shape[1]),
            _const_spec((1, D_MODEL)),
            _const_spec((D_MODEL, PROJ_COLS)),
            pl.BlockSpec((n_hist, HIST_ROWS, CONV_DIM), hist_map),
            _const_spec((CONV_W, CONV_DIM)),
            _const_spec((SUBLANES, LANES)),
        ],
        out_specs=[pl.BlockSpec((tm, w), lambda i: (i, 0)) for w in widths]
        + [pl.BlockSpec((n_hist, HIST_ROWS, CONV_DIM), hist_map)],
        out_shape=[jax.ShapeDtypeStruct((n, w), F32) for w in widths]
        + [jax.ShapeDtypeStruct((n_seq, HIST_ROWS, CONV_DIM), F32)],
        scratch_shapes=[pltpu.VMEM((tm, D_MODEL), BF16), pltpu.VMEM((xp_rows, CONV_DIM), F32)],
        compiler_params=_cparams(1),
        name="in_proj",
    )(x, mod, g, w_packed, hist8, conv_w, par)


def _unit_lower_inverses(ls, eye, level_masks):
    ts = [eye - jnp.where(level_masks[0], l, 0.0) for l in ls]
    for mask in level_masks[1:]:
        lk = [_split(jnp.where(mask, l, 0.0), INV_PASSES) for l in ls]
        tp = [_split(t, INV_PASSES) for t in ts]
        m1 = [_dot_split(a, b, INV_PASSES) for a, b in zip(lk, tp)]
        m2 = [_dot_split(a, _split(b, INV_PASSES), INV_PASSES) for a, b in zip(tp, m1)]
        ts = [t - m for t, m in zip(ts, m2)]
    return ts


def _gdn_kernel(qkv_ref, gate_ref, s0_ref, o_ref, sout_ref, s_ref, *, n_chunks, carry):
    pid = pl.program_id(0)

    if carry:
        def _load_state():
            s_ref[0] = s0_ref[0]

        pl.when(pid == 0)(_load_state)
    else:
        for c in range(n_chunks):
            s_ref[c] = s0_ref[c]

    ri = lax.broadcasted_iota(jnp.int32, (CHUNK, CHUNK), 0)
    ci = lax.broadcasted_iota(jnp.int32, (CHUNK, CHUNK), 1)
    causal = ri >= ci
    strict = ri > ci
    eye = (ri == ci).astype(F32)
    level_masks = []
    for lvl in range(int(math.log2(CHUNK))):
        same_block = (ri >> (lvl + 1)) == (ci >> (lvl + 1))
        level_masks.append(same_block & (((ri >> lvl) & 1) == 1) & (((ci >> lvl) & 1) == 0))

    items = [(c, h) for c in range(n_chunks) for h in range(GDN_HEADS)]
    gate = [gate_ref[c * CHUNK:(c + 1) * CHUNK, :] for c in range(n_chunks)]
    egc_all = [jnp.exp(g) for g in gate]
    g_last_all = [g[CHUNK - 1:CHUNK, :] for g in gate]
    kdf_all = [jnp.exp(gl - g) for gl, g in zip(g_last_all, gate)]
    eg_last_all = [jnp.exp(gl) for gl in g_last_all]

    def tile(c, t):
        return qkv_ref[c * CHUNK:(c + 1) * CHUNK, t * LANES:(t + 1) * LANES]

    def col(arrs, c, h):
        return arrs[c][:, GDN_HEADS + h:GDN_HEADS + h + 1]

    q = [tile(c, h) for c, h in items]
    k = [tile(c, GDN_HEADS + h) for c, h in items]
    v = [tile(c, 2 * GDN_HEADS + h) for c, h in items]
    beta = [gate[c][:, h:h + 1] for c, h in items]
    gcol = [col(gate, c, h) for c, h in items]
    grow = [jnp.sum(g * eye, axis=0, keepdims=True) for g in gcol]
    decay = [jnp.where(causal, jnp.exp(jnp.where(causal, gc - gr, 0.0)), 0.0) for gc, gr in zip(gcol, grow)]
    kb = [x * b for x, b in zip(k, beta)]
    k16 = [x.astype(BF16) for x in k]
    kq = [_dot_nt(jnp.concatenate([a.astype(BF16), b.astype(BF16)], axis=0), c) for a, b, c in zip(kb, q, k16)]
    kk = [x[0:CHUNK, :] for x in kq]
    qk = [x[CHUNK:, :] for x in kq]
    ls = [jnp.where(strict, a * d, 0.0) for a, d in zip(kk, decay)]
    intra16 = [jnp.where(causal, a * d, 0.0).astype(BF16) for a, d in zip(qk, decay)]
    ts = _unit_lower_inverses(ls, eye, level_masks)
    ys = [_split(t - eye, SOLVE_PASSES) for t in ts]
    egc = [col(egc_all, c, h) for c, h in items]
    vb = [x * b for x, b in zip(v, beta)]
    kbe = [x * e for x, e in zip(kb, egc)]
    rhs = [jnp.concatenate([a, b], axis=1) for a, b in zip(vb, kbe)]
    sol = [x + _dot_split(y, _split(x, SOLVE_PASSES), SOLVE_PASSES) for x, y in zip(rhs, ys)]
    u = [x[:, 0:GDN_DV] for x in sol]
    w16 = [x[:, GDN_DV:].astype(BF16) for x in sol]
    qg16 = [(x * e).astype(BF16) for x, e in zip(q, egc)]
    kd16 = [(x * col(kdf_all, c, h)).astype(BF16) for x, (c, h) in zip(k, items)]
    eg_last = [col(eg_last_all, c, h) for c, h in items]
    wq16 = [jnp.concatenate([a, b], axis=0) for a, b in zip(w16, qg16)]

    o = [None] * len(items)
    if carry:
        state = [s_ref[0, h] for h in range(GDN_HEADS)]
    for c in range(n_chunks):
        idx = [c * GDN_HEADS + h for h in range(GDN_HEADS)]
        if not carry:
            state = [s_ref[c, h] for h in range(GDN_HEADS)]
        s16 = [s.astype(BF16) for s in state]
        ws_qs = [_dot(wq16[i], s) for i, s in zip(idx, s16)]
        v_new = [u[i] - x[0:CHUNK, :] for i, x in zip(idx, ws_qs)]
        vn16 = [x.astype(BF16) for x in v_new]
        for i, x, vn in zip(idx, ws_qs, vn16):
            o[i] = x[CHUNK:, :] + _dot(intra16[i], vn)
        state = [s * eg_last[i] + _dot_tn(kd16[i], vn) for i, s, vn in zip(idx, state, vn16)]
        if not carry:
            for h in range(GDN_HEADS):
                sout_ref[c, h] = state[h]

    for (c, h), oo in zip(items, o):
        o_ref[c * CHUNK:(c + 1) * CHUNK, h * GDN_DV:(h + 1) * GDN_DV] = oo

    if carry:
        for h in range(GDN_HEADS):
            s_ref[0, h] = state[h]

        def _store_state():
            sout_ref[0] = s_ref[0]

        pl.when(pid == pl.num_programs(0) - 1)(_store_state)


def _gdn(qkv, gate, s0, seq_rows):
    n = qkv.shape[0]
    n_seq = n // seq_rows
    carry = n_seq == 1
    n_chunks = GDN_CHUNKS_PER_STEP
    rows = n_chunks * CHUNK
    assert n % rows == 0
    if carry:
        n_state = 1
        s_map = lambda i: (0, 0, 0, 0)
    else:
        assert seq_rows == CHUNK
        n_state = n_chunks
        s_map = lambda i: (i, 0, 0, 0)
    kern = functools.partial(_gdn_kernel, n_chunks=n_chunks, carry=carry)
    return pl.pallas_call(
        kern,
        grid=(n // rows,),
        in_specs=[
            pl.BlockSpec((rows, CONV_DIM), lambda i: (i, 0)),
            pl.BlockSpec((rows, LANES), lambda i: (i, 0)),
            pl.BlockSpec((n_state, GDN_HEADS, GDN_DK, GDN_DV), s_map),
        ],
        out_specs=[
            pl.BlockSpec((rows, GDN_HEADS * GDN_DV), lambda i: (i, 0)),
            pl.BlockSpec((n_state, GDN_HEADS, GDN_DK, GDN_DV), s_map),
        ],
        out_shape=[
            jax.ShapeDtypeStruct((n, GDN_HEADS * GDN_DV), F32),
            jax.ShapeDtypeStruct((n_seq, GDN_HEADS, GDN_DK, GDN_DV), F32),
        ],
        scratch_shapes=[pltpu.VMEM((n_state, GDN_HEADS, GDN_DK, GDN_DV), F32)],
        compiler_params=_cparams(1),
        name="gdn",
    )(qkv, gate, s0)


SWA_KEYS = WINDOW + CHUNK
SWA_KEYS_PAD = 2 * LANES
KV_COLS = 2 * SWA_KV_HEADS * SWA_HD


def _bias_kernel(bucket_ref, rb_ref, o_ref):
    bucket = bucket_ref[...]
    h = pl.program_id(0)
    acc = jnp.zeros((CHUNK, SWA_KEYS), F32)
    for b in range(NUM_BUCKETS):
        acc = jnp.where(bucket == b, rb_ref[b * SWA_HEADS + h], acc)
    o_ref[0] = acc


def _bias_table(bucket, rel_bias_flat):
    return pl.pallas_call(
        _bias_kernel,
        grid=(SWA_HEADS,),
        in_specs=[
            pl.BlockSpec((CHUNK, SWA_KEYS), lambda h: (0, 0)),
            pl.BlockSpec(memory_space=pltpu.SMEM),
        ],
        out_specs=pl.BlockSpec((1, CHUNK, SWA_KEYS), lambda h: (h, 0, 0)),
        out_shape=jax.ShapeDtypeStruct((SWA_HEADS, CHUNK, SWA_KEYS), F32),
        compiler_params=_cparams(1),
        name="swa_bias",
    )(bucket, rel_bias_flat)


def _swa_kernel(q_ref, kvc_ref, kvp_ref, bias_ref, sink_ref, o_ref, kf_ref, *, n_chunks, one_sequence):
    n_pad = SWA_KEYS_PAD - SWA_KEYS
    kf_ref[kf_ref.shape[0] - n_pad:, :] = jnp.zeros((n_pad, KV_COLS), F32)
    if one_sequence:
        kf_ref[0:WINDOW, :] = kvp_ref[...]
        kf_ref[WINDOW:WINDOW + n_chunks * CHUNK, :] = kvc_ref[...]
        key_start = [c * CHUNK for c in range(n_chunks)]
        key_pos = lax.broadcasted_iota(jnp.int32, (CHUNK, SWA_KEYS), 1)
        tile_start = pl.program_id(0) * (n_chunks * CHUNK)
        valid = [key_pos + (tile_start + c * CHUNK - WINDOW) >= 0 for c in range(n_chunks)]
    else:
        for c in range(n_chunks):
            kf_ref[c * SWA_KEYS:c * SWA_KEYS + WINDOW, :] = kvp_ref[c * WINDOW:(c + 1) * WINDOW, :]
            kf_ref[c * SWA_KEYS + WINDOW:(c + 1) * SWA_KEYS, :] = kvc_ref[c * CHUNK:(c + 1) * CHUNK, :]
        key_start = [c * SWA_KEYS for c in range(n_chunks)]

    pairs = [(c, kh) for c in range(n_chunks) for kh in range(SWA_KV_HEADS)]
    half = SWA_KV_HEADS * SWA_HD
    keys_t = [kf_ref[key_start[c]:key_start[c] + SWA_KEYS_PAD, 0:half].T.astype(BF16) for c in range(n_chunks)]
    keys = {(c, kh): keys_t[c][kh * SWA_HD:(kh + 1) * SWA_HD, :] for c, kh in pairs}
    ones = jnp.ones((SWA_KEYS, LANES - SWA_HD), BF16)
    vals = {(c, kh): jnp.concatenate(
        [kf_ref[key_start[c]:key_start[c] + SWA_KEYS,
                (SWA_KV_HEADS + kh) * SWA_HD:(SWA_KV_HEADS + kh + 1) * SWA_HD].astype(BF16), ones], axis=1)
            for c, kh in pairs}
    items = [(c, hd) for c in range(n_chunks) for hd in range(SWA_HEADS)]
    q = [q_ref[c * CHUNK:(c + 1) * CHUNK, hd * SWA_HD:(hd + 1) * SWA_HD].astype(BF16) for c, hd in items]
    def group_rows(c, kh):
        return range(c * SWA_HEADS + kh * SWA_GROUP, c * SWA_HEADS + (kh + 1) * SWA_GROUP)

    qk = {(c, kh): _dot(jnp.concatenate([q[i] for i in group_rows(c, kh)], axis=0), keys[(c, kh)])
          for c, kh in pairs}
    logits = [qk[(c, hd // SWA_GROUP)][(hd % SWA_GROUP) * CHUNK:(hd % SWA_GROUP + 1) * CHUNK, 0:SWA_KEYS] + bias_ref[hd]
              for c, hd in items]
    if one_sequence:
        logits = [jnp.where(valid[c], x, -jnp.inf) for x, (c, _) in zip(logits, items)]
    sink = [sink_ref[hd] for _, hd in items]
    m = [jnp.maximum(jnp.max(x, axis=-1, keepdims=True), s) for x, s in zip(logits, sink)]
    p16 = [jnp.exp(x - mm).astype(BF16) for x, mm in zip(logits, m)]
    pv = {(c, kh): _dot(jnp.concatenate([p16[i] for i in group_rows(c, kh)], axis=0), vals[(c, kh)])
          for c, kh in pairs}
    pv = [pv[(c, hd // SWA_GROUP)][(hd % SWA_GROUP) * CHUNK:(hd % SWA_GROUP + 1) * CHUNK, :] for c, hd in items]
    denom = [x[:, SWA_HD:SWA_HD + 1] + jnp.exp(s - mm) for x, s, mm in zip(pv, sink, m)]
    out = [x[:, 0:SWA_HD] / d for x, d in zip(pv, denom)]
    for x, (c, hd) in zip(out, items):
        o_ref[c * CHUNK:(c + 1) * CHUNK, hd * SWA_HD:(hd + 1) * SWA_HD] = x


def _swa(qb, kv, kv_hist, bias, sinks, seq_rows):
    n = qb.shape[0]
    n_chunks = SWA_CHUNKS_PER_STEP
    rows = n_chunks * CHUNK
    assert n % rows == 0
    one_sequence = kv_hist is None
    if one_sequence:
        assert rows % WINDOW == 0
        per = rows // WINDOW
        prev_arr = kv
        prev_spec = pl.BlockSpec((WINDOW, KV_COLS), lambda i: (jnp.maximum(i * per - 1, 0), 0))
        kf_rows = WINDOW + rows
    else:
        assert seq_rows == CHUNK
        prev_arr = kv_hist
        prev_spec = pl.BlockSpec((n_chunks * WINDOW, KV_COLS), lambda i: (i, 0))
        kf_rows = n_chunks * SWA_KEYS
    kern = functools.partial(_swa_kernel, n_chunks=n_chunks, one_sequence=one_sequence)
    return pl.pallas_call(
        kern,
        grid=(n // rows,),
        in_specs=[
            pl.BlockSpec((rows, SWA_HEADS * SWA_HD), lambda i: (i, 0)),
            pl.BlockSpec((rows, KV_COLS), lambda i: (i, 0)),
            prev_spec,
            pl.BlockSpec((SWA_HEADS, CHUNK, SWA_KEYS), lambda i: (0, 0, 0)),
            pl.BlockSpec(memory_space=pltpu.SMEM),
        ],
        out_specs=pl.BlockSpec((rows, SWA_HEADS * SWA_HD), lambda i: (i, 0)),
        out_shape=jax.ShapeDtypeStruct((n, SWA_HEADS * SWA_HD), F32),
        scratch_shapes=[pltpu.VMEM((kf_rows + SWA_KEYS_PAD - SWA_KEYS, KV_COLS), F32)],
        compiler_params=_cparams(1),
        name="swa",
    )(qb, kv, prev_arr, bias, sinks)


def _merge_kernel(x_ref, oa_ref, z_ref, ob_ref, gates_ref, mod_ref, nw_ref, wa_ref, wb_ref, wo_ref, o_ref, oa16_ref,
                  *, spt):
    for h in range(GDN_HEADS):
        cols = slice(h * GDN_DV, (h + 1) * GDN_DV)
        oo = oa_ref[:, cols]
        zz = z_ref[:, cols]
        on = oo * lax.rsqrt(jnp.mean(oo * oo, axis=-1, keepdims=True) + EPS) * nw_ref[...] * (zz * jax.nn.sigmoid(zz))
        oa16_ref[:, cols] = on.astype(BF16)
    ya = _dot(oa16_ref[...], wa_ref[...])
    yb = _dot(ob_ref[...].astype(BF16), wb_ref[...])
    merged = (jax.nn.sigmoid(gates_ref[:, 0:D_MODEL]) * ya
              + jax.nn.sigmoid(gates_ref[:, D_MODEL:2 * D_MODEL]) * yb)
    y = _dot(merged.astype(BF16), wo_ref[...])
    rows = x_ref.shape[0] // spt
    for j in range(spt):
        sl = slice(j * rows, (j + 1) * rows)
        o_ref[sl, :] = x_ref[sl, :] + _mod_row(mod_ref, 5, spt, j) * y[sl, :]


def _merge(x, oa, z, ob, gates, mod, seq_rows, norm_w, wa, wb, wo):
    n = x.shape[0]
    tm, spt = _tile_rows(n, seq_rows, 512)
    row_spec = lambda w: pl.BlockSpec((tm, w), lambda i: (i, 0))
    return pl.pallas_call(
        functools.partial(_merge_kernel, spt=spt),
        grid=(n // tm,),
        in_specs=[
            row_spec(D_MODEL), row_spec(D_MODEL), row_spec(D_MODEL), row_spec(D_MODEL), row_spec(2 * D_MODEL),
            _mod_spec(mod.shape[1]),
            _const_spec((1, GDN_DV)),
            _const_spec((D_MODEL, D_MODEL)), _const_spec((D_MODEL, D_MODEL)), _const_spec((D_MODEL, D_MODEL)),
        ],
        out_specs=row_spec(D_MODEL),
        out_shape=jax.ShapeDtypeStruct((n, D_MODEL), F32),
        scratch_shapes=[pltpu.VMEM((tm, GDN_HEADS * GDN_DV), BF16)],
        compiler_params=_cparams(1),
        name="merge",
    )(x, oa, z, ob, gates, mod, norm_w, wa, wb, wo)


def _t5_bucket(rel):
    half = NUM_BUCKETS // 2
    max_exact = half // 2
    n = jnp.abs(rel)
    large = max_exact + (jnp.log(jnp.maximum(n, 1).astype(jnp.float32) / max_exact)
                         / math.log(MAX_DISTANCE / max_exact) * (half - max_exact)).astype(jnp.int32)
    large = jnp.minimum(large, half - 1)
    return jnp.where(rel > 0, half, 0) + jnp.where(n < max_exact, n, large)


def _trunk(x, mod, conv_hist, s_hist, k_hist, v_hist, w):
    bsz, seq, _ = x.shape
    n = bsz * seq
    x2 = x.reshape(n, D_MODEL)
    x1 = _ffn(x2, mod, seq, w["norm_ffn1"], w["ffn1_in"], w["ffn1_out"], w["norm_final"], sub=0, final=False)
    if conv_hist is None:
        hist8 = jnp.zeros((bsz, HIST_ROWS, CONV_DIM), F32)
        s0 = jnp.zeros((bsz, GDN_HEADS, GDN_DK, GDN_DV), F32)
        kv_hist = None
    else:
        hist8 = jnp.concatenate([jnp.zeros((bsz, HIST_ROWS - (CONV_W - 1), CONV_DIM), F32), conv_hist], axis=1)
        s0 = s_hist
        kv_hist = jnp.concatenate([k_hist.reshape(bsz, WINDOW, SWA_KV_HEADS * SWA_HD),
                                   v_hist.reshape(bsz, WINDOW, SWA_KV_HEADS * SWA_HD)], axis=-1)
        kv_hist = kv_hist.reshape(bsz * WINDOW, KV_COLS)
    qkv, z, qb, kv, gates, gate, tail = _proj(x1, mod, seq, w["norm_mix"], w["w_in"], hist8, w["conv_w"], w["gdn_par"])
    oa, s_new = _gdn(qkv, gate, s0, seq)
    ob = _swa(qb, kv, kv_hist, w["bias"], w["sinks"], seq)
    x3 = _merge(x1, oa, z, ob, gates, mod, seq, w["gdn_norm_w"], w["w_a"], w["w_b"], w["w_out"])
    y = _ffn(x3, mod, seq, w["norm_ffn2"], w["ffn2_in"], w["ffn2_out"], w["norm_final"], sub=2, final=True)

    conv_new = tail[:, HIST_ROWS - (CONV_W - 1):]
    half = SWA_KV_HEADS * SWA_HD
    kv3 = kv.reshape(bsz, seq, KV_COLS)
    if kv_hist is None:
        k_new = kv3[:, seq - WINDOW:, :half]
        v_new = kv3[:, seq - WINDOW:, half:]
    else:
        keep = WINDOW - seq
        k_new = jnp.concatenate([k_hist.reshape(bsz, WINDOW, half)[:, WINDOW - keep:], kv3[:, :, :half]], axis=1)
        v_new = jnp.concatenate([v_hist.reshape(bsz, WINDOW, half)[:, WINDOW - keep:], kv3[:, :, half:]], axis=1)
    k_new = k_new.reshape(bsz, WINDOW, SWA_KV_HEADS, SWA_HD)
    v_new = v_new.reshape(bsz, WINDOW, SWA_KV_HEADS, SWA_HD)
    return (y.reshape(bsz, seq, D_MODEL), conv_new[None], s_new[None], k_new[None], v_new[None])


def kernel(x_prompt, x_sample, state_gdn_conv, state_gdn_s, cache_swa_k, cache_swa_v, c_prompt, c_sample,
           norm_ffn1, w_ffn1_in, w_ffn1_out, norm_mix, w_in, gdn_conv_w, gdn_a_log, gdn_dt_bias, gdn_norm_w,
           swa_sinks, rel_bias, w_branch_a, w_branch_b, w_out, norm_ffn2, w_ffn2_in, w_ffn2_out,
           w_ada, b_ada, norm_final):
    bp = x_prompt.shape[0]
    bs = x_sample.shape[0]
    assert bp == 1 and x_sample.shape[1] == CHUNK and cache_swa_k.shape[2] == WINDOW

    n_c = bp + bs
    pad = -n_c % SUBLANES
    c_all = jnp.concatenate([c_prompt, c_sample, jnp.zeros((pad, D_MODEL), F32)], axis=0)
    mod_all = _modulation(c_all, w_ada[0], b_ada[0][None, :])
    mod_p = mod_all[:, :bp]
    mod_s = mod_all[:, bp:n_c]

    w_packed = _pack_w_in(w_in)
    par = jnp.zeros((SUBLANES, LANES), F32)
    par = par.at[0, GDN_HEADS:2 * GDN_HEADS].set(gdn_a_log[0]).at[1, GDN_HEADS:2 * GDN_HEADS].set(gdn_dt_bias[0])

    rel = jnp.arange(SWA_KEYS)[None, :] - WINDOW - jnp.arange(CHUNK)[:, None]
    bias = _bias_table(_t5_bucket(rel).astype(jnp.int32), rel_bias.reshape(-1))

    w = {
        "norm_ffn1": norm_ffn1, "ffn1_in": w_ffn1_in[0].astype(BF16), "ffn1_out": w_ffn1_out[0].astype(BF16),
        "norm_mix": norm_mix, "w_in": w_packed, "conv_w": gdn_conv_w[0], "gdn_par": par,
        "gdn_norm_w": gdn_norm_w, "bias": bias, "sinks": swa_sinks[0],
        "w_a": w_branch_a[0].astype(BF16), "w_b": w_branch_b[0].astype(BF16), "w_out": w_out[0].astype(BF16),
        "norm_ffn2": norm_ffn2, "ffn2_in": w_ffn2_in[0].astype(BF16), "ffn2_out": w_ffn2_out[0].astype(BF16),
        "norm_final": norm_final[None, :],
    }
    y_p, p_conv, p_s, p_k, p_v = _trunk(x_prompt, mod_p, None, None, None, None, w)
    y_s, s_conv, s_s, s_k, s_v = _trunk(x_sample, mod_s, state_gdn_conv[0], state_gdn_s[0],
                                        cache_swa_k[0], cache_swa_v[0], w)
    return (y_p, y_s, p_conv, p_s, p_k, p_v, s_conv, s_s, s_k, s_v)
```

```python
import functools
import math

import jax
import jax.numpy as jnp
from jax import lax
from jax.experimental import pallas as pl
from jax.experimental.pallas import tpu as pltpu

F32 = jnp.float32
BF16 = jnp.bfloat16

D_MODEL = 1024
CHUNK = 64
GDN_HEADS = 8
GDN_DK = 128
GDN_DV = 128
CONV_W = 4
CONV_DIM = GDN_HEADS * (2 * GDN_DK + GDN_DV)
SWA_HEADS = 16
SWA_KV_HEADS = 2
SWA_GROUP = SWA_HEADS // SWA_KV_HEADS
SWA_HD = 64
WINDOW = 128
NUM_BUCKETS = 32
MAX_DISTANCE = 128
D_FF = 2816
N_MOD = 9
EPS = 1e-6

LANES = 128
SUBLANES = 8
VMEM_LIMIT = 56 * 1024 * 1024

FF_CHUNK = D_FF // 2
HIST_ROWS = SUBLANES
INV_PASSES = 1
SOLVE_PASSES = 1
GDN_CHUNKS_PER_STEP = 4
SWA_CHUNKS_PER_STEP = 4


def _dot(a, b):
    return jnp.dot(a, b, preferred_element_type=F32)


def _dot_nt(a, b):
    return lax.dot_general(a, b, (((1,), (1,)), ((), ())), preferred_element_type=F32)


def _dot_tn(a, b):
    return lax.dot_general(a, b, (((0,), (0,)), ((), ())), preferred_element_type=F32)


def _split(a, passes):
    hi = a.astype(BF16)
    if passes == 1:
        return hi, None
    return hi, (a - hi.astype(F32)).astype(BF16)


def _dot_split(a_parts, b_parts, passes):
    ah, al = a_parts
    bh, bl = b_parts
    out = _dot(ah, bh)
    if passes == 3:
        out = out + _dot(al, bh) + _dot(ah, bl)
    return out


def _cparams(n_grid):
    return pltpu.CompilerParams(dimension_semantics=("arbitrary",) * n_grid, vmem_limit_bytes=VMEM_LIMIT)


def _const_spec(shape):
    nd = len(shape)
    return pl.BlockSpec(shape, lambda i: (0,) * nd, pipeline_mode=pl.Buffered(1))


def _mod_kernel(c_ref, w_ref, b_ref, o_ref):
    c = c_ref[...]
    a = (c * jax.nn.sigmoid(c)).astype(BF16)
    o_ref[0] = _dot(a, w_ref[...].astype(BF16)) + b_ref[...]


def _modulation(c_pad, w_ada, b_ada):
    rows = c_pad.shape[0]
    return pl.pallas_call(
        _mod_kernel,
        grid=(N_MOD,),
        in_specs=[
            pl.BlockSpec((rows, D_MODEL), lambda j: (0, 0)),
            pl.BlockSpec((D_MODEL, D_MODEL), lambda j: (0, j)),
            pl.BlockSpec((1, D_MODEL), lambda j: (0, j)),
        ],
        out_specs=pl.BlockSpec((1, rows, D_MODEL), lambda j: (j, 0, 0)),
        out_shape=jax.ShapeDtypeStruct((N_MOD, rows, D_MODEL), F32),
        compiler_params=_cparams(1),
        name="adaln_mod",
    )(c_pad, w_ada, b_ada)


def _mod_row(mod_ref, idx, spt, j):
    if mod_ref.shape[1] == 1:
        return mod_ref[idx, 0:1, :]
    return mod_ref[idx, pl.ds(pl.program_id(0) * spt + j, 1), :]


def _norm_mod_store(h_ref, x_ref, g_ref, mod_ref, sub, spt):
    rows = x_ref.shape[0] // spt
    g = g_ref[...]
    for j in range(spt):
        xs = x_ref[j * rows:(j + 1) * rows, :]
        ms = jnp.mean(xs * xs, axis=-1, keepdims=True)
        y = xs * lax.rsqrt(ms + EPS) * g
        sh = _mod_row(mod_ref, 3 * sub, spt, j)
        sc = _mod_row(mod_ref, 3 * sub + 1, spt, j)
        h_ref[j * rows:(j + 1) * rows, :] = (y * (1.0 + sc) + sh).astype(h_ref.dtype)


def _tile_rows(n_rows, seq_rows, target):
    if seq_rows >= target:
        assert seq_rows % target == 0
        return target, 1
    assert target % seq_rows == 0
    tm = min(target, n_rows)
    assert n_rows % tm == 0
    return tm, tm // seq_rows


def _mod_spec(n_seq_total):
    return pl.BlockSpec((N_MOD, n_seq_total, D_MODEL), lambda i: (0, 0, 0))


def _ffn_kernel(x_ref, mod_ref, g_ref, w1_ref, w2_ref, gf_ref, o_ref, h_ref, *, sub, final, spt):
    _norm_mod_store(h_ref, x_ref, g_ref, mod_ref, sub, spt)
    h = h_ref[...]
    acc = None
    for c in range(D_FF // FF_CHUNK):
        c0 = c * FF_CHUNK
        gate = _dot(h, w1_ref[:, c0:c0 + FF_CHUNK])
        up = _dot(h, w1_ref[:, D_FF + c0:D_FF + c0 + FF_CHUNK])
        a = (gate * jax.nn.sigmoid(gate) * up).astype(BF16)
        part = _dot(a, w2_ref[c0:c0 + FF_CHUNK, :])
        acc = part if acc is None else acc + part
    rows = x_ref.shape[0] // spt
    for j in range(spt):
        sl = slice(j * rows, (j + 1) * rows)
        ga = _mod_row(mod_ref, 3 * sub + 2, spt, j)
        xn = x_ref[sl, :] + 0.5 * ga * acc[sl, :]
        if final:
            ms = jnp.mean(xn * xn, axis=-1, keepdims=True)
            xn = xn * lax.rsqrt(ms + EPS) * gf_ref[...]
        o_ref[sl, :] = xn


def _ffn(x, mod, seq_rows, g, w1, w2, gf, *, sub, final):
    n = x.shape[0]
    tm, spt = _tile_rows(n, seq_rows, 512)
    kern = functools.partial(_ffn_kernel, sub=sub, final=final, spt=spt)
    return pl.pallas_call(
        kern,
        grid=(n // tm,),
        in_specs=[
            pl.BlockSpec((tm, D_MODEL), lambda i: (i, 0)),
            _mod_spec(mod.shape[1]),
            _const_spec((1, D_MODEL)),
            _const_spec((D_MODEL, 2 * D_FF)),
            _const_spec((D_FF, D_MODEL)),
            _const_spec((1, D_MODEL)),
        ],
        out_specs=pl.BlockSpec((tm, D_MODEL), lambda i: (i, 0)),
        out_shape=jax.ShapeDtypeStruct((n, D_MODEL), F32),
        scratch_shapes=[pltpu.VMEM((tm, D_MODEL), BF16)],
        compiler_params=_cparams(1),
        name="ffn_final" if final else "ffn",
    )(x, mod, g, w1, w2, gf)


PROJ_GROUPS = (("qkv", CONV_DIM), ("z", GDN_HEADS * GDN_DV), ("qb", SWA_HEADS * SWA_HD),
               ("kv", 2 * SWA_KV_HEADS * SWA_HD), ("gates", 2 * D_MODEL), ("ba", LANES))
PROJ_COLS = sum(w for _, w in PROJ_GROUPS)


assert math.log2(SWA_HD) % 2 == 0
W_IN_ALIGNED = CONV_DIM + GDN_HEADS * GDN_DV
W_IN_SMALL = 2 * GDN_HEADS
W_IN_REST = PROJ_COLS - W_IN_ALIGNED - LANES


def _pack_w_in_kernel(w_ref, o_ref):
    o_ref[:, 0:W_IN_ALIGNED] = w_ref[:, 0:W_IN_ALIGNED].astype(BF16)
    hi = W_IN_ALIGNED + W_IN_SMALL
    n_qb = SWA_HEADS * SWA_HD
    o_ref[:, W_IN_ALIGNED:W_IN_ALIGNED + n_qb] = (w_ref[:, hi:hi + n_qb] * (SWA_HD ** -0.5)).astype(BF16)
    o_ref[:, W_IN_ALIGNED + n_qb:W_IN_ALIGNED + W_IN_REST] = w_ref[:, hi + n_qb:hi + W_IN_REST].astype(BF16)
    lane = lax.broadcasted_iota(jnp.int32, (w_ref.shape[0], LANES), 1)
    small = jnp.where(lane < W_IN_SMALL, w_ref[:, W_IN_ALIGNED:W_IN_ALIGNED + LANES], 0.0)
    o_ref[:, W_IN_ALIGNED + W_IN_REST:PROJ_COLS] = small.astype(BF16)


def _pack_w_in(w_in):
    n_cols = W_IN_ALIGNED + W_IN_SMALL + W_IN_REST
    assert w_in.shape == (1, D_MODEL, n_cols)
    rows = 256
    return pl.pallas_call(
        _pack_w_in_kernel,
        grid=(D_MODEL // rows,),
        in_specs=[pl.BlockSpec((None, rows, n_cols), lambda i: (0, i, 0))],
        out_specs=pl.BlockSpec((rows, PROJ_COLS), lambda i: (i, 0)),
        out_shape=jax.ShapeDtypeStruct((D_MODEL, PROJ_COLS), BF16),
        compiler_params=_cparams(1),
        name="pack_w_in",
    )(w_in)


PROJ_OFFSETS = {}
_off = 0
for _name, _width in PROJ_GROUPS:
    PROJ_OFFSETS[_name] = (_off, _width)
    _off += _width
PROJ_ROWS = 256


def _proj_kernel(x_ref, mod_ref, g_ref, w_ref, hist_ref, cw_ref, par_ref,
                 qkv_ref, z_ref, qb_ref, kv_ref, gates_ref, gate_ref, tail_ref,
                 h_ref, xp_ref, *, spt, carry):
    pid = pl.program_id(0)
    tm = x_ref.shape[0]
    n_chunks = tm // CHUNK
    seg = CHUNK if carry else HIST_ROWS + CHUNK

    _norm_mod_store(h_ref, x_ref, g_ref, mod_ref, 1, spt)
    h = h_ref[...]

    def group(name, lo=0, width=None):
        c0, full = PROJ_OFFSETS[name]
        width = full if width is None else width
        return _dot(h, w_ref[:, c0 + lo:c0 + lo + width])

    if carry:
        def _load_hist():
            xp_ref[0:HIST_ROWS, :] = hist_ref[0]

        pl.when(pid == 0)(_load_hist)
    else:
        for c in range(n_chunks):
            xp_ref[c * seg:c * seg + HIST_ROWS, :] = hist_ref[c]

    part = GDN_HEADS * GDN_DK

    def project_part(p):
        raw = group("qkv", p * part, part)
        cols = slice(p * part, (p + 1) * part)
        if carry:
            xp_ref[HIST_ROWS:HIST_ROWS + tm, cols] = raw
        else:
            for c in range(n_chunks):
                xp_ref[c * seg + HIST_ROWS:(c + 1) * seg, cols] = raw[c * CHUNK:(c + 1) * CHUNK, :]

    def conv_part(p):
        for c in range(n_chunks):
            for t in range(p * GDN_HEADS, (p + 1) * GDN_HEADS):
                c0 = t * LANES
                win = xp_ref[c * seg:c * seg + HIST_ROWS + CHUNK, c0:c0 + LANES]
                acc = cw_ref[CONV_W - 1:CONV_W, c0:c0 + LANES] * win[HIST_ROWS:, :]
                for s in range(1, CONV_W):
                    tap = pltpu.roll(win, s, 0)[HIST_ROWS:, :]
                    acc = acc + cw_ref[CONV_W - 1 - s:CONV_W - s, c0:c0 + LANES] * tap
                y = acc * jax.nn.sigmoid(acc)
                if p < 2:
                    y = y * lax.rsqrt(jnp.sum(y * y, axis=-1, keepdims=True) + EPS)
                    if p == 0:
                        y = y * (GDN_DK ** -0.5)
                qkv_ref[c * CHUNK:(c + 1) * CHUNK, c0:c0 + LANES] = y

    def gate_math():
        ba = group("ba")
        ri = lax.broadcasted_iota(jnp.int32, (CHUNK, CHUNK), 0)
        ci = lax.broadcasted_iota(jnp.int32, (CHUNK, CHUNK), 1)
        tri = (ri >= ci).astype(BF16)
        lane = lax.broadcasted_iota(jnp.int32, (CHUNK, LANES), 1)
        a_coef = -jnp.exp(par_ref[0:1, :])
        dt_bias = par_ref[1:2, :]
        for c in range(n_chunks):
            bac = ba[c * CHUNK:(c + 1) * CHUNK, :]
            g_all = a_coef * jax.nn.softplus(bac + dt_bias)
            g_hi = g_all.astype(BF16)
            g_r1 = g_all - g_hi.astype(F32)
            g_mid = g_r1.astype(BF16)
            g_lo = (g_r1 - g_mid.astype(F32)).astype(BF16)
            gc = _dot(tri, g_hi) + _dot(tri, g_mid) + _dot(tri, g_lo)
            gate_ref[c * CHUNK:(c + 1) * CHUNK, :] = jnp.where(lane < GDN_HEADS, jax.nn.sigmoid(bac), gc)

    project_part(0)
    project_part(1)
    conv_part(0)
    project_part(2)
    conv_part(1)
    z_ref[...] = group("z").astype(z_ref.dtype)
    gate_math()
    qb_ref[...] = group("qb").astype(qb_ref.dtype)
    conv_part(2)
    kv_ref[...] = group("kv")
    gates_ref[...] = group("gates").astype(gates_ref.dtype)

    if carry:
        def _store_tail():
            tail_ref[0] = xp_ref[tm:tm + HIST_ROWS, :]

        pl.when(pid == pl.num_programs(0) - 1)(_store_tail)
        xp_ref[0:HIST_ROWS, :] = xp_ref[tm:tm + HIST_ROWS, :]
    else:
        for c in range(n_chunks):
            tail_ref[c] = xp_ref[c * seg + CHUNK:c * seg + CHUNK + HIST_ROWS, :]


def _proj(x, mod, seq_rows, g, w_packed, hist8, conv_w, par):
    n = x.shape[0]
    n_seq = n // seq_rows
    carry = n_seq == 1
    tm, spt = _tile_rows(n, seq_rows, PROJ_ROWS)
    if carry:
        n_hist = 1
        hist_map = lambda i: (0, 0, 0)
        xp_rows = HIST_ROWS + tm
    else:
        assert seq_rows == CHUNK
        n_hist = tm // CHUNK
        hist_map = lambda i: (i, 0, 0)
        xp_rows = n_hist * (HIST_ROWS + CHUNK)
    widths = [CONV_DIM, GDN_HEADS * GDN_DV, SWA_HEADS * SWA_HD, KV_COLS, 2 * D_MODEL, LANES]
    return pl.pallas_call(
        functools.partial(_proj_kernel, spt=spt, carry=carry),
        grid=(n // tm,),
        in_specs=[
            pl.BlockSpec((tm, D_MODEL), lambda i: (i, 0)),
            _mod_spec(mod.shape[1]),
            _const_spec((1, D_MODEL)),
            _const_spec((D_MODEL, PROJ_COLS)),
            pl.BlockSpec((n_hist, HIST_ROWS, CONV_DIM), hist_map),
            _const_spec((CONV_W, CONV_DIM)),
            _const_spec((SUBLANES, LANES)),
        ],
        out_specs=[pl.BlockSpec((tm, w), lambda i: (i, 0)) for w in widths]
        + [pl.BlockSpec((n_hist, HIST_ROWS, CONV_DIM), hist_map)],
        out_shape=[jax.ShapeDtypeStruct((n, w), BF16 if i in (1, 2, 4) else F32) for i, w in enumerate(widths)]
        + [jax.ShapeDtypeStruct((n_seq, HIST_ROWS, CONV_DIM), F32)],
        scratch_shapes=[pltpu.VMEM((tm, D_MODEL), BF16), pltpu.VMEM((xp_rows, CONV_DIM), F32)],
        compiler_params=_cparams(1),
        name="in_proj",
    )(x, mod, g, w_packed, hist8, conv_w, par)


def _unit_lower_inverses(ls, eye, level_masks):
    ts = [eye - jnp.where(level_masks[0], l, 0.0) for l in ls]
    for mask in level_masks[1:]:
        lk = [_split(jnp.where(mask, l, 0.0), INV_PASSES) for l in ls]
        tp = [_split(t, INV_PASSES) for t in ts]
        m1 = [_dot_split(a, b, INV_PASSES) for a, b in zip(lk, tp)]
        m2 = [_dot_split(a, _split(b, INV_PASSES), INV_PASSES) for a, b in zip(tp, m1)]
        ts = [t - m for t, m in zip(ts, m2)]
    return ts


def _gdn_kernel(qkv_ref, gate_ref, s0_ref, o_ref, sout_ref, s_ref, *, n_chunks, carry):
    pid = pl.program_id(0)

    if carry:
        def _load_state():
            s_ref[0] = s0_ref[0]

        pl.when(pid == 0)(_load_state)
    else:
        for c in range(n_chunks):
            s_ref[c] = s0_ref[c]

    ri = lax.broadcasted_iota(jnp.int32, (CHUNK, CHUNK), 0)
    ci = lax.broadcasted_iota(jnp.int32, (CHUNK, CHUNK), 1)
    causal = ri >= ci
    strict = ri > ci
    eye = (ri == ci).astype(F32)
    level_masks = []
    for lvl in range(int(math.log2(CHUNK))):
        same_block = (ri >> (lvl + 1)) == (ci >> (lvl + 1))
        level_masks.append(same_block & (((ri >> lvl) & 1) == 1) & (((ci >> lvl) & 1) == 0))

    items = [(c, h) for c in range(n_chunks) for h in range(GDN_HEADS)]
    gate = [gate_ref[c * CHUNK:(c + 1) * CHUNK, :] for c in range(n_chunks)]
    egc_all = [jnp.exp(g) for g in gate]
    g_last_all = [g[CHUNK - 1:CHUNK, :] for g in gate]
    kdf_all = [jnp.exp(gl - g) for gl, g in zip(g_last_all, gate)]
    eg_last_all = [jnp.exp(gl) for gl in g_last_all]

    def tile(c, t):
        return qkv_ref[c * CHUNK:(c + 1) * CHUNK, t * LANES:(t + 1) * LANES]

    def col(arrs, c, h):
        return arrs[c][:, GDN_HEADS + h:GDN_HEADS + h + 1]

    q = [tile(c, h) for c, h in items]
    k = [tile(c, GDN_HEADS + h) for c, h in items]
    v = [tile(c, 2 * GDN_HEADS + h) for c, h in items]
    beta = [gate[c][:, h:h + 1] for c, h in items]
    gcol = [col(gate, c, h) for c, h in items]
    grow = [jnp.sum(g * eye, axis=0, keepdims=True) for g in gcol]
    decay = [jnp.where(causal, jnp.exp(jnp.where(causal, gc - gr, 0.0)), 0.0) for gc, gr in zip(gcol, grow)]
    kb = [x * b for x, b in zip(k, beta)]
    k16 = [x.astype(BF16) for x in k]
    kq = [_dot_nt(jnp.concatenate([a.astype(BF16), b.astype(BF16)], axis=0), c) for a, b, c in zip(kb, q, k16)]
    kk = [x[0:CHUNK, :] for x in kq]
    qk = [x[CHUNK:, :] for x in kq]
    ls = [jnp.where(strict, a * d, 0.0) for a, d in zip(kk, decay)]
    intra16 = [jnp.where(causal, a * d, 0.0).astype(BF16) for a, d in zip(qk, decay)]
    ts = _unit_lower_inverses(ls, eye, level_masks)
    ys = [_split(t - eye, SOLVE_PASSES) for t in ts]
    egc = [col(egc_all, c, h) for c, h in items]
    vb = [x * b for x, b in zip(v, beta)]
    kbe = [x * e for x, e in zip(kb, egc)]
    rhs = [jnp.concatenate([a, b], axis=1) for a, b in zip(vb, kbe)]
    sol = [x + _dot_split(y, _split(x, SOLVE_PASSES), SOLVE_PASSES) for x, y in zip(rhs, ys)]
    u = [x[:, 0:GDN_DV] for x in sol]
    w16 = [x[:, GDN_DV:].astype(BF16) for x in sol]
    qg16 = [(x * e).astype(BF16) for x, e in zip(q, egc)]
    kd16 = [(x * col(kdf_all, c, h)).astype(BF16) for x, (c, h) in zip(k, items)]
    eg_last = [col(eg_last_all, c, h) for c, h in items]
    wq16 = [jnp.concatenate([a, b], axis=0) for a, b in zip(w16, qg16)]

    o = [None] * len(items)
    if carry:
        state = [s_ref[0, h] for h in range(GDN_HEADS)]
    for c in range(n_chunks):
        idx = [c * GDN_HEADS + h for h in range(GDN_HEADS)]
        if not carry:
            state = [s_ref[c, h] for h in range(GDN_HEADS)]
        s16 = [s.astype(BF16) for s in state]
        ws_qs = [_dot(wq16[i], s) for i, s in zip(idx, s16)]
        v_new = [u[i] - x[0:CHUNK, :] for i, x in zip(idx, ws_qs)]
        vn16 = [x.astype(BF16) for x in v_new]
        for i, x, vn in zip(idx, ws_qs, vn16):
            o[i] = x[CHUNK:, :] + _dot(intra16[i], vn)
        state = [s * eg_last[i] + _dot_tn(kd16[i], vn) for i, s, vn in zip(idx, state, vn16)]
        if not carry:
            for h in range(GDN_HEADS):
                sout_ref[c, h] = state[h]

    for (c, h), oo in zip(items, o):
        o_ref[c * CHUNK:(c + 1) * CHUNK, h * GDN_DV:(h + 1) * GDN_DV] = oo.astype(o_ref.dtype)

    if carry:
        for h in range(GDN_HEADS):
            s_ref[0, h] = state[h]

        def _store_state():
            sout_ref[0] = s_ref[0]

        pl.when(pid == pl.num_programs(0) - 1)(_store_state)


def _gdn(qkv, gate, s0, seq_rows):
    n = qkv.shape[0]
    n_seq = n // seq_rows
    carry = n_seq == 1
    n_chunks = GDN_CHUNKS_PER_STEP
    rows = n_chunks * CHUNK
    assert n % rows == 0
    if carry:
        n_state = 1
        s_map = lambda i: (0, 0, 0, 0)
    else:
        assert seq_rows == CHUNK
        n_state = n_chunks
        s_map = lambda i: (i, 0, 0, 0)
    kern = functools.partial(_gdn_kernel, n_chunks=n_chunks, carry=carry)
    return pl.pallas_call(
        kern,
        grid=(n // rows,),
        in_specs=[
            pl.BlockSpec((rows, CONV_DIM), lambda i: (i, 0)),
            pl.BlockSpec((rows, LANES), lambda i: (i, 0)),
            pl.BlockSpec((n_state, GDN_HEADS, GDN_DK, GDN_DV), s_map),
        ],
        out_specs=[
            pl.BlockSpec((rows, GDN_HEADS * GDN_DV), lambda i: (i, 0)),
            pl.BlockSpec((n_state, GDN_HEADS, GDN_DK, GDN_DV), s_map),
        ],
        out_shape=[
            jax.ShapeDtypeStruct((n, GDN_HEADS * GDN_DV), BF16),
            jax.ShapeDtypeStruct((n_seq, GDN_HEADS, GDN_DK, GDN_DV), F32),
        ],
        scratch_shapes=[pltpu.VMEM((n_state, GDN_HEADS, GDN_DK, GDN_DV), F32)],
        compiler_params=_cparams(1),
        name="gdn",
    )(qkv, gate, s0)


SWA_KEYS = WINDOW + CHUNK
SWA_KEYS_PAD = 2 * LANES
KV_COLS = 2 * SWA_KV_HEADS * SWA_HD


def _bias_kernel(bucket_ref, rb_ref, o_ref):
    bucket = bucket_ref[...]
    h = pl.program_id(0)
    acc = jnp.zeros((CHUNK, SWA_KEYS), F32)
    for b in range(NUM_BUCKETS):
        acc = jnp.where(bucket == b, rb_ref[b * SWA_HEADS + h], acc)
    o_ref[0] = acc


def _bias_table(bucket, rel_bias_flat):
    return pl.pallas_call(
        _bias_kernel,
        grid=(SWA_HEADS,),
        in_specs=[
            pl.BlockSpec((CHUNK, SWA_KEYS), lambda h: (0, 0)),
            pl.BlockSpec(memory_space=pltpu.SMEM),
        ],
        out_specs=pl.BlockSpec((1, CHUNK, SWA_KEYS), lambda h: (h, 0, 0)),
        out_shape=jax.ShapeDtypeStruct((SWA_HEADS, CHUNK, SWA_KEYS), F32),
        compiler_params=_cparams(1),
        name="swa_bias",
    )(bucket, rel_bias_flat)


def _swa_kernel(q_ref, kvc_ref, kvp_ref, bias_ref, sink_ref, o_ref, kf_ref, *, n_chunks, one_sequence):
    n_pad = SWA_KEYS_PAD - SWA_KEYS
    kf_ref[kf_ref.shape[0] - n_pad:, :] = jnp.zeros((n_pad, KV_COLS), F32)
    if one_sequence:
        kf_ref[0:WINDOW, :] = kvp_ref[...]
        kf_ref[WINDOW:WINDOW + n_chunks * CHUNK, :] = kvc_ref[...]
        key_start = [c * CHUNK for c in range(n_chunks)]
        key_pos = lax.broadcasted_iota(jnp.int32, (CHUNK, SWA_KEYS), 1)
        tile_start = pl.program_id(0) * (n_chunks * CHUNK)
        valid = [key_pos + (tile_start + c * CHUNK - WINDOW) >= 0 for c in range(n_chunks)]
    else:
        for c in range(n_chunks):
            kf_ref[c * SWA_KEYS:c * SWA_KEYS + WINDOW, :] = kvp_ref[c * WINDOW:(c + 1) * WINDOW, :]
            kf_ref[c * SWA_KEYS + WINDOW:(c + 1) * SWA_KEYS, :] = kvc_ref[c * CHUNK:(c + 1) * CHUNK, :]
        key_start = [c * SWA_KEYS for c in range(n_chunks)]

    pairs = [(c, kh) for c in range(n_chunks) for kh in range(SWA_KV_HEADS)]
    half = SWA_KV_HEADS * SWA_HD
    keys_t = [kf_ref[key_start[c]:key_start[c] + SWA_KEYS_PAD, 0:half].T.astype(BF16) for c in range(n_chunks)]
    keys = {(c, kh): keys_t[c][kh * SWA_HD:(kh + 1) * SWA_HD, :] for c, kh in pairs}
    ones = jnp.ones((SWA_KEYS, LANES - SWA_HD), BF16)
    vals = {(c, kh): jnp.concatenate(
        [kf_ref[key_start[c]:key_start[c] + SWA_KEYS,
                (SWA_KV_HEADS + kh) * SWA_HD:(SWA_KV_HEADS + kh + 1) * SWA_HD].astype(BF16), ones], axis=1)
            for c, kh in pairs}
    items = [(c, hd) for c in range(n_chunks) for hd in range(SWA_HEADS)]
    q = [q_ref[c * CHUNK:(c + 1) * CHUNK, hd * SWA_HD:(hd + 1) * SWA_HD].astype(BF16) for c, hd in items]
    def group_rows(c, kh):
        return range(c * SWA_HEADS + kh * SWA_GROUP, c * SWA_HEADS + (kh + 1) * SWA_GROUP)

    qk = {(c, kh): _dot(jnp.concatenate([q[i] for i in group_rows(c, kh)], axis=0), keys[(c, kh)])
          for c, kh in pairs}
    logits = [qk[(c, hd // SWA_GROUP)][(hd % SWA_GROUP) * CHUNK:(hd % SWA_GROUP + 1) * CHUNK, 0:SWA_KEYS] + bias_ref[hd]
              for c, hd in items]
    if one_sequence:
        logits = [jnp.where(valid[c], x, -jnp.inf) for x, (c, _) in zip(logits, items)]
    sink = [sink_ref[hd] for _, hd in items]
    m = [jnp.maximum(jnp.max(x, axis=-1, keepdims=True), s) for x, s in zip(logits, sink)]
    p16 = [jnp.exp(x - mm).astype(BF16) for x, mm in zip(logits, m)]
    pv = {(c, kh): _dot(jnp.concatenate([p16[i] for i in group_rows(c, kh)], axis=0), vals[(c, kh)])
          for c, kh in pairs}
    pv = [pv[(c, hd // SWA_GROUP)][(hd % SWA_GROUP) * CHUNK:(hd % SWA_GROUP + 1) * CHUNK, :] for c, hd in items]
    denom = [x[:, SWA_HD:SWA_HD + 1] + jnp.exp(s - mm) for x, s, mm in zip(pv, sink, m)]
    out = [x[:, 0:SWA_HD] / d for x, d in zip(pv, denom)]
    for x, (c, hd) in zip(out, items):
        o_ref[c * CHUNK:(c + 1) * CHUNK, hd * SWA_HD:(hd + 1) * SWA_HD] = x.astype(o_ref.dtype)


def _swa(qb, kv, kv_hist, bias, sinks, seq_rows):
    n = qb.shape[0]
    n_chunks = SWA_CHUNKS_PER_STEP
    rows = n_chunks * CHUNK
    assert n % rows == 0
    one_sequence = kv_hist is None
    if one_sequence:
        assert rows % WINDOW == 0
        per = rows // WINDOW
        prev_arr = kv
        prev_spec = pl.BlockSpec((WINDOW, KV_COLS), lambda i: (jnp.maximum(i * per - 1, 0), 0))
        kf_rows = WINDOW + rows
    else:
        assert seq_rows == CHUNK
        prev_arr = kv_hist
        prev_spec = pl.BlockSpec((n_chunks * WINDOW, KV_COLS), lambda i: (i, 0))
        kf_rows = n_chunks * SWA_KEYS
    kern = functools.partial(_swa_kernel, n_chunks=n_chunks, one_sequence=one_sequence)
    return pl.pallas_call(
        kern,
        grid=(n // rows,),
        in_specs=[
            pl.BlockSpec((rows, SWA_HEADS * SWA_HD), lambda i: (i, 0)),
            pl.BlockSpec((rows, KV_COLS), lambda i: (i, 0)),
            prev_spec,
            pl.BlockSpec((SWA_HEADS, CHUNK, SWA_KEYS), lambda i: (0, 0, 0)),
            pl.BlockSpec(memory_space=pltpu.SMEM),
        ],
        out_specs=pl.BlockSpec((rows, SWA_HEADS * SWA_HD), lambda i: (i, 0)),
        out_shape=jax.ShapeDtypeStruct((n, SWA_HEADS * SWA_HD), BF16),
        scratch_shapes=[pltpu.VMEM((kf_rows + SWA_KEYS_PAD - SWA_KEYS, KV_COLS), F32)],
        compiler_params=_cparams(1),
        name="swa",
    )(qb, kv, prev_arr, bias, sinks)


def _merge_kernel(x_ref, oa_ref, z_ref, ob_ref, gates_ref, mod_ref, nw_ref, wa_ref, wb_ref, wo_ref, o_ref, oa16_ref,
                  *, spt):
    for h in range(GDN_HEADS):
        cols = slice(h * GDN_DV, (h + 1) * GDN_DV)
        oo = oa_ref[:, cols].astype(F32)
        zz = z_ref[:, cols].astype(F32)
        on = oo * lax.rsqrt(jnp.mean(oo * oo, axis=-1, keepdims=True) + EPS) * nw_ref[...] * (zz * jax.nn.sigmoid(zz))
        oa16_ref[:, cols] = on.astype(BF16)
    ya = _dot(oa16_ref[...], wa_ref[...])
    yb = _dot(ob_ref[...].astype(BF16), wb_ref[...])
    merged = (jax.nn.sigmoid(gates_ref[:, 0:D_MODEL].astype(F32)) * ya
              + jax.nn.sigmoid(gates_ref[:, D_MODEL:2 * D_MODEL].astype(F32)) * yb)
    y = _dot(merged.astype(BF16), wo_ref[...])
    rows = x_ref.shape[0] // spt
    for j in range(spt):
        sl = slice(j * rows, (j + 1) * rows)
        o_ref[sl, :] = x_ref[sl, :] + _mod_row(mod_ref, 5, spt, j) * y[sl, :]


def _merge(x, oa, z, ob, gates, mod, seq_rows, norm_w, wa, wb, wo):
    n = x.shape[0]
    tm, spt = _tile_rows(n, seq_rows, 512)
    row_spec = lambda w: pl.BlockSpec((tm, w), lambda i: (i, 0))
    return pl.pallas_call(
        functools.partial(_merge_kernel, spt=spt),
        grid=(n // tm,),
        in_specs=[
            row_spec(D_MODEL), row_spec(D_MODEL), row_spec(D_MODEL), row_spec(D_MODEL), row_spec(2 * D_MODEL),
            _mod_spec(mod.shape[1]),
            _const_spec((1, GDN_DV)),
            _const_spec((D_MODEL, D_MODEL)), _const_spec((D_MODEL, D_MODEL)), _const_spec((D_MODEL, D_MODEL)),
        ],
        out_specs=row_spec(D_MODEL),
        out_shape=jax.ShapeDtypeStruct((n, D_MODEL), F32),
        scratch_shapes=[pltpu.VMEM((tm, GDN_HEADS * GDN_DV), BF16)],
        compiler_params=_cparams(1),
        name="merge",
    )(x, oa, z, ob, gates, mod, norm_w, wa, wb, wo)


def _t5_bucket(rel):
    half = NUM_BUCKETS // 2
    max_exact = half // 2
    n = jnp.abs(rel)
    large = max_exact + (jnp.log(jnp.maximum(n, 1).astype(jnp.float32) / max_exact)
                         / math.log(MAX_DISTANCE / max_exact) * (half - max_exact)).astype(jnp.int32)
    large = jnp.minimum(large, half - 1)
    return jnp.where(rel > 0, half, 0) + jnp.where(n < max_exact, n, large)


def _trunk(x, mod, conv_hist, s_hist, k_hist, v_hist, w):
    bsz, seq, _ = x.shape
    n = bsz * seq
    x2 = x.reshape(n, D_MODEL)
    x1 = _ffn(x2, mod, seq, w["norm_ffn1"], w["ffn1_in"], w["ffn1_out"], w["norm_final"], sub=0, final=False)
    if conv_hist is None:
        hist8 = jnp.zeros((bsz, HIST_ROWS, CONV_DIM), F32)
        s0 = jnp.zeros((bsz, GDN_HEADS, GDN_DK, GDN_DV), F32)
        kv_hist = None
    else:
        hist8 = jnp.concatenate([jnp.zeros((bsz, HIST_ROWS - (CONV_W - 1), CONV_DIM), F32), conv_hist], axis=1)
        s0 = s_hist
        kv_hist = jnp.concatenate([k_hist.reshape(bsz, WINDOW, SWA_KV_HEADS * SWA_HD),
                                   v_hist.reshape(bsz, WINDOW, SWA_KV_HEADS * SWA_HD)], axis=-1)
        kv_hist = kv_hist.reshape(bsz * WINDOW, KV_COLS)
    qkv, z, qb, kv, gates, gate, tail = _proj(x1, mod, seq, w["norm_mix"], w["w_in"], hist8, w["conv_w"], w["gdn_par"])
    oa, s_new = _gdn(qkv, gate, s0, seq)
    ob = _swa(qb, kv, kv_hist, w["bias"], w["sinks"], seq)
    x3 = _merge(x1, oa, z, ob, gates, mod, seq, w["gdn_norm_w"], w["w_a"], w["w_b"], w["w_out"])
    y = _ffn(x3, mod, seq, w["norm_ffn2"], w["ffn2_in"], w["ffn2_out"], w["norm_final"], sub=2, final=True)

    conv_new = tail[:, HIST_ROWS - (CONV_W - 1):]
    half = SWA_KV_HEADS * SWA_HD
    kv3 = kv.reshape(bsz, seq, KV_COLS)
    if kv_hist is None:
        k_new = kv3[:, seq - WINDOW:, :half]
        v_new = kv3[:, seq - WINDOW:, half:]
    else:
        keep = WINDOW - seq
        k_new = jnp.concatenate([k_hist.reshape(bsz, WINDOW, half)[:, WINDOW - keep:], kv3[:, :, :half]], axis=1)
        v_new = jnp.concatenate([v_hist.reshape(bsz, WINDOW, half)[:, WINDOW - keep:], kv3[:, :, half:]], axis=1)
    k_new = k_new.reshape(bsz, WINDOW, SWA_KV_HEADS, SWA_HD)
    v_new = v_new.reshape(bsz, WINDOW, SWA_KV_HEADS, SWA_HD)
    return (y.reshape(bsz, seq, D_MODEL), conv_new[None], s_new[None], k_new[None], v_new[None])


def kernel(x_prompt, x_sample, state_gdn_conv, state_gdn_s, cache_swa_k, cache_swa_v, c_prompt, c_sample,
           norm_ffn1, w_ffn1_in, w_ffn1_out, norm_mix, w_in, gdn_conv_w, gdn_a_log, gdn_dt_bias, gdn_norm_w,
           swa_sinks, rel_bias, w_branch_a, w_branch_b, w_out, norm_ffn2, w_ffn2_in, w_ffn2_out,
           w_ada, b_ada, norm_final):
    bp = x_prompt.shape[0]
    bs = x_sample.shape[0]
    assert bp == 1 and x_sample.shape[1] == CHUNK and cache_swa_k.shape[2] == WINDOW

    n_c = bp + bs
    pad = -n_c % SUBLANES
    c_all = jnp.concatenate([c_prompt, c_sample, jnp.zeros((pad, D_MODEL), F32)], axis=0)
    mod_all = _modulation(c_all, w_ada[0], b_ada[0][None, :])
    mod_p = mod_all[:, :bp]
    mod_s = mod_all[:, bp:n_c]

    w_packed = _pack_w_in(w_in)
    par = jnp.zeros((SUBLANES, LANES), F32)
    par = par.at[0, GDN_HEADS:2 * GDN_HEADS].set(gdn_a_log[0]).at[1, GDN_HEADS:2 * GDN_HEADS].set(gdn_dt_bias[0])

    rel = jnp.arange(SWA_KEYS)[None, :] - WINDOW - jnp.arange(CHUNK)[:, None]
    bias = _bias_table(_t5_bucket(rel).astype(jnp.int32), rel_bias.reshape(-1))

    w = {
        "norm_ffn1": norm_ffn1, "ffn1_in": w_ffn1_in[0].astype(BF16), "ffn1_out": w_ffn1_out[0].astype(BF16),
        "norm_mix": norm_mix, "w_in": w_packed, "conv_w": gdn_conv_w[0], "gdn_par": par,
        "gdn_norm_w": gdn_norm_w, "bias": bias, "sinks": swa_sinks[0],
        "w_a": w_branch_a[0].astype(BF16), "w_b": w_branch_b[0].astype(BF16), "w_out": w_out[0].astype(BF16),
        "norm_ffn2": norm_ffn2, "ffn2_in": w_ffn2_in[0].astype(BF16), "ffn2_out": w_ffn2_out[0].astype(BF16),
        "norm_final": norm_final[None, :],
    }
    y_p, p_conv, p_s, p_k, p_v = _trunk(x_prompt, mod_p, None, None, None, None, w)
    y_s, s_conv, s_s, s_k, s_v = _trunk(x_sample, mod_s, state_gdn_conv[0], state_gdn_s[0],
                                        cache_swa_k[0], cache_swa_v[0], w)
    return (y_p, y_s, p_conv, p_s, p_k, p_v, s_conv, s_s, s_k, s_v)
```

```python
import functools
import math

import jax
import jax.numpy as jnp
from jax import lax
from jax.experimental import pallas as pl
from jax.experimental.pallas import tpu as pltpu

F32 = jnp.float32
BF16 = jnp.bfloat16

D_MODEL = 1024
CHUNK = 64
GDN_HEADS = 8
GDN_DK = 128
GDN_DV = 128
CONV_W = 4
CONV_DIM = GDN_HEADS * (2 * GDN_DK + GDN_DV)
SWA_HEADS = 16
SWA_KV_HEADS = 2
SWA_GROUP = SWA_HEADS // SWA_KV_HEADS
SWA_HD = 64
WINDOW = 128
NUM_BUCKETS = 32
MAX_DISTANCE = 128
D_FF = 2816
N_MOD = 9
EPS = 1e-6

LANES = 128
SUBLANES = 8
VMEM_LIMIT = 56 * 1024 * 1024

FF_CHUNK = D_FF // 2
FFN_ROWS = 1024
HIST_ROWS = SUBLANES
INV_PASSES = 1
SOLVE_PASSES = 1
GDN_CHUNKS_PER_STEP = 4
SWA_CHUNKS_PER_STEP = 4


def _dot(a, b):
    return jnp.dot(a, b, preferred_element_type=F32)


def _dot_nt(a, b):
    return lax.dot_general(a, b, (((1,), (1,)), ((), ())), preferred_element_type=F32)


def _dot_tn(a, b):
    return lax.dot_general(a, b, (((0,), (0,)), ((), ())), preferred_element_type=F32)


def _split(a, passes):
    hi = a.astype(BF16)
    if passes == 1:
        return hi, None
    return hi, (a - hi.astype(F32)).astype(BF16)


def _dot_split(a_parts, b_parts, passes):
    ah, al = a_parts
    bh, bl = b_parts
    out = _dot(ah, bh)
    if passes == 3:
        out = out + _dot(al, bh) + _dot(ah, bl)
    return out


def _cparams(n_grid):
    return pltpu.CompilerParams(dimension_semantics=("arbitrary",) * n_grid, vmem_limit_bytes=VMEM_LIMIT)


def _const_spec(shape):
    nd = len(shape)
    return pl.BlockSpec(shape, lambda i: (0,) * nd, pipeline_mode=pl.Buffered(1))


def _mod_kernel(c_ref, w_ref, b_ref, o_ref):
    c = c_ref[...]
    a = (c * jax.nn.sigmoid(c)).astype(BF16)
    o_ref[0] = _dot(a, w_ref[...].astype(BF16)) + b_ref[...]


def _modulation(c_pad, w_ada, b_ada):
    rows = c_pad.shape[0]
    return pl.pallas_call(
        _mod_kernel,
        grid=(N_MOD,),
        in_specs=[
            pl.BlockSpec((rows, D_MODEL), lambda j: (0, 0)),
            pl.BlockSpec((D_MODEL, D_MODEL), lambda j: (0, j)),
            pl.BlockSpec((1, D_MODEL), lambda j: (0, j)),
        ],
        out_specs=pl.BlockSpec((1, rows, D_MODEL), lambda j: (j, 0, 0)),
        out_shape=jax.ShapeDtypeStruct((N_MOD, rows, D_MODEL), F32),
        compiler_params=_cparams(1),
        name="adaln_mod",
    )(c_pad, w_ada, b_ada)


def _mod_row(mod_ref, idx, spt, j):
    if mod_ref.shape[1] == 1:
        return mod_ref[idx, 0:1, :]
    return mod_ref[idx, pl.ds(pl.program_id(0) * spt + j, 1), :]


def _norm_mod_store(h_ref, x_ref, g_ref, mod_ref, sub, spt):
    rows = x_ref.shape[0] // spt
    g = g_ref[...]
    for j in range(spt):
        xs = x_ref[j * rows:(j + 1) * rows, :]
        ms = jnp.mean(xs * xs, axis=-1, keepdims=True)
        y = xs * lax.rsqrt(ms + EPS) * g
        sh = _mod_row(mod_ref, 3 * sub, spt, j)
        sc = _mod_row(mod_ref, 3 * sub + 1, spt, j)
        h_ref[j * rows:(j + 1) * rows, :] = (y * (1.0 + sc) + sh).astype(h_ref.dtype)


def _tile_rows(n_rows, seq_rows, target):
    if seq_rows >= target:
        assert seq_rows % target == 0
        return target, 1
    assert target % seq_rows == 0
    tm = min(target, n_rows)
    assert n_rows % tm == 0
    return tm, tm // seq_rows


def _mod_spec(n_seq_total):
    return pl.BlockSpec((N_MOD, n_seq_total, D_MODEL), lambda i: (0, 0, 0))


def _ffn_kernel(x_ref, mod_ref, g_ref, w1_ref, w2_ref, gf_ref, o_ref, h_ref, *, sub, final, spt):
    _norm_mod_store(h_ref, x_ref, g_ref, mod_ref, sub, spt)
    h = h_ref[...]
    acc = None
    for c in range(D_FF // FF_CHUNK):
        c0 = c * FF_CHUNK
        gate = _dot(h, w1_ref[:, c0:c0 + FF_CHUNK])
        up = _dot(h, w1_ref[:, D_FF + c0:D_FF + c0 + FF_CHUNK])
        a = (gate * jax.nn.sigmoid(gate) * up).astype(BF16)
        part = _dot(a, w2_ref[c0:c0 + FF_CHUNK, :])
        acc = part if acc is None else acc + part
    rows = x_ref.shape[0] // spt
    for j in range(spt):
        sl = slice(j * rows, (j + 1) * rows)
        ga = _mod_row(mod_ref, 3 * sub + 2, spt, j)
        xn = x_ref[sl, :] + 0.5 * ga * acc[sl, :]
        if final:
            ms = jnp.mean(xn * xn, axis=-1, keepdims=True)
            xn = xn * lax.rsqrt(ms + EPS) * gf_ref[...]
        o_ref[sl, :] = xn


def _ffn(x, mod, seq_rows, g, w1, w2, gf, *, sub, final):
    n = x.shape[0]
    tm, spt = _tile_rows(n, seq_rows, FFN_ROWS)
    kern = functools.partial(_ffn_kernel, sub=sub, final=final, spt=spt)
    return pl.pallas_call(
        kern,
        grid=(n // tm,),
        in_specs=[
            pl.BlockSpec((tm, D_MODEL), lambda i: (i, 0)),
            _mod_spec(mod.shape[1]),
            _const_spec((1, D_MODEL)),
            _const_spec((D_MODEL, 2 * D_FF)),
            _const_spec((D_FF, D_MODEL)),
            _const_spec((1, D_MODEL)),
        ],
        out_specs=pl.BlockSpec((tm, D_MODEL), lambda i: (i, 0)),
        out_shape=jax.ShapeDtypeStruct((n, D_MODEL), F32),
        scratch_shapes=[pltpu.VMEM((tm, D_MODEL), BF16)],
        compiler_params=_cparams(1),
        name="ffn_final" if final else "ffn",
    )(x, mod, g, w1, w2, gf)


PROJ_GROUPS = (("qkv", CONV_DIM), ("z", GDN_HEADS * GDN_DV), ("qb", SWA_HEADS * SWA_HD),
               ("kv", 2 * SWA_KV_HEADS * SWA_HD), ("gates", 2 * D_MODEL), ("ba", LANES))
PROJ_COLS = sum(w for _, w in PROJ_GROUPS)


assert math.log2(SWA_HD) % 2 == 0
W_IN_ALIGNED = CONV_DIM + GDN_HEADS * GDN_DV
W_IN_SMALL = 2 * GDN_HEADS
W_IN_REST = PROJ_COLS - W_IN_ALIGNED - LANES


def _pack_w_in_kernel(w_ref, o_ref):
    o_ref[:, 0:W_IN_ALIGNED] = w_ref[:, 0:W_IN_ALIGNED].astype(BF16)
    hi = W_IN_ALIGNED + W_IN_SMALL
    n_qb = SWA_HEADS * SWA_HD
    o_ref[:, W_IN_ALIGNED:W_IN_ALIGNED + n_qb] = (w_ref[:, hi:hi + n_qb] * (SWA_HD ** -0.5)).astype(BF16)
    o_ref[:, W_IN_ALIGNED + n_qb:W_IN_ALIGNED + W_IN_REST] = w_ref[:, hi + n_qb:hi + W_IN_REST].astype(BF16)
    lane = lax.broadcasted_iota(jnp.int32, (w_ref.shape[0], LANES), 1)
    small = jnp.where(lane < W_IN_SMALL, w_ref[:, W_IN_ALIGNED:W_IN_ALIGNED + LANES], 0.0)
    o_ref[:, W_IN_ALIGNED + W_IN_REST:PROJ_COLS] = small.astype(BF16)


def _pack_w_in(w_in):
    n_cols = W_IN_ALIGNED + W_IN_SMALL + W_IN_REST
    assert w_in.shape == (1, D_MODEL, n_cols)
    rows = 256
    return pl.pallas_call(
        _pack_w_in_kernel,
        grid=(D_MODEL // rows,),
        in_specs=[pl.BlockSpec((None, rows, n_cols), lambda i: (0, i, 0))],
        out_specs=pl.BlockSpec((rows, PROJ_COLS), lambda i: (i, 0)),
        out_shape=jax.ShapeDtypeStruct((D_MODEL, PROJ_COLS), BF16),
        compiler_params=_cparams(1),
        name="pack_w_in",
    )(w_in)


PROJ_OFFSETS = {}
_off = 0
for _name, _width in PROJ_GROUPS:
    PROJ_OFFSETS[_name] = (_off, _width)
    _off += _width
PROJ_ROWS = 512


def _proj_kernel(x_ref, mod_ref, g_ref, w_ref, hist_ref, cw_ref, par_ref,
                 qkv_ref, z_ref, qb_ref, kv_ref, gates_ref, gate_ref, tail_ref,
                 h_ref, xp_ref, *, spt, carry):
    pid = pl.program_id(0)
    tm = x_ref.shape[0]
    n_chunks = tm // CHUNK
    seg = CHUNK if carry else HIST_ROWS + CHUNK

    _norm_mod_store(h_ref, x_ref, g_ref, mod_ref, 1, spt)
    h = h_ref[...]

    def group(name, lo=0, width=None):
        c0, full = PROJ_OFFSETS[name]
        width = full if width is None else width
        return _dot(h, w_ref[:, c0 + lo:c0 + lo + width])

    if carry:
        def _load_hist():
            xp_ref[0:HIST_ROWS, :] = hist_ref[0]

        pl.when(pid == 0)(_load_hist)
    else:
        for c in range(n_chunks):
            xp_ref[c * seg:c * seg + HIST_ROWS, :] = hist_ref[c]

    part = GDN_HEADS * GDN_DK

    def project_part(p):
        raw = group("qkv", p * part, part)
        cols = slice(p * part, (p + 1) * part)
        if carry:
            xp_ref[HIST_ROWS:HIST_ROWS + tm, cols] = raw
        else:
            for c in range(n_chunks):
                xp_ref[c * seg + HIST_ROWS:(c + 1) * seg, cols] = raw[c * CHUNK:(c + 1) * CHUNK, :]

    def conv_part(p):
        for c in range(n_chunks):
            for t in range(p * GDN_HEADS, (p + 1) * GDN_HEADS):
                c0 = t * LANES
                win = xp_ref[c * seg:c * seg + HIST_ROWS + CHUNK, c0:c0 + LANES]
                acc = cw_ref[CONV_W - 1:CONV_W, c0:c0 + LANES] * win[HIST_ROWS:, :]
                for s in range(1, CONV_W):
                    tap = pltpu.roll(win, s, 0)[HIST_ROWS:, :]
                    acc = acc + cw_ref[CONV_W - 1 - s:CONV_W - s, c0:c0 + LANES] * tap
                y = acc * jax.nn.sigmoid(acc)
                if p < 2:
                    y = y * lax.rsqrt(jnp.sum(y * y, axis=-1, keepdims=True) + EPS)
                    if p == 0:
                        y = y * (GDN_DK ** -0.5)
                qkv_ref[c * CHUNK:(c + 1) * CHUNK, c0:c0 + LANES] = y

    def gate_math():
        ba = group("ba")
        ri = lax.broadcasted_iota(jnp.int32, (CHUNK, CHUNK), 0)
        ci = lax.broadcasted_iota(jnp.int32, (CHUNK, CHUNK), 1)
        tri = (ri >= ci).astype(BF16)
        lane = lax.broadcasted_iota(jnp.int32, (CHUNK, LANES), 1)
        a_coef = -jnp.exp(par_ref[0:1, :])
        dt_bias = par_ref[1:2, :]
        for c in range(n_chunks):
            bac = ba[c * CHUNK:(c + 1) * CHUNK, :]
            g_all = a_coef * jax.nn.softplus(bac + dt_bias)
            g_hi = g_all.astype(BF16)
            g_r1 = g_all - g_hi.astype(F32)
            g_mid = g_r1.astype(BF16)
            g_lo = (g_r1 - g_mid.astype(F32)).astype(BF16)
            gc = _dot(tri, g_hi) + _dot(tri, g_mid) + _dot(tri, g_lo)
            gate_ref[c * CHUNK:(c + 1) * CHUNK, :] = jnp.where(lane < GDN_HEADS, jax.nn.sigmoid(bac), gc)

    project_part(0)
    project_part(1)
    conv_part(0)
    project_part(2)
    conv_part(1)
    z_ref[...] = group("z").astype(z_ref.dtype)
    gate_math()
    qb_ref[...] = group("qb").astype(qb_ref.dtype)
    conv_part(2)
    kv_ref[...] = group("kv")
    gates_ref[...] = group("gates").astype(gates_ref.dtype)

    if carry:
        def _store_tail():
            tail_ref[0] = xp_ref[tm:tm + HIST_ROWS, :]

        pl.when(pid == pl.num_programs(0) - 1)(_store_tail)
        xp_ref[0:HIST_ROWS, :] = xp_ref[tm:tm + HIST_ROWS, :]
    else:
        for c in range(n_chunks):
            tail_ref[c] = xp_ref[c * seg + CHUNK:c * seg + CHUNK + HIST_ROWS, :]


def _proj(x, mod, seq_rows, g, w_packed, hist8, conv_w, par):
    n = x.shape[0]
    n_seq = n // seq_rows
    carry = n_seq == 1
    tm, spt = _tile_rows(n, seq_rows, PROJ_ROWS)
    if carry:
        n_hist = 1
        hist_map = lambda i: (0, 0, 0)
        xp_rows = HIST_ROWS + tm
    else:
        assert seq_rows == CHUNK
        n_hist = tm // CHUNK
        hist_map = lambda i: (i, 0, 0)
        xp_rows = n_hist * (HIST_ROWS + CHUNK)
    widths = [CONV_DIM, GDN_HEADS * GDN_DV, SWA_HEADS * SWA_HD, KV_COLS, 2 * D_MODEL, LANES]
    return pl.pallas_call(
        functools.partial(_proj_kernel, spt=spt, carry=carry),
        grid=(n // tm,),
        in_specs=[
            pl.BlockSpec((tm, D_MODEL), lambda i: (i, 0)),
            _mod_spec(mod.shape[1]),
            _const_spec((1, D_MODEL)),
            _const_spec((D_MODEL, PROJ_COLS)),
            pl.BlockSpec((n_hist, HIST_ROWS, CONV_DIM), hist_map),
            _const_spec((CONV_W, CONV_DIM)),
            _const_spec((SUBLANES, LANES)),
        ],
        out_specs=[pl.BlockSpec((tm, w), lambda i: (i, 0)) for w in widths]
        + [pl.BlockSpec((n_hist, HIST_ROWS, CONV_DIM), hist_map)],
        out_shape=[jax.ShapeDtypeStruct((n, w), BF16 if i in (1, 2, 4) else F32) for i, w in enumerate(widths)]
        + [jax.ShapeDtypeStruct((n_seq, HIST_ROWS, CONV_DIM), F32)],
        scratch_shapes=[pltpu.VMEM((tm, D_MODEL), BF16), pltpu.VMEM((xp_rows, CONV_DIM), F32)],
        compiler_params=_cparams(1),
        name="in_proj",
    )(x, mod, g, w_packed, hist8, conv_w, par)


def _unit_lower_inverses(ls, eye, level_masks):
    ts = [eye - jnp.where(level_masks[0], l, 0.0) for l in ls]
    for mask in level_masks[1:]:
        lk = [_split(jnp.where(mask, l, 0.0), INV_PASSES) for l in ls]
        tp = [_split(t, INV_PASSES) for t in ts]
        m1 = [_dot_split(a, b, INV_PASSES) for a, b in zip(lk, tp)]
        m2 = [_dot_split(a, _split(b, INV_PASSES), INV_PASSES) for a, b in zip(tp, m1)]
        ts = [t - m for t, m in zip(ts, m2)]
    return ts


def _gdn_kernel(qkv_ref, gate_ref, s0_ref, o_ref, sout_ref, s_ref, *, n_chunks, carry):
    pid = pl.program_id(0)

    if carry:
        def _load_state():
            s_ref[0] = s0_ref[0]

        pl.when(pid == 0)(_load_state)
    else:
        for c in range(n_chunks):
            s_ref[c] = s0_ref[c]

    ri = lax.broadcasted_iota(jnp.int32, (CHUNK, CHUNK), 0)
    ci = lax.broadcasted_iota(jnp.int32, (CHUNK, CHUNK), 1)
    causal = ri >= ci
    strict = ri > ci
    eye = (ri == ci).astype(F32)
    level_masks = []
    for lvl in range(int(math.log2(CHUNK))):
        same_block = (ri >> (lvl + 1)) == (ci >> (lvl + 1))
        level_masks.append(same_block & (((ri >> lvl) & 1) == 1) & (((ci >> lvl) & 1) == 0))

    items = [(c, h) for c in range(n_chunks) for h in range(GDN_HEADS)]
    gate = [gate_ref[c * CHUNK:(c + 1) * CHUNK, :] for c in range(n_chunks)]
    egc_all = [jnp.exp(g) for g in gate]
    g_last_all = [g[CHUNK - 1:CHUNK, :] for g in gate]
    kdf_all = [jnp.exp(gl - g) for gl, g in zip(g_last_all, gate)]
    eg_last_all = [jnp.exp(gl) for gl in g_last_all]

    def tile(c, t):
        return qkv_ref[c * CHUNK:(c + 1) * CHUNK, t * LANES:(t + 1) * LANES]

    def col(arrs, c, h):
        return arrs[c][:, GDN_HEADS + h:GDN_HEADS + h + 1]

    q = [tile(c, h) for c, h in items]
    k = [tile(c, GDN_HEADS + h) for c, h in items]
    v = [tile(c, 2 * GDN_HEADS + h) for c, h in items]
    beta = [gate[c][:, h:h + 1] for c, h in items]
    gcol = [col(gate, c, h) for c, h in items]
    grow = [jnp.sum(g * eye, axis=0, keepdims=True) for g in gcol]
    decay = [jnp.where(causal, jnp.exp(jnp.where(causal, gc - gr, 0.0)), 0.0) for gc, gr in zip(gcol, grow)]
    kb = [x * b for x, b in zip(k, beta)]
    k16 = [x.astype(BF16) for x in k]
    kq = [_dot_nt(jnp.concatenate([a.astype(BF16), b.astype(BF16)], axis=0), c) for a, b, c in zip(kb, q, k16)]
    kk = [x[0:CHUNK, :] for x in kq]
    qk = [x[CHUNK:, :] for x in kq]
    ls = [jnp.where(strict, a * d, 0.0) for a, d in zip(kk, decay)]
    intra16 = [jnp.where(causal, a * d, 0.0).astype(BF16) for a, d in zip(qk, decay)]
    ts = _unit_lower_inverses(ls, eye, level_masks)
    ys = [_split(t - eye, SOLVE_PASSES) for t in ts]
    egc = [col(egc_all, c, h) for c, h in items]
    vb = [x * b for x, b in zip(v, beta)]
    kbe = [x * e for x, e in zip(kb, egc)]
    rhs = [jnp.concatenate([a, b], axis=1) for a, b in zip(vb, kbe)]
    sol = [x + _dot_split(y, _split(x, SOLVE_PASSES), SOLVE_PASSES) for x, y in zip(rhs, ys)]
    u = [x[:, 0:GDN_DV] for x in sol]
    w16 = [x[:, GDN_DV:].astype(BF16) for x in sol]
    qg16 = [(x * e).astype(BF16) for x, e in zip(q, egc)]
    kd16 = [(x * col(kdf_all, c, h)).astype(BF16) for x, (c, h) in zip(k, items)]
    eg_last = [col(eg_last_all, c, h) for c, h in items]
    wq16 = [jnp.concatenate([a, b], axis=0) for a, b in zip(w16, qg16)]

    o = [None] * len(items)
    if carry:
        state = [s_ref[0, h] for h in range(GDN_HEADS)]
    for c in range(n_chunks):
        idx = [c * GDN_HEADS + h for h in range(GDN_HEADS)]
        if not carry:
            state = [s_ref[c, h] for h in range(GDN_HEADS)]
        s16 = [s.astype(BF16) for s in state]
        ws_qs = [_dot(wq16[i], s) for i, s in zip(idx, s16)]
        v_new = [u[i] - x[0:CHUNK, :] for i, x in zip(idx, ws_qs)]
        vn16 = [x.astype(BF16) for x in v_new]
        for i, x, vn in zip(idx, ws_qs, vn16):
            o[i] = x[CHUNK:, :] + _dot(intra16[i], vn)
        state = [s * eg_last[i] + _dot_tn(kd16[i], vn) for i, s, vn in zip(idx, state, vn16)]
        if not carry:
            for h in range(GDN_HEADS):
                sout_ref[c, h] = state[h]

    for (c, h), oo in zip(items, o):
        o_ref[c * CHUNK:(c + 1) * CHUNK, h * GDN_DV:(h + 1) * GDN_DV] = oo.astype(o_ref.dtype)

    if carry:
        for h in range(GDN_HEADS):
            s_ref[0, h] = state[h]

        def _store_state():
            sout_ref[0] = s_ref[0]

        pl.when(pid == pl.num_programs(0) - 1)(_store_state)


def _gdn(qkv, gate, s0, seq_rows):
    n = qkv.shape[0]
    n_seq = n // seq_rows
    carry = n_seq == 1
    n_chunks = GDN_CHUNKS_PER_STEP
    rows = n_chunks * CHUNK
    assert n % rows == 0
    if carry:
        n_state = 1
        s_map = lambda i: (0, 0, 0, 0)
    else:
        assert seq_rows == CHUNK
        n_state = n_chunks
        s_map = lambda i: (i, 0, 0, 0)
    kern = functools.partial(_gdn_kernel, n_chunks=n_chunks, carry=carry)
    return pl.pallas_call(
        kern,
        grid=(n // rows,),
        in_specs=[
            pl.BlockSpec((rows, CONV_DIM), lambda i: (i, 0)),
            pl.BlockSpec((rows, LANES), lambda i: (i, 0)),
            pl.BlockSpec((n_state, GDN_HEADS, GDN_DK, GDN_DV), s_map),
        ],
        out_specs=[
            pl.BlockSpec((rows, GDN_HEADS * GDN_DV), lambda i: (i, 0)),
            pl.BlockSpec((n_state, GDN_HEADS, GDN_DK, GDN_DV), s_map),
        ],
        out_shape=[
            jax.ShapeDtypeStruct((n, GDN_HEADS * GDN_DV), BF16),
            jax.ShapeDtypeStruct((n_seq, GDN_HEADS, GDN_DK, GDN_DV), F32),
        ],
        scratch_shapes=[pltpu.VMEM((n_state, GDN_HEADS, GDN_DK, GDN_DV), F32)],
        compiler_params=_cparams(1),
        name="gdn",
    )(qkv, gate, s0)


SWA_KEYS = WINDOW + CHUNK
SWA_KEYS_PAD = 2 * LANES
KV_COLS = 2 * SWA_KV_HEADS * SWA_HD


def _bias_kernel(bucket_ref, rb_ref, o_ref):
    bucket = bucket_ref[...]
    h = pl.program_id(0)
    acc = jnp.zeros((CHUNK, SWA_KEYS), F32)
    for b in range(NUM_BUCKETS):
        acc = jnp.where(bucket == b, rb_ref[b * SWA_HEADS + h], acc)
    o_ref[0] = acc


def _bias_table(bucket, rel_bias_flat):
    return pl.pallas_call(
        _bias_kernel,
        grid=(SWA_HEADS,),
        in_specs=[
            pl.BlockSpec((CHUNK, SWA_KEYS), lambda h: (0, 0)),
            pl.BlockSpec(memory_space=pltpu.SMEM),
        ],
        out_specs=pl.BlockSpec((1, CHUNK, SWA_KEYS), lambda h: (h, 0, 0)),
        out_shape=jax.ShapeDtypeStruct((SWA_HEADS, CHUNK, SWA_KEYS), F32),
        compiler_params=_cparams(1),
        name="swa_bias",
    )(bucket, rel_bias_flat)


def _swa_kernel(q_ref, kvc_ref, kvp_ref, bias_ref, sink_ref, o_ref, kf_ref, *, n_chunks, one_sequence):
    n_pad = SWA_KEYS_PAD - SWA_KEYS
    kf_ref[kf_ref.shape[0] - n_pad:, :] = jnp.zeros((n_pad, KV_COLS), F32)
    if one_sequence:
        kf_ref[0:WINDOW, :] = kvp_ref[...]
        kf_ref[WINDOW:WINDOW + n_chunks * CHUNK, :] = kvc_ref[...]
        key_start = [c * CHUNK for c in range(n_chunks)]
        key_pos = lax.broadcasted_iota(jnp.int32, (CHUNK, SWA_KEYS), 1)
        tile_start = pl.program_id(0) * (n_chunks * CHUNK)
        valid = [key_pos + (tile_start + c * CHUNK - WINDOW) >= 0 for c in range(n_chunks)]
    else:
        for c in range(n_chunks):
            kf_ref[c * SWA_KEYS:c * SWA_KEYS + WINDOW, :] = kvp_ref[c * WINDOW:(c + 1) * WINDOW, :]
            kf_ref[c * SWA_KEYS + WINDOW:(c + 1) * SWA_KEYS, :] = kvc_ref[c * CHUNK:(c + 1) * CHUNK, :]
        key_start = [c * SWA_KEYS for c in range(n_chunks)]

    pairs = [(c, kh) for c in range(n_chunks) for kh in range(SWA_KV_HEADS)]
    half = SWA_KV_HEADS * SWA_HD
    keys_t = [kf_ref[key_start[c]:key_start[c] + SWA_KEYS_PAD, 0:half].T.astype(BF16) for c in range(n_chunks)]
    keys = {(c, kh): keys_t[c][kh * SWA_HD:(kh + 1) * SWA_HD, :] for c, kh in pairs}
    ones = jnp.ones((SWA_KEYS, LANES - SWA_HD), BF16)
    vals = {(c, kh): jnp.concatenate(
        [kf_ref[key_start[c]:key_start[c] + SWA_KEYS,
                (SWA_KV_HEADS + kh) * SWA_HD:(SWA_KV_HEADS + kh + 1) * SWA_HD].astype(BF16), ones], axis=1)
            for c, kh in pairs}
    items = [(c, hd) for c in range(n_chunks) for hd in range(SWA_HEADS)]
    q = [q_ref[c * CHUNK:(c + 1) * CHUNK, hd * SWA_HD:(hd + 1) * SWA_HD].astype(BF16) for c, hd in items]
    def group_rows(c, kh):
        return range(c * SWA_HEADS + kh * SWA_GROUP, c * SWA_HEADS + (kh + 1) * SWA_GROUP)

    qk = {(c, kh): _dot(jnp.concatenate([q[i] for i in group_rows(c, kh)], axis=0), keys[(c, kh)])
          for c, kh in pairs}
    logits = [qk[(c, hd // SWA_GROUP)][(hd % SWA_GROUP) * CHUNK:(hd % SWA_GROUP + 1) * CHUNK, 0:SWA_KEYS] + bias_ref[hd]
              for c, hd in items]
    if one_sequence:
        logits = [jnp.where(valid[c], x, -jnp.inf) for x, (c, _) in zip(logits, items)]
    sink = [sink_ref[hd] for _, hd in items]
    m = [jnp.maximum(jnp.max(x, axis=-1, keepdims=True), s) for x, s in zip(logits, sink)]
    p16 = [jnp.exp(x - mm).astype(BF16) for x, mm in zip(logits, m)]
    pv = {(c, kh): _dot(jnp.concatenate([p16[i] for i in group_rows(c, kh)], axis=0), vals[(c, kh)])
          for c, kh in pairs}
    pv = [pv[(c, hd // SWA_GROUP)][(hd % SWA_GROUP) * CHUNK:(hd % SWA_GROUP + 1) * CHUNK, :] for c, hd in items]
    denom = [x[:, SWA_HD:SWA_HD + 1] + jnp.exp(s - mm) for x, s, mm in zip(pv, sink, m)]
    out = [x[:, 0:SWA_HD] / d for x, d in zip(pv, denom)]
    for x, (c, hd) in zip(out, items):
        o_ref[c * CHUNK:(c + 1) * CHUNK, hd * SWA_HD:(hd + 1) * SWA_HD] = x.astype(o_ref.dtype)


def _swa(qb, kv, kv_hist, bias, sinks, seq_rows):
    n = qb.shape[0]
    n_chunks = SWA_CHUNKS_PER_STEP
    rows = n_chunks * CHUNK
    assert n % rows == 0
    one_sequence = kv_hist is None
    if one_sequence:
        assert rows % WINDOW == 0
        per = rows // WINDOW
        prev_arr = kv
        prev_spec = pl.BlockSpec((WINDOW, KV_COLS), lambda i: (jnp.maximum(i * per - 1, 0), 0))
        kf_rows = WINDOW + rows
    else:
        assert seq_rows == CHUNK
        prev_arr = kv_hist
        prev_spec = pl.BlockSpec((n_chunks * WINDOW, KV_COLS), lambda i: (i, 0))
        kf_rows = n_chunks * SWA_KEYS
    kern = functools.partial(_swa_kernel, n_chunks=n_chunks, one_sequence=one_sequence)
    return pl.pallas_call(
        kern,
        grid=(n // rows,),
        in_specs=[
            pl.BlockSpec((rows, SWA_HEADS * SWA_HD), lambda i: (i, 0)),
            pl.BlockSpec((rows, KV_COLS), lambda i: (i, 0)),
            prev_spec,
            pl.BlockSpec((SWA_HEADS, CHUNK, SWA_KEYS), lambda i: (0, 0, 0)),
            pl.BlockSpec(memory_space=pltpu.SMEM),
        ],
        out_specs=pl.BlockSpec((rows, SWA_HEADS * SWA_HD), lambda i: (i, 0)),
        out_shape=jax.ShapeDtypeStruct((n, SWA_HEADS * SWA_HD), BF16),
        scratch_shapes=[pltpu.VMEM((kf_rows + SWA_KEYS_PAD - SWA_KEYS, KV_COLS), F32)],
        compiler_params=_cparams(1),
        name="swa",
    )(qb, kv, prev_arr, bias, sinks)


def _merge_kernel(x_ref, oa_ref, z_ref, ob_ref, gates_ref, mod_ref, nw_ref, wa_ref, wb_ref, wo_ref, o_ref, oa16_ref,
                  *, spt):
    for h in range(GDN_HEADS):
        cols = slice(h * GDN_DV, (h + 1) * GDN_DV)
        oo = oa_ref[:, cols].astype(F32)
        zz = z_ref[:, cols].astype(F32)
        on = oo * lax.rsqrt(jnp.mean(oo * oo, axis=-1, keepdims=True) + EPS) * nw_ref[...] * (zz * jax.nn.sigmoid(zz))
        oa16_ref[:, cols] = on.astype(BF16)
    ya = _dot(oa16_ref[...], wa_ref[...])
    yb = _dot(ob_ref[...].astype(BF16), wb_ref[...])
    merged = (jax.nn.sigmoid(gates_ref[:, 0:D_MODEL].astype(F32)) * ya
              + jax.nn.sigmoid(gates_ref[:, D_MODEL:2 * D_MODEL].astype(F32)) * yb)
    y = _dot(merged.astype(BF16), wo_ref[...])
    rows = x_ref.shape[0] // spt
    for j in range(spt):
        sl = slice(j * rows, (j + 1) * rows)
        o_ref[sl, :] = x_ref[sl, :] + _mod_row(mod_ref, 5, spt, j) * y[sl, :]


def _merge(x, oa, z, ob, gates, mod, seq_rows, norm_w, wa, wb, wo):
    n = x.shape[0]
    tm, spt = _tile_rows(n, seq_rows, 512)
    row_spec = lambda w: pl.BlockSpec((tm, w), lambda i: (i, 0))
    return pl.pallas_call(
        functools.partial(_merge_kernel, spt=spt),
        grid=(n // tm,),
        in_specs=[
            row_spec(D_MODEL), row_spec(D_MODEL), row_spec(D_MODEL), row_spec(D_MODEL), row_spec(2 * D_MODEL),
            _mod_spec(mod.shape[1]),
            _const_spec((1, GDN_DV)),
            _const_spec((D_MODEL, D_MODEL)), _const_spec((D_MODEL, D_MODEL)), _const_spec((D_MODEL, D_MODEL)),
        ],
        out_specs=row_spec(D_MODEL),
        out_shape=jax.ShapeDtypeStruct((n, D_MODEL), F32),
        scratch_shapes=[pltpu.VMEM((tm, GDN_HEADS * GDN_DV), BF16)],
        compiler_params=_cparams(1),
        name="merge",
    )(x, oa, z, ob, gates, mod, norm_w, wa, wb, wo)


def _t5_bucket(rel):
    half = NUM_BUCKETS // 2
    max_exact = half // 2
    n = jnp.abs(rel)
    large = max_exact + (jnp.log(jnp.maximum(n, 1).astype(jnp.float32) / max_exact)
                         / math.log(MAX_DISTANCE / max_exact) * (half - max_exact)).astype(jnp.int32)
    large = jnp.minimum(large, half - 1)
    return jnp.where(rel > 0, half, 0) + jnp.where(n < max_exact, n, large)


def _trunk(x, mod, conv_hist, s_hist, k_hist, v_hist, w):
    bsz, seq, _ = x.shape
    n = bsz * seq
    x2 = x.reshape(n, D_MODEL)
    x1 = _ffn(x2, mod, seq, w["norm_ffn1"], w["ffn1_in"], w["ffn1_out"], w["norm_final"], sub=0, final=False)
    if conv_hist is None:
        hist8 = jnp.zeros((bsz, HIST_ROWS, CONV_DIM), F32)
        s0 = jnp.zeros((bsz, GDN_HEADS, GDN_DK, GDN_DV), F32)
        kv_hist = None
    else:
        hist8 = jnp.concatenate([jnp.zeros((bsz, HIST_ROWS - (CONV_W - 1), CONV_DIM), F32), conv_hist], axis=1)
        s0 = s_hist
        kv_hist = jnp.concatenate([k_hist.reshape(bsz, WINDOW, SWA_KV_HEADS * SWA_HD),
                                   v_hist.reshape(bsz, WINDOW, SWA_KV_HEADS * SWA_HD)], axis=-1)
        kv_hist = kv_hist.reshape(bsz * WINDOW, KV_COLS)
    qkv, z, qb, kv, gates, gate, tail = _proj(x1, mod, seq, w["norm_mix"], w["w_in"], hist8, w["conv_w"], w["gdn_par"])
    oa, s_new = _gdn(qkv, gate, s0, seq)
    ob = _swa(qb, kv, kv_hist, w["bias"], w["sinks"], seq)
    x3 = _merge(x1, oa, z, ob, gates, mod, seq, w["gdn_norm_w"], w["w_a"], w["w_b"], w["w_out"])
    y = _ffn(x3, mod, seq, w["norm_ffn2"], w["ffn2_in"], w["ffn2_out"], w["norm_final"], sub=2, final=True)

    conv_new = tail[:, HIST_ROWS - (CONV_W - 1):]
    half = SWA_KV_HEADS * SWA_HD
    kv3 = kv.reshape(bsz, seq, KV_COLS)
    if kv_hist is None:
        k_new = kv3[:, seq - WINDOW:, :half]
        v_new = kv3[:, seq - WINDOW:, half:]
    else:
        keep = WINDOW - seq
        k_new = jnp.concatenate([k_hist.reshape(bsz, WINDOW, half)[:, WINDOW - keep:], kv3[:, :, :half]], axis=1)
        v_new = jnp.concatenate([v_hist.reshape(bsz, WINDOW, half)[:, WINDOW - keep:], kv3[:, :, half:]], axis=1)
    k_new = k_new.reshape(bsz, WINDOW, SWA_KV_HEADS, SWA_HD)
    v_new = v_new.reshape(bsz, WINDOW, SWA_KV_HEADS, SWA_HD)
    return (y.reshape(bsz, seq, D_MODEL), conv_new[None], s_new[None], k_new[None], v_new[None])


def kernel(x_prompt, x_sample, state_gdn_conv, state_gdn_s, cache_swa_k, cache_swa_v, c_prompt, c_sample,
           norm_ffn1, w_ffn1_in, w_ffn1_out, norm_mix, w_in, gdn_conv_w, gdn_a_log, gdn_dt_bias, gdn_norm_w,
           swa_sinks, rel_bias, w_branch_a, w_branch_b, w_out, norm_ffn2, w_ffn2_in, w_ffn2_out,
           w_ada, b_ada, norm_final):
    bp = x_prompt.shape[0]
    bs = x_sample.shape[0]
    assert bp == 1 and x_sample.shape[1] == CHUNK and cache_swa_k.shape[2] == WINDOW

    n_c = bp + bs
    pad = -n_c % SUBLANES
    c_all = jnp.concatenate([c_prompt, c_sample, jnp.zeros((pad, D_MODEL), F32)], axis=0)
    mod_all = _modulation(c_all, w_ada[0], b_ada[0][None, :])
    mod_p = mod_all[:, :bp]
    mod_s = mod_all[:, bp:n_c]

    w_packed = _pack_w_in(w_in)
    par = jnp.zeros((SUBLANES, LANES), F32)
    par = par.at[0, GDN_HEADS:2 * GDN_HEADS].set(gdn_a_log[0]).at[1, GDN_HEADS:2 * GDN_HEADS].set(gdn_dt_bias[0])

    rel = jnp.arange(SWA_KEYS)[None, :] - WINDOW - jnp.arange(CHUNK)[:, None]
    bias = _bias_table(_t5_bucket(rel).astype(jnp.int32), rel_bias.reshape(-1))

    w = {
        "norm_ffn1": norm_ffn1, "ffn1_in": w_ffn1_in[0].astype(BF16), "ffn1_out": w_ffn1_out[0].astype(BF16),
        "norm_mix": norm_mix, "w_in": w_packed, "conv_w": gdn_conv_w[0], "gdn_par": par,
        "gdn_norm_w": gdn_norm_w, "bias": bias, "sinks": swa_sinks[0],
        "w_a": w_branch_a[0].astype(BF16), "w_b": w_branch_b[0].astype(BF16), "w_out": w_out[0].astype(BF16),
        "norm_ffn2": norm_ffn2, "ffn2_in": w_ffn2_in[0].astype(BF16), "ffn2_out": w_ffn2_out[0].astype(BF16),
        "norm_final": norm_final[None, :],
    }
    y_p, p_conv, p_s, p_k, p_v = _trunk(x_prompt, mod_p, None, None, None, None, w)
    y_s, s_conv, s_s, s_k, s_v = _trunk(x_sample, mod_s, state_gdn_conv[0], state_gdn_s[0],
                                        cache_swa_k[0], cache_swa_v[0], w)
    return (y_p, y_s, p_conv, p_s, p_k, p_v, s_conv, s_s, s_k, s_v)
```

```python
import functools
import math

import jax
import jax.numpy as jnp
from jax import lax
from jax.experimental import pallas as pl
from jax.experimental.pallas import tpu as pltpu

F32 = jnp.float32
BF16 = jnp.bfloat16

D_MODEL = 1024
CHUNK = 64
GDN_HEADS = 8
GDN_DK = 128
GDN_DV = 128
CONV_W = 4
CONV_DIM = GDN_HEADS * (2 * GDN_DK + GDN_DV)
SWA_HEADS = 16
SWA_KV_HEADS = 2
SWA_GROUP = SWA_HEADS // SWA_KV_HEADS
SWA_HD = 64
WINDOW = 128
NUM_BUCKETS = 32
MAX_DISTANCE = 128
D_FF = 2816
N_MOD = 9
EPS = 1e-6

LANES = 128
SUBLANES = 8
VMEM_LIMIT = 56 * 1024 * 1024

FF_CHUNK = D_FF // 2
FFN_ROWS = 1024
MERGE_ROWS = 1024
HIST_ROWS = SUBLANES
INV_PASSES = 1
SOLVE_PASSES = 1
GDN_CHUNKS_PER_STEP = 4
PREP_SKEW = 5
SWA_CHUNKS_PER_STEP = 4


def _dot(a, b):
    return jnp.dot(a, b, preferred_element_type=F32)


def _dot_nt(a, b):
    return lax.dot_general(a, b, (((1,), (1,)), ((), ())), preferred_element_type=F32)


def _dot_tn(a, b):
    return lax.dot_general(a, b, (((0,), (0,)), ((), ())), preferred_element_type=F32)


def _split(a, passes):
    hi = a.astype(BF16)
    if passes == 1:
        return hi, None
    return hi, (a - hi.astype(F32)).astype(BF16)


def _dot_split(a_parts, b_parts, passes):
    ah, al = a_parts
    bh, bl = b_parts
    out = _dot(ah, bh)
    if passes == 3:
        out = out + _dot(al, bh) + _dot(ah, bl)
    return out


def _cparams(n_grid):
    return pltpu.CompilerParams(dimension_semantics=("arbitrary",) * n_grid, vmem_limit_bytes=VMEM_LIMIT)


def _const_spec(shape):
    nd = len(shape)
    return pl.BlockSpec(shape, lambda i: (0,) * nd, pipeline_mode=pl.Buffered(1))


def _mod_kernel(c_ref, w_ref, b_ref, o_ref):
    c = c_ref[...]
    a = (c * jax.nn.sigmoid(c)).astype(BF16)
    o_ref[0] = _dot(a, w_ref[...].astype(BF16)) + b_ref[...]


def _modulation(c_pad, w_ada, b_ada):
    rows = c_pad.shape[0]
    return pl.pallas_call(
        _mod_kernel,
        grid=(N_MOD,),
        in_specs=[
            pl.BlockSpec((rows, D_MODEL), lambda j: (0, 0)),
            pl.BlockSpec((D_MODEL, D_MODEL), lambda j: (0, j)),
            pl.BlockSpec((1, D_MODEL), lambda j: (0, j)),
        ],
        out_specs=pl.BlockSpec((1, rows, D_MODEL), lambda j: (j, 0, 0)),
        out_shape=jax.ShapeDtypeStruct((N_MOD, rows, D_MODEL), F32),
        compiler_params=_cparams(1),
        name="adaln_mod",
    )(c_pad, w_ada, b_ada)


def _mod_row(mod_ref, idx, spt, j):
    if mod_ref.shape[1] == 1:
        return mod_ref[idx, 0:1, :]
    return mod_ref[idx, pl.ds(pl.program_id(0) * spt + j, 1), :]


def _norm_mod_store(h_ref, x_ref, g_ref, mod_ref, sub, spt):
    rows = x_ref.shape[0] // spt
    g = g_ref[...]
    for j in range(spt):
        xs = x_ref[j * rows:(j + 1) * rows, :]
        ms = jnp.mean(xs * xs, axis=-1, keepdims=True)
        y = xs * lax.rsqrt(ms + EPS) * g
        sh = _mod_row(mod_ref, 3 * sub, spt, j)
        sc = _mod_row(mod_ref, 3 * sub + 1, spt, j)
        h_ref[j * rows:(j + 1) * rows, :] = (y * (1.0 + sc) + sh).astype(h_ref.dtype)


def _tile_rows(n_rows, seq_rows, target):
    if seq_rows >= target:
        assert seq_rows % target == 0
        return target, 1
    assert target % seq_rows == 0
    tm = min(target, n_rows)
    assert n_rows % tm == 0
    return tm, tm // seq_rows


def _mod_spec(n_seq_total):
    return pl.BlockSpec((N_MOD, n_seq_total, D_MODEL), lambda i: (0, 0, 0))


def _ffn_kernel(x_ref, mod_ref, g_ref, w1_ref, w2_ref, gf_ref, o_ref, h_ref, *, sub, final, spt):
    _norm_mod_store(h_ref, x_ref, g_ref, mod_ref, sub, spt)
    h = h_ref[...]
    acc = None
    for c in range(D_FF // FF_CHUNK):
        c0 = c * FF_CHUNK
        gate = _dot(h, w1_ref[:, c0:c0 + FF_CHUNK])
        up = _dot(h, w1_ref[:, D_FF + c0:D_FF + c0 + FF_CHUNK])
        a = (gate * jax.nn.sigmoid(gate) * up).astype(BF16)
        part = _dot(a, w2_ref[c0:c0 + FF_CHUNK, :])
        acc = part if acc is None else acc + part
    rows = x_ref.shape[0] // spt
    for j in range(spt):
        sl = slice(j * rows, (j + 1) * rows)
        ga = _mod_row(mod_ref, 3 * sub + 2, spt, j)
        xn = x_ref[sl, :] + 0.5 * ga * acc[sl, :]
        if final:
            ms = jnp.mean(xn * xn, axis=-1, keepdims=True)
            xn = xn * lax.rsqrt(ms + EPS) * gf_ref[...]
        o_ref[sl, :] = xn


def _ffn(x, mod, seq_rows, g, w1, w2, gf, *, sub, final):
    n = x.shape[0]
    tm, spt = _tile_rows(n, seq_rows, FFN_ROWS)
    kern = functools.partial(_ffn_kernel, sub=sub, final=final, spt=spt)
    return pl.pallas_call(
        kern,
        grid=(n // tm,),
        in_specs=[
            pl.BlockSpec((tm, D_MODEL), lambda i: (i, 0)),
            _mod_spec(mod.shape[1]),
            _const_spec((1, D_MODEL)),
            _const_spec((D_MODEL, 2 * D_FF)),
            _const_spec((D_FF, D_MODEL)),
            _const_spec((1, D_MODEL)),
        ],
        out_specs=pl.BlockSpec((tm, D_MODEL), lambda i: (i, 0)),
        out_shape=jax.ShapeDtypeStruct((n, D_MODEL), F32),
        scratch_shapes=[pltpu.VMEM((tm, D_MODEL), BF16)],
        compiler_params=_cparams(1),
        name="ffn_final" if final else "ffn",
    )(x, mod, g, w1, w2, gf)


PROJ_GROUPS = (("qkv", CONV_DIM), ("z", GDN_HEADS * GDN_DV), ("qb", SWA_HEADS * SWA_HD),
               ("kv", 2 * SWA_KV_HEADS * SWA_HD), ("gates", 2 * D_MODEL), ("ba", LANES))
PROJ_COLS = sum(w for _, w in PROJ_GROUPS)


assert math.log2(SWA_HD) % 2 == 0
W_IN_ALIGNED = CONV_DIM + GDN_HEADS * GDN_DV
W_IN_SMALL = 2 * GDN_HEADS
W_IN_REST = PROJ_COLS - W_IN_ALIGNED - LANES


def _pack_w_in_kernel(w_ref, o_ref):
    o_ref[:, 0:W_IN_ALIGNED] = w_ref[:, 0:W_IN_ALIGNED].astype(BF16)
    hi = W_IN_ALIGNED + W_IN_SMALL
    n_qb = SWA_HEADS * SWA_HD
    o_ref[:, W_IN_ALIGNED:W_IN_ALIGNED + n_qb] = (w_ref[:, hi:hi + n_qb] * (SWA_HD ** -0.5)).astype(BF16)
    o_ref[:, W_IN_ALIGNED + n_qb:W_IN_ALIGNED + W_IN_REST] = w_ref[:, hi + n_qb:hi + W_IN_REST].astype(BF16)
    lane = lax.broadcasted_iota(jnp.int32, (w_ref.shape[0], LANES), 1)
    small = jnp.where(lane < W_IN_SMALL, w_ref[:, W_IN_ALIGNED:W_IN_ALIGNED + LANES], 0.0)
    o_ref[:, W_IN_ALIGNED + W_IN_REST:PROJ_COLS] = small.astype(BF16)


def _pack_w_in(w_in):
    n_cols = W_IN_ALIGNED + W_IN_SMALL + W_IN_REST
    assert w_in.shape == (1, D_MODEL, n_cols)
    rows = 256
    return pl.pallas_call(
        _pack_w_in_kernel,
        grid=(D_MODEL // rows,),
        in_specs=[pl.BlockSpec((None, rows, n_cols), lambda i: (0, i, 0))],
        out_specs=pl.BlockSpec((rows, PROJ_COLS), lambda i: (i, 0)),
        out_shape=jax.ShapeDtypeStruct((D_MODEL, PROJ_COLS), BF16),
        compiler_params=_cparams(1),
        name="pack_w_in",
    )(w_in)


PROJ_OFFSETS = {}
_off = 0
for _name, _width in PROJ_GROUPS:
    PROJ_OFFSETS[_name] = (_off, _width)
    _off += _width
PROJ_ROWS = 512


def _proj_kernel(x_ref, mod_ref, g_ref, w_ref, hist_ref, cw_ref, par_ref,
                 qkv_ref, z_ref, qb_ref, kv_ref, gates_ref, gate_ref, tail_ref,
                 h_ref, xp_ref, *, spt, carry):
    pid = pl.program_id(0)
    tm = x_ref.shape[0]
    n_chunks = tm // CHUNK
    seg = CHUNK if carry else HIST_ROWS + CHUNK

    _norm_mod_store(h_ref, x_ref, g_ref, mod_ref, 1, spt)
    h = h_ref[...]

    def group(name, lo=0, width=None):
        c0, full = PROJ_OFFSETS[name]
        width = full if width is None else width
        return _dot(h, w_ref[:, c0 + lo:c0 + lo + width])

    if carry:
        def _load_hist():
            xp_ref[0:HIST_ROWS, :] = hist_ref[0]

        pl.when(pid == 0)(_load_hist)
    else:
        for c in range(n_chunks):
            xp_ref[c * seg:c * seg + HIST_ROWS, :] = hist_ref[c]

    part = GDN_HEADS * GDN_DK

    def project_part(p):
        raw = group("qkv", p * part, part)
        cols = slice(p * part, (p + 1) * part)
        if carry:
            xp_ref[HIST_ROWS:HIST_ROWS + tm, cols] = raw
        else:
            for c in range(n_chunks):
                xp_ref[c * seg + HIST_ROWS:(c + 1) * seg, cols] = raw[c * CHUNK:(c + 1) * CHUNK, :]

    def conv_part(p):
        for c in range(n_chunks):
            for t in range(p * GDN_HEADS, (p + 1) * GDN_HEADS):
                c0 = t * LANES
                win = xp_ref[c * seg:c * seg + HIST_ROWS + CHUNK, c0:c0 + LANES]
                acc = cw_ref[CONV_W - 1:CONV_W, c0:c0 + LANES] * win[HIST_ROWS:, :]
                for s in range(1, CONV_W):
                    tap = pltpu.roll(win, s, 0)[HIST_ROWS:, :]
                    acc = acc + cw_ref[CONV_W - 1 - s:CONV_W - s, c0:c0 + LANES] * tap
                y = acc * jax.nn.sigmoid(acc)
                if p < 2:
                    y = y * lax.rsqrt(jnp.sum(y * y, axis=-1, keepdims=True) + EPS)
                    if p == 0:
                        y = y * (GDN_DK ** -0.5)
                qkv_ref[c * CHUNK:(c + 1) * CHUNK, c0:c0 + LANES] = y

    def gate_math():
        ba = group("ba")
        ri = lax.broadcasted_iota(jnp.int32, (CHUNK, CHUNK), 0)
        ci = lax.broadcasted_iota(jnp.int32, (CHUNK, CHUNK), 1)
        tri = (ri >= ci).astype(BF16)
        lane = lax.broadcasted_iota(jnp.int32, (CHUNK, LANES), 1)
        a_coef = -jnp.exp(par_ref[0:1, :])
        dt_bias = par_ref[1:2, :]
        for c in range(n_chunks):
            bac = ba[c * CHUNK:(c + 1) * CHUNK, :]
            g_all = a_coef * jax.nn.softplus(bac + dt_bias)
            g_hi = g_all.astype(BF16)
            g_r1 = g_all - g_hi.astype(F32)
            g_mid = g_r1.astype(BF16)
            g_lo = (g_r1 - g_mid.astype(F32)).astype(BF16)
            gc = _dot(tri, g_hi) + _dot(tri, g_mid) + _dot(tri, g_lo)
            gate_ref[c * CHUNK:(c + 1) * CHUNK, :] = jnp.where(lane < GDN_HEADS, jax.nn.sigmoid(bac), gc)

    project_part(0)
    project_part(1)
    conv_part(0)
    project_part(2)
    conv_part(1)
    z_ref[...] = group("z").astype(z_ref.dtype)
    gate_math()
    qb_ref[...] = group("qb").astype(qb_ref.dtype)
    conv_part(2)
    kv_ref[...] = group("kv")
    gates_ref[...] = group("gates").astype(gates_ref.dtype)

    if carry:
        def _store_tail():
            tail_ref[0] = xp_ref[tm:tm + HIST_ROWS, :]

        pl.when(pid == pl.num_programs(0) - 1)(_store_tail)
        xp_ref[0:HIST_ROWS, :] = xp_ref[tm:tm + HIST_ROWS, :]
    else:
        for c in range(n_chunks):
            tail_ref[c] = xp_ref[c * seg + CHUNK:c * seg + CHUNK + HIST_ROWS, :]


def _proj(x, mod, seq_rows, g, w_packed, hist8, conv_w, par):
    n = x.shape[0]
    n_seq = n // seq_rows
    carry = n_seq == 1
    tm, spt = _tile_rows(n, seq_rows, PROJ_ROWS)
    if carry:
        n_hist = 1
        hist_map = lambda i: (0, 0, 0)
        xp_rows = HIST_ROWS + tm
    else:
        assert seq_rows == CHUNK
        n_hist = tm // CHUNK
        hist_map = lambda i: (i, 0, 0)
        xp_rows = n_hist * (HIST_ROWS + CHUNK)
    widths = [CONV_DIM, GDN_HEADS * GDN_DV, SWA_HEADS * SWA_HD, KV_COLS, 2 * D_MODEL, LANES]
    return pl.pallas_call(
        functools.partial(_proj_kernel, spt=spt, carry=carry),
        grid=(n // tm,),
        in_specs=[
            pl.BlockSpec((tm, D_MODEL), lambda i: (i, 0)),
            _mod_spec(mod.shape[1]),
            _const_spec((1, D_MODEL)),
            _const_spec((D_MODEL, PROJ_COLS)),
            pl.BlockSpec((n_hist, HIST_ROWS, CONV_DIM), hist_map),
            _const_spec((CONV_W, CONV_DIM)),
            _const_spec((SUBLANES, LANES)),
        ],
        out_specs=[pl.BlockSpec((tm, w), lambda i: (i, 0)) for w in widths]
        + [pl.BlockSpec((n_hist, HIST_ROWS, CONV_DIM), hist_map)],
        out_shape=[jax.ShapeDtypeStruct((n, w), BF16 if i in (1, 2, 4) else F32) for i, w in enumerate(widths)]
        + [jax.ShapeDtypeStruct((n_seq, HIST_ROWS, CONV_DIM), F32)],
        scratch_shapes=[pltpu.VMEM((tm, D_MODEL), BF16), pltpu.VMEM((xp_rows, CONV_DIM), F32)],
        compiler_params=_cparams(1),
        name="in_proj",
    )(x, mod, g, w_packed, hist8, conv_w, par)


def _gdn_kernel(qkv_ref, gate_ref, s0_ref, o_ref, sout_ref, s_ref, u_ref, wq_ref, kd_ref, in_ref, eg_ref,
                *, n_chunks, carry):
    pid = pl.program_id(0)
    n_items = n_chunks * GDN_HEADS

    def _init():
        if carry:
            s_ref[0] = s0_ref[0]
        for j in range(n_items):
            u_ref[j] = jnp.zeros((CHUNK, GDN_DV), F32)
            wq_ref[j] = jnp.zeros((2 * CHUNK, GDN_DK), BF16)
            kd_ref[j] = jnp.zeros((CHUNK, GDN_DK), BF16)
            in_ref[j] = jnp.zeros((CHUNK, CHUNK), BF16)
        for c in range(n_chunks):
            eg_ref[c] = jnp.ones((1, LANES), F32)

    pl.when(pid == 0)(_init)

    def recurrence():
        state = [s_ref[0, h] for h in range(GDN_HEADS)] if carry else None
        for c in range(n_chunks):
            idx = [c * GDN_HEADS + h for h in range(GDN_HEADS)]
            if not carry:
                state = [s0_ref[c, h] for h in range(GDN_HEADS)]
            eg_row = eg_ref[c]
            s16 = [s.astype(BF16) for s in state]
            ws_qs = [_dot(wq_ref[i], s) for i, s in zip(idx, s16)]
            yield
            vn16 = [(u_ref[i] - x[0:CHUNK, :]).astype(BF16) for i, x in zip(idx, ws_qs)]
            upd = [_dot_tn(kd_ref[i], vn) for i, vn in zip(idx, vn16)]
            yield
            state = [s * eg_row[:, GDN_HEADS + h:GDN_HEADS + h + 1] + x for h, (s, x) in enumerate(zip(state, upd))]
            for h, (i, x, vn) in enumerate(zip(idx, ws_qs, vn16)):
                o_ref[c * CHUNK:(c + 1) * CHUNK, h * GDN_DV:(h + 1) * GDN_DV] = (
                    x[CHUNK:, :] + _dot(in_ref[i], vn)).astype(o_ref.dtype)
            if not carry:
                for h in range(GDN_HEADS):
                    sout_ref[c, h] = state[h]
            yield
        if carry:
            for h in range(GDN_HEADS):
                s_ref[0, h] = state[h]

    rec = recurrence()

    ri = lax.broadcasted_iota(jnp.int32, (CHUNK, CHUNK), 0)
    ci = lax.broadcasted_iota(jnp.int32, (CHUNK, CHUNK), 1)
    causal = ri >= ci
    strict = ri > ci
    eye = (ri == ci).astype(F32)
    level_masks = []
    for lvl in range(int(math.log2(CHUNK))):
        same_block = (ri >> (lvl + 1)) == (ci >> (lvl + 1))
        level_masks.append(same_block & (((ri >> lvl) & 1) == 1) & (((ci >> lvl) & 1) == 0))

    gate = [gate_ref[c * CHUNK:(c + 1) * CHUNK, :] for c in range(n_chunks)]
    egc_all = [jnp.exp(g) for g in gate]
    g_last_all = [g[CHUNK - 1:CHUNK, :] for g in gate]
    kdf_all = [jnp.exp(gl - g) for gl, g in zip(g_last_all, gate)]
    eg_last_all = [jnp.exp(gl) for gl in g_last_all]

    def tile(c, t):
        return qkv_ref[c * CHUNK:(c + 1) * CHUNK, t * LANES:(t + 1) * LANES]

    def col(arrs, c, h):
        return arrs[c][:, GDN_HEADS + h:GDN_HEADS + h + 1]

    results = {}

    def prepare(items):
        q = [tile(c, h) for c, h in items]
        k = [tile(c, GDN_HEADS + h) for c, h in items]
        v = [tile(c, 2 * GDN_HEADS + h) for c, h in items]
        beta = [gate[c][:, h:h + 1] for c, h in items]
        gcol = [col(gate, c, h) for c, h in items]
        grow = [jnp.sum(g * eye, axis=0, keepdims=True) for g in gcol]
        yield
        decay = [jnp.where(causal, jnp.exp(jnp.where(causal, gc - gr, 0.0)), 0.0) for gc, gr in zip(gcol, grow)]
        yield
        kb = [x * b for x, b in zip(k, beta)]
        k16 = [x.astype(BF16) for x in k]
        kq = [_dot_nt(jnp.concatenate([a.astype(BF16), b.astype(BF16)], axis=0), c) for a, b, c in zip(kb, q, k16)]
        yield
        ls = [jnp.where(strict, x[0:CHUNK, :] * d, 0.0) for x, d in zip(kq, decay)]
        yield
        intra16 = [jnp.where(causal, x[CHUNK:, :] * d, 0.0).astype(BF16) for x, d in zip(kq, decay)]
        yield
        ts = [eye - jnp.where(level_masks[0], l, 0.0) for l in ls]
        for mask in level_masks[1:]:
            lk = [_split(jnp.where(mask, l, 0.0), INV_PASSES) for l in ls]
            tp = [_split(t, INV_PASSES) for t in ts]
            m1 = [_dot_split(a, b, INV_PASSES) for a, b in zip(lk, tp)]
            yield
            m2 = [_dot_split(a, _split(b, INV_PASSES), INV_PASSES) for a, b in zip(tp, m1)]
            ts = [t - m for t, m in zip(ts, m2)]
            yield
        ys = [_split(t - eye, SOLVE_PASSES) for t in ts]
        egc = [col(egc_all, c, h) for c, h in items]
        vb = [x * b for x, b in zip(v, beta)]
        kbe = [x * e for x, e in zip(kb, egc)]
        yield
        rhs = [jnp.concatenate([a, b], axis=1) for a, b in zip(vb, kbe)]
        sol = [x + _dot_split(y, _split(x, SOLVE_PASSES), SOLVE_PASSES) for x, y in zip(rhs, ys)]
        yield
        qg16 = [(x * e).astype(BF16) for x, e in zip(q, egc)]
        kd16 = [(x * col(kdf_all, c, h)).astype(BF16) for x, (c, h) in zip(k, items)]
        for (c, h), x, a, b, d in zip(items, sol, qg16, kd16, intra16):
            results[c * GDN_HEADS + h] = (x[:, 0:GDN_DV], jnp.concatenate([x[:, GDN_DV:].astype(BF16), a], axis=0), b, d)
        yield

    half = n_chunks // 2
    gens = [prepare([(c, h) for c in range(0, half) for h in range(GDN_HEADS)]),
            prepare([(c, h) for c in range(half, n_chunks) for h in range(GDN_HEADS)])]
    live = [True, True]

    def advance(g):
        if live[g]:
            live[g] = next(gens[g], "done") != "done"
            next(rec, None)

    for _ in range(PREP_SKEW):
        advance(0)
    while any(live):
        advance(0)
        advance(1)
    for _ in rec:
        pass

    for j in range(n_items):
        u_ref[j], wq_ref[j], kd_ref[j], in_ref[j] = results[j]
    for c in range(n_chunks):
        eg_ref[c] = eg_last_all[c]

    if carry:
        def _store_state():
            sout_ref[0] = s_ref[0]

        pl.when(pid == pl.num_programs(0) - 1)(_store_state)


def _gdn(qkv, gate, s0, seq_rows):
    n = qkv.shape[0]
    n_seq = n // seq_rows
    carry = n_seq == 1
    n_chunks = GDN_CHUNKS_PER_STEP
    rows = n_chunks * CHUNK
    assert n % rows == 0
    n_tiles = n // rows
    cur = lambda i: (jnp.minimum(i, n_tiles - 1), 0)
    done = lambda i: (jnp.maximum(i - 1, 0), 0)
    if carry:
        n_state = 1
        s_map = lambda i: (0, 0, 0, 0)
    else:
        assert seq_rows == CHUNK
        n_state = n_chunks
        s_map = lambda i: (jnp.maximum(i - 1, 0), 0, 0, 0)
    n_items = n_chunks * GDN_HEADS
    kern = functools.partial(_gdn_kernel, n_chunks=n_chunks, carry=carry)
    return pl.pallas_call(
        kern,
        grid=(n_tiles + 1,),
        in_specs=[
            pl.BlockSpec((rows, CONV_DIM), cur),
            pl.BlockSpec((rows, LANES), cur),
            pl.BlockSpec((n_state, GDN_HEADS, GDN_DK, GDN_DV), s_map),
        ],
        out_specs=[
            pl.BlockSpec((rows, GDN_HEADS * GDN_DV), done),
            pl.BlockSpec((n_state, GDN_HEADS, GDN_DK, GDN_DV), s_map),
        ],
        out_shape=[
            jax.ShapeDtypeStruct((n, GDN_HEADS * GDN_DV), BF16),
            jax.ShapeDtypeStruct((n_seq, GDN_HEADS, GDN_DK, GDN_DV), F32),
        ],
        scratch_shapes=[
            pltpu.VMEM((1, GDN_HEADS, GDN_DK, GDN_DV), F32),
            pltpu.VMEM((n_items, CHUNK, GDN_DV), F32),
            pltpu.VMEM((n_items, 2 * CHUNK, GDN_DK), BF16),
            pltpu.VMEM((n_items, CHUNK, GDN_DK), BF16),
            pltpu.VMEM((n_items, CHUNK, CHUNK), BF16),
            pltpu.VMEM((n_chunks, 1, LANES), F32),
        ],
        compiler_params=_cparams(1),
        name="gdn",
    )(qkv, gate, s0)


SWA_KEYS = WINDOW + CHUNK
SWA_KEYS_PAD = 2 * LANES
KV_COLS = 2 * SWA_KV_HEADS * SWA_HD


def _bias_kernel(bucket_ref, rb_ref, o_ref):
    bucket = bucket_ref[...]
    h = pl.program_id(0)
    acc = jnp.zeros((CHUNK, SWA_KEYS), F32)
    for b in range(NUM_BUCKETS):
        acc = jnp.where(bucket == b, rb_ref[b * SWA_HEADS + h], acc)
    o_ref[0] = acc


def _bias_table(bucket, rel_bias_flat):
    return pl.pallas_call(
        _bias_kernel,
        grid=(SWA_HEADS,),
        in_specs=[
            pl.BlockSpec((CHUNK, SWA_KEYS), lambda h: (0, 0)),
            pl.BlockSpec(memory_space=pltpu.SMEM),
        ],
        out_specs=pl.BlockSpec((1, CHUNK, SWA_KEYS), lambda h: (h, 0, 0)),
        out_shape=jax.ShapeDtypeStruct((SWA_HEADS, CHUNK, SWA_KEYS), F32),
        compiler_params=_cparams(1),
        name="swa_bias",
    )(bucket, rel_bias_flat)


def _swa_kernel(q_ref, kvc_ref, kvp_ref, bias_ref, sink_ref, o_ref, kf_ref, *, n_chunks, one_sequence):
    n_pad = SWA_KEYS_PAD - SWA_KEYS
    kf_ref[kf_ref.shape[0] - n_pad:, :] = jnp.zeros((n_pad, KV_COLS), F32)
    if one_sequence:
        kf_ref[0:WINDOW, :] = kvp_ref[...]
        kf_ref[WINDOW:WINDOW + n_chunks * CHUNK, :] = kvc_ref[...]
        key_start = [c * CHUNK for c in range(n_chunks)]
        key_pos = lax.broadcasted_iota(jnp.int32, (CHUNK, SWA_KEYS), 1)
        tile_start = pl.program_id(0) * (n_chunks * CHUNK)
        valid = [key_pos + (tile_start + c * CHUNK - WINDOW) >= 0 for c in range(n_chunks)]
    else:
        for c in range(n_chunks):
            kf_ref[c * SWA_KEYS:c * SWA_KEYS + WINDOW, :] = kvp_ref[c * WINDOW:(c + 1) * WINDOW, :]
            kf_ref[c * SWA_KEYS + WINDOW:(c + 1) * SWA_KEYS, :] = kvc_ref[c * CHUNK:(c + 1) * CHUNK, :]
        key_start = [c * SWA_KEYS for c in range(n_chunks)]

    pairs = [(c, kh) for c in range(n_chunks) for kh in range(SWA_KV_HEADS)]
    half = SWA_KV_HEADS * SWA_HD
    keys_t = [kf_ref[key_start[c]:key_start[c] + SWA_KEYS_PAD, 0:half].T.astype(BF16) for c in range(n_chunks)]
    keys = {(c, kh): keys_t[c][kh * SWA_HD:(kh + 1) * SWA_HD, :] for c, kh in pairs}
    ones = jnp.ones((SWA_KEYS, LANES - SWA_HD), BF16)
    vals = {(c, kh): jnp.concatenate(
        [kf_ref[key_start[c]:key_start[c] + SWA_KEYS,
                (SWA_KV_HEADS + kh) * SWA_HD:(SWA_KV_HEADS + kh + 1) * SWA_HD].astype(BF16), ones], axis=1)
            for c, kh in pairs}
    items = [(c, hd) for c in range(n_chunks) for hd in range(SWA_HEADS)]
    q = [q_ref[c * CHUNK:(c + 1) * CHUNK, hd * SWA_HD:(hd + 1) * SWA_HD].astype(BF16) for c, hd in items]
    def group_rows(c, kh):
        return range(c * SWA_HEADS + kh * SWA_GROUP, c * SWA_HEADS + (kh + 1) * SWA_GROUP)

    qk = {(c, kh): _dot(jnp.concatenate([q[i] for i in group_rows(c, kh)], axis=0), keys[(c, kh)])
          for c, kh in pairs}
    logits = [qk[(c, hd // SWA_GROUP)][(hd % SWA_GROUP) * CHUNK:(hd % SWA_GROUP + 1) * CHUNK, 0:SWA_KEYS] + bias_ref[hd]
              for c, hd in items]
    if one_sequence:
        logits = [jnp.where(valid[c], x, -jnp.inf) for x, (c, _) in zip(logits, items)]
    sink = [sink_ref[hd] for _, hd in items]
    m = [jnp.maximum(jnp.max(x, axis=-1, keepdims=True), s) for x, s in zip(logits, sink)]
    p16 = [jnp.exp(x - mm).astype(BF16) for x, mm in zip(logits, m)]
    pv = {(c, kh): _dot(jnp.concatenate([p16[i] for i in group_rows(c, kh)], axis=0), vals[(c, kh)])
          for c, kh in pairs}
    pv = [pv[(c, hd // SWA_GROUP)][(hd % SWA_GROUP) * CHUNK:(hd % SWA_GROUP + 1) * CHUNK, :] for c, hd in items]
    denom = [x[:, SWA_HD:SWA_HD + 1] + jnp.exp(s - mm) for x, s, mm in zip(pv, sink, m)]
    out = [x[:, 0:SWA_HD] / d for x, d in zip(pv, denom)]
    for x, (c, hd) in zip(out, items):
        o_ref[c * CHUNK:(c + 1) * CHUNK, hd * SWA_HD:(hd + 1) * SWA_HD] = x.astype(o_ref.dtype)


def _swa(qb, kv, kv_hist, bias, sinks, seq_rows):
    n = qb.shape[0]
    n_chunks = SWA_CHUNKS_PER_STEP
    rows = n_chunks * CHUNK
    assert n % rows == 0
    one_sequence = kv_hist is None
    if one_sequence:
        assert rows % WINDOW == 0
        per = rows // WINDOW
        prev_arr = kv
        prev_spec = pl.BlockSpec((WINDOW, KV_COLS), lambda i: (jnp.maximum(i * per - 1, 0), 0))
        kf_rows = WINDOW + rows
    else:
        assert seq_rows == CHUNK
        prev_arr = kv_hist
        prev_spec = pl.BlockSpec((n_chunks * WINDOW, KV_COLS), lambda i: (i, 0))
        kf_rows = n_chunks * SWA_KEYS
    kern = functools.partial(_swa_kernel, n_chunks=n_chunks, one_sequence=one_sequence)
    return pl.pallas_call(
        kern,
        grid=(n // rows,),
        in_specs=[
            pl.BlockSpec((rows, SWA_HEADS * SWA_HD), lambda i: (i, 0)),
            pl.BlockSpec((rows, KV_COLS), lambda i: (i, 0)),
            prev_spec,
            pl.BlockSpec((SWA_HEADS, CHUNK, SWA_KEYS), lambda i: (0, 0, 0)),
            pl.BlockSpec(memory_space=pltpu.SMEM),
        ],
        out_specs=pl.BlockSpec((rows, SWA_HEADS * SWA_HD), lambda i: (i, 0)),
        out_shape=jax.ShapeDtypeStruct((n, SWA_HEADS * SWA_HD), BF16),
        scratch_shapes=[pltpu.VMEM((kf_rows + SWA_KEYS_PAD - SWA_KEYS, KV_COLS), F32)],
        compiler_params=_cparams(1),
        name="swa",
    )(qb, kv, prev_arr, bias, sinks)


def _merge_kernel(x_ref, oa_ref, z_ref, ob_ref, gates_ref, mod_ref, nw_ref, wa_ref, wb_ref, wo_ref, o_ref, oa16_ref,
                  *, spt):
    for h in range(GDN_HEADS):
        cols = slice(h * GDN_DV, (h + 1) * GDN_DV)
        oo = oa_ref[:, cols].astype(F32)
        zz = z_ref[:, cols].astype(F32)
        on = oo * lax.rsqrt(jnp.mean(oo * oo, axis=-1, keepdims=True) + EPS) * nw_ref[...] * (zz * jax.nn.sigmoid(zz))
        oa16_ref[:, cols] = on.astype(BF16)
    ya = _dot(oa16_ref[...], wa_ref[...])
    yb = _dot(ob_ref[...].astype(BF16), wb_ref[...])
    merged = (jax.nn.sigmoid(gates_ref[:, 0:D_MODEL].astype(F32)) * ya
              + jax.nn.sigmoid(gates_ref[:, D_MODEL:2 * D_MODEL].astype(F32)) * yb)
    y = _dot(merged.astype(BF16), wo_ref[...])
    rows = x_ref.shape[0] // spt
    for j in range(spt):
        sl = slice(j * rows, (j + 1) * rows)
        o_ref[sl, :] = x_ref[sl, :] + _mod_row(mod_ref, 5, spt, j) * y[sl, :]


def _merge(x, oa, z, ob, gates, mod, seq_rows, norm_w, wa, wb, wo):
    n = x.shape[0]
    tm, spt = _tile_rows(n, seq_rows, MERGE_ROWS)
    row_spec = lambda w: pl.BlockSpec((tm, w), lambda i: (i, 0))
    return pl.pallas_call(
        functools.partial(_merge_kernel, spt=spt),
        grid=(n // tm,),
        in_specs=[
            row_spec(D_MODEL), row_spec(D_MODEL), row_spec(D_MODEL), row_spec(D_MODEL), row_spec(2 * D_MODEL),
            _mod_spec(mod.shape[1]),
            _const_spec((1, GDN_DV)),
            _const_spec((D_MODEL, D_MODEL)), _const_spec((D_MODEL, D_MODEL)), _const_spec((D_MODEL, D_MODEL)),
        ],
        out_specs=row_spec(D_MODEL),
        out_shape=jax.ShapeDtypeStruct((n, D_MODEL), F32),
        scratch_shapes=[pltpu.VMEM((tm, GDN_HEADS * GDN_DV), BF16)],
        compiler_params=_cparams(1),
        name="merge",
    )(x, oa, z, ob, gates, mod, norm_w, wa, wb, wo)


def _t5_bucket(rel):
    half = NUM_BUCKETS // 2
    max_exact = half // 2
    n = jnp.abs(rel)
    large = max_exact + (jnp.log(jnp.maximum(n, 1).astype(jnp.float32) / max_exact)
                         / math.log(MAX_DISTANCE / max_exact) * (half - max_exact)).astype(jnp.int32)
    large = jnp.minimum(large, half - 1)
    return jnp.where(rel > 0, half, 0) + jnp.where(n < max_exact, n, large)


def _trunk(x, mod, conv_hist, s_hist, k_hist, v_hist, w):
    bsz, seq, _ = x.shape
    n = bsz * seq
    x2 = x.reshape(n, D_MODEL)
    x1 = _ffn(x2, mod, seq, w["norm_ffn1"], w["ffn1_in"], w["ffn1_out"], w["norm_final"], sub=0, final=False)
    if conv_hist is None:
        hist8 = jnp.zeros((bsz, HIST_ROWS, CONV_DIM), F32)
        s0 = jnp.zeros((bsz, GDN_HEADS, GDN_DK, GDN_DV), F32)
        kv_hist = None
    else:
        hist8 = jnp.concatenate([jnp.zeros((bsz, HIST_ROWS - (CONV_W - 1), CONV_DIM), F32), conv_hist], axis=1)
        s0 = s_hist
        kv_hist = jnp.concatenate([k_hist.reshape(bsz, WINDOW, SWA_KV_HEADS * SWA_HD),
                                   v_hist.reshape(bsz, WINDOW, SWA_KV_HEADS * SWA_HD)], axis=-1)
        kv_hist = kv_hist.reshape(bsz * WINDOW, KV_COLS)
    qkv, z, qb, kv, gates, gate, tail = _proj(x1, mod, seq, w["norm_mix"], w["w_in"], hist8, w["conv_w"], w["gdn_par"])
    oa, s_new = _gdn(qkv, gate, s0, seq)
    ob = _swa(qb, kv, kv_hist, w["bias"], w["sinks"], seq)
    x3 = _merge(x1, oa, z, ob, gates, mod, seq, w["gdn_norm_w"], w["w_a"], w["w_b"], w["w_out"])
    y = _ffn(x3, mod, seq, w["norm_ffn2"], w["ffn2_in"], w["ffn2_out"], w["norm_final"], sub=2, final=True)

    conv_new = tail[:, HIST_ROWS - (CONV_W - 1):]
    half = SWA_KV_HEADS * SWA_HD
    kv3 = kv.reshape(bsz, seq, KV_COLS)
    if kv_hist is None:
        k_new = kv3[:, seq - WINDOW:, :half]
        v_new = kv3[:, seq - WINDOW:, half:]
    else:
        keep = WINDOW - seq
        k_new = jnp.concatenate([k_hist.reshape(bsz, WINDOW, half)[:, WINDOW - keep:], kv3[:, :, :half]], axis=1)
        v_new = jnp.concatenate([v_hist.reshape(bsz, WINDOW, half)[:, WINDOW - keep:], kv3[:, :, half:]], axis=1)
    k_new = k_new.reshape(bsz, WINDOW, SWA_KV_HEADS, SWA_HD)
    v_new = v_new.reshape(bsz, WINDOW, SWA_KV_HEADS, SWA_HD)
    return (y.reshape(bsz, seq, D_MODEL), conv_new[None], s_new[None], k_new[None], v_new[None])


def kernel(x_prompt, x_sample, state_gdn_conv, state_gdn_s, cache_swa_k, cache_swa_v, c_prompt, c_sample,
           norm_ffn1, w_ffn1_in, w_ffn1_out, norm_mix, w_in, gdn_conv_w, gdn_a_log, gdn_dt_bias, gdn_norm_w,
           swa_sinks, rel_bias, w_branch_a, w_branch_b, w_out, norm_ffn2, w_ffn2_in, w_ffn2_out,
           w_ada, b_ada, norm_final):
    bp = x_prompt.shape[0]
    bs = x_sample.shape[0]
    assert bp == 1 and x_sample.shape[1] == CHUNK and cache_swa_k.shape[2] == WINDOW

    n_c = bp + bs
    pad = -n_c % SUBLANES
    c_all = jnp.concatenate([c_prompt, c_sample, jnp.zeros((pad, D_MODEL), F32)], axis=0)
    mod_all = _modulation(c_all, w_ada[0], b_ada[0][None, :])
    mod_p = mod_all[:, :bp]
    mod_s = mod_all[:, bp:n_c]

    w_packed = _pack_w_in(w_in)
    par = jnp.zeros((SUBLANES, LANES), F32)
    par = par.at[0, GDN_HEADS:2 * GDN_HEADS].set(gdn_a_log[0]).at[1, GDN_HEADS:2 * GDN_HEADS].set(gdn_dt_bias[0])

    rel = jnp.arange(SWA_KEYS)[None, :] - WINDOW - jnp.arange(CHUNK)[:, None]
    bias = _bias_table(_t5_bucket(rel).astype(jnp.int32), rel_bias.reshape(-1))

    w = {
        "norm_ffn1": norm_ffn1, "ffn1_in": w_ffn1_in[0].astype(BF16), "ffn1_out": w_ffn1_out[0].astype(BF16),
        "norm_mix": norm_mix, "w_in": w_packed, "conv_w": gdn_conv_w[0], "gdn_par": par,
        "gdn_norm_w": gdn_norm_w, "bias": bias, "sinks": swa_sinks[0],
        "w_a": w_branch_a[0].astype(BF16), "w_b": w_branch_b[0].astype(BF16), "w_out": w_out[0].astype(BF16),
        "norm_ffn2": norm_ffn2, "ffn2_in": w_ffn2_in[0].astype(BF16), "ffn2_out": w_ffn2_out[0].astype(BF16),
        "norm_final": norm_final[None, :],
    }
    y_p, p_conv, p_s, p_k, p_v = _trunk(x_prompt, mod_p, None, None, None, None, w)
    y_s, s_conv, s_s, s_k, s_v = _trunk(x_sample, mod_s, state_gdn_conv[0], state_gdn_s[0],
                                        cache_swa_k[0], cache_swa_v[0], w)
    return (y_p, y_s, p_conv, p_s, p_k, p_v, s_conv, s_s, s_k, s_v)
```

```python
import functools
import math

import jax
import jax.numpy as jnp
from jax import lax
from jax.experimental import pallas as pl
from jax.experimental.pallas import tpu as pltpu

F32 = jnp.float32
BF16 = jnp.bfloat16

D_MODEL = 1024
CHUNK = 64
GDN_HEADS = 8
GDN_DK = 128
GDN_DV = 128
CONV_W = 4
CONV_DIM = GDN_HEADS * (2 * GDN_DK + GDN_DV)
SWA_HEADS = 16
SWA_KV_HEADS = 2
SWA_GROUP = SWA_HEADS // SWA_KV_HEADS
SWA_HD = 64
WINDOW = 128
NUM_BUCKETS = 32
MAX_DISTANCE = 128
D_FF = 2816
N_MOD = 9
EPS = 1e-6

LANES = 128
SUBLANES = 8
VMEM_LIMIT = 56 * 1024 * 1024

FF_CHUNK = D_FF // 2
FFN_ROWS = 1024
MERGE_ROWS = 512
HIST_ROWS = SUBLANES
INV_PASSES = 1
SOLVE_PASSES = 1
GDN_CHUNKS_PER_STEP = 4
SWA_CHUNKS_PER_STEP = 4


def _dot(a, b):
    return jnp.dot(a, b, preferred_element_type=F32)


def _dot_nt(a, b):
    return lax.dot_general(a, b, (((1,), (1,)), ((), ())), preferred_element_type=F32)


def _dot_tn(a, b):
    return lax.dot_general(a, b, (((0,), (0,)), ((), ())), preferred_element_type=F32)


def _split(a, passes):
    hi = a.astype(BF16)
    if passes == 1:
        return hi, None
    return hi, (a - hi.astype(F32)).astype(BF16)


def _dot_split(a_parts, b_parts, passes):
    ah, al = a_parts
    bh, bl = b_parts
    out = _dot(ah, bh)
    if passes == 3:
        out = out + _dot(al, bh) + _dot(ah, bl)
    return out


def _cparams(n_grid):
    return pltpu.CompilerParams(dimension_semantics=("arbitrary",) * n_grid, vmem_limit_bytes=VMEM_LIMIT)


def _const_spec(shape):
    nd = len(shape)
    return pl.BlockSpec(shape, lambda i: (0,) * nd, pipeline_mode=pl.Buffered(1))


def _mod_kernel(c_ref, w_ref, b_ref, o_ref):
    c = c_ref[...]
    a = (c * jax.nn.sigmoid(c)).astype(BF16)
    o_ref[0] = _dot(a, w_ref[...].astype(BF16)) + b_ref[...]


def _modulation(c_pad, w_ada, b_ada):
    rows = c_pad.shape[0]
    return pl.pallas_call(
        _mod_kernel,
        grid=(N_MOD,),
        in_specs=[
            pl.BlockSpec((rows, D_MODEL), lambda j: (0, 0)),
            pl.BlockSpec((D_MODEL, D_MODEL), lambda j: (0, j)),
            pl.BlockSpec((1, D_MODEL), lambda j: (0, j)),
        ],
        out_specs=pl.BlockSpec((1, rows, D_MODEL), lambda j: (j, 0, 0)),
        out_shape=jax.ShapeDtypeStruct((N_MOD, rows, D_MODEL), F32),
        compiler_params=_cparams(1),
        name="adaln_mod",
    )(c_pad, w_ada, b_ada)


def _mod_row(mod_ref, idx, spt, j):
    if mod_ref.shape[1] == 1:
        return mod_ref[idx, 0:1, :]
    return mod_ref[idx, pl.ds(pl.program_id(0) * spt + j, 1), :]


def _norm_mod_store(h_ref, x_ref, g_ref, mod_ref, sub, spt):
    rows = x_ref.shape[0] // spt
    g = g_ref[...]
    for j in range(spt):
        xs = x_ref[j * rows:(j + 1) * rows, :]
        ms = jnp.mean(xs * xs, axis=-1, keepdims=True)
        y = xs * lax.rsqrt(ms + EPS) * g
        sh = _mod_row(mod_ref, 3 * sub, spt, j)
        sc = _mod_row(mod_ref, 3 * sub + 1, spt, j)
        h_ref[j * rows:(j + 1) * rows, :] = (y * (1.0 + sc) + sh).astype(h_ref.dtype)


def _tile_rows(n_rows, seq_rows, target):
    if seq_rows >= target:
        assert seq_rows % target == 0
        return target, 1
    assert target % seq_rows == 0
    tm = min(target, n_rows)
    assert n_rows % tm == 0
    return tm, tm // seq_rows


def _mod_spec(n_seq_total):
    return pl.BlockSpec((N_MOD, n_seq_total, D_MODEL), lambda i: (0, 0, 0))


def _ffn_kernel(x_ref, mod_ref, g_ref, w1_ref, w2_ref, gf_ref, o_ref, h_ref, *, sub, final, spt):
    _norm_mod_store(h_ref, x_ref, g_ref, mod_ref, sub, spt)
    h = h_ref[...]
    acc = None
    for c in range(D_FF // FF_CHUNK):
        c0 = c * FF_CHUNK
        gate = _dot(h, w1_ref[:, c0:c0 + FF_CHUNK])
        up = _dot(h, w1_ref[:, D_FF + c0:D_FF + c0 + FF_CHUNK])
        a = (gate * jax.nn.sigmoid(gate) * up).astype(BF16)
        part = _dot(a, w2_ref[c0:c0 + FF_CHUNK, :])
        acc = part if acc is None else acc + part
    rows = x_ref.shape[0] // spt
    for j in range(spt):
        sl = slice(j * rows, (j + 1) * rows)
        ga = _mod_row(mod_ref, 3 * sub + 2, spt, j)
        xn = x_ref[sl, :] + 0.5 * ga * acc[sl, :]
        if final:
            ms = jnp.mean(xn * xn, axis=-1, keepdims=True)
            xn = xn * lax.rsqrt(ms + EPS) * gf_ref[...]
        o_ref[sl, :] = xn


def _ffn(x, mod, seq_rows, g, w1, w2, gf, *, sub, final):
    n = x.shape[0]
    tm, spt = _tile_rows(n, seq_rows, FFN_ROWS)
    kern = functools.partial(_ffn_kernel, sub=sub, final=final, spt=spt)
    return pl.pallas_call(
        kern,
        grid=(n // tm,),
        in_specs=[
            pl.BlockSpec((tm, D_MODEL), lambda i: (i, 0)),
            _mod_spec(mod.shape[1]),
            _const_spec((1, D_MODEL)),
            _const_spec((D_MODEL, 2 * D_FF)),
            _const_spec((D_FF, D_MODEL)),
            _const_spec((1, D_MODEL)),
        ],
        out_specs=pl.BlockSpec((tm, D_MODEL), lambda i: (i, 0)),
        out_shape=jax.ShapeDtypeStruct((n, D_MODEL), F32),
        scratch_shapes=[pltpu.VMEM((tm, D_MODEL), BF16)],
        compiler_params=_cparams(1),
        name="ffn_final" if final else "ffn",
    )(x, mod, g, w1, w2, gf)


PROJ_GROUPS = (("qkv", CONV_DIM), ("z", GDN_HEADS * GDN_DV), ("qb", SWA_HEADS * SWA_HD),
               ("kv", 2 * SWA_KV_HEADS * SWA_HD), ("gates", 2 * D_MODEL), ("ba", LANES))
PROJ_COLS = sum(w for _, w in PROJ_GROUPS)


assert math.log2(SWA_HD) % 2 == 0
W_IN_ALIGNED = CONV_DIM + GDN_HEADS * GDN_DV
W_IN_SMALL = 2 * GDN_HEADS
W_IN_REST = PROJ_COLS - W_IN_ALIGNED - LANES


def _pack_w_in_kernel(w_ref, o_ref):
    o_ref[:, 0:W_IN_ALIGNED] = w_ref[:, 0:W_IN_ALIGNED].astype(BF16)
    hi = W_IN_ALIGNED + W_IN_SMALL
    n_qb = SWA_HEADS * SWA_HD
    o_ref[:, W_IN_ALIGNED:W_IN_ALIGNED + n_qb] = (w_ref[:, hi:hi + n_qb] * (SWA_HD ** -0.5)).astype(BF16)
    o_ref[:, W_IN_ALIGNED + n_qb:W_IN_ALIGNED + W_IN_REST] = w_ref[:, hi + n_qb:hi + W_IN_REST].astype(BF16)
    lane = lax.broadcasted_iota(jnp.int32, (w_ref.shape[0], LANES), 1)
    small = jnp.where(lane < W_IN_SMALL, w_ref[:, W_IN_ALIGNED:W_IN_ALIGNED + LANES], 0.0)
    o_ref[:, W_IN_ALIGNED + W_IN_REST:PROJ_COLS] = small.astype(BF16)


def _pack_w_in(w_in):
    n_cols = W_IN_ALIGNED + W_IN_SMALL + W_IN_REST
    assert w_in.shape == (1, D_MODEL, n_cols)
    rows = 256
    return pl.pallas_call(
        _pack_w_in_kernel,
        grid=(D_MODEL // rows,),
        in_specs=[pl.BlockSpec((None, rows, n_cols), lambda i: (0, i, 0))],
        out_specs=pl.BlockSpec((rows, PROJ_COLS), lambda i: (i, 0)),
        out_shape=jax.ShapeDtypeStruct((D_MODEL, PROJ_COLS), BF16),
        compiler_params=_cparams(1),
        name="pack_w_in",
    )(w_in)


PROJ_OFFSETS = {}
_off = 0
for _name, _width in PROJ_GROUPS:
    PROJ_OFFSETS[_name] = (_off, _width)
    _off += _width
PROJ_ROWS = 512
F32_OUT = {"qkv": (0, CONV_DIM), "kv": (CONV_DIM, 2 * SWA_KV_HEADS * SWA_HD),
           "gate": (CONV_DIM + 2 * SWA_KV_HEADS * SWA_HD, LANES)}
F32_OUT_COLS = CONV_DIM + 2 * SWA_KV_HEADS * SWA_HD + LANES
B16_OUT = {"z": (0, GDN_HEADS * GDN_DV), "qb": (GDN_HEADS * GDN_DV, SWA_HEADS * SWA_HD),
           "gates": (GDN_HEADS * GDN_DV + SWA_HEADS * SWA_HD, 2 * D_MODEL)}
B16_OUT_COLS = GDN_HEADS * GDN_DV + SWA_HEADS * SWA_HD + 2 * D_MODEL
for _o, _w in list(F32_OUT.values()) + list(B16_OUT.values()):
    assert _o % _w == 0


def _col_block(layout, name, rows, row_map):
    off, width = layout[name]
    return pl.BlockSpec((rows, width), lambda i: (row_map(i), off // width))


def _proj_kernel(x_ref, mod_ref, g_ref, w_ref, hist_ref, cw_ref, par_ref,
                 f32_ref, b16_ref, tail_ref,
                 h_ref, xp_ref, *, spt, carry):
    pid = pl.program_id(0)
    tm = x_ref.shape[0]
    n_chunks = tm // CHUNK
    seg = CHUNK if carry else HIST_ROWS + CHUNK
    qkv_ref = f32_ref.at[:, F32_OUT["qkv"][0]:F32_OUT["qkv"][0] + F32_OUT["qkv"][1]]
    kv_ref = f32_ref.at[:, F32_OUT["kv"][0]:F32_OUT["kv"][0] + F32_OUT["kv"][1]]
    gate_ref = f32_ref.at[:, F32_OUT["gate"][0]:F32_OUT["gate"][0] + F32_OUT["gate"][1]]
    z_ref = b16_ref.at[:, B16_OUT["z"][0]:B16_OUT["z"][0] + B16_OUT["z"][1]]
    qb_ref = b16_ref.at[:, B16_OUT["qb"][0]:B16_OUT["qb"][0] + B16_OUT["qb"][1]]
    gates_ref = b16_ref.at[:, B16_OUT["gates"][0]:B16_OUT["gates"][0] + B16_OUT["gates"][1]]

    _norm_mod_store(h_ref, x_ref, g_ref, mod_ref, 1, spt)
    h = h_ref[...]

    def group(name, lo=0, width=None):
        c0, full = PROJ_OFFSETS[name]
        width = full if width is None else width
        return _dot(h, w_ref[:, c0 + lo:c0 + lo + width])

    if carry:
        def _load_hist():
            xp_ref[0:HIST_ROWS, :] = hist_ref[0]

        pl.when(pid == 0)(_load_hist)
    else:
        for c in range(n_chunks):
            xp_ref[c * seg:c * seg + HIST_ROWS, :] = hist_ref[c]

    part = GDN_HEADS * GDN_DK

    def project_part(p):
        raw = group("qkv", p * part, part)
        cols = slice(p * part, (p + 1) * part)
        if carry:
            xp_ref[HIST_ROWS:HIST_ROWS + tm, cols] = raw
        else:
            for c in range(n_chunks):
                xp_ref[c * seg + HIST_ROWS:(c + 1) * seg, cols] = raw[c * CHUNK:(c + 1) * CHUNK, :]

    def conv_part(p):
        for c in range(n_chunks):
            for t in range(p * GDN_HEADS, (p + 1) * GDN_HEADS):
                c0 = t * LANES
                win = xp_ref[c * seg:c * seg + HIST_ROWS + CHUNK, c0:c0 + LANES]
                acc = cw_ref[CONV_W - 1:CONV_W, c0:c0 + LANES] * win[HIST_ROWS:, :]
                for s in range(1, CONV_W):
                    tap = pltpu.roll(win, s, 0)[HIST_ROWS:, :]
                    acc = acc + cw_ref[CONV_W - 1 - s:CONV_W - s, c0:c0 + LANES] * tap
                y = acc * jax.nn.sigmoid(acc)
                if p < 2:
                    y = y * lax.rsqrt(jnp.sum(y * y, axis=-1, keepdims=True) + EPS)
                    if p == 0:
                        y = y * (GDN_DK ** -0.5)
                qkv_ref[c * CHUNK:(c + 1) * CHUNK, c0:c0 + LANES] = y

    def gate_math():
        ba = group("ba")
        ri = lax.broadcasted_iota(jnp.int32, (CHUNK, CHUNK), 0)
        ci = lax.broadcasted_iota(jnp.int32, (CHUNK, CHUNK), 1)
        tri = (ri >= ci).astype(BF16)
        lane = lax.broadcasted_iota(jnp.int32, (CHUNK, LANES), 1)
        a_coef = -jnp.exp(par_ref[0:1, :])
        dt_bias = par_ref[1:2, :]
        for c in range(n_chunks):
            bac = ba[c * CHUNK:(c + 1) * CHUNK, :]
            g_all = a_coef * jax.nn.softplus(bac + dt_bias)
            g_hi = g_all.astype(BF16)
            g_r1 = g_all - g_hi.astype(F32)
            g_mid = g_r1.astype(BF16)
            g_lo = (g_r1 - g_mid.astype(F32)).astype(BF16)
            gc = _dot(tri, g_hi) + _dot(tri, g_mid) + _dot(tri, g_lo)
            gate_ref[c * CHUNK:(c + 1) * CHUNK, :] = jnp.where(lane < GDN_HEADS, jax.nn.sigmoid(bac), gc)

    project_part(0)
    project_part(1)
    conv_part(0)
    project_part(2)
    conv_part(1)
    z_ref[...] = group("z").astype(z_ref.dtype)
    gate_math()
    qb_ref[...] = group("qb").astype(qb_ref.dtype)
    conv_part(2)
    kv_ref[...] = group("kv")
    gates_ref[...] = group("gates").astype(gates_ref.dtype)

    if carry:
        def _store_tail():
            tail_ref[0] = xp_ref[tm:tm + HIST_ROWS, :]

        pl.when(pid == pl.num_programs(0) - 1)(_store_tail)
        xp_ref[0:HIST_ROWS, :] = xp_ref[tm:tm + HIST_ROWS, :]
    else:
        for c in range(n_chunks):
            tail_ref[c] = xp_ref[c * seg + CHUNK:c * seg + CHUNK + HIST_ROWS, :]


def _proj(x, mod, seq_rows, g, w_packed, hist8, conv_w, par):
    n = x.shape[0]
    n_seq = n // seq_rows
    carry = n_seq == 1
    tm, spt = _tile_rows(n, seq_rows, PROJ_ROWS)
    if carry:
        n_hist = 1
        hist_map = lambda i: (0, 0, 0)
        xp_rows = HIST_ROWS + tm
    else:
        assert seq_rows == CHUNK
        n_hist = tm // CHUNK
        hist_map = lambda i: (i, 0, 0)
        xp_rows = n_hist * (HIST_ROWS + CHUNK)
    return pl.pallas_call(
        functools.partial(_proj_kernel, spt=spt, carry=carry),
        grid=(n // tm,),
        in_specs=[
            pl.BlockSpec((tm, D_MODEL), lambda i: (i, 0)),
            _mod_spec(mod.shape[1]),
            _const_spec((1, D_MODEL)),
            _const_spec((D_MODEL, PROJ_COLS)),
            pl.BlockSpec((n_hist, HIST_ROWS, CONV_DIM), hist_map),
            _const_spec((CONV_W, CONV_DIM)),
            _const_spec((SUBLANES, LANES)),
        ],
        out_specs=[pl.BlockSpec((tm, F32_OUT_COLS), lambda i: (i, 0)),
                   pl.BlockSpec((tm, B16_OUT_COLS), lambda i: (i, 0)),
                   pl.BlockSpec((n_hist, HIST_ROWS, CONV_DIM), hist_map)],
        out_shape=[jax.ShapeDtypeStruct((n, F32_OUT_COLS), F32),
                   jax.ShapeDtypeStruct((n, B16_OUT_COLS), BF16),
                   jax.ShapeDtypeStruct((n_seq, HIST_ROWS, CONV_DIM), F32)],
        scratch_shapes=[pltpu.VMEM((tm, D_MODEL), BF16), pltpu.VMEM((xp_rows, CONV_DIM), F32)],
        compiler_params=_cparams(1),
        name="in_proj",
    )(x, mod, g, w_packed, hist8, conv_w, par)


def _unit_lower_inverses(ls, eye, level_masks):
    ts = [eye - jnp.where(level_masks[0], l, 0.0) for l in ls]
    for mask in level_masks[1:]:
        lk = [_split(jnp.where(mask, l, 0.0), INV_PASSES) for l in ls]
        tp = [_split(t, INV_PASSES) for t in ts]
        m1 = [_dot_split(a, b, INV_PASSES) for a, b in zip(lk, tp)]
        m2 = [_dot_split(a, _split(b, INV_PASSES), INV_PASSES) for a, b in zip(tp, m1)]
        ts = [t - m for t, m in zip(ts, m2)]
    return ts


def _gdn_kernel(qkv_ref, gate_ref, s0_ref, o_ref, sout_ref, s_ref, *, n_chunks, carry):
    pid = pl.program_id(0)

    if carry:
        def _load_state():
            s_ref[0] = s0_ref[0]

        pl.when(pid == 0)(_load_state)
    else:
        for c in range(n_chunks):
            s_ref[c] = s0_ref[c]

    ri = lax.broadcasted_iota(jnp.int32, (CHUNK, CHUNK), 0)
    ci = lax.broadcasted_iota(jnp.int32, (CHUNK, CHUNK), 1)
    causal = ri >= ci
    strict = ri > ci
    eye = (ri == ci).astype(F32)
    level_masks = []
    for lvl in range(int(math.log2(CHUNK))):
        same_block = (ri >> (lvl + 1)) == (ci >> (lvl + 1))
        level_masks.append(same_block & (((ri >> lvl) & 1) == 1) & (((ci >> lvl) & 1) == 0))

    items = [(c, h) for c in range(n_chunks) for h in range(GDN_HEADS)]
    gate = [gate_ref[c * CHUNK:(c + 1) * CHUNK, :] for c in range(n_chunks)]
    egc_all = [jnp.exp(g) for g in gate]
    g_last_all = [g[CHUNK - 1:CHUNK, :] for g in gate]
    kdf_all = [jnp.exp(gl - g) for gl, g in zip(g_last_all, gate)]
    eg_last_all = [jnp.exp(gl) for gl in g_last_all]

    def tile(c, t):
        return qkv_ref[c * CHUNK:(c + 1) * CHUNK, t * LANES:(t + 1) * LANES]

    def col(arrs, c, h):
        return arrs[c][:, GDN_HEADS + h:GDN_HEADS + h + 1]

    q = [tile(c, h) for c, h in items]
    k = [tile(c, GDN_HEADS + h) for c, h in items]
    v = [tile(c, 2 * GDN_HEADS + h) for c, h in items]
    beta = [gate[c][:, h:h + 1] for c, h in items]
    gcol = [col(gate, c, h) for c, h in items]
    grow = [jnp.sum(g * eye, axis=0, keepdims=True) for g in gcol]
    decay = [jnp.where(causal, jnp.exp(jnp.where(causal, gc - gr, 0.0)), 0.0) for gc, gr in zip(gcol, grow)]
    kb = [x * b for x, b in zip(k, beta)]
    k16 = [x.astype(BF16) for x in k]
    kq = [_dot_nt(jnp.concatenate([a.astype(BF16), b.astype(BF16)], axis=0), c) for a, b, c in zip(kb, q, k16)]
    kk = [x[0:CHUNK, :] for x in kq]
    qk = [x[CHUNK:, :] for x in kq]
    ls = [jnp.where(strict, a * d, 0.0) for a, d in zip(kk, decay)]
    intra16 = [jnp.where(causal, a * d, 0.0).astype(BF16) for a, d in zip(qk, decay)]
    ts = _unit_lower_inverses(ls, eye, level_masks)
    ys = [_split(t - eye, SOLVE_PASSES) for t in ts]
    egc = [col(egc_all, c, h) for c, h in items]
    vb = [x * b for x, b in zip(v, beta)]
    kbe = [x * e for x, e in zip(kb, egc)]
    rhs = [jnp.concatenate([a, b], axis=1) for a, b in zip(vb, kbe)]
    sol = [x + _dot_split(y, _split(x, SOLVE_PASSES), SOLVE_PASSES) for x, y in zip(rhs, ys)]
    u = [x[:, 0:GDN_DV] for x in sol]
    w16 = [x[:, GDN_DV:].astype(BF16) for x in sol]
    qg16 = [(x * e).astype(BF16) for x, e in zip(q, egc)]
    kd16 = [(x * col(kdf_all, c, h)).astype(BF16) for x, (c, h) in zip(k, items)]
    eg_last = [col(eg_last_all, c, h) for c, h in items]
    wq16 = [jnp.concatenate([a, b], axis=0) for a, b in zip(w16, qg16)]

    o = [None] * len(items)
    if carry:
        state = [s_ref[0, h] for h in range(GDN_HEADS)]
    for c in range(n_chunks):
        idx = [c * GDN_HEADS + h for h in range(GDN_HEADS)]
        if not carry:
            state = [s_ref[c, h] for h in range(GDN_HEADS)]
        s16 = [s.astype(BF16) for s in state]
        ws_qs = [_dot(wq16[i], s) for i, s in zip(idx, s16)]
        v_new = [u[i] - x[0:CHUNK, :] for i, x in zip(idx, ws_qs)]
        vn16 = [x.astype(BF16) for x in v_new]
        for i, x, vn in zip(idx, ws_qs, vn16):
            o[i] = x[CHUNK:, :] + _dot(intra16[i], vn)
        state = [s * eg_last[i] + _dot_tn(kd16[i], vn) for i, s, vn in zip(idx, state, vn16)]
        if not carry:
            for h in range(GDN_HEADS):
                sout_ref[c, h] = state[h]

    for (c, h), oo in zip(items, o):
        o_ref[c * CHUNK:(c + 1) * CHUNK, h * GDN_DV:(h + 1) * GDN_DV] = oo.astype(o_ref.dtype)

    if carry:
        for h in range(GDN_HEADS):
            s_ref[0, h] = state[h]

        def _store_state():
            sout_ref[0] = s_ref[0]

        pl.when(pid == pl.num_programs(0) - 1)(_store_state)


def _gdn(f32_out, s0, seq_rows):
    n = f32_out.shape[0]
    n_seq = n // seq_rows
    carry = n_seq == 1
    n_chunks = GDN_CHUNKS_PER_STEP
    rows = n_chunks * CHUNK
    assert n % rows == 0
    if carry:
        n_state = 1
        s_map = lambda i: (0, 0, 0, 0)
    else:
        assert seq_rows == CHUNK
        n_state = n_chunks
        s_map = lambda i: (i, 0, 0, 0)
    kern = functools.partial(_gdn_kernel, n_chunks=n_chunks, carry=carry)
    return pl.pallas_call(
        kern,
        grid=(n // rows,),
        in_specs=[
            _col_block(F32_OUT, "qkv", rows, lambda i: i),
            _col_block(F32_OUT, "gate", rows, lambda i: i),
            pl.BlockSpec((n_state, GDN_HEADS, GDN_DK, GDN_DV), s_map),
        ],
        out_specs=[
            pl.BlockSpec((rows, GDN_HEADS * GDN_DV), lambda i: (i, 0)),
            pl.BlockSpec((n_state, GDN_HEADS, GDN_DK, GDN_DV), s_map),
        ],
        out_shape=[
            jax.ShapeDtypeStruct((n, GDN_HEADS * GDN_DV), BF16),
            jax.ShapeDtypeStruct((n_seq, GDN_HEADS, GDN_DK, GDN_DV), F32),
        ],
        scratch_shapes=[pltpu.VMEM((n_state, GDN_HEADS, GDN_DK, GDN_DV), F32)],
        compiler_params=_cparams(1),
        name="gdn",
    )(f32_out, f32_out, s0)


SWA_KEYS = WINDOW + CHUNK
SWA_KEYS_PAD = 2 * LANES
KV_COLS = 2 * SWA_KV_HEADS * SWA_HD


def _bias_kernel(bucket_ref, rb_ref, o_ref):
    bucket = bucket_ref[...]
    h = pl.program_id(0)
    acc = jnp.zeros((CHUNK, SWA_KEYS), F32)
    for b in range(NUM_BUCKETS):
        acc = jnp.where(bucket == b, rb_ref[b * SWA_HEADS + h], acc)
    o_ref[0] = acc


def _bias_table(bucket, rel_bias_flat):
    return pl.pallas_call(
        _bias_kernel,
        grid=(SWA_HEADS,),
        in_specs=[
            pl.BlockSpec((CHUNK, SWA_KEYS), lambda h: (0, 0)),
            pl.BlockSpec(memory_space=pltpu.SMEM),
        ],
        out_specs=pl.BlockSpec((1, CHUNK, SWA_KEYS), lambda h: (h, 0, 0)),
        out_shape=jax.ShapeDtypeStruct((SWA_HEADS, CHUNK, SWA_KEYS), F32),
        compiler_params=_cparams(1),
        name="swa_bias",
    )(bucket, rel_bias_flat)


def _swa_kernel(q_ref, kvc_ref, kvp_ref, bias_ref, sink_ref, o_ref, kf_ref, *, n_chunks, one_sequence):
    n_pad = SWA_KEYS_PAD - SWA_KEYS
    kf_ref[kf_ref.shape[0] - n_pad:, :] = jnp.zeros((n_pad, KV_COLS), F32)
    if one_sequence:
        kf_ref[0:WINDOW, :] = kvp_ref[...]
        kf_ref[WINDOW:WINDOW + n_chunks * CHUNK, :] = kvc_ref[...]
        key_start = [c * CHUNK for c in range(n_chunks)]
        key_pos = lax.broadcasted_iota(jnp.int32, (CHUNK, SWA_KEYS), 1)
        tile_start = pl.program_id(0) * (n_chunks * CHUNK)
        valid = [key_pos + (tile_start + c * CHUNK - WINDOW) >= 0 for c in range(n_chunks)]
    else:
        for c in range(n_chunks):
            kf_ref[c * SWA_KEYS:c * SWA_KEYS + WINDOW, :] = kvp_ref[c * WINDOW:(c + 1) * WINDOW, :]
            kf_ref[c * SWA_KEYS + WINDOW:(c + 1) * SWA_KEYS, :] = kvc_ref[c * CHUNK:(c + 1) * CHUNK, :]
        key_start = [c * SWA_KEYS for c in range(n_chunks)]

    pairs = [(c, kh) for c in range(n_chunks) for kh in range(SWA_KV_HEADS)]
    half = SWA_KV_HEADS * SWA_HD
    keys_t = [kf_ref[key_start[c]:key_start[c] + SWA_KEYS_PAD, 0:half].T.astype(BF16) for c in range(n_chunks)]
    keys = {(c, kh): keys_t[c][kh * SWA_HD:(kh + 1) * SWA_HD, :] for c, kh in pairs}
    ones = jnp.ones((SWA_KEYS, LANES - SWA_HD), BF16)
    vals = {(c, kh): jnp.concatenate(
        [kf_ref[key_start[c]:key_start[c] + SWA_KEYS,
                (SWA_KV_HEADS + kh) * SWA_HD:(SWA_KV_HEADS + kh + 1) * SWA_HD].astype(BF16), ones], axis=1)
            for c, kh in pairs}
    items = [(c, hd) for c in range(n_chunks) for hd in range(SWA_HEADS)]
    q = [q_ref[c * CHUNK:(c + 1) * CHUNK, hd * SWA_HD:(hd + 1) * SWA_HD].astype(BF16) for c, hd in items]
    def group_rows(c, kh):
        return range(c * SWA_HEADS + kh * SWA_GROUP, c * SWA_HEADS + (kh + 1) * SWA_GROUP)

    qk = {(c, kh): _dot(jnp.concatenate([q[i] for i in group_rows(c, kh)], axis=0), keys[(c, kh)])
          for c, kh in pairs}
    logits = [qk[(c, hd // SWA_GROUP)][(hd % SWA_GROUP) * CHUNK:(hd % SWA_GROUP + 1) * CHUNK, 0:SWA_KEYS] + bias_ref[hd]
              for c, hd in items]
    if one_sequence:
        logits = [jnp.where(valid[c], x, -jnp.inf) for x, (c, _) in zip(logits, items)]
    sink = [sink_ref[hd] for _, hd in items]
    m = [jnp.maximum(jnp.max(x, axis=-1, keepdims=True), s) for x, s in zip(logits, sink)]
    p16 = [jnp.exp(x - mm).astype(BF16) for x, mm in zip(logits, m)]
    pv = {(c, kh): _dot(jnp.concatenate([p16[i] for i in group_rows(c, kh)], axis=0), vals[(c, kh)])
          for c, kh in pairs}
    pv = [pv[(c, hd // SWA_GROUP)][(hd % SWA_GROUP) * CHUNK:(hd % SWA_GROUP + 1) * CHUNK, :] for c, hd in items]
    denom = [x[:, SWA_HD:SWA_HD + 1] + jnp.exp(s - mm) for x, s, mm in zip(pv, sink, m)]
    out = [x[:, 0:SWA_HD] / d for x, d in zip(pv, denom)]
    for x, (c, hd) in zip(out, items):
        o_ref[c * CHUNK:(c + 1) * CHUNK, hd * SWA_HD:(hd + 1) * SWA_HD] = x.astype(o_ref.dtype)


def _swa(b16_out, f32_out, kv_hist, bias, sinks, seq_rows):
    n = b16_out.shape[0]
    n_chunks = SWA_CHUNKS_PER_STEP
    rows = n_chunks * CHUNK
    assert n % rows == 0
    one_sequence = kv_hist is None
    if one_sequence:
        assert rows % WINDOW == 0
        per = rows // WINDOW
        prev_arr = f32_out
        prev_spec = _col_block(F32_OUT, "kv", WINDOW, lambda i: jnp.maximum(i * per - 1, 0))
        kf_rows = WINDOW + rows
    else:
        assert seq_rows == CHUNK
        prev_arr = kv_hist
        prev_spec = pl.BlockSpec((n_chunks * WINDOW, KV_COLS), lambda i: (i, 0))
        kf_rows = n_chunks * SWA_KEYS
    kern = functools.partial(_swa_kernel, n_chunks=n_chunks, one_sequence=one_sequence)
    return pl.pallas_call(
        kern,
        grid=(n // rows,),
        in_specs=[
            _col_block(B16_OUT, "qb", rows, lambda i: i),
            _col_block(F32_OUT, "kv", rows, lambda i: i),
            prev_spec,
            pl.BlockSpec((SWA_HEADS, CHUNK, SWA_KEYS), lambda i: (0, 0, 0)),
            pl.BlockSpec(memory_space=pltpu.SMEM),
        ],
        out_specs=pl.BlockSpec((rows, SWA_HEADS * SWA_HD), lambda i: (i, 0)),
        out_shape=jax.ShapeDtypeStruct((n, SWA_HEADS * SWA_HD), BF16),
        scratch_shapes=[pltpu.VMEM((kf_rows + SWA_KEYS_PAD - SWA_KEYS, KV_COLS), F32)],
        compiler_params=_cparams(1),
        name="swa",
    )(b16_out, f32_out, prev_arr, bias, sinks)


def _merge_kernel(x_ref, oa_ref, z_ref, ob_ref, gates_ref, mod_ref, nw_ref, wa_ref, wb_ref, wo_ref, o_ref, oa16_ref,
                  *, spt):
    for h in range(GDN_HEADS):
        cols = slice(h * GDN_DV, (h + 1) * GDN_DV)
        oo = oa_ref[:, cols].astype(F32)
        zz = z_ref[:, cols].astype(F32)
        on = oo * lax.rsqrt(jnp.mean(oo * oo, axis=-1, keepdims=True) + EPS) * nw_ref[...] * (zz * jax.nn.sigmoid(zz))
        oa16_ref[:, cols] = on.astype(BF16)
    ya = _dot(oa16_ref[...], wa_ref[...])
    yb = _dot(ob_ref[...].astype(BF16), wb_ref[...])
    merged = (jax.nn.sigmoid(gates_ref[:, 0:D_MODEL].astype(F32)) * ya
              + jax.nn.sigmoid(gates_ref[:, D_MODEL:2 * D_MODEL].astype(F32)) * yb)
    y = _dot(merged.astype(BF16), wo_ref[...])
    rows = x_ref.shape[0] // spt
    for j in range(spt):
        sl = slice(j * rows, (j + 1) * rows)
        o_ref[sl, :] = x_ref[sl, :] + _mod_row(mod_ref, 5, spt, j) * y[sl, :]


def _merge(x, oa, b16_out, ob, mod, seq_rows, norm_w, wa, wb, wo):
    n = x.shape[0]
    tm, spt = _tile_rows(n, seq_rows, MERGE_ROWS)
    row_spec = lambda w: pl.BlockSpec((tm, w), lambda i: (i, 0))
    return pl.pallas_call(
        functools.partial(_merge_kernel, spt=spt),
        grid=(n // tm,),
        in_specs=[
            row_spec(D_MODEL), row_spec(D_MODEL), _col_block(B16_OUT, "z", tm, lambda i: i), row_spec(D_MODEL),
            _col_block(B16_OUT, "gates", tm, lambda i: i),
            _mod_spec(mod.shape[1]),
            _const_spec((1, GDN_DV)),
            _const_spec((D_MODEL, D_MODEL)), _const_spec((D_MODEL, D_MODEL)), _const_spec((D_MODEL, D_MODEL)),
        ],
        out_specs=row_spec(D_MODEL),
        out_shape=jax.ShapeDtypeStruct((n, D_MODEL), F32),
        scratch_shapes=[pltpu.VMEM((tm, GDN_HEADS * GDN_DV), BF16)],
        compiler_params=_cparams(1),
        name="merge",
    )(x, oa, b16_out, ob, b16_out, mod, norm_w, wa, wb, wo)


def _t5_bucket(rel):
    half = NUM_BUCKETS // 2
    max_exact = half // 2
    n = jnp.abs(rel)
    large = max_exact + (jnp.log(jnp.maximum(n, 1).astype(jnp.float32) / max_exact)
                         / math.log(MAX_DISTANCE / max_exact) * (half - max_exact)).astype(jnp.int32)
    large = jnp.minimum(large, half - 1)
    return jnp.where(rel > 0, half, 0) + jnp.where(n < max_exact, n, large)


def _trunk(x, mod, conv_hist, s_hist, k_hist, v_hist, w):
    bsz, seq, _ = x.shape
    n = bsz * seq
    x2 = x.reshape(n, D_MODEL)
    x1 = _ffn(x2, mod, seq, w["norm_ffn1"], w["ffn1_in"], w["ffn1_out"], w["norm_final"], sub=0, final=False)
    if conv_hist is None:
        hist8 = jnp.zeros((bsz, HIST_ROWS, CONV_DIM), F32)
        s0 = jnp.zeros((bsz, GDN_HEADS, GDN_DK, GDN_DV), F32)
        kv_hist = None
    else:
        hist8 = jnp.concatenate([jnp.zeros((bsz, HIST_ROWS - (CONV_W - 1), CONV_DIM), F32), conv_hist], axis=1)
        s0 = s_hist
        kv_hist = jnp.concatenate([k_hist.reshape(bsz, WINDOW, SWA_KV_HEADS * SWA_HD),
                                   v_hist.reshape(bsz, WINDOW, SWA_KV_HEADS * SWA_HD)], axis=-1)
        kv_hist = kv_hist.reshape(bsz * WINDOW, KV_COLS)
    f32_out, b16_out, tail = _proj(x1, mod, seq, w["norm_mix"], w["w_in"], hist8, w["conv_w"], w["gdn_par"])
    oa, s_new = _gdn(f32_out, s0, seq)
    ob = _swa(b16_out, f32_out, kv_hist, w["bias"], w["sinks"], seq)
    x3 = _merge(x1, oa, b16_out, ob, mod, seq, w["gdn_norm_w"], w["w_a"], w["w_b"], w["w_out"])
    y = _ffn(x3, mod, seq, w["norm_ffn2"], w["ffn2_in"], w["ffn2_out"], w["norm_final"], sub=2, final=True)

    conv_new = tail[:, HIST_ROWS - (CONV_W - 1):]
    half = SWA_KV_HEADS * SWA_HD
    kv_off = F32_OUT["kv"][0]
    first = seq - min(seq, WINDOW)
    kv3 = f32_out.reshape(bsz, seq, F32_OUT_COLS)[:, first:, kv_off:kv_off + KV_COLS]
    if kv_hist is None:
        k_new = kv3[:, :, :half]
        v_new = kv3[:, :, half:]
    else:
        keep = WINDOW - seq
        k_new = jnp.concatenate([k_hist.reshape(bsz, WINDOW, half)[:, WINDOW - keep:], kv3[:, :, :half]], axis=1)
        v_new = jnp.concatenate([v_hist.reshape(bsz, WINDOW, half)[:, WINDOW - keep:], kv3[:, :, half:]], axis=1)
    k_new = k_new.reshape(bsz, WINDOW, SWA_KV_HEADS, SWA_HD)
    v_new = v_new.reshape(bsz, WINDOW, SWA_KV_HEADS, SWA_HD)
    return (y.reshape(bsz, seq, D_MODEL), conv_new[None], s_new[None], k_new[None], v_new[None])


def kernel(x_prompt, x_sample, state_gdn_conv, state_gdn_s, cache_swa_k, cache_swa_v, c_prompt, c_sample,
           norm_ffn1, w_ffn1_in, w_ffn1_out, norm_mix, w_in, gdn_conv_w, gdn_a_log, gdn_dt_bias, gdn_norm_w,
           swa_sinks, rel_bias, w_branch_a, w_branch_b, w_out, norm_ffn2, w_ffn2_in, w_ffn2_out,
           w_ada, b_ada, norm_final):
    bp = x_prompt.shape[0]
    bs = x_sample.shape[0]
    assert bp == 1 and x_sample.shape[1] == CHUNK and cache_swa_k.shape[2] == WINDOW

    n_c = bp + bs
    pad = -n_c % SUBLANES
    c_all = jnp.concatenate([c_prompt, c_sample, jnp.zeros((pad, D_MODEL), F32)], axis=0)
    mod_all = _modulation(c_all, w_ada[0], b_ada[0][None, :])
    mod_p = mod_all[:, :bp]
    mod_s = mod_all[:, bp:n_c]

    w_packed = _pack_w_in(w_in)
    par = jnp.zeros((SUBLANES, LANES), F32)
    par = par.at[0, GDN_HEADS:2 * GDN_HEADS].set(gdn_a_log[0]).at[1, GDN_HEADS:2 * GDN_HEADS].set(gdn_dt_bias[0])

    rel = jnp.arange(SWA_KEYS)[None, :] - WINDOW - jnp.arange(CHUNK)[:, None]
    bias = _bias_table(_t5_bucket(rel).astype(jnp.int32), rel_bias.reshape(-1))

    w = {
        "norm_ffn1": norm_ffn1, "ffn1_in": w_ffn1_in[0].astype(BF16), "ffn1_out": w_ffn1_out[0].astype(BF16),
        "norm_mix": norm_mix, "w_in": w_packed, "conv_w": gdn_conv_w[0], "gdn_par": par,
        "gdn_norm_w": gdn_norm_w, "bias": bias, "sinks": swa_sinks[0],
        "w_a": w_branch_a[0].astype(BF16), "w_b": w_branch_b[0].astype(BF16), "w_out": w_out[0].astype(BF16),
        "norm_ffn2": norm_ffn2, "ffn2_in": w_ffn2_in[0].astype(BF16), "ffn2_out": w_ffn2_out[0].astype(BF16),
        "norm_final": norm_final[None, :],
    }
    y_p, p_conv, p_s, p_k, p_v = _trunk(x_prompt, mod_p, None, None, None, None, w)
    y_s, s_conv, s_s, s_k, s_v = _trunk(x_sample, mod_s, state_gdn_conv[0], state_gdn_s[0],
                                        cache_swa_k[0], cache_swa_v[0], w)
    return (y_p, y_s, p_conv, p_s, p_k, p_v, s_conv, s_s, s_k, s_v)
```

```python
import functools
import math

import jax
import jax.numpy as jnp
from jax import lax
from jax.experimental import pallas as pl
from jax.experimental.pallas import tpu as pltpu

F32 = jnp.float32
BF16 = jnp.bfloat16

D_MODEL = 1024
CHUNK = 64
GDN_HEADS = 8
GDN_DK = 128
GDN_DV = 128
CONV_W = 4
CONV_DIM = GDN_HEADS * (2 * GDN_DK + GDN_DV)
SWA_HEADS = 16
SWA_KV_HEADS = 2
SWA_GROUP = SWA_HEADS // SWA_KV_HEADS
SWA_HD = 64
WINDOW = 128
NUM_BUCKETS = 32
MAX_DISTANCE = 128
D_FF = 2816
N_MOD = 9
EPS = 1e-6

LANES = 128
SUBLANES = 8
VMEM_LIMIT = 56 * 1024 * 1024

FF_CHUNK = D_FF // 2
FFN_ROWS = 1024
HIST_ROWS = SUBLANES
MERGE_ROWS = 512
PACK_ROWS = 256
GDN_CHUNKS_PER_STEP = 4
SWA_CHUNKS_PER_STEP = 4


def _dot(a, b):
    return jnp.dot(a, b, preferred_element_type=F32)


def _dot_nt(a, b):
    return lax.dot_general(a, b, (((1,), (1,)), ((), ())), preferred_element_type=F32)


def _dot_tn(a, b):
    return lax.dot_general(a, b, (((0,), (0,)), ((), ())), preferred_element_type=F32)


def _cparams(n_grid):
    return pltpu.CompilerParams(dimension_semantics=("arbitrary",) * n_grid, vmem_limit_bytes=VMEM_LIMIT)


def _const_spec(shape):
    nd = len(shape)
    return pl.BlockSpec(shape, lambda i: (0,) * nd, pipeline_mode=pl.Buffered(1))


def _mod_kernel(c_ref, w_ref, b_ref, o_ref):
    c = c_ref[...]
    a = (c * jax.nn.sigmoid(c)).astype(BF16)
    o_ref[0] = _dot(a, w_ref[...].astype(BF16)) + b_ref[...]


def _modulation(c_pad, w_ada, b_ada):
    rows = c_pad.shape[0]
    return pl.pallas_call(
        _mod_kernel,
        grid=(N_MOD,),
        in_specs=[
            pl.BlockSpec((rows, D_MODEL), lambda j: (0, 0)),
            pl.BlockSpec((D_MODEL, D_MODEL), lambda j: (0, j)),
            pl.BlockSpec((1, D_MODEL), lambda j: (0, j)),
        ],
        out_specs=pl.BlockSpec((1, rows, D_MODEL), lambda j: (j, 0, 0)),
        out_shape=jax.ShapeDtypeStruct((N_MOD, rows, D_MODEL), F32),
        compiler_params=_cparams(1),
        name="adaln_mod",
    )(c_pad, w_ada, b_ada)


def _mod_row(mod_ref, idx, spt, j):
    if mod_ref.shape[1] == 1:
        return mod_ref[idx, 0:1, :]
    return mod_ref[idx, pl.ds(pl.program_id(0) * spt + j, 1), :]


def _norm_mod_store(h_ref, x_ref, g_ref, mod_ref, sub, spt):
    rows = x_ref.shape[0] // spt
    g = g_ref[...]
    for j in range(spt):
        xs = x_ref[j * rows:(j + 1) * rows, :]
        ms = jnp.mean(xs * xs, axis=-1, keepdims=True)
        y = xs * lax.rsqrt(ms + EPS) * g
        sh = _mod_row(mod_ref, 3 * sub, spt, j)
        sc = _mod_row(mod_ref, 3 * sub + 1, spt, j)
        h_ref[j * rows:(j + 1) * rows, :] = (y * (1.0 + sc) + sh).astype(h_ref.dtype)


def _tile_rows(n_rows, seq_rows, target):
    if seq_rows >= target:
        assert seq_rows % target == 0
        return target, 1
    assert target % seq_rows == 0
    tm = min(target, n_rows)
    assert n_rows % tm == 0
    return tm, tm // seq_rows


def _mod_spec(n_seq_total):
    return pl.BlockSpec((N_MOD, n_seq_total, D_MODEL), lambda i: (0, 0, 0))


def _ffn_kernel(x_ref, mod_ref, g_ref, w1_ref, w2_ref, gf_ref, o_ref, h_ref, *, sub, final, spt):
    _norm_mod_store(h_ref, x_ref, g_ref, mod_ref, sub, spt)
    h = h_ref[...]
    acc = None
    for c in range(D_FF // FF_CHUNK):
        c0 = c * FF_CHUNK
        gate = _dot(h, w1_ref[:, c0:c0 + FF_CHUNK])
        up = _dot(h, w1_ref[:, D_FF + c0:D_FF + c0 + FF_CHUNK])
        a = (gate * jax.nn.sigmoid(gate) * up).astype(BF16)
        part = _dot(a, w2_ref[c0:c0 + FF_CHUNK, :])
        acc = part if acc is None else acc + part
    rows = x_ref.shape[0] // spt
    for j in range(spt):
        sl = slice(j * rows, (j + 1) * rows)
        ga = _mod_row(mod_ref, 3 * sub + 2, spt, j)
        xn = x_ref[sl, :] + 0.5 * ga * acc[sl, :]
        if final:
            ms = jnp.mean(xn * xn, axis=-1, keepdims=True)
            xn = xn * lax.rsqrt(ms + EPS) * gf_ref[...]
        o_ref[sl, :] = xn


def _ffn(x, mod, seq_rows, g, w1, w2, gf, *, sub, final):
    n = x.shape[0]
    tm, spt = _tile_rows(n, seq_rows, FFN_ROWS)
    kern = functools.partial(_ffn_kernel, sub=sub, final=final, spt=spt)
    return pl.pallas_call(
        kern,
        grid=(n // tm,),
        in_specs=[
            pl.BlockSpec((tm, D_MODEL), lambda i: (i, 0)),
            _mod_spec(mod.shape[1]),
            _const_spec((1, D_MODEL)),
            _const_spec((D_MODEL, 2 * D_FF)),
            _const_spec((D_FF, D_MODEL)),
            _const_spec((1, D_MODEL)),
        ],
        out_specs=pl.BlockSpec((tm, D_MODEL), lambda i: (i, 0)),
        out_shape=jax.ShapeDtypeStruct((n, D_MODEL), F32),
        scratch_shapes=[pltpu.VMEM((tm, D_MODEL), BF16)],
        compiler_params=_cparams(1),
        name="ffn_final" if final else "ffn",
    )(x, mod, g, w1, w2, gf)


PROJ_GROUPS = (("qkv", CONV_DIM), ("z", GDN_HEADS * GDN_DV), ("qb", SWA_HEADS * SWA_HD),
               ("kv", 2 * SWA_KV_HEADS * SWA_HD), ("gates", 2 * D_MODEL), ("ba", LANES))
PROJ_COLS = sum(w for _, w in PROJ_GROUPS)


assert math.log2(SWA_HD) % 2 == 0
W_IN_ALIGNED = CONV_DIM + GDN_HEADS * GDN_DV
W_IN_SMALL = 2 * GDN_HEADS
W_IN_REST = PROJ_COLS - W_IN_ALIGNED - LANES


def _pack_w_in_kernel(w_ref, o_ref):
    o_ref[:, 0:W_IN_ALIGNED] = w_ref[:, 0:W_IN_ALIGNED].astype(BF16)
    hi = W_IN_ALIGNED + W_IN_SMALL
    n_qb = SWA_HEADS * SWA_HD
    o_ref[:, W_IN_ALIGNED:W_IN_ALIGNED + n_qb] = (w_ref[:, hi:hi + n_qb] * (SWA_HD ** -0.5)).astype(BF16)
    o_ref[:, W_IN_ALIGNED + n_qb:W_IN_ALIGNED + W_IN_REST] = w_ref[:, hi + n_qb:hi + W_IN_REST].astype(BF16)
    lane = lax.broadcasted_iota(jnp.int32, (w_ref.shape[0], LANES), 1)
    small = jnp.where(lane < W_IN_SMALL, w_ref[:, W_IN_ALIGNED:W_IN_ALIGNED + LANES], 0.0)
    o_ref[:, W_IN_ALIGNED + W_IN_REST:PROJ_COLS] = small.astype(BF16)


def _pack_w_in(w_in):
    n_cols = W_IN_ALIGNED + W_IN_SMALL + W_IN_REST
    assert w_in.shape == (1, D_MODEL, n_cols)
    rows = PACK_ROWS
    return pl.pallas_call(
        _pack_w_in_kernel,
        grid=(D_MODEL // rows,),
        in_specs=[pl.BlockSpec((None, rows, n_cols), lambda i: (0, i, 0))],
        out_specs=pl.BlockSpec((rows, PROJ_COLS), lambda i: (i, 0)),
        out_shape=jax.ShapeDtypeStruct((D_MODEL, PROJ_COLS), BF16),
        compiler_params=_cparams(1),
        name="pack_w_in",
    )(w_in)


PROJ_OFFSETS = {}
_off = 0
for _name, _width in PROJ_GROUPS:
    PROJ_OFFSETS[_name] = (_off, _width)
    _off += _width
PROJ_ROWS = 512


def _proj_kernel(x_ref, mod_ref, g_ref, w_ref, hist_ref, cw_ref, par_ref,
                 qkv_ref, z_ref, qb_ref, kv_ref, gates_ref, gate_ref, tail_ref,
                 h_ref, xp_ref, *, spt, carry):
    pid = pl.program_id(0)
    tm = x_ref.shape[0]
    n_chunks = tm // CHUNK
    seg = CHUNK if carry else HIST_ROWS + CHUNK

    _norm_mod_store(h_ref, x_ref, g_ref, mod_ref, 1, spt)
    h = h_ref[...]

    def group(name, lo=0, width=None):
        c0, full = PROJ_OFFSETS[name]
        width = full if width is None else width
        return _dot(h, w_ref[:, c0 + lo:c0 + lo + width])

    if carry:
        def _load_hist():
            xp_ref[0:HIST_ROWS, :] = hist_ref[0]

        pl.when(pid == 0)(_load_hist)
    else:
        for c in range(n_chunks):
            xp_ref[c * seg:c * seg + HIST_ROWS, :] = hist_ref[c]

    part = GDN_HEADS * GDN_DK

    def project_part(p):
        raw = group("qkv", p * part, part)
        cols = slice(p * part, (p + 1) * part)
        if carry:
            xp_ref[HIST_ROWS:HIST_ROWS + tm, cols] = raw
        else:
            for c in range(n_chunks):
                xp_ref[c * seg + HIST_ROWS:(c + 1) * seg, cols] = raw[c * CHUNK:(c + 1) * CHUNK, :]

    def conv_part(p):
        for c in range(n_chunks):
            for t in range(p * GDN_HEADS, (p + 1) * GDN_HEADS):
                c0 = t * LANES
                win = xp_ref[c * seg:c * seg + HIST_ROWS + CHUNK, c0:c0 + LANES]
                acc = cw_ref[CONV_W - 1:CONV_W, c0:c0 + LANES] * win[HIST_ROWS:, :]
                for s in range(1, CONV_W):
                    tap = pltpu.roll(win, s, 0)[HIST_ROWS:, :]
                    acc = acc + cw_ref[CONV_W - 1 - s:CONV_W - s, c0:c0 + LANES] * tap
                y = acc * jax.nn.sigmoid(acc)
                if p < 2:
                    y = y * lax.rsqrt(jnp.sum(y * y, axis=-1, keepdims=True) + EPS)
                    if p == 0:
                        y = y * (GDN_DK ** -0.5)
                qkv_ref[c * CHUNK:(c + 1) * CHUNK, c0:c0 + LANES] = y

    def gate_math():
        ba = group("ba")
        ri = lax.broadcasted_iota(jnp.int32, (CHUNK, CHUNK), 0)
        ci = lax.broadcasted_iota(jnp.int32, (CHUNK, CHUNK), 1)
        tri = (ri >= ci).astype(BF16)
        lane = lax.broadcasted_iota(jnp.int32, (CHUNK, LANES), 1)
        a_coef = -jnp.exp(par_ref[0:1, :])
        dt_bias = par_ref[1:2, :]
        for c in range(n_chunks):
            bac = ba[c * CHUNK:(c + 1) * CHUNK, :]
            g_all = a_coef * jax.nn.softplus(bac + dt_bias)
            g_hi = g_all.astype(BF16)
            g_r1 = g_all - g_hi.astype(F32)
            g_mid = g_r1.astype(BF16)
            g_lo = (g_r1 - g_mid.astype(F32)).astype(BF16)
            gc = _dot(tri, g_hi) + _dot(tri, g_mid) + _dot(tri, g_lo)
            gate_ref[c * CHUNK:(c + 1) * CHUNK, :] = jnp.where(lane < GDN_HEADS, jax.nn.sigmoid(bac), gc)

    project_part(0)
    project_part(1)
    conv_part(0)
    project_part(2)
    conv_part(1)
    z_ref[...] = group("z").astype(z_ref.dtype)
    gate_math()
    qb_ref[...] = group("qb").astype(qb_ref.dtype)
    conv_part(2)
    kv_ref[...] = group("kv")
    gates_ref[...] = group("gates").astype(gates_ref.dtype)

    if carry:
        def _store_tail():
            tail_ref[0] = xp_ref[tm:tm + HIST_ROWS, :]

        pl.when(pid == pl.num_programs(0) - 1)(_store_tail)
        xp_ref[0:HIST_ROWS, :] = xp_ref[tm:tm + HIST_ROWS, :]
    else:
        for c in range(n_chunks):
            tail_ref[c] = xp_ref[c * seg + CHUNK:c * seg + CHUNK + HIST_ROWS, :]


def _proj(x, mod, seq_rows, g, w_packed, hist8, conv_w, par):
    n = x.shape[0]
    n_seq = n // seq_rows
    carry = n_seq == 1
    tm, spt = _tile_rows(n, seq_rows, PROJ_ROWS)
    if carry:
        n_hist = 1
        hist_map = lambda i: (0, 0, 0)
        xp_rows = HIST_ROWS + tm
    else:
        assert seq_rows == CHUNK
        n_hist = tm // CHUNK
        hist_map = lambda i: (i, 0, 0)
        xp_rows = n_hist * (HIST_ROWS + CHUNK)
    widths = [CONV_DIM, GDN_HEADS * GDN_DV, SWA_HEADS * SWA_HD, KV_COLS, 2 * D_MODEL, LANES]
    return pl.pallas_call(
        functools.partial(_proj_kernel, spt=spt, carry=carry),
        grid=(n // tm,),
        in_specs=[
            pl.BlockSpec((tm, D_MODEL), lambda i: (i, 0)),
            _mod_spec(mod.shape[1]),
            _const_spec((1, D_MODEL)),
            _const_spec((D_MODEL, PROJ_COLS)),
            pl.BlockSpec((n_hist, HIST_ROWS, CONV_DIM), hist_map),
            _const_spec((CONV_W, CONV_DIM)),
            _const_spec((SUBLANES, LANES)),
        ],
        out_specs=[pl.BlockSpec((tm, w), lambda i: (i, 0)) for w in widths]
        + [pl.BlockSpec((n_hist, HIST_ROWS, CONV_DIM), hist_map)],
        out_shape=[jax.ShapeDtypeStruct((n, w), BF16 if i in (1, 2, 4) else F32) for i, w in enumerate(widths)]
        + [jax.ShapeDtypeStruct((n_seq, HIST_ROWS, CONV_DIM), F32)],
        scratch_shapes=[pltpu.VMEM((tm, D_MODEL), BF16), pltpu.VMEM((xp_rows, CONV_DIM), F32)],
        compiler_params=_cparams(1),
        name="in_proj",
    )(x, mod, g, w_packed, hist8, conv_w, par)


def _unit_lower_inverses(ls, eye, level_masks):
    ts = [eye - jnp.where(level_masks[0], l, 0.0) for l in ls]
    for mask in level_masks[1:]:
        lk = [jnp.where(mask, l, 0.0).astype(BF16) for l in ls]
        tp = [t.astype(BF16) for t in ts]
        m1 = [_dot(a, b) for a, b in zip(lk, tp)]
        m2 = [_dot(a, b.astype(BF16)) for a, b in zip(tp, m1)]
        ts = [t - m for t, m in zip(ts, m2)]
    return ts


def _gdn_kernel(qkv_ref, gate_ref, s0_ref, o_ref, sout_ref, s_ref, *, n_chunks, carry):
    pid = pl.program_id(0)

    if carry:
        def _load_state():
            s_ref[0] = s0_ref[0]

        pl.when(pid == 0)(_load_state)
    else:
        for c in range(n_chunks):
            s_ref[c] = s0_ref[c]

    ri = lax.broadcasted_iota(jnp.int32, (CHUNK, CHUNK), 0)
    ci = lax.broadcasted_iota(jnp.int32, (CHUNK, CHUNK), 1)
    causal = ri >= ci
    strict = ri > ci
    eye = (ri == ci).astype(F32)
    level_masks = []
    for lvl in range(int(math.log2(CHUNK))):
        same_block = (ri >> (lvl + 1)) == (ci >> (lvl + 1))
        level_masks.append(same_block & (((ri >> lvl) & 1) == 1) & (((ci >> lvl) & 1) == 0))

    items = [(c, h) for c in range(n_chunks) for h in range(GDN_HEADS)]
    gate = [gate_ref[c * CHUNK:(c + 1) * CHUNK, :] for c in range(n_chunks)]
    egc_all = [jnp.exp(g) for g in gate]
    g_last_all = [g[CHUNK - 1:CHUNK, :] for g in gate]
    kdf_all = [jnp.exp(gl - g) for gl, g in zip(g_last_all, gate)]
    eg_last_all = [jnp.exp(gl) for gl in g_last_all]

    def tile(c, t):
        return qkv_ref[c * CHUNK:(c + 1) * CHUNK, t * LANES:(t + 1) * LANES]

    def col(arrs, c, h):
        return arrs[c][:, GDN_HEADS + h:GDN_HEADS + h + 1]

    q = [tile(c, h) for c, h in items]
    k = [tile(c, GDN_HEADS + h) for c, h in items]
    v = [tile(c, 2 * GDN_HEADS + h) for c, h in items]
    beta = [gate[c][:, h:h + 1] for c, h in items]
    gcol = [col(gate, c, h) for c, h in items]
    grow = [jnp.sum(g * eye, axis=0, keepdims=True) for g in gcol]
    decay = [jnp.where(causal, jnp.exp(jnp.where(causal, gc - gr, 0.0)), 0.0) for gc, gr in zip(gcol, grow)]
    kb = [x * b for x, b in zip(k, beta)]
    k16 = [x.astype(BF16) for x in k]
    kq = [_dot_nt(jnp.concatenate([a.astype(BF16), b.astype(BF16)], axis=0), c) for a, b, c in zip(kb, q, k16)]
    kk = [x[0:CHUNK, :] for x in kq]
    qk = [x[CHUNK:, :] for x in kq]
    ls = [jnp.where(strict, a * d, 0.0) for a, d in zip(kk, decay)]
    intra16 = [jnp.where(causal, a * d, 0.0).astype(BF16) for a, d in zip(qk, decay)]
    ts = _unit_lower_inverses(ls, eye, level_masks)
    ys = [(t - eye).astype(BF16) for t in ts]
    egc = [col(egc_all, c, h) for c, h in items]
    vb = [x * b for x, b in zip(v, beta)]
    kbe = [x * e for x, e in zip(kb, egc)]
    rhs = [jnp.concatenate([a, b], axis=1) for a, b in zip(vb, kbe)]
    sol = [x + _dot(y, x.astype(BF16)) for x, y in zip(rhs, ys)]
    u = [x[:, 0:GDN_DV] for x in sol]
    w16 = [x[:, GDN_DV:].astype(BF16) for x in sol]
    qg16 = [(x * e).astype(BF16) for x, e in zip(q, egc)]
    kd16 = [(x * col(kdf_all, c, h)).astype(BF16) for x, (c, h) in zip(k, items)]
    eg_last = [col(eg_last_all, c, h) for c, h in items]
    wq16 = [jnp.concatenate([a, b], axis=0) for a, b in zip(w16, qg16)]

    o = [None] * len(items)
    if carry:
        state = [s_ref[0, h] for h in range(GDN_HEADS)]
    for c in range(n_chunks):
        idx = [c * GDN_HEADS + h for h in range(GDN_HEADS)]
        if not carry:
            state = [s_ref[c, h] for h in range(GDN_HEADS)]
        s16 = [s.astype(BF16) for s in state]
        ws_qs = [_dot(wq16[i], s) for i, s in zip(idx, s16)]
        v_new = [u[i] - x[0:CHUNK, :] for i, x in zip(idx, ws_qs)]
        vn16 = [x.astype(BF16) for x in v_new]
        for i, x, vn in zip(idx, ws_qs, vn16):
            o[i] = x[CHUNK:, :] + _dot(intra16[i], vn)
        state = [s * eg_last[i] + _dot_tn(kd16[i], vn) for i, s, vn in zip(idx, state, vn16)]
        if not carry:
            for h in range(GDN_HEADS):
                sout_ref[c, h] = state[h]

    for (c, h), oo in zip(items, o):
        o_ref[c * CHUNK:(c + 1) * CHUNK, h * GDN_DV:(h + 1) * GDN_DV] = oo.astype(o_ref.dtype)

    if carry:
        for h in range(GDN_HEADS):
            s_ref[0, h] = state[h]

        def _store_state():
            sout_ref[0] = s_ref[0]

        pl.when(pid == pl.num_programs(0) - 1)(_store_state)


def _gdn(qkv, gate, s0, seq_rows):
    n = qkv.shape[0]
    n_seq = n // seq_rows
    carry = n_seq == 1
    n_chunks = GDN_CHUNKS_PER_STEP
    rows = n_chunks * CHUNK
    assert n % rows == 0
    if carry:
        n_state = 1
        s_map = lambda i: (0, 0, 0, 0)
    else:
        assert seq_rows == CHUNK
        n_state = n_chunks
        s_map = lambda i: (i, 0, 0, 0)
    kern = functools.partial(_gdn_kernel, n_chunks=n_chunks, carry=carry)
    return pl.pallas_call(
        kern,
        grid=(n // rows,),
        in_specs=[
            pl.BlockSpec((rows, CONV_DIM), lambda i: (i, 0)),
            pl.BlockSpec((rows, LANES), lambda i: (i, 0)),
            pl.BlockSpec((n_state, GDN_HEADS, GDN_DK, GDN_DV), s_map),
        ],
        out_specs=[
            pl.BlockSpec((rows, GDN_HEADS * GDN_DV), lambda i: (i, 0)),
            pl.BlockSpec((n_state, GDN_HEADS, GDN_DK, GDN_DV), s_map),
        ],
        out_shape=[
            jax.ShapeDtypeStruct((n, GDN_HEADS * GDN_DV), BF16),
            jax.ShapeDtypeStruct((n_seq, GDN_HEADS, GDN_DK, GDN_DV), F32),
        ],
        scratch_shapes=[pltpu.VMEM((n_state, GDN_HEADS, GDN_DK, GDN_DV), F32)],
        compiler_params=_cparams(1),
        name="gdn",
    )(qkv, gate, s0)


SWA_KEYS = WINDOW + CHUNK
SWA_KEYS_PAD = 2 * LANES
KV_COLS = 2 * SWA_KV_HEADS * SWA_HD


def _bias_kernel(bucket_ref, rb_ref, o_ref):
    bucket = bucket_ref[...]
    h = pl.program_id(0)
    acc = jnp.zeros((CHUNK, SWA_KEYS), F32)
    for b in range(NUM_BUCKETS):
        acc = jnp.where(bucket == b, rb_ref[b * SWA_HEADS + h], acc)
    o_ref[0] = acc


def _bias_table(bucket, rel_bias_flat):
    return pl.pallas_call(
        _bias_kernel,
        grid=(SWA_HEADS,),
        in_specs=[
            pl.BlockSpec((CHUNK, SWA_KEYS), lambda h: (0, 0)),
            pl.BlockSpec(memory_space=pltpu.SMEM),
        ],
        out_specs=pl.BlockSpec((1, CHUNK, SWA_KEYS), lambda h: (h, 0, 0)),
        out_shape=jax.ShapeDtypeStruct((SWA_HEADS, CHUNK, SWA_KEYS), F32),
        compiler_params=_cparams(1),
        name="swa_bias",
    )(bucket, rel_bias_flat)


def _swa_kernel(q_ref, kvc_ref, kvp_ref, bias_ref, sink_ref, o_ref, kf_ref, *, n_chunks, one_sequence):
    n_pad = SWA_KEYS_PAD - SWA_KEYS
    kf_ref[kf_ref.shape[0] - n_pad:, :] = jnp.zeros((n_pad, KV_COLS), F32)
    if one_sequence:
        kf_ref[0:WINDOW, :] = kvp_ref[...]
        kf_ref[WINDOW:WINDOW + n_chunks * CHUNK, :] = kvc_ref[...]
        key_start = [c * CHUNK for c in range(n_chunks)]
        key_pos = lax.broadcasted_iota(jnp.int32, (CHUNK, SWA_KEYS), 1)
        tile_start = pl.program_id(0) * (n_chunks * CHUNK)
        valid = [key_pos + (tile_start + c * CHUNK - WINDOW) >= 0 for c in range(n_chunks)]
    else:
        for c in range(n_chunks):
            kf_ref[c * SWA_KEYS:c * SWA_KEYS + WINDOW, :] = kvp_ref[c * WINDOW:(c + 1) * WINDOW, :]
            kf_ref[c * SWA_KEYS + WINDOW:(c + 1) * SWA_KEYS, :] = kvc_ref[c * CHUNK:(c + 1) * CHUNK, :]
        key_start = [c * SWA_KEYS for c in range(n_chunks)]

    pairs = [(c, kh) for c in range(n_chunks) for kh in range(SWA_KV_HEADS)]
    half = SWA_KV_HEADS * SWA_HD
    keys_t = [kf_ref[key_start[c]:key_start[c] + SWA_KEYS_PAD, 0:half].T.astype(BF16) for c in range(n_chunks)]
    keys = {(c, kh): keys_t[c][kh * SWA_HD:(kh + 1) * SWA_HD, :] for c, kh in pairs}
    ones = jnp.ones((SWA_KEYS, LANES - SWA_HD), BF16)
    vals = {(c, kh): jnp.concatenate(
        [kf_ref[key_start[c]:key_start[c] + SWA_KEYS,
                (SWA_KV_HEADS + kh) * SWA_HD:(SWA_KV_HEADS + kh + 1) * SWA_HD].astype(BF16), ones], axis=1)
            for c, kh in pairs}
    items = [(c, hd) for c in range(n_chunks) for hd in range(SWA_HEADS)]
    q = [q_ref[c * CHUNK:(c + 1) * CHUNK, hd * SWA_HD:(hd + 1) * SWA_HD] for c, hd in items]
    def group_rows(c, kh):
        return range(c * SWA_HEADS + kh * SWA_GROUP, c * SWA_HEADS + (kh + 1) * SWA_GROUP)

    qk = {(c, kh): _dot(jnp.concatenate([q[i] for i in group_rows(c, kh)], axis=0), keys[(c, kh)])
          for c, kh in pairs}
    logits = [qk[(c, hd // SWA_GROUP)][(hd % SWA_GROUP) * CHUNK:(hd % SWA_GROUP + 1) * CHUNK, 0:SWA_KEYS] + bias_ref[hd]
              for c, hd in items]
    if one_sequence:
        logits = [jnp.where(valid[c], x, -jnp.inf) for x, (c, _) in zip(logits, items)]
    sink = [sink_ref[hd] for _, hd in items]
    m = [jnp.maximum(jnp.max(x, axis=-1, keepdims=True), s) for x, s in zip(logits, sink)]
    p16 = [jnp.exp(x - mm).astype(BF16) for x, mm in zip(logits, m)]
    pv = {(c, kh): _dot(jnp.concatenate([p16[i] for i in group_rows(c, kh)], axis=0), vals[(c, kh)])
          for c, kh in pairs}
    pv = [pv[(c, hd // SWA_GROUP)][(hd % SWA_GROUP) * CHUNK:(hd % SWA_GROUP + 1) * CHUNK, :] for c, hd in items]
    denom = [x[:, SWA_HD:SWA_HD + 1] + jnp.exp(s - mm) for x, s, mm in zip(pv, sink, m)]
    out = [x[:, 0:SWA_HD] / d for x, d in zip(pv, denom)]
    for x, (c, hd) in zip(out, items):
        o_ref[c * CHUNK:(c + 1) * CHUNK, hd * SWA_HD:(hd + 1) * SWA_HD] = x.astype(o_ref.dtype)


def _swa(qb, kv, kv_hist, bias, sinks, seq_rows):
    n = qb.shape[0]
    n_chunks = SWA_CHUNKS_PER_STEP
    rows = n_chunks * CHUNK
    assert n % rows == 0
    one_sequence = kv_hist is None
    if one_sequence:
        assert rows % WINDOW == 0
        per = rows // WINDOW
        prev_arr = kv
        prev_spec = pl.BlockSpec((WINDOW, KV_COLS), lambda i: (jnp.maximum(i * per - 1, 0), 0))
        kf_rows = WINDOW + rows
    else:
        assert seq_rows == CHUNK
        prev_arr = kv_hist
        prev_spec = pl.BlockSpec((n_chunks * WINDOW, KV_COLS), lambda i: (i, 0))
        kf_rows = n_chunks * SWA_KEYS
    kern = functools.partial(_swa_kernel, n_chunks=n_chunks, one_sequence=one_sequence)
    return pl.pallas_call(
        kern,
        grid=(n // rows,),
        in_specs=[
            pl.BlockSpec((rows, SWA_HEADS * SWA_HD), lambda i: (i, 0)),
            pl.BlockSpec((rows, KV_COLS), lambda i: (i, 0)),
            prev_spec,
            pl.BlockSpec((SWA_HEADS, CHUNK, SWA_KEYS), lambda i: (0, 0, 0)),
            pl.BlockSpec(memory_space=pltpu.SMEM),
        ],
        out_specs=pl.BlockSpec((rows, SWA_HEADS * SWA_HD), lambda i: (i, 0)),
        out_shape=jax.ShapeDtypeStruct((n, SWA_HEADS * SWA_HD), BF16),
        scratch_shapes=[pltpu.VMEM((kf_rows + SWA_KEYS_PAD - SWA_KEYS, KV_COLS), F32)],
        compiler_params=_cparams(1),
        name="swa",
    )(qb, kv, prev_arr, bias, sinks)


def _merge_kernel(x_ref, oa_ref, z_ref, ob_ref, gates_ref, mod_ref, nw_ref, wa_ref, wb_ref, wo_ref, o_ref, oa16_ref,
                  *, spt):
    for h in range(GDN_HEADS):
        cols = slice(h * GDN_DV, (h + 1) * GDN_DV)
        oo = oa_ref[:, cols].astype(F32)
        zz = z_ref[:, cols].astype(F32)
        on = oo * lax.rsqrt(jnp.mean(oo * oo, axis=-1, keepdims=True) + EPS) * nw_ref[...] * (zz * jax.nn.sigmoid(zz))
        oa16_ref[:, cols] = on.astype(BF16)
    ya = _dot(oa16_ref[...], wa_ref[...])
    yb = _dot(ob_ref[...], wb_ref[...])
    merged = (jax.nn.sigmoid(gates_ref[:, 0:D_MODEL].astype(F32)) * ya
              + jax.nn.sigmoid(gates_ref[:, D_MODEL:2 * D_MODEL].astype(F32)) * yb)
    y = _dot(merged.astype(BF16), wo_ref[...])
    rows = x_ref.shape[0] // spt
    for j in range(spt):
        sl = slice(j * rows, (j + 1) * rows)
        o_ref[sl, :] = x_ref[sl, :] + _mod_row(mod_ref, 5, spt, j) * y[sl, :]


def _merge(x, oa, z, ob, gates, mod, seq_rows, norm_w, wa, wb, wo):
    n = x.shape[0]
    tm, spt = _tile_rows(n, seq_rows, MERGE_ROWS)
    row_spec = lambda w: pl.BlockSpec((tm, w), lambda i: (i, 0))
    return pl.pallas_call(
        functools.partial(_merge_kernel, spt=spt),
        grid=(n // tm,),
        in_specs=[
            row_spec(D_MODEL), row_spec(D_MODEL), row_spec(D_MODEL), row_spec(D_MODEL), row_spec(2 * D_MODEL),
            _mod_spec(mod.shape[1]),
            _const_spec((1, GDN_DV)),
            _const_spec((D_MODEL, D_MODEL)), _const_spec((D_MODEL, D_MODEL)), _const_spec((D_MODEL, D_MODEL)),
        ],
        out_specs=row_spec(D_MODEL),
        out_shape=jax.ShapeDtypeStruct((n, D_MODEL), F32),
        scratch_shapes=[pltpu.VMEM((tm, GDN_HEADS * GDN_DV), BF16)],
        compiler_params=_cparams(1),
        name="merge",
    )(x, oa, z, ob, gates, mod, norm_w, wa, wb, wo)


def _t5_bucket(rel):
    half = NUM_BUCKETS // 2
    max_exact = half // 2
    n = jnp.abs(rel)
    large = max_exact + (jnp.log(jnp.maximum(n, 1).astype(jnp.float32) / max_exact)
                         / math.log(MAX_DISTANCE / max_exact) * (half - max_exact)).astype(jnp.int32)
    large = jnp.minimum(large, half - 1)
    return jnp.where(rel > 0, half, 0) + jnp.where(n < max_exact, n, large)


def _trunk(x, mod, conv_hist, s_hist, k_hist, v_hist, w):
    bsz, seq, _ = x.shape
    n = bsz * seq
    x2 = x.reshape(n, D_MODEL)
    x1 = _ffn(x2, mod, seq, w["norm_ffn1"], w["ffn1_in"], w["ffn1_out"], w["norm_final"], sub=0, final=False)
    if conv_hist is None:
        hist8 = jnp.zeros((bsz, HIST_ROWS, CONV_DIM), F32)
        s0 = jnp.zeros((bsz, GDN_HEADS, GDN_DK, GDN_DV), F32)
        kv_hist = None
    else:
        hist8 = jnp.concatenate([jnp.zeros((bsz, HIST_ROWS - (CONV_W - 1), CONV_DIM), F32), conv_hist], axis=1)
        s0 = s_hist
        kv_hist = jnp.concatenate([k_hist.reshape(bsz, WINDOW, SWA_KV_HEADS * SWA_HD),
                                   v_hist.reshape(bsz, WINDOW, SWA_KV_HEADS * SWA_HD)], axis=-1)
        kv_hist = kv_hist.reshape(bsz * WINDOW, KV_COLS)
    qkv, z, qb, kv, gates, gate, tail = _proj(x1, mod, seq, w["norm_mix"], w["w_in"], hist8, w["conv_w"], w["gdn_par"])
    oa, s_new = _gdn(qkv, gate, s0, seq)
    ob = _swa(qb, kv, kv_hist, w["bias"], w["sinks"], seq)
    x3 = _merge(x1, oa, z, ob, gates, mod, seq, w["gdn_norm_w"], w["w_a"], w["w_b"], w["w_out"])
    y = _ffn(x3, mod, seq, w["norm_ffn2"], w["ffn2_in"], w["ffn2_out"], w["norm_final"], sub=2, final=True)

    conv_new = tail[:, HIST_ROWS - (CONV_W - 1):]
    half = SWA_KV_HEADS * SWA_HD
    kv3 = kv.reshape(bsz, seq, KV_COLS)
    if kv_hist is None:
        k_new = kv3[:, seq - WINDOW:, :half]
        v_new = kv3[:, seq - WINDOW:, half:]
    else:
        keep = WINDOW - seq
        k_new = jnp.concatenate([k_hist.reshape(bsz, WINDOW, half)[:, WINDOW - keep:], kv3[:, :, :half]], axis=1)
        v_new = jnp.concatenate([v_hist.reshape(bsz, WINDOW, half)[:, WINDOW - keep:], kv3[:, :, half:]], axis=1)
    k_new = k_new.reshape(bsz, WINDOW, SWA_KV_HEADS, SWA_HD)
    v_new = v_new.reshape(bsz, WINDOW, SWA_KV_HEADS, SWA_HD)
    return (y.reshape(bsz, seq, D_MODEL), conv_new[None], s_new[None], k_new[None], v_new[None])


def kernel(x_prompt, x_sample, state_gdn_conv, state_gdn_s, cache_swa_k, cache_swa_v, c_prompt, c_sample,
           norm_ffn1, w_ffn1_in, w_ffn1_out, norm_mix, w_in, gdn_conv_w, gdn_a_log, gdn_dt_bias, gdn_norm_w,
           swa_sinks, rel_bias, w_branch_a, w_branch_b, w_out, norm_ffn2, w_ffn2_in, w_ffn2_out,
           w_ada, b_ada, norm_final):
    bp = x_prompt.shape[0]
    bs = x_sample.shape[0]
    assert bp == 1 and x_sample.shape[1] == CHUNK and cache_swa_k.shape[2] == WINDOW

    n_c = bp + bs
    pad = -n_c % SUBLANES
    c_all = jnp.concatenate([c_prompt, c_sample, jnp.zeros((pad, D_MODEL), F32)], axis=0)
    mod_all = _modulation(c_all, w_ada[0], b_ada[0][None, :])
    mod_p = mod_all[:, :bp]
    mod_s = mod_all[:, bp:n_c]

    w_packed = _pack_w_in(w_in)
    par = jnp.zeros((SUBLANES, LANES), F32)
    par = par.at[0, GDN_HEADS:2 * GDN_HEADS].set(gdn_a_log[0]).at[1, GDN_HEADS:2 * GDN_HEADS].set(gdn_dt_bias[0])

    rel = jnp.arange(SWA_KEYS)[None, :] - WINDOW - jnp.arange(CHUNK)[:, None]
    bias = _bias_table(_t5_bucket(rel).astype(jnp.int32), rel_bias.reshape(-1))

    w = {
        "norm_ffn1": norm_ffn1, "ffn1_in": w_ffn1_in[0].astype(BF16), "ffn1_out": w_ffn1_out[0].astype(BF16),
        "norm_mix": norm_mix, "w_in": w_packed, "conv_w": gdn_conv_w[0], "gdn_par": par,
        "gdn_norm_w": gdn_norm_w, "bias": bias, "sinks": swa_sinks[0],
        "w_a": w_branch_a[0].astype(BF16), "w_b": w_branch_b[0].astype(BF16), "w_out": w_out[0].astype(BF16),
        "norm_ffn2": norm_ffn2, "ffn2_in": w_ffn2_in[0].astype(BF16), "ffn2_out": w_ffn2_out[0].astype(BF16),
        "norm_final": norm_final[None, :],
    }
    y_p, p_conv, p_s, p_k, p_v = _trunk(x_prompt, mod_p, None, None, None, None, w)
    y_s, s_conv, s_s, s_k, s_v = _trunk(x_sample, mod_s, state_gdn_conv[0], state_gdn_s[0],
                                        cache_swa_k[0], cache_swa_v[0], w)
    return (y_p, y_s, p_conv, p_s, p_k, p_v, s_conv, s_s, s_k, s_v)
```

```python
import functools
import math

import jax
import jax.numpy as jnp
from jax import lax
from jax.experimental import pallas as pl
from jax.experimental.pallas import tpu as pltpu

F32 = jnp.float32
BF16 = jnp.bfloat16

D_MODEL = 1024
CHUNK = 64
GDN_HEADS = 8
GDN_DK = 128
GDN_DV = 128
CONV_W = 4
CONV_DIM = GDN_HEADS * (2 * GDN_DK + GDN_DV)
SWA_HEADS = 16
SWA_KV_HEADS = 2
SWA_GROUP = SWA_HEADS // SWA_KV_HEADS
SWA_HD = 64
WINDOW = 128
NUM_BUCKETS = 32
MAX_DISTANCE = 128
D_FF = 2816
N_MOD = 9
EPS = 1e-6

LANES = 128
SUBLANES = 8
VMEM_LIMIT = 56 * 1024 * 1024

FF_CHUNK = D_FF // 2
FFN_ROWS = 1024
HIST_ROWS = SUBLANES
MERGE_ROWS = 512
PACK_ROWS = 256
GDN_CHUNKS_PER_STEP = 4
SWA_CHUNKS_PER_STEP = 8


def _dot(a, b):
    return jnp.dot(a, b, preferred_element_type=F32)


def _dot_nt(a, b):
    return lax.dot_general(a, b, (((1,), (1,)), ((), ())), preferred_element_type=F32)


def _dot_tn(a, b):
    return lax.dot_general(a, b, (((0,), (0,)), ((), ())), preferred_element_type=F32)


def _cparams(n_grid):
    return pltpu.CompilerParams(dimension_semantics=("arbitrary",) * n_grid, vmem_limit_bytes=VMEM_LIMIT)


def _const_spec(shape):
    nd = len(shape)
    return pl.BlockSpec(shape, lambda i: (0,) * nd, pipeline_mode=pl.Buffered(1))


def _mod_kernel(c_ref, w_ref, b_ref, o_ref):
    c = c_ref[...]
    a = (c * jax.nn.sigmoid(c)).astype(BF16)
    o_ref[0] = _dot(a, w_ref[...].astype(BF16)) + b_ref[...]


def _modulation(c_pad, w_ada, b_ada):
    rows = c_pad.shape[0]
    return pl.pallas_call(
        _mod_kernel,
        grid=(N_MOD,),
        in_specs=[
            pl.BlockSpec((rows, D_MODEL), lambda j: (0, 0)),
            pl.BlockSpec((D_MODEL, D_MODEL), lambda j: (0, j)),
            pl.BlockSpec((1, D_MODEL), lambda j: (0, j)),
        ],
        out_specs=pl.BlockSpec((1, rows, D_MODEL), lambda j: (j, 0, 0)),
        out_shape=jax.ShapeDtypeStruct((N_MOD, rows, D_MODEL), F32),
        compiler_params=_cparams(1),
        name="adaln_mod",
    )(c_pad, w_ada, b_ada)


def _mod_row(mod_ref, idx, spt, j):
    if mod_ref.shape[1] == 1:
        return mod_ref[idx, 0:1, :]
    return mod_ref[idx, pl.ds(pl.program_id(0) * spt + j, 1), :]


def _norm_mod_store(h_ref, x_ref, g_ref, mod_ref, sub, spt):
    rows = x_ref.shape[0] // spt
    g = g_ref[...]
    for j in range(spt):
        xs = x_ref[j * rows:(j + 1) * rows, :]
        ms = jnp.mean(xs * xs, axis=-1, keepdims=True)
        y = xs * lax.rsqrt(ms + EPS) * g
        sh = _mod_row(mod_ref, 3 * sub, spt, j)
        sc = _mod_row(mod_ref, 3 * sub + 1, spt, j)
        h_ref[j * rows:(j + 1) * rows, :] = (y * (1.0 + sc) + sh).astype(h_ref.dtype)


def _tile_rows(n_rows, seq_rows, target):
    if seq_rows >= target:
        assert seq_rows % target == 0
        return target, 1
    assert target % seq_rows == 0
    tm = min(target, n_rows)
    assert n_rows % tm == 0
    return tm, tm // seq_rows


def _mod_spec(n_seq_total):
    return pl.BlockSpec((N_MOD, n_seq_total, D_MODEL), lambda i: (0, 0, 0))


def _ffn_kernel(x_ref, mod_ref, g_ref, w1_ref, w2_ref, gf_ref, o_ref, h_ref, *, sub, final, spt):
    _norm_mod_store(h_ref, x_ref, g_ref, mod_ref, sub, spt)
    h = h_ref[...]
    acc = None
    for c in range(D_FF // FF_CHUNK):
        c0 = c * FF_CHUNK
        gate = _dot(h, w1_ref[:, c0:c0 + FF_CHUNK])
        up = _dot(h, w1_ref[:, D_FF + c0:D_FF + c0 + FF_CHUNK])
        a = (gate * jax.nn.sigmoid(gate) * up).astype(BF16)
        part = _dot(a, w2_ref[c0:c0 + FF_CHUNK, :])
        acc = part if acc is None else acc + part
    rows = x_ref.shape[0] // spt
    for j in range(spt):
        sl = slice(j * rows, (j + 1) * rows)
        ga = _mod_row(mod_ref, 3 * sub + 2, spt, j)
        xn = x_ref[sl, :] + 0.5 * ga * acc[sl, :]
        if final:
            ms = jnp.mean(xn * xn, axis=-1, keepdims=True)
            xn = xn * lax.rsqrt(ms + EPS) * gf_ref[...]
        o_ref[sl, :] = xn


def _ffn(x, mod, seq_rows, g, w1, w2, gf, *, sub, final):
    n = x.shape[0]
    tm, spt = _tile_rows(n, seq_rows, FFN_ROWS)
    kern = functools.partial(_ffn_kernel, sub=sub, final=final, spt=spt)
    return pl.pallas_call(
        kern,
        grid=(n // tm,),
        in_specs=[
            pl.BlockSpec((tm, D_MODEL), lambda i: (i, 0)),
            _mod_spec(mod.shape[1]),
            _const_spec((1, D_MODEL)),
            _const_spec((D_MODEL, 2 * D_FF)),
            _const_spec((D_FF, D_MODEL)),
            _const_spec((1, D_MODEL)),
        ],
        out_specs=pl.BlockSpec((tm, D_MODEL), lambda i: (i, 0)),
        out_shape=jax.ShapeDtypeStruct((n, D_MODEL), F32),
        scratch_shapes=[pltpu.VMEM((tm, D_MODEL), BF16)],
        compiler_params=_cparams(1),
        name="ffn_final" if final else "ffn",
    )(x, mod, g, w1, w2, gf)


PROJ_GROUPS = (("qkv", CONV_DIM), ("z", GDN_HEADS * GDN_DV), ("qb", SWA_HEADS * SWA_HD),
               ("kv", 2 * SWA_KV_HEADS * SWA_HD), ("gates", 2 * D_MODEL), ("ba", LANES))
PROJ_COLS = sum(w for _, w in PROJ_GROUPS)


assert math.log2(SWA_HD) % 2 == 0
W_IN_ALIGNED = CONV_DIM + GDN_HEADS * GDN_DV
W_IN_SMALL = 2 * GDN_HEADS
W_IN_REST = PROJ_COLS - W_IN_ALIGNED - LANES


def _pack_w_in_kernel(w_ref, o_ref):
    o_ref[:, 0:W_IN_ALIGNED] = w_ref[:, 0:W_IN_ALIGNED].astype(BF16)
    hi = W_IN_ALIGNED + W_IN_SMALL
    n_qb = SWA_HEADS * SWA_HD
    o_ref[:, W_IN_ALIGNED:W_IN_ALIGNED + n_qb] = (w_ref[:, hi:hi + n_qb] * (SWA_HD ** -0.5)).astype(BF16)
    o_ref[:, W_IN_ALIGNED + n_qb:W_IN_ALIGNED + W_IN_REST] = w_ref[:, hi + n_qb:hi + W_IN_REST].astype(BF16)
    lane = lax.broadcasted_iota(jnp.int32, (w_ref.shape[0], LANES), 1)
    small = jnp.where(lane < W_IN_SMALL, w_ref[:, W_IN_ALIGNED:W_IN_ALIGNED + LANES], 0.0)
    o_ref[:, W_IN_ALIGNED + W_IN_REST:PROJ_COLS] = small.astype(BF16)


def _pack_w_in(w_in):
    n_cols = W_IN_ALIGNED + W_IN_SMALL + W_IN_REST
    assert w_in.shape == (1, D_MODEL, n_cols)
    rows = PACK_ROWS
    return pl.pallas_call(
        _pack_w_in_kernel,
        grid=(D_MODEL // rows,),
        in_specs=[pl.BlockSpec((None, rows, n_cols), lambda i: (0, i, 0))],
        out_specs=pl.BlockSpec((rows, PROJ_COLS), lambda i: (i, 0)),
        out_shape=jax.ShapeDtypeStruct((D_MODEL, PROJ_COLS), BF16),
        compiler_params=_cparams(1),
        name="pack_w_in",
    )(w_in)


PROJ_OFFSETS = {}
_off = 0
for _name, _width in PROJ_GROUPS:
    PROJ_OFFSETS[_name] = (_off, _width)
    _off += _width
PROJ_ROWS = 512


def _proj_kernel(x_ref, mod_ref, g_ref, w_ref, hist_ref, cw_ref, par_ref,
                 qkv_ref, z_ref, qb_ref, kv_ref, gates_ref, gate_ref, tail_ref,
                 h_ref, xp_ref, *, spt, carry):
    pid = pl.program_id(0)
    tm = x_ref.shape[0]
    n_chunks = tm // CHUNK
    seg = CHUNK if carry else HIST_ROWS + CHUNK

    _norm_mod_store(h_ref, x_ref, g_ref, mod_ref, 1, spt)
    h = h_ref[...]

    def group(name, lo=0, width=None):
        c0, full = PROJ_OFFSETS[name]
        width = full if width is None else width
        return _dot(h, w_ref[:, c0 + lo:c0 + lo + width])

    if carry:
        def _load_hist():
            xp_ref[0:HIST_ROWS, :] = hist_ref[0]

        pl.when(pid == 0)(_load_hist)
    else:
        for c in range(n_chunks):
            xp_ref[c * seg:c * seg + HIST_ROWS, :] = hist_ref[c]

    part = GDN_HEADS * GDN_DK

    def project_part(p):
        raw = group("qkv", p * part, part)
        cols = slice(p * part, (p + 1) * part)
        if carry:
            xp_ref[HIST_ROWS:HIST_ROWS + tm, cols] = raw
        else:
            for c in range(n_chunks):
                xp_ref[c * seg + HIST_ROWS:(c + 1) * seg, cols] = raw[c * CHUNK:(c + 1) * CHUNK, :]

    def conv_part(p):
        for c in range(n_chunks):
            for t in range(p * GDN_HEADS, (p + 1) * GDN_HEADS):
                c0 = t * LANES
                win = xp_ref[c * seg:c * seg + HIST_ROWS + CHUNK, c0:c0 + LANES]
                acc = cw_ref[CONV_W - 1:CONV_W, c0:c0 + LANES] * win[HIST_ROWS:, :]
                for s in range(1, CONV_W):
                    tap = pltpu.roll(win, s, 0)[HIST_ROWS:, :]
                    acc = acc + cw_ref[CONV_W - 1 - s:CONV_W - s, c0:c0 + LANES] * tap
                y = acc * jax.nn.sigmoid(acc)
                if p < 2:
                    y = y * lax.rsqrt(jnp.sum(y * y, axis=-1, keepdims=True) + EPS)
                    if p == 0:
                        y = y * (GDN_DK ** -0.5)
                qkv_ref[c * CHUNK:(c + 1) * CHUNK, c0:c0 + LANES] = y

    def gate_math():
        ba = group("ba")
        ri = lax.broadcasted_iota(jnp.int32, (CHUNK, CHUNK), 0)
        ci = lax.broadcasted_iota(jnp.int32, (CHUNK, CHUNK), 1)
        tri = (ri >= ci).astype(BF16)
        lane = lax.broadcasted_iota(jnp.int32, (CHUNK, LANES), 1)
        a_coef = -jnp.exp(par_ref[0:1, :])
        dt_bias = par_ref[1:2, :]
        for c in range(n_chunks):
            bac = ba[c * CHUNK:(c + 1) * CHUNK, :]
            g_all = a_coef * jax.nn.softplus(bac + dt_bias)
            g_hi = g_all.astype(BF16)
            g_r1 = g_all - g_hi.astype(F32)
            g_mid = g_r1.astype(BF16)
            g_lo = (g_r1 - g_mid.astype(F32)).astype(BF16)
            gc = _dot(tri, g_hi) + _dot(tri, g_mid) + _dot(tri, g_lo)
            gate_ref[c * CHUNK:(c + 1) * CHUNK, :] = jnp.where(lane < GDN_HEADS, jax.nn.sigmoid(bac), gc)

    project_part(0)
    project_part(1)
    conv_part(0)
    project_part(2)
    conv_part(1)
    z_ref[...] = group("z").astype(z_ref.dtype)
    gate_math()
    qb_ref[...] = group("qb").astype(qb_ref.dtype)
    conv_part(2)
    kv_ref[...] = group("kv")
    gates_ref[...] = group("gates").astype(gates_ref.dtype)

    if carry:
        def _store_tail():
            tail_ref[0] = xp_ref[tm:tm + HIST_ROWS, :]

        pl.when(pid == pl.num_programs(0) - 1)(_store_tail)
        xp_ref[0:HIST_ROWS, :] = xp_ref[tm:tm + HIST_ROWS, :]
    else:
        for c in range(n_chunks):
            tail_ref[c] = xp_ref[c * seg + CHUNK:c * seg + CHUNK + HIST_ROWS, :]


def _proj(x, mod, seq_rows, g, w_packed, hist8, conv_w, par):
    n = x.shape[0]
    n_seq = n // seq_rows
    carry = n_seq == 1
    tm, spt = _tile_rows(n, seq_rows, PROJ_ROWS)
    if carry:
        n_hist = 1
        hist_map = lambda i: (0, 0, 0)
        xp_rows = HIST_ROWS + tm
    else:
        assert seq_rows == CHUNK
        n_hist = tm // CHUNK
        hist_map = lambda i: (i, 0, 0)
        xp_rows = n_hist * (HIST_ROWS + CHUNK)
    widths = [CONV_DIM, GDN_HEADS * GDN_DV, SWA_HEADS * SWA_HD, KV_COLS, 2 * D_MODEL, LANES]
    return pl.pallas_call(
        functools.partial(_proj_kernel, spt=spt, carry=carry),
        grid=(n // tm,),
        in_specs=[
            pl.BlockSpec((tm, D_MODEL), lambda i: (i, 0)),
            _mod_spec(mod.shape[1]),
            _const_spec((1, D_MODEL)),
            _const_spec((D_MODEL, PROJ_COLS)),
            pl.BlockSpec((n_hist, HIST_ROWS, CONV_DIM), hist_map),
            _const_spec((CONV_W, CONV_DIM)),
            _const_spec((SUBLANES, LANES)),
        ],
        out_specs=[pl.BlockSpec((tm, w), lambda i: (i, 0)) for w in widths]
        + [pl.BlockSpec((n_hist, HIST_ROWS, CONV_DIM), hist_map)],
        out_shape=[jax.ShapeDtypeStruct((n, w), BF16 if i in (1, 2, 4) else F32) for i, w in enumerate(widths)]
        + [jax.ShapeDtypeStruct((n_seq, HIST_ROWS, CONV_DIM), F32)],
        scratch_shapes=[pltpu.VMEM((tm, D_MODEL), BF16), pltpu.VMEM((xp_rows, CONV_DIM), F32)],
        compiler_params=_cparams(1),
        name="in_proj",
    )(x, mod, g, w_packed, hist8, conv_w, par)


def _unit_lower_inverses(ls, eye, level_masks):
    ts = [eye - jnp.where(level_masks[0], l, 0.0) for l in ls]
    for mask in level_masks[1:]:
        lk = [jnp.where(mask, l, 0.0).astype(BF16) for l in ls]
        tp = [t.astype(BF16) for t in ts]
        m1 = [_dot(a, b) for a, b in zip(lk, tp)]
        m2 = [_dot(a, b.astype(BF16)) for a, b in zip(tp, m1)]
        ts = [t - m for t, m in zip(ts, m2)]
    return ts


def _gdn_kernel(qkv_ref, gate_ref, s0_ref, o_ref, sout_ref, s_ref, *, n_chunks, carry):
    pid = pl.program_id(0)

    if carry:
        def _load_state():
            s_ref[0] = s0_ref[0]

        pl.when(pid == 0)(_load_state)
    else:
        for c in range(n_chunks):
            s_ref[c] = s0_ref[c]

    ri = lax.broadcasted_iota(jnp.int32, (CHUNK, CHUNK), 0)
    ci = lax.broadcasted_iota(jnp.int32, (CHUNK, CHUNK), 1)
    causal = ri >= ci
    strict = ri > ci
    eye = (ri == ci).astype(F32)
    level_masks = []
    for lvl in range(int(math.log2(CHUNK))):
        same_block = (ri >> (lvl + 1)) == (ci >> (lvl + 1))
        level_masks.append(same_block & (((ri >> lvl) & 1) == 1) & (((ci >> lvl) & 1) == 0))

    items = [(c, h) for c in range(n_chunks) for h in range(GDN_HEADS)]
    gate = [gate_ref[c * CHUNK:(c + 1) * CHUNK, :] for c in range(n_chunks)]
    egc_all = [jnp.exp(g) for g in gate]
    g_last_all = [g[CHUNK - 1:CHUNK, :] for g in gate]
    kdf_all = [jnp.exp(gl - g) for gl, g in zip(g_last_all, gate)]
    eg_last_all = [jnp.exp(gl) for gl in g_last_all]

    def tile(c, t):
        return qkv_ref[c * CHUNK:(c + 1) * CHUNK, t * LANES:(t + 1) * LANES]

    def col(arrs, c, h):
        return arrs[c][:, GDN_HEADS + h:GDN_HEADS + h + 1]

    q = [tile(c, h) for c, h in items]
    k = [tile(c, GDN_HEADS + h) for c, h in items]
    v = [tile(c, 2 * GDN_HEADS + h) for c, h in items]
    beta = [gate[c][:, h:h + 1] for c, h in items]
    gcol = [col(gate, c, h) for c, h in items]
    grow = [jnp.sum(g * eye, axis=0, keepdims=True) for g in gcol]
    decay = [jnp.where(causal, jnp.exp(jnp.where(causal, gc - gr, 0.0)), 0.0) for gc, gr in zip(gcol, grow)]
    kb = [x * b for x, b in zip(k, beta)]
    k16 = [x.astype(BF16) for x in k]
    kq = [_dot_nt(jnp.concatenate([a.astype(BF16), b.astype(BF16)], axis=0), c) for a, b, c in zip(kb, q, k16)]
    kk = [x[0:CHUNK, :] for x in kq]
    qk = [x[CHUNK:, :] for x in kq]
    ls = [jnp.where(strict, a * d, 0.0) for a, d in zip(kk, decay)]
    intra16 = [jnp.where(causal, a * d, 0.0).astype(BF16) for a, d in zip(qk, decay)]
    ts = _unit_lower_inverses(ls, eye, level_masks)
    ys = [(t - eye).astype(BF16) for t in ts]
    egc = [col(egc_all, c, h) for c, h in items]
    vb = [x * b for x, b in zip(v, beta)]
    kbe = [x * e for x, e in zip(kb, egc)]
    rhs = [jnp.concatenate([a, b], axis=1) for a, b in zip(vb, kbe)]
    sol = [x + _dot(y, x.astype(BF16)) for x, y in zip(rhs, ys)]
    u = [x[:, 0:GDN_DV] for x in sol]
    w16 = [x[:, GDN_DV:].astype(BF16) for x in sol]
    qg16 = [(x * e).astype(BF16) for x, e in zip(q, egc)]
    kd16 = [(x * col(kdf_all, c, h)).astype(BF16) for x, (c, h) in zip(k, items)]
    eg_last = [col(eg_last_all, c, h) for c, h in items]
    wq16 = [jnp.concatenate([a, b], axis=0) for a, b in zip(w16, qg16)]

    o = [None] * len(items)
    if carry:
        state = [s_ref[0, h] for h in range(GDN_HEADS)]
    for c in range(n_chunks):
        idx = [c * GDN_HEADS + h for h in range(GDN_HEADS)]
        if not carry:
            state = [s_ref[c, h] for h in range(GDN_HEADS)]
        s16 = [s.astype(BF16) for s in state]
        ws_qs = [_dot(wq16[i], s) for i, s in zip(idx, s16)]
        v_new = [u[i] - x[0:CHUNK, :] for i, x in zip(idx, ws_qs)]
        vn16 = [x.astype(BF16) for x in v_new]
        for i, x, vn in zip(idx, ws_qs, vn16):
            o[i] = x[CHUNK:, :] + _dot(intra16[i], vn)
        state = [s * eg_last[i] + _dot_tn(kd16[i], vn) for i, s, vn in zip(idx, state, vn16)]
        if not carry:
            for h in range(GDN_HEADS):
                sout_ref[c, h] = state[h]

    for (c, h), oo in zip(items, o):
        o_ref[c * CHUNK:(c + 1) * CHUNK, h * GDN_DV:(h + 1) * GDN_DV] = oo.astype(o_ref.dtype)

    if carry:
        for h in range(GDN_HEADS):
            s_ref[0, h] = state[h]

        def _store_state():
            sout_ref[0] = s_ref[0]

        pl.when(pid == pl.num_programs(0) - 1)(_store_state)


def _gdn(qkv, gate, s0, seq_rows):
    n = qkv.shape[0]
    n_seq = n // seq_rows
    carry = n_seq == 1
    n_chunks = GDN_CHUNKS_PER_STEP
    rows = n_chunks * CHUNK
    assert n % rows == 0
    if carry:
        n_state = 1
        s_map = lambda i: (0, 0, 0, 0)
    else:
        assert seq_rows == CHUNK
        n_state = n_chunks
        s_map = lambda i: (i, 0, 0, 0)
    kern = functools.partial(_gdn_kernel, n_chunks=n_chunks, carry=carry)
    return pl.pallas_call(
        kern,
        grid=(n // rows,),
        in_specs=[
            pl.BlockSpec((rows, CONV_DIM), lambda i: (i, 0)),
            pl.BlockSpec((rows, LANES), lambda i: (i, 0)),
            pl.BlockSpec((n_state, GDN_HEADS, GDN_DK, GDN_DV), s_map),
        ],
        out_specs=[
            pl.BlockSpec((rows, GDN_HEADS * GDN_DV), lambda i: (i, 0)),
            pl.BlockSpec((n_state, GDN_HEADS, GDN_DK, GDN_DV), s_map),
        ],
        out_shape=[
            jax.ShapeDtypeStruct((n, GDN_HEADS * GDN_DV), BF16),
            jax.ShapeDtypeStruct((n_seq, GDN_HEADS, GDN_DK, GDN_DV), F32),
        ],
        scratch_shapes=[pltpu.VMEM((n_state, GDN_HEADS, GDN_DK, GDN_DV), F32)],
        compiler_params=_cparams(1),
        name="gdn",
    )(qkv, gate, s0)


SWA_KEYS = WINDOW + CHUNK
SWA_KEYS_PAD = 2 * LANES
KV_COLS = 2 * SWA_KV_HEADS * SWA_HD


def _bias_kernel(bucket_ref, rb_ref, o_ref):
    bucket = bucket_ref[...]
    h = pl.program_id(0)
    acc = jnp.zeros((CHUNK, SWA_KEYS), F32)
    for b in range(NUM_BUCKETS):
        acc = jnp.where(bucket == b, rb_ref[b * SWA_HEADS + h], acc)
    o_ref[0] = acc


def _bias_table(bucket, rel_bias_flat):
    return pl.pallas_call(
        _bias_kernel,
        grid=(SWA_HEADS,),
        in_specs=[
            pl.BlockSpec((CHUNK, SWA_KEYS), lambda h: (0, 0)),
            pl.BlockSpec(memory_space=pltpu.SMEM),
        ],
        out_specs=pl.BlockSpec((1, CHUNK, SWA_KEYS), lambda h: (h, 0, 0)),
        out_shape=jax.ShapeDtypeStruct((SWA_HEADS, CHUNK, SWA_KEYS), F32),
        compiler_params=_cparams(1),
        name="swa_bias",
    )(bucket, rel_bias_flat)


def _swa_kernel(q_ref, kvc_ref, kvp_ref, bias_ref, sink_ref, o_ref, kf_ref, *, n_chunks, one_sequence):
    n_pad = SWA_KEYS_PAD - SWA_KEYS
    kf_ref[kf_ref.shape[0] - n_pad:, :] = jnp.zeros((n_pad, KV_COLS), F32)
    if one_sequence:
        kf_ref[0:WINDOW, :] = kvp_ref[...]
        kf_ref[WINDOW:WINDOW + n_chunks * CHUNK, :] = kvc_ref[...]
        key_start = [c * CHUNK for c in range(n_chunks)]
        key_pos = lax.broadcasted_iota(jnp.int32, (CHUNK, SWA_KEYS), 1)
        tile_start = pl.program_id(0) * (n_chunks * CHUNK)
        valid = [key_pos + (tile_start + c * CHUNK - WINDOW) >= 0 for c in range(n_chunks)]
    else:
        for c in range(n_chunks):
            kf_ref[c * SWA_KEYS:c * SWA_KEYS + WINDOW, :] = kvp_ref[c * WINDOW:(c + 1) * WINDOW, :]
            kf_ref[c * SWA_KEYS + WINDOW:(c + 1) * SWA_KEYS, :] = kvc_ref[c * CHUNK:(c + 1) * CHUNK, :]
        key_start = [c * SWA_KEYS for c in range(n_chunks)]

    pairs = [(c, kh) for c in range(n_chunks) for kh in range(SWA_KV_HEADS)]
    half = SWA_KV_HEADS * SWA_HD
    keys_t = [kf_ref[key_start[c]:key_start[c] + SWA_KEYS_PAD, 0:half].T.astype(BF16) for c in range(n_chunks)]
    keys = {(c, kh): keys_t[c][kh * SWA_HD:(kh + 1) * SWA_HD, :] for c, kh in pairs}
    ones = jnp.ones((SWA_KEYS, LANES - SWA_HD), BF16)
    vals = {(c, kh): jnp.concatenate(
        [kf_ref[key_start[c]:key_start[c] + SWA_KEYS,
                (SWA_KV_HEADS + kh) * SWA_HD:(SWA_KV_HEADS + kh + 1) * SWA_HD].astype(BF16), ones], axis=1)
            for c, kh in pairs}
    items = [(c, hd) for c in range(n_chunks) for hd in range(SWA_HEADS)]
    q = [q_ref[c * CHUNK:(c + 1) * CHUNK, hd * SWA_HD:(hd + 1) * SWA_HD] for c, hd in items]
    def group_rows(c, kh):
        return range(c * SWA_HEADS + kh * SWA_GROUP, c * SWA_HEADS + (kh + 1) * SWA_GROUP)

    qk = {(c, kh): _dot(jnp.concatenate([q[i] for i in group_rows(c, kh)], axis=0), keys[(c, kh)])
          for c, kh in pairs}
    logits = [qk[(c, hd // SWA_GROUP)][(hd % SWA_GROUP) * CHUNK:(hd % SWA_GROUP + 1) * CHUNK, 0:SWA_KEYS] + bias_ref[hd]
              for c, hd in items]
    if one_sequence:
        logits = [jnp.where(valid[c], x, -jnp.inf) if c < WINDOW // CHUNK else x for x, (c, _) in zip(logits, items)]
    sink = [sink_ref[hd] for _, hd in items]
    m = [jnp.maximum(jnp.max(x, axis=-1, keepdims=True), s) for x, s in zip(logits, sink)]
    p16 = [jnp.exp(x - mm).astype(BF16) for x, mm in zip(logits, m)]
    pv = {(c, kh): _dot(jnp.concatenate([p16[i] for i in group_rows(c, kh)], axis=0), vals[(c, kh)])
          for c, kh in pairs}
    pv = [pv[(c, hd // SWA_GROUP)][(hd % SWA_GROUP) * CHUNK:(hd % SWA_GROUP + 1) * CHUNK, :] for c, hd in items]
    denom = [x[:, SWA_HD:SWA_HD + 1] + jnp.exp(s - mm) for x, s, mm in zip(pv, sink, m)]
    out = [x[:, 0:SWA_HD] / d for x, d in zip(pv, denom)]
    for x, (c, hd) in zip(out, items):
        o_ref[c * CHUNK:(c + 1) * CHUNK, hd * SWA_HD:(hd + 1) * SWA_HD] = x.astype(o_ref.dtype)


def _swa(qb, kv, kv_hist, bias, sinks, seq_rows):
    n = qb.shape[0]
    n_chunks = SWA_CHUNKS_PER_STEP
    rows = n_chunks * CHUNK
    assert n % rows == 0
    one_sequence = kv_hist is None
    if one_sequence:
        assert rows % WINDOW == 0
        per = rows // WINDOW
        prev_arr = kv
        prev_spec = pl.BlockSpec((WINDOW, KV_COLS), lambda i: (jnp.maximum(i * per - 1, 0), 0))
        kf_rows = WINDOW + rows
    else:
        assert seq_rows == CHUNK
        prev_arr = kv_hist
        prev_spec = pl.BlockSpec((n_chunks * WINDOW, KV_COLS), lambda i: (i, 0))
        kf_rows = n_chunks * SWA_KEYS
    kern = functools.partial(_swa_kernel, n_chunks=n_chunks, one_sequence=one_sequence)
    return pl.pallas_call(
        kern,
        grid=(n // rows,),
        in_specs=[
            pl.BlockSpec((rows, SWA_HEADS * SWA_HD), lambda i: (i, 0)),
            pl.BlockSpec((rows, KV_COLS), lambda i: (i, 0)),
            prev_spec,
            pl.BlockSpec((SWA_HEADS, CHUNK, SWA_KEYS), lambda i: (0, 0, 0)),
            pl.BlockSpec(memory_space=pltpu.SMEM),
        ],
        out_specs=pl.BlockSpec((rows, SWA_HEADS * SWA_HD), lambda i: (i, 0)),
        out_shape=jax.ShapeDtypeStruct((n, SWA_HEADS * SWA_HD), BF16),
        scratch_shapes=[pltpu.VMEM((kf_rows + SWA_KEYS_PAD - SWA_KEYS, KV_COLS), F32)],
        compiler_params=_cparams(1),
        name="swa",
    )(qb, kv, prev_arr, bias, sinks)


def _merge_kernel(x_ref, oa_ref, z_ref, ob_ref, gates_ref, mod_ref, nw_ref, wa_ref, wb_ref, wo_ref, o_ref, oa16_ref,
                  *, spt):
    for h in range(GDN_HEADS):
        cols = slice(h * GDN_DV, (h + 1) * GDN_DV)
        oo = oa_ref[:, cols].astype(F32)
        zz = z_ref[:, cols].astype(F32)
        on = oo * lax.rsqrt(jnp.mean(oo * oo, axis=-1, keepdims=True) + EPS) * nw_ref[...] * (zz * jax.nn.sigmoid(zz))
        oa16_ref[:, cols] = on.astype(BF16)
    ya = _dot(oa16_ref[...], wa_ref[...])
    yb = _dot(ob_ref[...], wb_ref[...])
    merged = (jax.nn.sigmoid(gates_ref[:, 0:D_MODEL].astype(F32)) * ya
              + jax.nn.sigmoid(gates_ref[:, D_MODEL:2 * D_MODEL].astype(F32)) * yb)
    y = _dot(merged.astype(BF16), wo_ref[...])
    rows = x_ref.shape[0] // spt
    for j in range(spt):
        sl = slice(j * rows, (j + 1) * rows)
        o_ref[sl, :] = x_ref[sl, :] + _mod_row(mod_ref, 5, spt, j) * y[sl, :]


def _merge(x, oa, z, ob, gates, mod, seq_rows, norm_w, wa, wb, wo):
    n = x.shape[0]
    tm, spt = _tile_rows(n, seq_rows, MERGE_ROWS)
    row_spec = lambda w: pl.BlockSpec((tm, w), lambda i: (i, 0))
    return pl.pallas_call(
        functools.partial(_merge_kernel, spt=spt),
        grid=(n // tm,),
        in_specs=[
            row_spec(D_MODEL), row_spec(D_MODEL), row_spec(D_MODEL), row_spec(D_MODEL), row_spec(2 * D_MODEL),
            _mod_spec(mod.shape[1]),
            _const_spec((1, GDN_DV)),
            _const_spec((D_MODEL, D_MODEL)), _const_spec((D_MODEL, D_MODEL)), _const_spec((D_MODEL, D_MODEL)),
        ],
        out_specs=row_spec(D_MODEL),
        out_shape=jax.ShapeDtypeStruct((n, D_MODEL), F32),
        scratch_shapes=[pltpu.VMEM((tm, GDN_HEADS * GDN_DV), BF16)],
        compiler_params=_cparams(1),
        name="merge",
    )(x, oa, z, ob, gates, mod, norm_w, wa, wb, wo)


def _t5_bucket(rel):
    half = NUM_BUCKETS // 2
    max_exact = half // 2
    n = jnp.abs(rel)
    large = max_exact + (jnp.log(jnp.maximum(n, 1).astype(jnp.float32) / max_exact)
                         / math.log(MAX_DISTANCE / max_exact) * (half - max_exact)).astype(jnp.int32)
    large = jnp.minimum(large, half - 1)
    return jnp.where(rel > 0, half, 0) + jnp.where(n < max_exact, n, large)


def _trunk(x, mod, conv_hist, s_hist, k_hist, v_hist, w):
    bsz, seq, _ = x.shape
    n = bsz * seq
    x2 = x.reshape(n, D_MODEL)
    x1 = _ffn(x2, mod, seq, w["norm_ffn1"], w["ffn1_in"], w["ffn1_out"], w["norm_final"], sub=0, final=False)
    if conv_hist is None:
        hist8 = jnp.zeros((bsz, HIST_ROWS, CONV_DIM), F32)
        s0 = jnp.zeros((bsz, GDN_HEADS, GDN_DK, GDN_DV), F32)
        kv_hist = None
    else:
        hist8 = jnp.concatenate([jnp.zeros((bsz, HIST_ROWS - (CONV_W - 1), CONV_DIM), F32), conv_hist], axis=1)
        s0 = s_hist
        kv_hist = jnp.concatenate([k_hist, v_hist], axis=2).reshape(bsz * WINDOW, KV_COLS)
    qkv, z, qb, kv, gates, gate, tail = _proj(x1, mod, seq, w["norm_mix"], w["w_in"], hist8, w["conv_w"], w["gdn_par"])
    oa, s_new = _gdn(qkv, gate, s0, seq)
    ob = _swa(qb, kv, kv_hist, w["bias"], w["sinks"], seq)
    x3 = _merge(x1, oa, z, ob, gates, mod, seq, w["gdn_norm_w"], w["w_a"], w["w_b"], w["w_out"])
    y = _ffn(x3, mod, seq, w["norm_ffn2"], w["ffn2_in"], w["ffn2_out"], w["norm_final"], sub=2, final=True)

    conv_new = tail[:, HIST_ROWS - (CONV_W - 1):]
    half = SWA_KV_HEADS * SWA_HD
    kv3 = kv.reshape(bsz, seq, KV_COLS)
    if kv_hist is None:
        k_new = kv3[:, seq - WINDOW:, :half]
        v_new = kv3[:, seq - WINDOW:, half:]
    else:
        keep = WINDOW - seq
        k_new = jnp.concatenate([k_hist.reshape(bsz, WINDOW, half)[:, WINDOW - keep:], kv3[:, :, :half]], axis=1)
        v_new = jnp.concatenate([v_hist.reshape(bsz, WINDOW, half)[:, WINDOW - keep:], kv3[:, :, half:]], axis=1)
    k_new = k_new.reshape(bsz, WINDOW, SWA_KV_HEADS, SWA_HD)
    v_new = v_new.reshape(bsz, WINDOW, SWA_KV_HEADS, SWA_HD)
    return (y.reshape(bsz, seq, D_MODEL), conv_new[None], s_new[None], k_new[None], v_new[None])


def kernel(x_prompt, x_sample, state_gdn_conv, state_gdn_s, cache_swa_k, cache_swa_v, c_prompt, c_sample,
           norm_ffn1, w_ffn1_in, w_ffn1_out, norm_mix, w_in, gdn_conv_w, gdn_a_log, gdn_dt_bias, gdn_norm_w,
           swa_sinks, rel_bias, w_branch_a, w_branch_b, w_out, norm_ffn2, w_ffn2_in, w_ffn2_out,
           w_ada, b_ada, norm_final):
    bp = x_prompt.shape[0]
    bs = x_sample.shape[0]
    assert bp == 1 and x_sample.shape[1] == CHUNK and cache_swa_k.shape[2] == WINDOW

    n_c = bp + bs
    pad = -n_c % SUBLANES
    c_all = jnp.concatenate([c_prompt, c_sample, jnp.zeros((pad, D_MODEL), F32)], axis=0)
    mod_all = _modulation(c_all, w_ada[0], b_ada[0][None, :])
    mod_p = mod_all[:, :bp]
    mod_s = mod_all[:, bp:n_c]

    w_packed = _pack_w_in(w_in)
    par = jnp.zeros((SUBLANES, LANES), F32)
    par = par.at[0, GDN_HEADS:2 * GDN_HEADS].set(gdn_a_log[0]).at[1, GDN_HEADS:2 * GDN_HEADS].set(gdn_dt_bias[0])

    rel = jnp.arange(SWA_KEYS)[None, :] - WINDOW - jnp.arange(CHUNK)[:, None]
    bias = _bias_table(_t5_bucket(rel).astype(jnp.int32), rel_bias.reshape(-1))

    w = {
        "norm_ffn1": norm_ffn1, "ffn1_in": w_ffn1_in[0].astype(BF16), "ffn1_out": w_ffn1_out[0].astype(BF16),
        "norm_mix": norm_mix, "w_in": w_packed, "conv_w": gdn_conv_w[0], "gdn_par": par,
        "gdn_norm_w": gdn_norm_w, "bias": bias, "sinks": swa_sinks[0],
        "w_a": w_branch_a[0].astype(BF16), "w_b": w_branch_b[0].astype(BF16), "w_out": w_out[0].astype(BF16),
        "norm_ffn2": norm_ffn2, "ffn2_in": w_ffn2_in[0].astype(BF16), "ffn2_out": w_ffn2_out[0].astype(BF16),
        "norm_final": norm_final[None, :],
    }
    y_p, p_conv, p_s, p_k, p_v = _trunk(x_prompt, mod_p, None, None, None, None, w)
    y_s, s_conv, s_s, s_k, s_v = _trunk(x_sample, mod_s, state_gdn_conv[0], state_gdn_s[0],
                                        cache_swa_k[0], cache_swa_v[0], w)
    return (y_p, y_s, p_conv, p_s, p_k, p_v, s_conv, s_s, s_k, s_v)
```

```python
import functools
import math

import jax
import jax.numpy as jnp
from jax import lax
from jax.experimental import pallas as pl
from jax.experimental.pallas import tpu as pltpu

F32 = jnp.float32
BF16 = jnp.bfloat16

D_MODEL = 1024
CHUNK = 64
GDN_HEADS = 8
GDN_DK = 128
GDN_DV = 128
CONV_W = 4
CONV_DIM = GDN_HEADS * (2 * GDN_DK + GDN_DV)
SWA_HEADS = 16
SWA_KV_HEADS = 2
SWA_GROUP = SWA_HEADS // SWA_KV_HEADS
SWA_HD = 64
WINDOW = 128
NUM_BUCKETS = 32
MAX_DISTANCE = 128
D_FF = 2816
N_MOD = 9
EPS = 1e-6

LANES = 128
SUBLANES = 8
VMEM_LIMIT = 56 * 1024 * 1024

FF_CHUNK = D_FF // 2
FFN_ROWS = 1024
HIST_ROWS = SUBLANES
MERGE_ROWS = 512
PACK_ROWS = 256
GDN_CHUNKS_PER_STEP = 4
SWA_CHUNKS_PER_STEP = 8


def _dot(a, b):
    return jnp.dot(a, b, preferred_element_type=F32)


def _dot_nt(a, b):
    return lax.dot_general(a, b, (((1,), (1,)), ((), ())), preferred_element_type=F32)


def _dot_tn(a, b):
    return lax.dot_general(a, b, (((0,), (0,)), ((), ())), preferred_element_type=F32)


def _cparams(n_grid):
    return pltpu.CompilerParams(dimension_semantics=("arbitrary",) * n_grid, vmem_limit_bytes=VMEM_LIMIT)


def _const_spec(shape):
    nd = len(shape)
    return pl.BlockSpec(shape, lambda i: (0,) * nd, pipeline_mode=pl.Buffered(1))


def _mod_kernel(c_ref, w_ref, b_ref, o_ref):
    c = c_ref[...]
    a = (c * jax.nn.sigmoid(c)).astype(BF16)
    o_ref[0] = _dot(a, w_ref[...].astype(BF16)) + b_ref[...]


def _modulation(c_pad, w_ada, b_ada):
    rows = c_pad.shape[0]
    return pl.pallas_call(
        _mod_kernel,
        grid=(N_MOD,),
        in_specs=[
            pl.BlockSpec((rows, D_MODEL), lambda j: (0, 0)),
            pl.BlockSpec((D_MODEL, D_MODEL), lambda j: (0, j)),
            pl.BlockSpec((1, D_MODEL), lambda j: (0, j)),
        ],
        out_specs=pl.BlockSpec((1, rows, D_MODEL), lambda j: (j, 0, 0)),
        out_shape=jax.ShapeDtypeStruct((N_MOD, rows, D_MODEL), F32),
        compiler_params=_cparams(1),
        name="adaln_mod",
    )(c_pad, w_ada, b_ada)


def _mod_row(mod_ref, idx, spt, j):
    if mod_ref.shape[1] == 1:
        return mod_ref[idx, 0:1, :]
    return mod_ref[idx, pl.ds(pl.program_id(0) * spt + j, 1), :]


def _norm_mod_store(h_ref, x_ref, g_ref, mod_ref, sub, spt):
    rows = x_ref.shape[0] // spt
    g = g_ref[...]
    for j in range(spt):
        xs = x_ref[j * rows:(j + 1) * rows, :]
        ms = jnp.mean(xs * xs, axis=-1, keepdims=True)
        y = xs * lax.rsqrt(ms + EPS) * g
        sh = _mod_row(mod_ref, 3 * sub, spt, j)
        sc = _mod_row(mod_ref, 3 * sub + 1, spt, j)
        h_ref[j * rows:(j + 1) * rows, :] = (y * (1.0 + sc) + sh).astype(h_ref.dtype)


def _tile_rows(n_rows, seq_rows, target):
    if seq_rows >= target:
        assert seq_rows % target == 0
        return target, 1
    assert target % seq_rows == 0
    tm = min(target, n_rows)
    assert n_rows % tm == 0
    return tm, tm // seq_rows


def _mod_spec(n_seq_total):
    return pl.BlockSpec((N_MOD, n_seq_total, D_MODEL), lambda i: (0, 0, 0))


def _ffn_kernel(x_ref, mod_ref, g_ref, w1_ref, w2_ref, gf_ref, o_ref, h_ref, *, sub, final, spt):
    _norm_mod_store(h_ref, x_ref, g_ref, mod_ref, sub, spt)
    h = h_ref[...]
    acc = None
    for c in range(D_FF // FF_CHUNK):
        c0 = c * FF_CHUNK
        gate = _dot(h, w1_ref[:, c0:c0 + FF_CHUNK])
        up = _dot(h, w1_ref[:, D_FF + c0:D_FF + c0 + FF_CHUNK])
        a = (gate * jax.nn.sigmoid(gate) * up).astype(BF16)
        part = _dot(a, w2_ref[c0:c0 + FF_CHUNK, :])
        acc = part if acc is None else acc + part
    rows = x_ref.shape[0] // spt
    for j in range(spt):
        sl = slice(j * rows, (j + 1) * rows)
        ga = _mod_row(mod_ref, 3 * sub + 2, spt, j)
        xn = x_ref[sl, :] + 0.5 * ga * acc[sl, :]
        if final:
            ms = jnp.mean(xn * xn, axis=-1, keepdims=True)
            xn = xn * lax.rsqrt(ms + EPS) * gf_ref[...]
        o_ref[sl, :] = xn


def _ffn(x, mod, seq_rows, g, w1, w2, gf, *, sub, final):
    n = x.shape[0]
    tm, spt = _tile_rows(n, seq_rows, FFN_ROWS)
    kern = functools.partial(_ffn_kernel, sub=sub, final=final, spt=spt)
    return pl.pallas_call(
        kern,
        grid=(n // tm,),
        in_specs=[
            pl.BlockSpec((tm, D_MODEL), lambda i: (i, 0)),
            _mod_spec(mod.shape[1]),
            _const_spec((1, D_MODEL)),
            _const_spec((D_MODEL, 2 * D_FF)),
            _const_spec((D_FF, D_MODEL)),
            _const_spec((1, D_MODEL)),
        ],
        out_specs=pl.BlockSpec((tm, D_MODEL), lambda i: (i, 0)),
        out_shape=jax.ShapeDtypeStruct((n, D_MODEL), F32),
        scratch_shapes=[pltpu.VMEM((tm, D_MODEL), BF16)],
        compiler_params=_cparams(1),
        name="ffn_final" if final else "ffn",
    )(x, mod, g, w1, w2, gf)


PROJ_GROUPS = (("qkv", CONV_DIM), ("z", GDN_HEADS * GDN_DV), ("qb", SWA_HEADS * SWA_HD),
               ("kv", 2 * SWA_KV_HEADS * SWA_HD), ("gates", 2 * D_MODEL), ("ba", LANES))
PROJ_COLS = sum(w for _, w in PROJ_GROUPS)


assert math.log2(SWA_HD) % 2 == 0
W_IN_ALIGNED = CONV_DIM + GDN_HEADS * GDN_DV
W_IN_SMALL = 2 * GDN_HEADS
W_IN_REST = PROJ_COLS - W_IN_ALIGNED - LANES


def _pack_w_in_kernel(w_ref, o_ref):
    o_ref[:, 0:W_IN_ALIGNED] = w_ref[:, 0:W_IN_ALIGNED].astype(BF16)
    hi = W_IN_ALIGNED + W_IN_SMALL
    n_qb = SWA_HEADS * SWA_HD
    o_ref[:, W_IN_ALIGNED:W_IN_ALIGNED + n_qb] = (w_ref[:, hi:hi + n_qb] * (SWA_HD ** -0.5)).astype(BF16)
    o_ref[:, W_IN_ALIGNED + n_qb:W_IN_ALIGNED + W_IN_REST] = w_ref[:, hi + n_qb:hi + W_IN_REST].astype(BF16)
    lane = lax.broadcasted_iota(jnp.int32, (w_ref.shape[0], LANES), 1)
    small = jnp.where(lane < W_IN_SMALL, w_ref[:, W_IN_ALIGNED:W_IN_ALIGNED + LANES], 0.0)
    o_ref[:, W_IN_ALIGNED + W_IN_REST:PROJ_COLS] = small.astype(BF16)


def _pack_w_in(w_in):
    n_cols = W_IN_ALIGNED + W_IN_SMALL + W_IN_REST
    assert w_in.shape == (1, D_MODEL, n_cols)
    rows = PACK_ROWS
    return pl.pallas_call(
        _pack_w_in_kernel,
        grid=(D_MODEL // rows,),
        in_specs=[pl.BlockSpec((None, rows, n_cols), lambda i: (0, i, 0))],
        out_specs=pl.BlockSpec((rows, PROJ_COLS), lambda i: (i, 0)),
        out_shape=jax.ShapeDtypeStruct((D_MODEL, PROJ_COLS), BF16),
        compiler_params=_cparams(1),
        name="pack_w_in",
    )(w_in)


PROJ_OFFSETS = {}
_off = 0
for _name, _width in PROJ_GROUPS:
    PROJ_OFFSETS[_name] = (_off, _width)
    _off += _width
PROJ_ROWS = 512


def _proj_kernel(x_ref, mod_ref, g_ref, w_ref, hist_ref, cw_ref, par_ref,
                 qkv_ref, z_ref, qb_ref, kv_ref, gates_ref, gate_ref, tail_ref,
                 h_ref, xp_ref, *, spt, carry):
    pid = pl.program_id(0)
    tm = x_ref.shape[0]
    n_chunks = tm // CHUNK
    seg = CHUNK if carry else HIST_ROWS + CHUNK

    _norm_mod_store(h_ref, x_ref, g_ref, mod_ref, 1, spt)
    h = h_ref[...]

    def group(name, lo=0, width=None):
        c0, full = PROJ_OFFSETS[name]
        width = full if width is None else width
        return _dot(h, w_ref[:, c0 + lo:c0 + lo + width])

    if carry:
        def _load_hist():
            xp_ref[0:HIST_ROWS, :] = hist_ref[0]

        pl.when(pid == 0)(_load_hist)
    else:
        for c in range(n_chunks):
            xp_ref[c * seg:c * seg + HIST_ROWS, :] = hist_ref[c]

    part = GDN_HEADS * GDN_DK

    def project_part(p):
        raw = group("qkv", p * part, part)
        cols = slice(p * part, (p + 1) * part)
        if carry:
            xp_ref[HIST_ROWS:HIST_ROWS + tm, cols] = raw
        else:
            for c in range(n_chunks):
                xp_ref[c * seg + HIST_ROWS:(c + 1) * seg, cols] = raw[c * CHUNK:(c + 1) * CHUNK, :]

    def conv_part(p):
        for c in range(n_chunks):
            for t in range(p * GDN_HEADS, (p + 1) * GDN_HEADS):
                c0 = t * LANES
                win = xp_ref[c * seg:c * seg + HIST_ROWS + CHUNK, c0:c0 + LANES]
                acc = cw_ref[CONV_W - 1:CONV_W, c0:c0 + LANES] * win[HIST_ROWS:, :]
                for s in range(1, CONV_W):
                    tap = pltpu.roll(win, s, 0)[HIST_ROWS:, :]
                    acc = acc + cw_ref[CONV_W - 1 - s:CONV_W - s, c0:c0 + LANES] * tap
                y = acc * jax.nn.sigmoid(acc)
                if p < 2:
                    y = y * lax.rsqrt(jnp.sum(y * y, axis=-1, keepdims=True) + EPS)
                    if p == 0:
                        y = y * (GDN_DK ** -0.5)
                qkv_ref[c * CHUNK:(c + 1) * CHUNK, c0:c0 + LANES] = y

    def gate_math():
        ba = group("ba")
        ri = lax.broadcasted_iota(jnp.int32, (CHUNK, CHUNK), 0)
        ci = lax.broadcasted_iota(jnp.int32, (CHUNK, CHUNK), 1)
        tri = (ri >= ci).astype(BF16)
        lane = lax.broadcasted_iota(jnp.int32, (CHUNK, LANES), 1)
        a_coef = -jnp.exp(par_ref[0:1, :])
        dt_bias = par_ref[1:2, :]
        for c in range(n_chunks):
            bac = ba[c * CHUNK:(c + 1) * CHUNK, :]
            g_all = a_coef * jax.nn.softplus(bac + dt_bias)
            g_hi = g_all.astype(BF16)
            g_r1 = g_all - g_hi.astype(F32)
            g_mid = g_r1.astype(BF16)
            g_lo = (g_r1 - g_mid.astype(F32)).astype(BF16)
            gc = _dot(tri, g_hi) + _dot(tri, g_mid) + _dot(tri, g_lo)
            gate_ref[c * CHUNK:(c + 1) * CHUNK, :] = jnp.where(lane < GDN_HEADS, jax.nn.sigmoid(bac), gc)

    project_part(0)
    project_part(1)
    conv_part(0)
    project_part(2)
    conv_part(1)
    z_ref[...] = group("z").astype(z_ref.dtype)
    gate_math()
    qb_ref[...] = group("qb").astype(qb_ref.dtype)
    conv_part(2)
    kv_ref[...] = group("kv")
    gates_ref[...] = group("gates").astype(gates_ref.dtype)

    if carry:
        def _store_tail():
            tail_ref[0] = xp_ref[tm:tm + HIST_ROWS, :]

        pl.when(pid == pl.num_programs(0) - 1)(_store_tail)
        xp_ref[0:HIST_ROWS, :] = xp_ref[tm:tm + HIST_ROWS, :]
    else:
        for c in range(n_chunks):
            tail_ref[c] = xp_ref[c * seg + CHUNK:c * seg + CHUNK + HIST_ROWS, :]


def _proj(x, mod, seq_rows, g, w_packed, hist8, conv_w, par):
    n = x.shape[0]
    n_seq = n // seq_rows
    carry = n_seq == 1
    tm, spt = _tile_rows(n, seq_rows, PROJ_ROWS)
    if carry:
        n_hist = 1
        hist_map = lambda i: (0, 0, 0)
        xp_rows = HIST_ROWS + tm
    else:
        assert seq_rows == CHUNK
        n_hist = tm // CHUNK
        hist_map = lambda i: (i, 0, 0)
        xp_rows = n_hist * (HIST_ROWS + CHUNK)
    widths = [CONV_DIM, GDN_HEADS * GDN_DV, SWA_HEADS * SWA_HD, KV_COLS, 2 * D_MODEL, LANES]
    return pl.pallas_call(
        functools.partial(_proj_kernel, spt=spt, carry=carry),
        grid=(n // tm,),
        in_specs=[
            pl.BlockSpec((tm, D_MODEL), lambda i: (i, 0)),
            _mod_spec(mod.shape[1]),
            _const_spec((1, D_MODEL)),
            _const_spec((D_MODEL, PROJ_COLS)),
            pl.BlockSpec((n_hist, HIST_ROWS, CONV_DIM), hist_map),
            _const_spec((CONV_W, CONV_DIM)),
            _const_spec((SUBLANES, LANES)),
        ],
        out_specs=[pl.BlockSpec((tm, w), lambda i: (i, 0)) for w in widths]
        + [pl.BlockSpec((n_hist, HIST_ROWS, CONV_DIM), hist_map)],
        out_shape=[jax.ShapeDtypeStruct((n, w), BF16 if i in (1, 2, 4) else F32) for i, w in enumerate(widths)]
        + [jax.ShapeDtypeStruct((n_seq, HIST_ROWS, CONV_DIM), F32)],
        scratch_shapes=[pltpu.VMEM((tm, D_MODEL), BF16), pltpu.VMEM((xp_rows, CONV_DIM), F32)],
        compiler_params=_cparams(1),
        name="in_proj",
    )(x, mod, g, w_packed, hist8, conv_w, par)


def _unit_lower_inverses(ls, eye, level_masks):
    ts = [eye - jnp.where(level_masks[0], l, 0.0) for l in ls]
    ls16 = [l.astype(BF16) for l in ls]
    for mask in level_masks[1:]:
        lk = [jnp.where(mask, l, jnp.zeros_like(l)) for l in ls16]
        tp = [t.astype(BF16) for t in ts]
        m1 = [_dot(a, b) for a, b in zip(lk, tp)]
        m2 = [_dot(a, b.astype(BF16)) for a, b in zip(tp, m1)]
        ts = [t - m for t, m in zip(ts, m2)]
    return ts


def _gdn_kernel(qkv_ref, gate_ref, s0_ref, o_ref, sout_ref, s_ref, *, n_chunks, carry):
    pid = pl.program_id(0)

    if carry:
        def _load_state():
            s_ref[0] = s0_ref[0]

        pl.when(pid == 0)(_load_state)
    else:
        for c in range(n_chunks):
            s_ref[c] = s0_ref[c]

    ri = lax.broadcasted_iota(jnp.int32, (CHUNK, CHUNK), 0)
    ci = lax.broadcasted_iota(jnp.int32, (CHUNK, CHUNK), 1)
    causal = ri >= ci
    strict = ri > ci
    eye = (ri == ci).astype(F32)
    level_masks = []
    for lvl in range(int(math.log2(CHUNK))):
        same_block = (ri >> (lvl + 1)) == (ci >> (lvl + 1))
        level_masks.append(same_block & (((ri >> lvl) & 1) == 1) & (((ci >> lvl) & 1) == 0))

    items = [(c, h) for c in range(n_chunks) for h in range(GDN_HEADS)]
    gate = [gate_ref[c * CHUNK:(c + 1) * CHUNK, :] for c in range(n_chunks)]
    egc_all = [jnp.exp(g) for g in gate]
    g_last_all = [g[CHUNK - 1:CHUNK, :] for g in gate]
    kdf_all = [jnp.exp(gl - g) for gl, g in zip(g_last_all, gate)]
    eg_last_all = [jnp.exp(gl) for gl in g_last_all]

    def tile(c, t):
        return qkv_ref[c * CHUNK:(c + 1) * CHUNK, t * LANES:(t + 1) * LANES]

    def col(arrs, c, h):
        return arrs[c][:, GDN_HEADS + h:GDN_HEADS + h + 1]

    q = [tile(c, h) for c, h in items]
    k = [tile(c, GDN_HEADS + h) for c, h in items]
    v = [tile(c, 2 * GDN_HEADS + h) for c, h in items]
    beta = [gate[c][:, h:h + 1] for c, h in items]
    gcol = [col(gate, c, h) for c, h in items]
    grow = [jnp.sum(g * eye, axis=0, keepdims=True) for g in gcol]
    decay = [jnp.where(causal, jnp.exp(jnp.where(causal, gc - gr, 0.0)), 0.0) for gc, gr in zip(gcol, grow)]
    kb = [x * b for x, b in zip(k, beta)]
    k16 = [x.astype(BF16) for x in k]
    kq = [_dot_nt(jnp.concatenate([a.astype(BF16), b.astype(BF16)], axis=0), c) for a, b, c in zip(kb, q, k16)]
    kk = [x[0:CHUNK, :] for x in kq]
    qk = [x[CHUNK:, :] for x in kq]
    ls = [jnp.where(strict, a * d, 0.0) for a, d in zip(kk, decay)]
    intra16 = [jnp.where(causal, a * d, 0.0).astype(BF16) for a, d in zip(qk, decay)]
    ts = _unit_lower_inverses(ls, eye, level_masks)
    ys = [(t - eye).astype(BF16) for t in ts]
    egc = [col(egc_all, c, h) for c, h in items]
    vb = [x * b for x, b in zip(v, beta)]
    kbe = [x * e for x, e in zip(kb, egc)]
    rhs = [jnp.concatenate([a, b], axis=1) for a, b in zip(vb, kbe)]
    sol = [x + _dot(y, x.astype(BF16)) for x, y in zip(rhs, ys)]
    u = [x[:, 0:GDN_DV] for x in sol]
    w16 = [x[:, GDN_DV:].astype(BF16) for x in sol]
    qg16 = [(x * e).astype(BF16) for x, e in zip(q, egc)]
    kd16 = [(x * col(kdf_all, c, h)).astype(BF16) for x, (c, h) in zip(k, items)]
    eg_last = [col(eg_last_all, c, h) for c, h in items]
    wq16 = [jnp.concatenate([a, b], axis=0) for a, b in zip(w16, qg16)]

    o = [None] * len(items)
    if carry:
        state = [s_ref[0, h] for h in range(GDN_HEADS)]
    for c in range(n_chunks):
        idx = [c * GDN_HEADS + h for h in range(GDN_HEADS)]
        if not carry:
            state = [s_ref[c, h] for h in range(GDN_HEADS)]
        s16 = [s.astype(BF16) for s in state]
        ws_qs = [_dot(wq16[i], s) for i, s in zip(idx, s16)]
        v_new = [u[i] - x[0:CHUNK, :] for i, x in zip(idx, ws_qs)]
        vn16 = [x.astype(BF16) for x in v_new]
        for i, x, vn in zip(idx, ws_qs, vn16):
            o[i] = x[CHUNK:, :] + _dot(intra16[i], vn)
        state = [s * eg_last[i] + _dot_tn(kd16[i], vn) for i, s, vn in zip(idx, state, vn16)]
        if not carry:
            for h in range(GDN_HEADS):
                sout_ref[c, h] = state[h]

    for (c, h), oo in zip(items, o):
        o_ref[c * CHUNK:(c + 1) * CHUNK, h * GDN_DV:(h + 1) * GDN_DV] = oo.astype(o_ref.dtype)

    if carry:
        for h in range(GDN_HEADS):
            s_ref[0, h] = state[h]

        def _store_state():
            sout_ref[0] = s_ref[0]

        pl.when(pid == pl.num_programs(0) - 1)(_store_state)


def _gdn(qkv, gate, s0, seq_rows):
    n = qkv.shape[0]
    n_seq = n // seq_rows
    carry = n_seq == 1
    n_chunks = GDN_CHUNKS_PER_STEP
    rows = n_chunks * CHUNK
    assert n % rows == 0
    if carry:
        n_state = 1
        s_map = lambda i: (0, 0, 0, 0)
    else:
        assert seq_rows == CHUNK
        n_state = n_chunks
        s_map = lambda i: (i, 0, 0, 0)
    kern = functools.partial(_gdn_kernel, n_chunks=n_chunks, carry=carry)
    return pl.pallas_call(
        kern,
        grid=(n // rows,),
        in_specs=[
            pl.BlockSpec((rows, CONV_DIM), lambda i: (i, 0)),
            pl.BlockSpec((rows, LANES), lambda i: (i, 0)),
            pl.BlockSpec((n_state, GDN_HEADS, GDN_DK, GDN_DV), s_map),
        ],
        out_specs=[
            pl.BlockSpec((rows, GDN_HEADS * GDN_DV), lambda i: (i, 0)),
            pl.BlockSpec((n_state, GDN_HEADS, GDN_DK, GDN_DV), s_map),
        ],
        out_shape=[
            jax.ShapeDtypeStruct((n, GDN_HEADS * GDN_DV), BF16),
            jax.ShapeDtypeStruct((n_seq, GDN_HEADS, GDN_DK, GDN_DV), F32),
        ],
        scratch_shapes=[pltpu.VMEM((n_state, GDN_HEADS, GDN_DK, GDN_DV), F32)],
        compiler_params=_cparams(1),
        name="gdn",
    )(qkv, gate, s0)


SWA_KEYS = WINDOW + CHUNK
SWA_KEYS_PAD = 2 * LANES
KV_COLS = 2 * SWA_KV_HEADS * SWA_HD


def _bias_kernel(bucket_ref, rb_ref, o_ref):
    bucket = bucket_ref[...]
    h = pl.program_id(0)
    acc = jnp.zeros((CHUNK, SWA_KEYS), F32)
    for b in range(NUM_BUCKETS):
        acc = jnp.where(bucket == b, rb_ref[b * SWA_HEADS + h], acc)
    o_ref[0] = acc


def _bias_table(bucket, rel_bias_flat):
    return pl.pallas_call(
        _bias_kernel,
        grid=(SWA_HEADS,),
        in_specs=[
            pl.BlockSpec((CHUNK, SWA_KEYS), lambda h: (0, 0)),
            pl.BlockSpec(memory_space=pltpu.SMEM),
        ],
        out_specs=pl.BlockSpec((1, CHUNK, SWA_KEYS), lambda h: (h, 0, 0)),
        out_shape=jax.ShapeDtypeStruct((SWA_HEADS, CHUNK, SWA_KEYS), F32),
        compiler_params=_cparams(1),
        name="swa_bias",
    )(bucket, rel_bias_flat)


def _swa_kernel(q_ref, kvc_ref, kvp_ref, bias_ref, sink_ref, o_ref, kf_ref, *, n_chunks, one_sequence):
    n_pad = SWA_KEYS_PAD - SWA_KEYS
    kf_ref[kf_ref.shape[0] - n_pad:, :] = jnp.zeros((n_pad, KV_COLS), F32)
    if one_sequence:
        kf_ref[0:WINDOW, :] = kvp_ref[...]
        kf_ref[WINDOW:WINDOW + n_chunks * CHUNK, :] = kvc_ref[...]
        key_start = [c * CHUNK for c in range(n_chunks)]
        key_pos = lax.broadcasted_iota(jnp.int32, (CHUNK, SWA_KEYS), 1)
        tile_start = pl.program_id(0) * (n_chunks * CHUNK)
        valid = [key_pos + (tile_start + c * CHUNK - WINDOW) >= 0 for c in range(n_chunks)]
    else:
        for c in range(n_chunks):
            kf_ref[c * SWA_KEYS:c * SWA_KEYS + WINDOW, :] = kvp_ref[c * WINDOW:(c + 1) * WINDOW, :]
            kf_ref[c * SWA_KEYS + WINDOW:(c + 1) * SWA_KEYS, :] = kvc_ref[c * CHUNK:(c + 1) * CHUNK, :]
        key_start = [c * SWA_KEYS for c in range(n_chunks)]

    pairs = [(c, kh) for c in range(n_chunks) for kh in range(SWA_KV_HEADS)]
    half = SWA_KV_HEADS * SWA_HD
    keys_t = [kf_ref[key_start[c]:key_start[c] + SWA_KEYS_PAD, 0:half].T.astype(BF16) for c in range(n_chunks)]
    keys = {(c, kh): keys_t[c][kh * SWA_HD:(kh + 1) * SWA_HD, :] for c, kh in pairs}
    ones = jnp.ones((SWA_KEYS, LANES - SWA_HD), BF16)
    vals = {(c, kh): jnp.concatenate(
        [kf_ref[key_start[c]:key_start[c] + SWA_KEYS,
                (SWA_KV_HEADS + kh) * SWA_HD:(SWA_KV_HEADS + kh + 1) * SWA_HD].astype(BF16), ones], axis=1)
            for c, kh in pairs}
    items = [(c, hd) for c in range(n_chunks) for hd in range(SWA_HEADS)]
    q = [q_ref[c * CHUNK:(c + 1) * CHUNK, hd * SWA_HD:(hd + 1) * SWA_HD] for c, hd in items]
    def group_rows(c, kh):
        return range(c * SWA_HEADS + kh * SWA_GROUP, c * SWA_HEADS + (kh + 1) * SWA_GROUP)

    qk = {(c, kh): _dot(jnp.concatenate([q[i] for i in group_rows(c, kh)], axis=0), keys[(c, kh)])
          for c, kh in pairs}
    logits = [qk[(c, hd // SWA_GROUP)][(hd % SWA_GROUP) * CHUNK:(hd % SWA_GROUP + 1) * CHUNK, 0:SWA_KEYS] + bias_ref[hd]
              for c, hd in items]
    if one_sequence:
        logits = [jnp.where(valid[c], x, -jnp.inf) if c < WINDOW // CHUNK else x for x, (c, _) in zip(logits, items)]
    sink = [sink_ref[hd] for _, hd in items]
    m = [jnp.maximum(jnp.max(x, axis=-1, keepdims=True), s) for x, s in zip(logits, sink)]
    p16 = [jnp.exp(x - mm).astype(BF16) for x, mm in zip(logits, m)]
    pv = {(c, kh): _dot(jnp.concatenate([p16[i] for i in group_rows(c, kh)], axis=0), vals[(c, kh)])
          for c, kh in pairs}
    pv = [pv[(c, hd // SWA_GROUP)][(hd % SWA_GROUP) * CHUNK:(hd % SWA_GROUP + 1) * CHUNK, :] for c, hd in items]
    denom = [x[:, SWA_HD:SWA_HD + 1] + jnp.exp(s - mm) for x, s, mm in zip(pv, sink, m)]
    out = [x[:, 0:SWA_HD] / d for x, d in zip(pv, denom)]
    for x, (c, hd) in zip(out, items):
        o_ref[c * CHUNK:(c + 1) * CHUNK, hd * SWA_HD:(hd + 1) * SWA_HD] = x.astype(o_ref.dtype)


def _swa(qb, kv, kv_hist, bias, sinks, seq_rows):
    n = qb.shape[0]
    n_chunks = SWA_CHUNKS_PER_STEP
    rows = n_chunks * CHUNK
    assert n % rows == 0
    one_sequence = kv_hist is None
    if one_sequence:
        assert rows % WINDOW == 0
        per = rows // WINDOW
        prev_arr = kv
        prev_spec = pl.BlockSpec((WINDOW, KV_COLS), lambda i: (jnp.maximum(i * per - 1, 0), 0))
        kf_rows = WINDOW + rows
    else:
        assert seq_rows == CHUNK
        prev_arr = kv_hist
        prev_spec = pl.BlockSpec((n_chunks * WINDOW, KV_COLS), lambda i: (i, 0))
        kf_rows = n_chunks * SWA_KEYS
    kern = functools.partial(_swa_kernel, n_chunks=n_chunks, one_sequence=one_sequence)
    return pl.pallas_call(
        kern,
        grid=(n // rows,),
        in_specs=[
            pl.BlockSpec((rows, SWA_HEADS * SWA_HD), lambda i: (i, 0)),
            pl.BlockSpec((rows, KV_COLS), lambda i: (i, 0)),
            prev_spec,
            pl.BlockSpec((SWA_HEADS, CHUNK, SWA_KEYS), lambda i: (0, 0, 0)),
            pl.BlockSpec(memory_space=pltpu.SMEM),
        ],
        out_specs=pl.BlockSpec((rows, SWA_HEADS * SWA_HD), lambda i: (i, 0)),
        out_shape=jax.ShapeDtypeStruct((n, SWA_HEADS * SWA_HD), BF16),
        scratch_shapes=[pltpu.VMEM((kf_rows + SWA_KEYS_PAD - SWA_KEYS, KV_COLS), F32)],
        compiler_params=_cparams(1),
        name="swa",
    )(qb, kv, prev_arr, bias, sinks)


def _merge_kernel(x_ref, oa_ref, z_ref, ob_ref, gates_ref, mod_ref, nw_ref, wa_ref, wb_ref, wo_ref, o_ref, oa16_ref,
                  *, spt):
    for h in range(GDN_HEADS):
        cols = slice(h * GDN_DV, (h + 1) * GDN_DV)
        oo = oa_ref[:, cols].astype(F32)
        zz = z_ref[:, cols].astype(F32)
        on = oo * lax.rsqrt(jnp.mean(oo * oo, axis=-1, keepdims=True) + EPS) * nw_ref[...] * (zz * jax.nn.sigmoid(zz))
        oa16_ref[:, cols] = on.astype(BF16)
    ya = _dot(oa16_ref[...], wa_ref[...])
    yb = _dot(ob_ref[...], wb_ref[...])
    merged = (jax.nn.sigmoid(gates_ref[:, 0:D_MODEL].astype(F32)) * ya
              + jax.nn.sigmoid(gates_ref[:, D_MODEL:2 * D_MODEL].astype(F32)) * yb)
    y = _dot(merged.astype(BF16), wo_ref[...])
    rows = x_ref.shape[0] // spt
    for j in range(spt):
        sl = slice(j * rows, (j + 1) * rows)
        o_ref[sl, :] = x_ref[sl, :] + _mod_row(mod_ref, 5, spt, j) * y[sl, :]


def _merge(x, oa, z, ob, gates, mod, seq_rows, norm_w, wa, wb, wo):
    n = x.shape[0]
    tm, spt = _tile_rows(n, seq_rows, MERGE_ROWS)
    row_spec = lambda w: pl.BlockSpec((tm, w), lambda i: (i, 0))
    return pl.pallas_call(
        functools.partial(_merge_kernel, spt=spt),
        grid=(n // tm,),
        in_specs=[
            row_spec(D_MODEL), row_spec(D_MODEL), row_spec(D_MODEL), row_spec(D_MODEL), row_spec(2 * D_MODEL),
            _mod_spec(mod.shape[1]),
            _const_spec((1, GDN_DV)),
            _const_spec((D_MODEL, D_MODEL)), _const_spec((D_MODEL, D_MODEL)), _const_spec((D_MODEL, D_MODEL)),
        ],
        out_specs=row_spec(D_MODEL),
        out_shape=jax.ShapeDtypeStruct((n, D_MODEL), F32),
        scratch_shapes=[pltpu.VMEM((tm, GDN_HEADS * GDN_DV), BF16)],
        compiler_params=_cparams(1),
        name="merge",
    )(x, oa, z, ob, gates, mod, norm_w, wa, wb, wo)


def _t5_bucket(rel):
    half = NUM_BUCKETS // 2
    max_exact = half // 2
    n = jnp.abs(rel)
    large = max_exact + (jnp.log(jnp.maximum(n, 1).astype(jnp.float32) / max_exact)
                         / math.log(MAX_DISTANCE / max_exact) * (half - max_exact)).astype(jnp.int32)
    large = jnp.minimum(large, half - 1)
    return jnp.where(rel > 0, half, 0) + jnp.where(n < max_exact, n, large)


def _trunk(x, mod, conv_hist, s_hist, k_hist, v_hist, w):
    bsz, seq, _ = x.shape
    n = bsz * seq
    x2 = x.reshape(n, D_MODEL)
    x1 = _ffn(x2, mod, seq, w["norm_ffn1"], w["ffn1_in"], w["ffn1_out"], w["norm_final"], sub=0, final=False)
    if conv_hist is None:
        hist8 = jnp.zeros((bsz, HIST_ROWS, CONV_DIM), F32)
        s0 = jnp.zeros((bsz, GDN_HEADS, GDN_DK, GDN_DV), F32)
        kv_hist = None
    else:
        hist8 = jnp.concatenate([jnp.zeros((bsz, HIST_ROWS - (CONV_W - 1), CONV_DIM), F32), conv_hist], axis=1)
        s0 = s_hist
        kv_hist = jnp.concatenate([k_hist, v_hist], axis=2).reshape(bsz * WINDOW, KV_COLS)
    qkv, z, qb, kv, gates, gate, tail = _proj(x1, mod, seq, w["norm_mix"], w["w_in"], hist8, w["conv_w"], w["gdn_par"])
    oa, s_new = _gdn(qkv, gate, s0, seq)
    ob = _swa(qb, kv, kv_hist, w["bias"], w["sinks"], seq)
    x3 = _merge(x1, oa, z, ob, gates, mod, seq, w["gdn_norm_w"], w["w_a"], w["w_b"], w["w_out"])
    y = _ffn(x3, mod, seq, w["norm_ffn2"], w["ffn2_in"], w["ffn2_out"], w["norm_final"], sub=2, final=True)

    conv_new = tail[:, HIST_ROWS - (CONV_W - 1):]
    half = SWA_KV_HEADS * SWA_HD
    kv3 = kv.reshape(bsz, seq, KV_COLS)
    if kv_hist is None:
        k_new = kv3[:, seq - WINDOW:, :half]
        v_new = kv3[:, seq - WINDOW:, half:]
    else:
        keep = WINDOW - seq
        k_new = jnp.concatenate([k_hist.reshape(bsz, WINDOW, half)[:, WINDOW - keep:], kv3[:, :, :half]], axis=1)
        v_new = jnp.concatenate([v_hist.reshape(bsz, WINDOW, half)[:, WINDOW - keep:], kv3[:, :, half:]], axis=1)
    k_new = k_new.reshape(bsz, WINDOW, SWA_KV_HEADS, SWA_HD)
    v_new = v_new.reshape(bsz, WINDOW, SWA_KV_HEADS, SWA_HD)
    return (y.reshape(bsz, seq, D_MODEL), conv_new[None], s_new[None], k_new[None], v_new[None])


def kernel(x_prompt, x_sample, state_gdn_conv, state_gdn_s, cache_swa_k, cache_swa_v, c_prompt, c_sample,
           norm_ffn1, w_ffn1_in, w_ffn1_out, norm_mix, w_in, gdn_conv_w, gdn_a_log, gdn_dt_bias, gdn_norm_w,
           swa_sinks, rel_bias, w_branch_a, w_branch_b, w_out, norm_ffn2, w_ffn2_in, w_ffn2_out,
           w_ada, b_ada, norm_final):
    bp = x_prompt.shape[0]
    bs = x_sample.shape[0]
    assert bp == 1 and x_sample.shape[1] == CHUNK and cache_swa_k.shape[2] == WINDOW

    n_c = bp + bs
    pad = -n_c % SUBLANES
    c_all = jnp.concatenate([c_prompt, c_sample, jnp.zeros((pad, D_MODEL), F32)], axis=0)
    mod_all = _modulation(c_all, w_ada[0], b_ada[0][None, :])
    mod_p = mod_all[:, :bp]
    mod_s = mod_all[:, bp:n_c]

    w_packed = _pack_w_in(w_in)
    par = jnp.zeros((SUBLANES, LANES), F32)
    par = par.at[0, GDN_HEADS:2 * GDN_HEADS].set(gdn_a_log[0]).at[1, GDN_HEADS:2 * GDN_HEADS].set(gdn_dt_bias[0])

    rel = jnp.arange(SWA_KEYS)[None, :] - WINDOW - jnp.arange(CHUNK)[:, None]
    bias = _bias_table(_t5_bucket(rel).astype(jnp.int32), rel_bias.reshape(-1))

    w = {
        "norm_ffn1": norm_ffn1, "ffn1_in": w_ffn1_in[0].astype(BF16), "ffn1_out": w_ffn1_out[0].astype(BF16),
        "norm_mix": norm_mix, "w_in": w_packed, "conv_w": gdn_conv_w[0], "gdn_par": par,
        "gdn_norm_w": gdn_norm_w, "bias": bias, "sinks": swa_sinks[0],
        "w_a": w_branch_a[0].astype(BF16), "w_b": w_branch_b[0].astype(BF16), "w_out": w_out[0].astype(BF16),
        "norm_ffn2": norm_ffn2, "ffn2_in": w_ffn2_in[0].astype(BF16), "ffn2_out": w_ffn2_out[0].astype(BF16),
        "norm_final": norm_final[None, :],
    }
    y_p, p_conv, p_s, p_k, p_v = _trunk(x_prompt, mod_p, None, None, None, None, w)
    y_s, s_conv, s_s, s_k, s_v = _trunk(x_sample, mod_s, state_gdn_conv[0], state_gdn_s[0],
                                        cache_swa_k[0], cache_swa_v[0], w)
    return (y_p, y_s, p_conv, p_s, p_k, p_v, s_conv, s_s, s_k, s_v)
```

```python
import functools
import math

import jax
import jax.numpy as jnp
from jax import lax
from jax.experimental import pallas as pl
from jax.experimental.pallas import tpu as pltpu

F32 = jnp.float32
BF16 = jnp.bfloat16

D_MODEL = 1024
CHUNK = 64
GDN_HEADS = 8
GDN_DK = 128
GDN_DV = 128
CONV_W = 4
CONV_DIM = GDN_HEADS * (2 * GDN_DK + GDN_DV)
SWA_HEADS = 16
SWA_KV_HEADS = 2
SWA_GROUP = SWA_HEADS // SWA_KV_HEADS
SWA_HD = 64
WINDOW = 128
NUM_BUCKETS = 32
MAX_DISTANCE = 128
D_FF = 2816
N_MOD = 9
EPS = 1e-6

LANES = 128
SUBLANES = 8
VMEM_LIMIT = 56 * 1024 * 1024

MXU_COLS = 256
FF_CHUNK = MXU_COLS
assert D_FF % FF_CHUNK == 0
FFN_ROWS = 1024
HIST_ROWS = SUBLANES
MERGE_ROWS = 512
PACK_ROWS = 256
GDN_CHUNKS_PER_STEP = 4
SWA_CHUNKS_PER_STEP = 8


def _dot(a, b):
    return jnp.dot(a, b, preferred_element_type=F32)


def _dot_nt(a, b):
    return lax.dot_general(a, b, (((1,), (1,)), ((), ())), preferred_element_type=F32)


def _dot_tn(a, b):
    return lax.dot_general(a, b, (((0,), (0,)), ((), ())), preferred_element_type=F32)


def _cparams(n_grid):
    return pltpu.CompilerParams(dimension_semantics=("arbitrary",) * n_grid, vmem_limit_bytes=VMEM_LIMIT)


def _const_spec(shape):
    nd = len(shape)
    return pl.BlockSpec(shape, lambda i: (0,) * nd, pipeline_mode=pl.Buffered(1))


def _mod_kernel(c_ref, w_ref, b_ref, o_ref):
    c = c_ref[...]
    a = (c * jax.nn.sigmoid(c)).astype(BF16)
    o_ref[0] = _dot(a, w_ref[...].astype(BF16)) + b_ref[...]


def _modulation(c_pad, w_ada, b_ada):
    rows = c_pad.shape[0]
    return pl.pallas_call(
        _mod_kernel,
        grid=(N_MOD,),
        in_specs=[
            pl.BlockSpec((rows, D_MODEL), lambda j: (0, 0)),
            pl.BlockSpec((D_MODEL, D_MODEL), lambda j: (0, j)),
            pl.BlockSpec((1, D_MODEL), lambda j: (0, j)),
        ],
        out_specs=pl.BlockSpec((1, rows, D_MODEL), lambda j: (j, 0, 0)),
        out_shape=jax.ShapeDtypeStruct((N_MOD, rows, D_MODEL), F32),
        compiler_params=_cparams(1),
        name="adaln_mod",
    )(c_pad, w_ada, b_ada)


def _mod_row(mod_ref, idx, spt, j):
    if mod_ref.shape[1] == 1:
        return mod_ref[idx, 0:1, :]
    return mod_ref[idx, pl.ds(pl.program_id(0) * spt + j, 1), :]


def _norm_mod_store(h_ref, x_ref, g_ref, mod_ref, sub, spt):
    rows = x_ref.shape[0] // spt
    g = g_ref[...]
    for j in range(spt):
        xs = x_ref[j * rows:(j + 1) * rows, :]
        ms = jnp.mean(xs * xs, axis=-1, keepdims=True)
        y = xs * lax.rsqrt(ms + EPS) * g
        sh = _mod_row(mod_ref, 3 * sub, spt, j)
        sc = _mod_row(mod_ref, 3 * sub + 1, spt, j)
        h_ref[j * rows:(j + 1) * rows, :] = (y * (1.0 + sc) + sh).astype(h_ref.dtype)


def _tile_rows(n_rows, seq_rows, target):
    if seq_rows >= target:
        assert seq_rows % target == 0
        return target, 1
    assert target % seq_rows == 0
    tm = min(target, n_rows)
    assert n_rows % tm == 0
    return tm, tm // seq_rows


def _mod_spec(n_seq_total):
    return pl.BlockSpec((N_MOD, n_seq_total, D_MODEL), lambda i: (0, 0, 0))


def _ffn_kernel(x_ref, mod_ref, g_ref, w1_ref, w2_ref, gf_ref, o_ref, h_ref, *, sub, final, spt):
    _norm_mod_store(h_ref, x_ref, g_ref, mod_ref, sub, spt)
    h = h_ref[...]
    acc = None
    for c in range(D_FF // FF_CHUNK):
        c0 = c * FF_CHUNK
        gate = _dot(h, w1_ref[:, c0:c0 + FF_CHUNK])
        up = _dot(h, w1_ref[:, D_FF + c0:D_FF + c0 + FF_CHUNK])
        a = (gate * jax.nn.sigmoid(gate) * up).astype(BF16)
        part = _dot(a, w2_ref[c0:c0 + FF_CHUNK, :])
        acc = part if acc is None else acc + part
    rows = x_ref.shape[0] // spt
    for j in range(spt):
        sl = slice(j * rows, (j + 1) * rows)
        ga = _mod_row(mod_ref, 3 * sub + 2, spt, j)
        xn = x_ref[sl, :] + 0.5 * ga * acc[sl, :]
        if final:
            ms = jnp.mean(xn * xn, axis=-1, keepdims=True)
            xn = xn * lax.rsqrt(ms + EPS) * gf_ref[...]
        o_ref[sl, :] = xn


def _ffn(x, mod, seq_rows, g, w1, w2, gf, *, sub, final):
    n = x.shape[0]
    tm, spt = _tile_rows(n, seq_rows, FFN_ROWS)
    kern = functools.partial(_ffn_kernel, sub=sub, final=final, spt=spt)
    return pl.pallas_call(
        kern,
        grid=(n // tm,),
        in_specs=[
            pl.BlockSpec((tm, D_MODEL), lambda i: (i, 0)),
            _mod_spec(mod.shape[1]),
            _const_spec((1, D_MODEL)),
            _const_spec((D_MODEL, 2 * D_FF)),
            _const_spec((D_FF, D_MODEL)),
            _const_spec((1, D_MODEL)),
        ],
        out_specs=pl.BlockSpec((tm, D_MODEL), lambda i: (i, 0)),
        out_shape=jax.ShapeDtypeStruct((n, D_MODEL), F32),
        scratch_shapes=[pltpu.VMEM((tm, D_MODEL), BF16)],
        compiler_params=_cparams(1),
        name="ffn_final" if final else "ffn",
    )(x, mod, g, w1, w2, gf)


PROJ_GROUPS = (("qkv", CONV_DIM), ("z", GDN_HEADS * GDN_DV), ("qb", SWA_HEADS * SWA_HD),
               ("kv", 2 * SWA_KV_HEADS * SWA_HD), ("gates", 2 * D_MODEL), ("ba", LANES))
PROJ_COLS = sum(w for _, w in PROJ_GROUPS)


assert math.log2(SWA_HD) % 2 == 0
W_IN_ALIGNED = CONV_DIM + GDN_HEADS * GDN_DV
W_IN_SMALL = 2 * GDN_HEADS
W_IN_REST = PROJ_COLS - W_IN_ALIGNED - LANES


def _pack_w_in_kernel(w_ref, o_ref):
    o_ref[:, 0:W_IN_ALIGNED] = w_ref[:, 0:W_IN_ALIGNED].astype(BF16)
    hi = W_IN_ALIGNED + W_IN_SMALL
    n_qb = SWA_HEADS * SWA_HD
    o_ref[:, W_IN_ALIGNED:W_IN_ALIGNED + n_qb] = (w_ref[:, hi:hi + n_qb] * (SWA_HD ** -0.5)).astype(BF16)
    o_ref[:, W_IN_ALIGNED + n_qb:W_IN_ALIGNED + W_IN_REST] = w_ref[:, hi + n_qb:hi + W_IN_REST].astype(BF16)
    lane = lax.broadcasted_iota(jnp.int32, (w_ref.shape[0], LANES), 1)
    small = jnp.where(lane < W_IN_SMALL, w_ref[:, W_IN_ALIGNED:W_IN_ALIGNED + LANES], 0.0)
    o_ref[:, W_IN_ALIGNED + W_IN_REST:PROJ_COLS] = small.astype(BF16)


def _pack_w_in(w_in):
    n_cols = W_IN_ALIGNED + W_IN_SMALL + W_IN_REST
    assert w_in.shape == (1, D_MODEL, n_cols)
    rows = PACK_ROWS
    return pl.pallas_call(
        _pack_w_in_kernel,
        grid=(D_MODEL // rows,),
        in_specs=[pl.BlockSpec((None, rows, n_cols), lambda i: (0, i, 0))],
        out_specs=pl.BlockSpec((rows, PROJ_COLS), lambda i: (i, 0)),
        out_shape=jax.ShapeDtypeStruct((D_MODEL, PROJ_COLS), BF16),
        compiler_params=_cparams(1),
        name="pack_w_in",
    )(w_in)


PROJ_OFFSETS = {}
_off = 0
for _name, _width in PROJ_GROUPS:
    PROJ_OFFSETS[_name] = (_off, _width)
    _off += _width
PROJ_ROWS = 512


def _proj_kernel(x_ref, mod_ref, g_ref, w_ref, hist_ref, cw_ref, par_ref,
                 qkv_ref, z_ref, qb_ref, kv_ref, gates_ref, gate_ref, tail_ref,
                 h_ref, xp_ref, *, spt, carry):
    pid = pl.program_id(0)
    tm = x_ref.shape[0]
    n_chunks = tm // CHUNK
    seg = CHUNK if carry else HIST_ROWS + CHUNK

    _norm_mod_store(h_ref, x_ref, g_ref, mod_ref, 1, spt)
    h = h_ref[...]

    def group(name, lo=0, width=None):
        c0, full = PROJ_OFFSETS[name]
        width = full if width is None else width
        return _dot(h, w_ref[:, c0 + lo:c0 + lo + width])

    if carry:
        def _load_hist():
            xp_ref[0:HIST_ROWS, :] = hist_ref[0]

        pl.when(pid == 0)(_load_hist)
    else:
        for c in range(n_chunks):
            xp_ref[c * seg:c * seg + HIST_ROWS, :] = hist_ref[c]

    part = GDN_HEADS * GDN_DK

    def project_part(p):
        raw = group("qkv", p * part, part)
        cols = slice(p * part, (p + 1) * part)
        if carry:
            xp_ref[HIST_ROWS:HIST_ROWS + tm, cols] = raw
        else:
            for c in range(n_chunks):
                xp_ref[c * seg + HIST_ROWS:(c + 1) * seg, cols] = raw[c * CHUNK:(c + 1) * CHUNK, :]

    def conv_part(p):
        for c in range(n_chunks):
            for t in range(p * GDN_HEADS, (p + 1) * GDN_HEADS):
                c0 = t * LANES
                win = xp_ref[c * seg:c * seg + HIST_ROWS + CHUNK, c0:c0 + LANES]
                acc = cw_ref[CONV_W - 1:CONV_W, c0:c0 + LANES] * win[HIST_ROWS:, :]
                for s in range(1, CONV_W):
                    tap = pltpu.roll(win, s, 0)[HIST_ROWS:, :]
                    acc = acc + cw_ref[CONV_W - 1 - s:CONV_W - s, c0:c0 + LANES] * tap
                y = acc * jax.nn.sigmoid(acc)
                if p < 2:
                    y = y * lax.rsqrt(jnp.sum(y * y, axis=-1, keepdims=True) + EPS)
                    if p == 0:
                        y = y * (GDN_DK ** -0.5)
                qkv_ref[c * CHUNK:(c + 1) * CHUNK, c0:c0 + LANES] = y

    def gate_math():
        ba = group("ba")
        ri = lax.broadcasted_iota(jnp.int32, (CHUNK, CHUNK), 0)
        ci = lax.broadcasted_iota(jnp.int32, (CHUNK, CHUNK), 1)
        tri = (ri >= ci).astype(BF16)
        lane = lax.broadcasted_iota(jnp.int32, (CHUNK, LANES), 1)
        a_coef = -jnp.exp(par_ref[0:1, :])
        dt_bias = par_ref[1:2, :]
        for c in range(n_chunks):
            bac = ba[c * CHUNK:(c + 1) * CHUNK, :]
            g_all = a_coef * jax.nn.softplus(bac + dt_bias)
            g_hi = g_all.astype(BF16)
            g_r1 = g_all - g_hi.astype(F32)
            g_mid = g_r1.astype(BF16)
            g_lo = (g_r1 - g_mid.astype(F32)).astype(BF16)
            gc = _dot(tri, g_hi) + _dot(tri, g_mid) + _dot(tri, g_lo)
            gate_ref[c * CHUNK:(c + 1) * CHUNK, :] = jnp.where(lane < GDN_HEADS, jax.nn.sigmoid(bac), gc)

    project_part(0)
    project_part(1)
    conv_part(0)
    project_part(2)
    conv_part(1)
    z_ref[...] = group("z").astype(z_ref.dtype)
    gate_math()
    qb_ref[...] = group("qb").astype(qb_ref.dtype)
    conv_part(2)
    kv_ref[...] = group("kv")
    gates_ref[...] = group("gates").astype(gates_ref.dtype)

    if carry:
        def _store_tail():
            tail_ref[0] = xp_ref[tm:tm + HIST_ROWS, :]

        pl.when(pid == pl.num_programs(0) - 1)(_store_tail)
        xp_ref[0:HIST_ROWS, :] = xp_ref[tm:tm + HIST_ROWS, :]
    else:
        for c in range(n_chunks):
            tail_ref[c] = xp_ref[c * seg + CHUNK:c * seg + CHUNK + HIST_ROWS, :]


def _proj(x, mod, seq_rows, g, w_packed, hist8, conv_w, par):
    n = x.shape[0]
    n_seq = n // seq_rows
    carry = n_seq == 1
    tm, spt = _tile_rows(n, seq_rows, PROJ_ROWS)
    if carry:
        n_hist = 1
        hist_map = lambda i: (0, 0, 0)
        xp_rows = HIST_ROWS + tm
    else:
        assert seq_rows == CHUNK
        n_hist = tm // CHUNK
        hist_map = lambda i: (i, 0, 0)
        xp_rows = n_hist * (HIST_ROWS + CHUNK)
    widths = [CONV_DIM, GDN_HEADS * GDN_DV, SWA_HEADS * SWA_HD, KV_COLS, 2 * D_MODEL, LANES]
    return pl.pallas_call(
        functools.partial(_proj_kernel, spt=spt, carry=carry),
        grid=(n // tm,),
        in_specs=[
            pl.BlockSpec((tm, D_MODEL), lambda i: (i, 0)),
            _mod_spec(mod.shape[1]),
            _const_spec((1, D_MODEL)),
            _const_spec((D_MODEL, PROJ_COLS)),
            pl.BlockSpec((n_hist, HIST_ROWS, CONV_DIM), hist_map),
            _const_spec((CONV_W, CONV_DIM)),
            _const_spec((SUBLANES, LANES)),
        ],
        out_specs=[pl.BlockSpec((tm, w), lambda i: (i, 0)) for w in widths]
        + [pl.BlockSpec((n_hist, HIST_ROWS, CONV_DIM), hist_map)],
        out_shape=[jax.ShapeDtypeStruct((n, w), BF16 if i in (1, 2, 4) else F32) for i, w in enumerate(widths)]
        + [jax.ShapeDtypeStruct((n_seq, HIST_ROWS, CONV_DIM), F32)],
        scratch_shapes=[pltpu.VMEM((tm, D_MODEL), BF16), pltpu.VMEM((xp_rows, CONV_DIM), F32)],
        compiler_params=_cparams(1),
        name="in_proj",
    )(x, mod, g, w_packed, hist8, conv_w, par)


def _unit_lower_inverses(ls, eye, level_masks):
    ts = [eye - jnp.where(level_masks[0], l, 0.0) for l in ls]
    ls16 = [l.astype(BF16) for l in ls]
    for mask in level_masks[1:]:
        lk = [jnp.where(mask, l, jnp.zeros_like(l)) for l in ls16]
        tp = [t.astype(BF16) for t in ts]
        m1 = [_dot(a, b) for a, b in zip(lk, tp)]
        m2 = [_dot(a, b.astype(BF16)) for a, b in zip(tp, m1)]
        ts = [t - m for t, m in zip(ts, m2)]
    return ts


def _gdn_kernel(qkv_ref, gate_ref, s0_ref, o_ref, sout_ref, s_ref, *, n_chunks, carry):
    pid = pl.program_id(0)

    if carry:
        def _load_state():
            s_ref[0] = s0_ref[0]

        pl.when(pid == 0)(_load_state)
    else:
        for c in range(n_chunks):
            s_ref[c] = s0_ref[c]

    ri = lax.broadcasted_iota(jnp.int32, (CHUNK, CHUNK), 0)
    ci = lax.broadcasted_iota(jnp.int32, (CHUNK, CHUNK), 1)
    causal = ri >= ci
    strict = ri > ci
    eye = (ri == ci).astype(F32)
    level_masks = []
    for lvl in range(int(math.log2(CHUNK))):
        same_block = (ri >> (lvl + 1)) == (ci >> (lvl + 1))
        level_masks.append(same_block & (((ri >> lvl) & 1) == 1) & (((ci >> lvl) & 1) == 0))

    items = [(c, h) for c in range(n_chunks) for h in range(GDN_HEADS)]
    gate = [gate_ref[c * CHUNK:(c + 1) * CHUNK, :] for c in range(n_chunks)]
    egc_all = [jnp.exp(g) for g in gate]
    g_last_all = [g[CHUNK - 1:CHUNK, :] for g in gate]
    kdf_all = [jnp.exp(gl - g) for gl, g in zip(g_last_all, gate)]
    eg_last_all = [jnp.exp(gl) for gl in g_last_all]

    def tile(c, t):
        return qkv_ref[c * CHUNK:(c + 1) * CHUNK, t * LANES:(t + 1) * LANES]

    def col(arrs, c, h):
        return arrs[c][:, GDN_HEADS + h:GDN_HEADS + h + 1]

    q = [tile(c, h) for c, h in items]
    k = [tile(c, GDN_HEADS + h) for c, h in items]
    v = [tile(c, 2 * GDN_HEADS + h) for c, h in items]
    beta = [gate[c][:, h:h + 1] for c, h in items]
    gcol = [col(gate, c, h) for c, h in items]
    grow = [jnp.sum(g * eye, axis=0, keepdims=True) for g in gcol]
    decay = [jnp.where(causal, jnp.exp(jnp.where(causal, gc - gr, 0.0)), 0.0) for gc, gr in zip(gcol, grow)]
    kb = [x * b for x, b in zip(k, beta)]
    k16 = [x.astype(BF16) for x in k]
    kq = [_dot_nt(jnp.concatenate([a.astype(BF16), b.astype(BF16)], axis=0), c) for a, b, c in zip(kb, q, k16)]
    kk = [x[0:CHUNK, :] for x in kq]
    qk = [x[CHUNK:, :] for x in kq]
    ls = [jnp.where(strict, a * d, 0.0) for a, d in zip(kk, decay)]
    intra16 = [jnp.where(causal, a * d, 0.0).astype(BF16) for a, d in zip(qk, decay)]
    ts = _unit_lower_inverses(ls, eye, level_masks)
    ys = [(t - eye).astype(BF16) for t in ts]
    egc = [col(egc_all, c, h) for c, h in items]
    vb = [x * b for x, b in zip(v, beta)]
    kbe = [x * e for x, e in zip(kb, egc)]
    rhs = [jnp.concatenate([a, b], axis=1) for a, b in zip(vb, kbe)]
    sol = [x + _dot(y, x.astype(BF16)) for x, y in zip(rhs, ys)]
    u = [x[:, 0:GDN_DV] for x in sol]
    w16 = [x[:, GDN_DV:].astype(BF16) for x in sol]
    qg16 = [(x * e).astype(BF16) for x, e in zip(q, egc)]
    kd16 = [(x * col(kdf_all, c, h)).astype(BF16) for x, (c, h) in zip(k, items)]
    eg_last = [col(eg_last_all, c, h) for c, h in items]
    wq16 = [jnp.concatenate([a, b], axis=0) for a, b in zip(w16, qg16)]

    o = [None] * len(items)
    if carry:
        state = [s_ref[0, h] for h in range(GDN_HEADS)]
    for c in range(n_chunks):
        idx = [c * GDN_HEADS + h for h in range(GDN_HEADS)]
        if not carry:
            state = [s_ref[c, h] for h in range(GDN_HEADS)]
        s16 = [s.astype(BF16) for s in state]
        ws_qs = [_dot(wq16[i], s) for i, s in zip(idx, s16)]
        v_new = [u[i] - x[0:CHUNK, :] for i, x in zip(idx, ws_qs)]
        vn16 = [x.astype(BF16) for x in v_new]
        for i, x, vn in zip(idx, ws_qs, vn16):
            o[i] = x[CHUNK:, :] + _dot(intra16[i], vn)
        state = [s * eg_last[i] + _dot_tn(kd16[i], vn) for i, s, vn in zip(idx, state, vn16)]
        if not carry:
            for h in range(GDN_HEADS):
                sout_ref[c, h] = state[h]

    for (c, h), oo in zip(items, o):
        o_ref[c * CHUNK:(c + 1) * CHUNK, h * GDN_DV:(h + 1) * GDN_DV] = oo.astype(o_ref.dtype)

    if carry:
        for h in range(GDN_HEADS):
            s_ref[0, h] = state[h]

        def _store_state():
            sout_ref[0] = s_ref[0]

        pl.when(pid == pl.num_programs(0) - 1)(_store_state)


def _gdn(qkv, gate, s0, seq_rows):
    n = qkv.shape[0]
    n_seq = n // seq_rows
    carry = n_seq == 1
    n_chunks = GDN_CHUNKS_PER_STEP
    rows = n_chunks * CHUNK
    assert n % rows == 0
    if carry:
        n_state = 1
        s_map = lambda i: (0, 0, 0, 0)
    else:
        assert seq_rows == CHUNK
        n_state = n_chunks
        s_map = lambda i: (i, 0, 0, 0)
    kern = functools.partial(_gdn_kernel, n_chunks=n_chunks, carry=carry)
    return pl.pallas_call(
        kern,
        grid=(n // rows,),
        in_specs=[
            pl.BlockSpec((rows, CONV_DIM), lambda i: (i, 0)),
            pl.BlockSpec((rows, LANES), lambda i: (i, 0)),
            pl.BlockSpec((n_state, GDN_HEADS, GDN_DK, GDN_DV), s_map),
        ],
        out_specs=[
            pl.BlockSpec((rows, GDN_HEADS * GDN_DV), lambda i: (i, 0)),
            pl.BlockSpec((n_state, GDN_HEADS, GDN_DK, GDN_DV), s_map),
        ],
        out_shape=[
            jax.ShapeDtypeStruct((n, GDN_HEADS * GDN_DV), BF16),
            jax.ShapeDtypeStruct((n_seq, GDN_HEADS, GDN_DK, GDN_DV), F32),
        ],
        scratch_shapes=[pltpu.VMEM((n_state, GDN_HEADS, GDN_DK, GDN_DV), F32)],
        compiler_params=_cparams(1),
        name="gdn",
    )(qkv, gate, s0)


SWA_KEYS = WINDOW + CHUNK
SWA_KEYS_PAD = 2 * LANES
KV_COLS = 2 * SWA_KV_HEADS * SWA_HD


def _bias_kernel(bucket_ref, rb_ref, o_ref):
    bucket = bucket_ref[...]
    h = pl.program_id(0)
    acc = jnp.zeros((CHUNK, SWA_KEYS), F32)
    for b in range(NUM_BUCKETS):
        acc = jnp.where(bucket == b, rb_ref[b * SWA_HEADS + h], acc)
    o_ref[0] = acc


def _bias_table(bucket, rel_bias_flat):
    return pl.pallas_call(
        _bias_kernel,
        grid=(SWA_HEADS,),
        in_specs=[
            pl.BlockSpec((CHUNK, SWA_KEYS), lambda h: (0, 0)),
            pl.BlockSpec(memory_space=pltpu.SMEM),
        ],
        out_specs=pl.BlockSpec((1, CHUNK, SWA_KEYS), lambda h: (h, 0, 0)),
        out_shape=jax.ShapeDtypeStruct((SWA_HEADS, CHUNK, SWA_KEYS), F32),
        compiler_params=_cparams(1),
        name="swa_bias",
    )(bucket, rel_bias_flat)


def _swa_kernel(q_ref, kvc_ref, kvp_ref, bias_ref, sink_ref, o_ref, kf_ref, *, n_chunks, one_sequence):
    n_pad = SWA_KEYS_PAD - SWA_KEYS
    kf_ref[kf_ref.shape[0] - n_pad:, :] = jnp.zeros((n_pad, KV_COLS), F32)
    if one_sequence:
        kf_ref[0:WINDOW, :] = kvp_ref[...]
        kf_ref[WINDOW:WINDOW + n_chunks * CHUNK, :] = kvc_ref[...]
        key_start = [c * CHUNK for c in range(n_chunks)]
        key_pos = lax.broadcasted_iota(jnp.int32, (CHUNK, SWA_KEYS), 1)
        tile_start = pl.program_id(0) * (n_chunks * CHUNK)
        valid = [key_pos + (tile_start + c * CHUNK - WINDOW) >= 0 for c in range(n_chunks)]
    else:
        for c in range(n_chunks):
            kf_ref[c * SWA_KEYS:c * SWA_KEYS + WINDOW, :] = kvp_ref[c * WINDOW:(c + 1) * WINDOW, :]
            kf_ref[c * SWA_KEYS + WINDOW:(c + 1) * SWA_KEYS, :] = kvc_ref[c * CHUNK:(c + 1) * CHUNK, :]
        key_start = [c * SWA_KEYS for c in range(n_chunks)]

    pairs = [(c, kh) for c in range(n_chunks) for kh in range(SWA_KV_HEADS)]
    half = SWA_KV_HEADS * SWA_HD
    keys_t = [kf_ref[key_start[c]:key_start[c] + SWA_KEYS_PAD, 0:half].T.astype(BF16) for c in range(n_chunks)]
    keys = {(c, kh): keys_t[c][kh * SWA_HD:(kh + 1) * SWA_HD, :] for c, kh in pairs}
    ones = jnp.ones((SWA_KEYS, LANES - SWA_HD), BF16)
    vals = {(c, kh): jnp.concatenate(
        [kf_ref[key_start[c]:key_start[c] + SWA_KEYS,
                (SWA_KV_HEADS + kh) * SWA_HD:(SWA_KV_HEADS + kh + 1) * SWA_HD].astype(BF16), ones], axis=1)
            for c, kh in pairs}
    items = [(c, hd) for c in range(n_chunks) for hd in range(SWA_HEADS)]
    q = [q_ref[c * CHUNK:(c + 1) * CHUNK, hd * SWA_HD:(hd + 1) * SWA_HD] for c, hd in items]
    def group_rows(c, kh):
        return range(c * SWA_HEADS + kh * SWA_GROUP, c * SWA_HEADS + (kh + 1) * SWA_GROUP)

    qk = {(c, kh): _dot(jnp.concatenate([q[i] for i in group_rows(c, kh)], axis=0), keys[(c, kh)])
          for c, kh in pairs}
    logits = [qk[(c, hd // SWA_GROUP)][(hd % SWA_GROUP) * CHUNK:(hd % SWA_GROUP + 1) * CHUNK, 0:SWA_KEYS] + bias_ref[hd]
              for c, hd in items]
    if one_sequence:
        logits = [jnp.where(valid[c], x, -jnp.inf) if c < WINDOW // CHUNK else x for x, (c, _) in zip(logits, items)]
    sink = [sink_ref[hd] for _, hd in items]
    m = [jnp.maximum(jnp.max(x, axis=-1, keepdims=True), s) for x, s in zip(logits, sink)]
    p16 = [jnp.exp(x - mm).astype(BF16) for x, mm in zip(logits, m)]
    pv = {(c, kh): _dot(jnp.concatenate([p16[i] for i in group_rows(c, kh)], axis=0), vals[(c, kh)])
          for c, kh in pairs}
    pv = [pv[(c, hd // SWA_GROUP)][(hd % SWA_GROUP) * CHUNK:(hd % SWA_GROUP + 1) * CHUNK, :] for c, hd in items]
    denom = [x[:, SWA_HD:SWA_HD + 1] + jnp.exp(s - mm) for x, s, mm in zip(pv, sink, m)]
    out = [x[:, 0:SWA_HD] / d for x, d in zip(pv, denom)]
    for x, (c, hd) in zip(out, items):
        o_ref[c * CHUNK:(c + 1) * CHUNK, hd * SWA_HD:(hd + 1) * SWA_HD] = x.astype(o_ref.dtype)


def _swa(qb, kv, kv_hist, bias, sinks, seq_rows):
    n = qb.shape[0]
    n_chunks = SWA_CHUNKS_PER_STEP
    rows = n_chunks * CHUNK
    assert n % rows == 0
    one_sequence = kv_hist is None
    if one_sequence:
        assert rows % WINDOW == 0
        per = rows // WINDOW
        prev_arr = kv
        prev_spec = pl.BlockSpec((WINDOW, KV_COLS), lambda i: (jnp.maximum(i * per - 1, 0), 0))
        kf_rows = WINDOW + rows
    else:
        assert seq_rows == CHUNK
        prev_arr = kv_hist
        prev_spec = pl.BlockSpec((n_chunks * WINDOW, KV_COLS), lambda i: (i, 0))
        kf_rows = n_chunks * SWA_KEYS
    kern = functools.partial(_swa_kernel, n_chunks=n_chunks, one_sequence=one_sequence)
    return pl.pallas_call(
        kern,
        grid=(n // rows,),
        in_specs=[
            pl.BlockSpec((rows, SWA_HEADS * SWA_HD), lambda i: (i, 0)),
            pl.BlockSpec((rows, KV_COLS), lambda i: (i, 0)),
            prev_spec,
            pl.BlockSpec((SWA_HEADS, CHUNK, SWA_KEYS), lambda i: (0, 0, 0)),
            pl.BlockSpec(memory_space=pltpu.SMEM),
        ],
        out_specs=pl.BlockSpec((rows, SWA_HEADS * SWA_HD), lambda i: (i, 0)),
        out_shape=jax.ShapeDtypeStruct((n, SWA_HEADS * SWA_HD), BF16),
        scratch_shapes=[pltpu.VMEM((kf_rows + SWA_KEYS_PAD - SWA_KEYS, KV_COLS), F32)],
        compiler_params=_cparams(1),
        name="swa",
    )(qb, kv, prev_arr, bias, sinks)


def _merge_kernel(x_ref, oa_ref, z_ref, ob_ref, gates_ref, mod_ref, nw_ref, wa_ref, wb_ref, wo_ref, o_ref, oa16_ref,
                  *, spt):
    for h in range(GDN_HEADS):
        cols = slice(h * GDN_DV, (h + 1) * GDN_DV)
        oo = oa_ref[:, cols].astype(F32)
        zz = z_ref[:, cols].astype(F32)
        on = oo * lax.rsqrt(jnp.mean(oo * oo, axis=-1, keepdims=True) + EPS) * nw_ref[...] * (zz * jax.nn.sigmoid(zz))
        oa16_ref[:, cols] = on.astype(BF16)
    ya = _dot(oa16_ref[...], wa_ref[...])
    yb = _dot(ob_ref[...], wb_ref[...])
    merged = (jax.nn.sigmoid(gates_ref[:, 0:D_MODEL].astype(F32)) * ya
              + jax.nn.sigmoid(gates_ref[:, D_MODEL:2 * D_MODEL].astype(F32)) * yb)
    y = _dot(merged.astype(BF16), wo_ref[...])
    rows = x_ref.shape[0] // spt
    for j in range(spt):
        sl = slice(j * rows, (j + 1) * rows)
        o_ref[sl, :] = x_ref[sl, :] + _mod_row(mod_ref, 5, spt, j) * y[sl, :]


def _merge(x, oa, z, ob, gates, mod, seq_rows, norm_w, wa, wb, wo):
    n = x.shape[0]
    tm, spt = _tile_rows(n, seq_rows, MERGE_ROWS)
    row_spec = lambda w: pl.BlockSpec((tm, w), lambda i: (i, 0))
    return pl.pallas_call(
        functools.partial(_merge_kernel, spt=spt),
        grid=(n // tm,),
        in_specs=[
            row_spec(D_MODEL), row_spec(D_MODEL), row_spec(D_MODEL), row_spec(D_MODEL), row_spec(2 * D_MODEL),
            _mod_spec(mod.shape[1]),
            _const_spec((1, GDN_DV)),
            _const_spec((D_MODEL, D_MODEL)), _const_spec((D_MODEL, D_MODEL)), _const_spec((D_MODEL, D_MODEL)),
        ],
        out_specs=row_spec(D_MODEL),
        out_shape=jax.ShapeDtypeStruct((n, D_MODEL), F32),
        scratch_shapes=[pltpu.VMEM((tm, GDN_HEADS * GDN_DV), BF16)],
        compiler_params=_cparams(1),
        name="merge",
    )(x, oa, z, ob, gates, mod, norm_w, wa, wb, wo)


def _t5_bucket(rel):
    half = NUM_BUCKETS // 2
    max_exact = half // 2
    n = jnp.abs(rel)
    large = max_exact + (jnp.log(jnp.maximum(n, 1).astype(jnp.float32) / max_exact)
                         / math.log(MAX_DISTANCE / max_exact) * (half - max_exact)).astype(jnp.int32)
    large = jnp.minimum(large, half - 1)
    return jnp.where(rel > 0, half, 0) + jnp.where(n < max_exact, n, large)


def _trunk(x, mod, conv_hist, s_hist, k_hist, v_hist, w):
    bsz, seq, _ = x.shape
    n = bsz * seq
    x2 = x.reshape(n, D_MODEL)
    x1 = _ffn(x2, mod, seq, w["norm_ffn1"], w["ffn1_in"], w["ffn1_out"], w["norm_final"], sub=0, final=False)
    if conv_hist is None:
        hist8 = jnp.zeros((bsz, HIST_ROWS, CONV_DIM), F32)
        s0 = jnp.zeros((bsz, GDN_HEADS, GDN_DK, GDN_DV), F32)
        kv_hist = None
    else:
        hist8 = jnp.concatenate([jnp.zeros((bsz, HIST_ROWS - (CONV_W - 1), CONV_DIM), F32), conv_hist], axis=1)
        s0 = s_hist
        kv_hist = jnp.concatenate([k_hist, v_hist], axis=2).reshape(bsz * WINDOW, KV_COLS)
    qkv, z, qb, kv, gates, gate, tail = _proj(x1, mod, seq, w["norm_mix"], w["w_in"], hist8, w["conv_w"], w["gdn_par"])
    oa, s_new = _gdn(qkv, gate, s0, seq)
    ob = _swa(qb, kv, kv_hist, w["bias"], w["sinks"], seq)
    x3 = _merge(x1, oa, z, ob, gates, mod, seq, w["gdn_norm_w"], w["w_a"], w["w_b"], w["w_out"])
    y = _ffn(x3, mod, seq, w["norm_ffn2"], w["ffn2_in"], w["ffn2_out"], w["norm_final"], sub=2, final=True)

    conv_new = tail[:, HIST_ROWS - (CONV_W - 1):]
    half = SWA_KV_HEADS * SWA_HD
    kv3 = kv.reshape(bsz, seq, KV_COLS)
    if kv_hist is None:
        k_new = kv3[:, seq - WINDOW:, :half]
        v_new = kv3[:, seq - WINDOW:, half:]
    else:
        keep = WINDOW - seq
        k_new = jnp.concatenate([k_hist.reshape(bsz, WINDOW, half)[:, WINDOW - keep:], kv3[:, :, :half]], axis=1)
        v_new = jnp.concatenate([v_hist.reshape(bsz, WINDOW, half)[:, WINDOW - keep:], kv3[:, :, half:]], axis=1)
    k_new = k_new.reshape(bsz, WINDOW, SWA_KV_HEADS, SWA_HD)
    v_new = v_new.reshape(bsz, WINDOW, SWA_KV_HEADS, SWA_HD)
    return (y.reshape(bsz, seq, D_MODEL), conv_new[None], s_new[None], k_new[None], v_new[None])


def kernel(x_prompt, x_sample, state_gdn_conv, state_gdn_s, cache_swa_k, cache_swa_v, c_prompt, c_sample,
           norm_ffn1, w_ffn1_in, w_ffn1_out, norm_mix, w_in, gdn_conv_w, gdn_a_log, gdn_dt_bias, gdn_norm_w,
           swa_sinks, rel_bias, w_branch_a, w_branch_b, w_out, norm_ffn2, w_ffn2_in, w_ffn2_out,
           w_ada, b_ada, norm_final):
    bp = x_prompt.shape[0]
    bs = x_sample.shape[0]
    assert bp == 1 and x_sample.shape[1] == CHUNK and cache_swa_k.shape[2] == WINDOW

    n_c = bp + bs
    pad = -n_c % SUBLANES
    c_all = jnp.concatenate([c_prompt, c_sample, jnp.zeros((pad, D_MODEL), F32)], axis=0)
    mod_all = _modulation(c_all, w_ada[0], b_ada[0][None, :])
    mod_p = mod_all[:, :bp]
    mod_s = mod_all[:, bp:n_c]

    w_packed = _pack_w_in(w_in)
    par = jnp.zeros((SUBLANES, LANES), F32)
    par = par.at[0, GDN_HEADS:2 * GDN_HEADS].set(gdn_a_log[0]).at[1, GDN_HEADS:2 * GDN_HEADS].set(gdn_dt_bias[0])

    rel = jnp.arange(SWA_KEYS)[None, :] - WINDOW - jnp.arange(CHUNK)[:, None]
    bias = _bias_table(_t5_bucket(rel).astype(jnp.int32), rel_bias.reshape(-1))

    w = {
        "norm_ffn1": norm_ffn1, "ffn1_in": w_ffn1_in[0].astype(BF16), "ffn1_out": w_ffn1_out[0].astype(BF16),
        "norm_mix": norm_mix, "w_in": w_packed, "conv_w": gdn_conv_w[0], "gdn_par": par,
        "gdn_norm_w": gdn_norm_w, "bias": bias, "sinks": swa_sinks[0],
        "w_a": w_branch_a[0].astype(BF16), "w_b": w_branch_b[0].astype(BF16), "w_out": w_out[0].astype(BF16),
        "norm_ffn2": norm_ffn2, "ffn2_in": w_ffn2_in[0].astype(BF16), "ffn2_out": w_ffn2_out[0].astype(BF16),
        "norm_final": norm_final[None, :],
    }
    y_p, p_conv, p_s, p_k, p_v = _trunk(x_prompt, mod_p, None, None, None, None, w)
    y_s, s_conv, s_s, s_k, s_v = _trunk(x_sample, mod_s, state_gdn_conv[0], state_gdn_s[0],
                                        cache_swa_k[0], cache_swa_v[0], w)
    return (y_p, y_s, p_conv, p_s, p_k, p_v, s_conv, s_s, s_k, s_v)
```

```python
import functools
import math

import jax
import jax.numpy as jnp
from jax import lax
from jax.experimental import pallas as pl
from jax.experimental.pallas import tpu as pltpu

F32 = jnp.float32
BF16 = jnp.bfloat16

D_MODEL = 1024
CHUNK = 64
GDN_HEADS = 8
GDN_DK = 128
GDN_DV = 128
CONV_W = 4
CONV_DIM = GDN_HEADS * (2 * GDN_DK + GDN_DV)
SWA_HEADS = 16
SWA_KV_HEADS = 2
SWA_GROUP = SWA_HEADS // SWA_KV_HEADS
SWA_HD = 64
WINDOW = 128
NUM_BUCKETS = 32
MAX_DISTANCE = 128
D_FF = 2816
N_MOD = 9
EPS = 1e-6

LANES = 128
SUBLANES = 8
VMEM_LIMIT = 56 * 1024 * 1024

MXU_COLS = 256
FF_CHUNK = MXU_COLS
assert D_FF % FF_CHUNK == 0
FFN_ROWS = 1024
HIST_ROWS = SUBLANES
MERGE_ROWS = 512
PACK_ROWS = 256
GDN_CHUNKS_PER_STEP = 4
SWA_CHUNKS_PER_STEP = 16


def _dot(a, b):
    return jnp.dot(a, b, preferred_element_type=F32)


def _dot_nt(a, b):
    return lax.dot_general(a, b, (((1,), (1,)), ((), ())), preferred_element_type=F32)


def _dot_tn(a, b):
    return lax.dot_general(a, b, (((0,), (0,)), ((), ())), preferred_element_type=F32)


def _cparams(n_grid):
    return pltpu.CompilerParams(dimension_semantics=("arbitrary",) * n_grid, vmem_limit_bytes=VMEM_LIMIT)


def _const_spec(shape):
    nd = len(shape)
    return pl.BlockSpec(shape, lambda i: (0,) * nd, pipeline_mode=pl.Buffered(1))


def _mod_kernel(c_ref, w_ref, b_ref, o_ref):
    c = c_ref[...]
    a = (c * jax.nn.sigmoid(c)).astype(BF16)
    o_ref[0] = _dot(a, w_ref[...].astype(BF16)) + b_ref[...]


def _modulation(c_pad, w_ada, b_ada):
    rows = c_pad.shape[0]
    return pl.pallas_call(
        _mod_kernel,
        grid=(N_MOD,),
        in_specs=[
            pl.BlockSpec((rows, D_MODEL), lambda j: (0, 0)),
            pl.BlockSpec((D_MODEL, D_MODEL), lambda j: (0, j)),
            pl.BlockSpec((1, D_MODEL), lambda j: (0, j)),
        ],
        out_specs=pl.BlockSpec((1, rows, D_MODEL), lambda j: (j, 0, 0)),
        out_shape=jax.ShapeDtypeStruct((N_MOD, rows, D_MODEL), F32),
        compiler_params=_cparams(1),
        name="adaln_mod",
    )(c_pad, w_ada, b_ada)


def _mod_row(mod_ref, idx, spt, j):
    if mod_ref.shape[1] == 1:
        return mod_ref[idx, 0:1, :]
    return mod_ref[idx, pl.ds(pl.program_id(0) * spt + j, 1), :]


def _norm_mod_store(h_ref, x_ref, g_ref, mod_ref, sub, spt):
    rows = x_ref.shape[0] // spt
    g = g_ref[...]
    for j in range(spt):
        xs = x_ref[j * rows:(j + 1) * rows, :]
        ms = jnp.mean(xs * xs, axis=-1, keepdims=True)
        y = xs * lax.rsqrt(ms + EPS) * g
        sh = _mod_row(mod_ref, 3 * sub, spt, j)
        sc = _mod_row(mod_ref, 3 * sub + 1, spt, j)
        h_ref[j * rows:(j + 1) * rows, :] = (y * (1.0 + sc) + sh).astype(h_ref.dtype)


def _tile_rows(n_rows, seq_rows, target):
    if seq_rows >= target:
        assert seq_rows % target == 0
        return target, 1
    assert target % seq_rows == 0
    tm = min(target, n_rows)
    assert n_rows % tm == 0
    return tm, tm // seq_rows


def _mod_spec(n_seq_total):
    return pl.BlockSpec((N_MOD, n_seq_total, D_MODEL), lambda i: (0, 0, 0))


def _ffn_kernel(x_ref, mod_ref, g_ref, w1_ref, w2_ref, gf_ref, o_ref, h_ref, *, sub, final, spt):
    _norm_mod_store(h_ref, x_ref, g_ref, mod_ref, sub, spt)
    h = h_ref[...]
    acc = None
    for c in range(D_FF // FF_CHUNK):
        c0 = c * FF_CHUNK
        gate = _dot(h, w1_ref[:, c0:c0 + FF_CHUNK])
        up = _dot(h, w1_ref[:, D_FF + c0:D_FF + c0 + FF_CHUNK])
        a = (gate * jax.nn.sigmoid(gate) * up).astype(BF16)
        part = _dot(a, w2_ref[c0:c0 + FF_CHUNK, :])
        acc = part if acc is None else acc + part
    rows = x_ref.shape[0] // spt
    for j in range(spt):
        sl = slice(j * rows, (j + 1) * rows)
        ga = _mod_row(mod_ref, 3 * sub + 2, spt, j)
        xn = x_ref[sl, :] + 0.5 * ga * acc[sl, :]
        if final:
            ms = jnp.mean(xn * xn, axis=-1, keepdims=True)
            xn = xn * lax.rsqrt(ms + EPS) * gf_ref[...]
        o_ref[sl, :] = xn


def _ffn(x, mod, seq_rows, g, w1, w2, gf, *, sub, final):
    n = x.shape[0]
    tm, spt = _tile_rows(n, seq_rows, FFN_ROWS)
    kern = functools.partial(_ffn_kernel, sub=sub, final=final, spt=spt)
    return pl.pallas_call(
        kern,
        grid=(n // tm,),
        in_specs=[
            pl.BlockSpec((tm, D_MODEL), lambda i: (i, 0)),
            _mod_spec(mod.shape[1]),
            _const_spec((1, D_MODEL)),
            _const_spec((D_MODEL, 2 * D_FF)),
            _const_spec((D_FF, D_MODEL)),
            _const_spec((1, D_MODEL)),
        ],
        out_specs=pl.BlockSpec((tm, D_MODEL), lambda i: (i, 0)),
        out_shape=jax.ShapeDtypeStruct((n, D_MODEL), F32),
        scratch_shapes=[pltpu.VMEM((tm, D_MODEL), BF16)],
        compiler_params=_cparams(1),
        name="ffn_final" if final else "ffn",
    )(x, mod, g, w1, w2, gf)


PROJ_GROUPS = (("qkv", CONV_DIM), ("z", GDN_HEADS * GDN_DV), ("qb", SWA_HEADS * SWA_HD),
               ("kv", 2 * SWA_KV_HEADS * SWA_HD), ("gates", 2 * D_MODEL), ("ba", LANES))
PROJ_COLS = sum(w for _, w in PROJ_GROUPS)


assert math.log2(SWA_HD) % 2 == 0
W_IN_ALIGNED = CONV_DIM + GDN_HEADS * GDN_DV
W_IN_SMALL = 2 * GDN_HEADS
W_IN_REST = PROJ_COLS - W_IN_ALIGNED - LANES


def _pack_w_in_kernel(w_ref, o_ref):
    o_ref[:, 0:W_IN_ALIGNED] = w_ref[:, 0:W_IN_ALIGNED].astype(BF16)
    hi = W_IN_ALIGNED + W_IN_SMALL
    n_qb = SWA_HEADS * SWA_HD
    o_ref[:, W_IN_ALIGNED:W_IN_ALIGNED + n_qb] = (w_ref[:, hi:hi + n_qb] * (SWA_HD ** -0.5)).astype(BF16)
    o_ref[:, W_IN_ALIGNED + n_qb:W_IN_ALIGNED + W_IN_REST] = w_ref[:, hi + n_qb:hi + W_IN_REST].astype(BF16)
    lane = lax.broadcasted_iota(jnp.int32, (w_ref.shape[0], LANES), 1)
    small = jnp.where(lane < W_IN_SMALL, w_ref[:, W_IN_ALIGNED:W_IN_ALIGNED + LANES], 0.0)
    o_ref[:, W_IN_ALIGNED + W_IN_REST:PROJ_COLS] = small.astype(BF16)


def _pack_w_in(w_in):
    n_cols = W_IN_ALIGNED + W_IN_SMALL + W_IN_REST
    assert w_in.shape == (1, D_MODEL, n_cols)
    rows = PACK_ROWS
    return pl.pallas_call(
        _pack_w_in_kernel,
        grid=(D_MODEL // rows,),
        in_specs=[pl.BlockSpec((None, rows, n_cols), lambda i: (0, i, 0))],
        out_specs=pl.BlockSpec((rows, PROJ_COLS), lambda i: (i, 0)),
        out_shape=jax.ShapeDtypeStruct((D_MODEL, PROJ_COLS), BF16),
        compiler_params=_cparams(1),
        name="pack_w_in",
    )(w_in)


PROJ_OFFSETS = {}
_off = 0
for _name, _width in PROJ_GROUPS:
    PROJ_OFFSETS[_name] = (_off, _width)
    _off += _width
PROJ_ROWS = 512


def _proj_kernel(x_ref, mod_ref, g_ref, w_ref, hist_ref, cw_ref, par_ref,
                 qkv_ref, z_ref, qb_ref, kv_ref, gates_ref, gate_ref, tail_ref,
                 h_ref, xp_ref, *, spt, carry):
    pid = pl.program_id(0)
    tm = x_ref.shape[0]
    n_chunks = tm // CHUNK
    seg = CHUNK if carry else HIST_ROWS + CHUNK

    _norm_mod_store(h_ref, x_ref, g_ref, mod_ref, 1, spt)
    h = h_ref[...]

    def group(name, lo=0, width=None):
        c0, full = PROJ_OFFSETS[name]
        width = full if width is None else width
        return _dot(h, w_ref[:, c0 + lo:c0 + lo + width])

    if carry:
        def _load_hist():
            xp_ref[0:HIST_ROWS, :] = hist_ref[0]

        pl.when(pid == 0)(_load_hist)
    else:
        for c in range(n_chunks):
            xp_ref[c * seg:c * seg + HIST_ROWS, :] = hist_ref[c]

    part = GDN_HEADS * GDN_DK

    def project_part(p):
        raw = group("qkv", p * part, part)
        cols = slice(p * part, (p + 1) * part)
        if carry:
            xp_ref[HIST_ROWS:HIST_ROWS + tm, cols] = raw
        else:
            for c in range(n_chunks):
                xp_ref[c * seg + HIST_ROWS:(c + 1) * seg, cols] = raw[c * CHUNK:(c + 1) * CHUNK, :]

    def conv_part(p):
        for c in range(n_chunks):
            for t in range(p * GDN_HEADS, (p + 1) * GDN_HEADS):
                c0 = t * LANES
                win = xp_ref[c * seg:c * seg + HIST_ROWS + CHUNK, c0:c0 + LANES]
                acc = cw_ref[CONV_W - 1:CONV_W, c0:c0 + LANES] * win[HIST_ROWS:, :]
                for s in range(1, CONV_W):
                    tap = pltpu.roll(win, s, 0)[HIST_ROWS:, :]
                    acc = acc + cw_ref[CONV_W - 1 - s:CONV_W - s, c0:c0 + LANES] * tap
                y = acc * jax.nn.sigmoid(acc)
                if p < 2:
                    y = y * lax.rsqrt(jnp.sum(y * y, axis=-1, keepdims=True) + EPS)
                    if p == 0:
                        y = y * (GDN_DK ** -0.5)
                qkv_ref[c * CHUNK:(c + 1) * CHUNK, c0:c0 + LANES] = y

    def gate_math():
        ba = group("ba")
        ri = lax.broadcasted_iota(jnp.int32, (CHUNK, CHUNK), 0)
        ci = lax.broadcasted_iota(jnp.int32, (CHUNK, CHUNK), 1)
        tri = (ri >= ci).astype(BF16)
        lane = lax.broadcasted_iota(jnp.int32, (CHUNK, LANES), 1)
        a_coef = -jnp.exp(par_ref[0:1, :])
        dt_bias = par_ref[1:2, :]
        for c in range(n_chunks):
            bac = ba[c * CHUNK:(c + 1) * CHUNK, :]
            g_all = a_coef * jax.nn.softplus(bac + dt_bias)
            g_hi = g_all.astype(BF16)
            g_r1 = g_all - g_hi.astype(F32)
            g_mid = g_r1.astype(BF16)
            g_lo = (g_r1 - g_mid.astype(F32)).astype(BF16)
            gc = _dot(tri, g_hi) + _dot(tri, g_mid) + _dot(tri, g_lo)
            gate_ref[c * CHUNK:(c + 1) * CHUNK, :] = jnp.where(lane < GDN_HEADS, jax.nn.sigmoid(bac), gc)

    project_part(0)
    project_part(1)
    conv_part(0)
    project_part(2)
    conv_part(1)
    z_ref[...] = group("z").astype(z_ref.dtype)
    gate_math()
    qb_ref[...] = group("qb").astype(qb_ref.dtype)
    conv_part(2)
    kv_ref[...] = group("kv")
    gates_ref[...] = group("gates").astype(gates_ref.dtype)

    if carry:
        def _store_tail():
            tail_ref[0] = xp_ref[tm:tm + HIST_ROWS, :]

        pl.when(pid == pl.num_programs(0) - 1)(_store_tail)
        xp_ref[0:HIST_ROWS, :] = xp_ref[tm:tm + HIST_ROWS, :]
    else:
        for c in range(n_chunks):
            tail_ref[c] = xp_ref[c * seg + CHUNK:c * seg + CHUNK + HIST_ROWS, :]


def _proj(x, mod, seq_rows, g, w_packed, hist8, conv_w, par):
    n = x.shape[0]
    n_seq = n // seq_rows
    carry = n_seq == 1
    tm, spt = _tile_rows(n, seq_rows, PROJ_ROWS)
    if carry:
        n_hist = 1
        hist_map = lambda i: (0, 0, 0)
        xp_rows = HIST_ROWS + tm
    else:
        assert seq_rows == CHUNK
        n_hist = tm // CHUNK
        hist_map = lambda i: (i, 0, 0)
        xp_rows = n_hist * (HIST_ROWS + CHUNK)
    widths = [CONV_DIM, GDN_HEADS * GDN_DV, SWA_HEADS * SWA_HD, KV_COLS, 2 * D_MODEL, LANES]
    return pl.pallas_call(
        functools.partial(_proj_kernel, spt=spt, carry=carry),
        grid=(n // tm,),
        in_specs=[
            pl.BlockSpec((tm, D_MODEL), lambda i: (i, 0)),
            _mod_spec(mod.shape[1]),
            _const_spec((1, D_MODEL)),
            _const_spec((D_MODEL, PROJ_COLS)),
            pl.BlockSpec((n_hist, HIST_ROWS, CONV_DIM), hist_map),
            _const_spec((CONV_W, CONV_DIM)),
            _const_spec((SUBLANES, LANES)),
        ],
        out_specs=[pl.BlockSpec((tm, w), lambda i: (i, 0)) for w in widths]
        + [pl.BlockSpec((n_hist, HIST_ROWS, CONV_DIM), hist_map)],
        out_shape=[jax.ShapeDtypeStruct((n, w), BF16 if i in (1, 2, 4) else F32) for i, w in enumerate(widths)]
        + [jax.ShapeDtypeStruct((n_seq, HIST_ROWS, CONV_DIM), F32)],
        scratch_shapes=[pltpu.VMEM((tm, D_MODEL), BF16), pltpu.VMEM((xp_rows, CONV_DIM), F32)],
        compiler_params=_cparams(1),
        name="in_proj",
    )(x, mod, g, w_packed, hist8, conv_w, par)


def _unit_lower_inverses(ls, eye, level_masks):
    ts = [eye - jnp.where(level_masks[0], l, 0.0) for l in ls]
    ls16 = [l.astype(BF16) for l in ls]
    for mask in level_masks[1:]:
        lk = [jnp.where(mask, l, jnp.zeros_like(l)) for l in ls16]
        tp = [t.astype(BF16) for t in ts]
        m1 = [_dot(a, b) for a, b in zip(lk, tp)]
        m2 = [_dot(a, b.astype(BF16)) for a, b in zip(tp, m1)]
        ts = [t - m for t, m in zip(ts, m2)]
    return ts


def _gdn_kernel(qkv_ref, gate_ref, s0_ref, o_ref, sout_ref, s_ref, *, n_chunks, carry):
    pid = pl.program_id(0)

    if carry:
        def _load_state():
            s_ref[0] = s0_ref[0]

        pl.when(pid == 0)(_load_state)
    else:
        for c in range(n_chunks):
            s_ref[c] = s0_ref[c]

    ri = lax.broadcasted_iota(jnp.int32, (CHUNK, CHUNK), 0)
    ci = lax.broadcasted_iota(jnp.int32, (CHUNK, CHUNK), 1)
    causal = ri >= ci
    strict = ri > ci
    eye = (ri == ci).astype(F32)
    level_masks = []
    for lvl in range(int(math.log2(CHUNK))):
        same_block = (ri >> (lvl + 1)) == (ci >> (lvl + 1))
        level_masks.append(same_block & (((ri >> lvl) & 1) == 1) & (((ci >> lvl) & 1) == 0))

    items = [(c, h) for c in range(n_chunks) for h in range(GDN_HEADS)]
    gate = [gate_ref[c * CHUNK:(c + 1) * CHUNK, :] for c in range(n_chunks)]
    egc_all = [jnp.exp(g) for g in gate]
    g_last_all = [g[CHUNK - 1:CHUNK, :] for g in gate]
    kdf_all = [jnp.exp(gl - g) for gl, g in zip(g_last_all, gate)]
    eg_last_all = [jnp.exp(gl) for gl in g_last_all]

    def tile(c, t):
        return qkv_ref[c * CHUNK:(c + 1) * CHUNK, t * LANES:(t + 1) * LANES]

    def col(arrs, c, h):
        return arrs[c][:, GDN_HEADS + h:GDN_HEADS + h + 1]

    q = [tile(c, h) for c, h in items]
    k = [tile(c, GDN_HEADS + h) for c, h in items]
    v = [tile(c, 2 * GDN_HEADS + h) for c, h in items]
    beta = [gate[c][:, h:h + 1] for c, h in items]
    gcol = [col(gate, c, h) for c, h in items]
    gate_t = [g.T for g in gate]
    grow = [gate_t[c][GDN_HEADS + h:GDN_HEADS + h + 1, :] for c, h in items]
    decay = [jnp.where(causal, jnp.exp(jnp.where(causal, gc - gr, 0.0)), 0.0) for gc, gr in zip(gcol, grow)]
    kb = [x * b for x, b in zip(k, beta)]
    k16 = [x.astype(BF16) for x in k]
    kq = [_dot_nt(jnp.concatenate([a.astype(BF16), b.astype(BF16)], axis=0), c) for a, b, c in zip(kb, q, k16)]
    kk = [x[0:CHUNK, :] for x in kq]
    qk = [x[CHUNK:, :] for x in kq]
    ls = [jnp.where(strict, a * d, 0.0) for a, d in zip(kk, decay)]
    intra16 = [jnp.where(causal, a * d, 0.0).astype(BF16) for a, d in zip(qk, decay)]
    ts = _unit_lower_inverses(ls, eye, level_masks)
    ys = [(t - eye).astype(BF16) for t in ts]
    egc = [col(egc_all, c, h) for c, h in items]
    vb = [x * b for x, b in zip(v, beta)]
    kbe = [x * e for x, e in zip(kb, egc)]
    rhs = [jnp.concatenate([a, b], axis=1) for a, b in zip(vb, kbe)]
    sol = [x + _dot(y, x.astype(BF16)) for x, y in zip(rhs, ys)]
    u = [x[:, 0:GDN_DV] for x in sol]
    w16 = [x[:, GDN_DV:].astype(BF16) for x in sol]
    qg16 = [(x * e).astype(BF16) for x, e in zip(q, egc)]
    kd16 = [(x * col(kdf_all, c, h)).astype(BF16) for x, (c, h) in zip(k, items)]
    eg_last = [col(eg_last_all, c, h) for c, h in items]
    wq16 = [jnp.concatenate([a, b], axis=0) for a, b in zip(w16, qg16)]

    o = [None] * len(items)
    if carry:
        state = [s_ref[0, h] for h in range(GDN_HEADS)]
    for c in range(n_chunks):
        idx = [c * GDN_HEADS + h for h in range(GDN_HEADS)]
        if not carry:
            state = [s_ref[c, h] for h in range(GDN_HEADS)]
        s16 = [s.astype(BF16) for s in state]
        ws_qs = [_dot(wq16[i], s) for i, s in zip(idx, s16)]
        v_new = [u[i] - x[0:CHUNK, :] for i, x in zip(idx, ws_qs)]
        vn16 = [x.astype(BF16) for x in v_new]
        for i, x, vn in zip(idx, ws_qs, vn16):
            o[i] = x[CHUNK:, :] + _dot(intra16[i], vn)
        state = [s * eg_last[i] + _dot_tn(kd16[i], vn) for i, s, vn in zip(idx, state, vn16)]
        if not carry:
            for h in range(GDN_HEADS):
                sout_ref[c, h] = state[h]

    for (c, h), oo in zip(items, o):
        o_ref[c * CHUNK:(c + 1) * CHUNK, h * GDN_DV:(h + 1) * GDN_DV] = oo.astype(o_ref.dtype)

    if carry:
        for h in range(GDN_HEADS):
            s_ref[0, h] = state[h]

        def _store_state():
            sout_ref[0] = s_ref[0]

        pl.when(pid == pl.num_programs(0) - 1)(_store_state)


def _gdn(qkv, gate, s0, seq_rows):
    n = qkv.shape[0]
    n_seq = n // seq_rows
    carry = n_seq == 1
    n_chunks = GDN_CHUNKS_PER_STEP
    rows = n_chunks * CHUNK
    assert n % rows == 0
    if carry:
        n_state = 1
        s_map = lambda i: (0, 0, 0, 0)
    else:
        assert seq_rows == CHUNK
        n_state = n_chunks
        s_map = lambda i: (i, 0, 0, 0)
    kern = functools.partial(_gdn_kernel, n_chunks=n_chunks, carry=carry)
    return pl.pallas_call(
        kern,
        grid=(n // rows,),
        in_specs=[
            pl.BlockSpec((rows, CONV_DIM), lambda i: (i, 0)),
            pl.BlockSpec((rows, LANES), lambda i: (i, 0)),
            pl.BlockSpec((n_state, GDN_HEADS, GDN_DK, GDN_DV), s_map),
        ],
        out_specs=[
            pl.BlockSpec((rows, GDN_HEADS * GDN_DV), lambda i: (i, 0)),
            pl.BlockSpec((n_state, GDN_HEADS, GDN_DK, GDN_DV), s_map),
        ],
        out_shape=[
            jax.ShapeDtypeStruct((n, GDN_HEADS * GDN_DV), BF16),
            jax.ShapeDtypeStruct((n_seq, GDN_HEADS, GDN_DK, GDN_DV), F32),
        ],
        scratch_shapes=[pltpu.VMEM((n_state, GDN_HEADS, GDN_DK, GDN_DV), F32)],
        compiler_params=_cparams(1),
        name="gdn",
    )(qkv, gate, s0)


SWA_KEYS = WINDOW + CHUNK
SWA_KEYS_PAD = 2 * LANES
KV_COLS = 2 * SWA_KV_HEADS * SWA_HD


def _bias_kernel(bucket_ref, rb_ref, o_ref):
    bucket = bucket_ref[...]
    h = pl.program_id(0)
    acc = jnp.zeros((CHUNK, SWA_KEYS), F32)
    for b in range(NUM_BUCKETS):
        acc = jnp.where(bucket == b, rb_ref[b * SWA_HEADS + h], acc)
    o_ref[0] = acc


def _bias_table(bucket, rel_bias_flat):
    return pl.pallas_call(
        _bias_kernel,
        grid=(SWA_HEADS,),
        in_specs=[
            pl.BlockSpec((CHUNK, SWA_KEYS), lambda h: (0, 0)),
            pl.BlockSpec(memory_space=pltpu.SMEM),
        ],
        out_specs=pl.BlockSpec((1, CHUNK, SWA_KEYS), lambda h: (h, 0, 0)),
        out_shape=jax.ShapeDtypeStruct((SWA_HEADS, CHUNK, SWA_KEYS), F32),
        compiler_params=_cparams(1),
        name="swa_bias",
    )(bucket, rel_bias_flat)


def _swa_kernel(q_ref, kvc_ref, kvp_ref, bias_ref, sink_ref, o_ref, kf_ref, *, n_chunks, one_sequence):
    n_pad = SWA_KEYS_PAD - SWA_KEYS
    kf_ref[kf_ref.shape[0] - n_pad:, :] = jnp.zeros((n_pad, KV_COLS), F32)
    if one_sequence:
        kf_ref[0:WINDOW, :] = kvp_ref[...]
        kf_ref[WINDOW:WINDOW + n_chunks * CHUNK, :] = kvc_ref[...]
        key_start = [c * CHUNK for c in range(n_chunks)]
        key_pos = lax.broadcasted_iota(jnp.int32, (CHUNK, SWA_KEYS), 1)
        tile_start = pl.program_id(0) * (n_chunks * CHUNK)
        valid = [key_pos + (tile_start + c * CHUNK - WINDOW) >= 0 for c in range(n_chunks)]
    else:
        for c in range(n_chunks):
            kf_ref[c * SWA_KEYS:c * SWA_KEYS + WINDOW, :] = kvp_ref[c * WINDOW:(c + 1) * WINDOW, :]
            kf_ref[c * SWA_KEYS + WINDOW:(c + 1) * SWA_KEYS, :] = kvc_ref[c * CHUNK:(c + 1) * CHUNK, :]
        key_start = [c * SWA_KEYS for c in range(n_chunks)]

    pairs = [(c, kh) for c in range(n_chunks) for kh in range(SWA_KV_HEADS)]
    half = SWA_KV_HEADS * SWA_HD
    keys_t = [kf_ref[key_start[c]:key_start[c] + SWA_KEYS_PAD, 0:half].T.astype(BF16) for c in range(n_chunks)]
    keys = {(c, kh): keys_t[c][kh * SWA_HD:(kh + 1) * SWA_HD, :] for c, kh in pairs}
    ones = jnp.ones((SWA_KEYS, LANES - SWA_HD), BF16)
    vals = {(c, kh): jnp.concatenate(
        [kf_ref[key_start[c]:key_start[c] + SWA_KEYS,
                (SWA_KV_HEADS + kh) * SWA_HD:(SWA_KV_HEADS + kh + 1) * SWA_HD].astype(BF16), ones], axis=1)
            for c, kh in pairs}
    items = [(c, hd) for c in range(n_chunks) for hd in range(SWA_HEADS)]
    q = [q_ref[c * CHUNK:(c + 1) * CHUNK, hd * SWA_HD:(hd + 1) * SWA_HD] for c, hd in items]
    def group_rows(c, kh):
        return range(c * SWA_HEADS + kh * SWA_GROUP, c * SWA_HEADS + (kh + 1) * SWA_GROUP)

    qk = {(c, kh): _dot(jnp.concatenate([q[i] for i in group_rows(c, kh)], axis=0), keys[(c, kh)])
          for c, kh in pairs}
    logits = [qk[(c, hd // SWA_GROUP)][(hd % SWA_GROUP) * CHUNK:(hd % SWA_GROUP + 1) * CHUNK, 0:SWA_KEYS] + bias_ref[hd]
              for c, hd in items]
    if one_sequence:
        logits = [jnp.where(valid[c], x, -jnp.inf) if c < WINDOW // CHUNK else x for x, (c, _) in zip(logits, items)]
    sink = [sink_ref[hd] for _, hd in items]
    m = [jnp.maximum(jnp.max(x, axis=-1, keepdims=True), s) for x, s in zip(logits, sink)]
    p16 = [jnp.exp(x - mm).astype(BF16) for x, mm in zip(logits, m)]
    pv = {(c, kh): _dot(jnp.concatenate([p16[i] for i in group_rows(c, kh)], axis=0), vals[(c, kh)])
          for c, kh in pairs}
    pv = [pv[(c, hd // SWA_GROUP)][(hd % SWA_GROUP) * CHUNK:(hd % SWA_GROUP + 1) * CHUNK, :] for c, hd in items]
    denom = [x[:, SWA_HD:SWA_HD + 1] + jnp.exp(s - mm) for x, s, mm in zip(pv, sink, m)]
    out = [x[:, 0:SWA_HD] / d for x, d in zip(pv, denom)]
    for x, (c, hd) in zip(out, items):
        o_ref[c * CHUNK:(c + 1) * CHUNK, hd * SWA_HD:(hd + 1) * SWA_HD] = x.astype(o_ref.dtype)


def _swa(qb, kv, kv_hist, bias, sinks, seq_rows):
    n = qb.shape[0]
    n_chunks = SWA_CHUNKS_PER_STEP
    rows = n_chunks * CHUNK
    assert n % rows == 0
    one_sequence = kv_hist is None
    if one_sequence:
        assert rows % WINDOW == 0
        per = rows // WINDOW
        prev_arr = kv
        prev_spec = pl.BlockSpec((WINDOW, KV_COLS), lambda i: (jnp.maximum(i * per - 1, 0), 0))
        kf_rows = WINDOW + rows
    else:
        assert seq_rows == CHUNK
        prev_arr = kv_hist
        prev_spec = pl.BlockSpec((n_chunks * WINDOW, KV_COLS), lambda i: (i, 0))
        kf_rows = n_chunks * SWA_KEYS
    kern = functools.partial(_swa_kernel, n_chunks=n_chunks, one_sequence=one_sequence)
    return pl.pallas_call(
        kern,
        grid=(n // rows,),
        in_specs=[
            pl.BlockSpec((rows, SWA_HEADS * SWA_HD), lambda i: (i, 0)),
            pl.BlockSpec((rows, KV_COLS), lambda i: (i, 0)),
            prev_spec,
            pl.BlockSpec((SWA_HEADS, CHUNK, SWA_KEYS), lambda i: (0, 0, 0)),
            pl.BlockSpec(memory_space=pltpu.SMEM),
        ],
        out_specs=pl.BlockSpec((rows, SWA_HEADS * SWA_HD), lambda i: (i, 0)),
        out_shape=jax.ShapeDtypeStruct((n, SWA_HEADS * SWA_HD), BF16),
        scratch_shapes=[pltpu.VMEM((kf_rows + SWA_KEYS_PAD - SWA_KEYS, KV_COLS), F32)],
        compiler_params=_cparams(1),
        name="swa",
    )(qb, kv, prev_arr, bias, sinks)


def _merge_kernel(x_ref, oa_ref, z_ref, ob_ref, gates_ref, mod_ref, nw_ref, wa_ref, wb_ref, wo_ref, o_ref, oa16_ref,
                  *, spt):
    for h in range(GDN_HEADS):
        cols = slice(h * GDN_DV, (h + 1) * GDN_DV)
        oo = oa_ref[:, cols].astype(F32)
        zz = z_ref[:, cols].astype(F32)
        on = oo * lax.rsqrt(jnp.mean(oo * oo, axis=-1, keepdims=True) + EPS) * nw_ref[...] * (zz * jax.nn.sigmoid(zz))
        oa16_ref[:, cols] = on.astype(BF16)
    ya = _dot(oa16_ref[...], wa_ref[...])
    yb = _dot(ob_ref[...], wb_ref[...])
    merged = (jax.nn.sigmoid(gates_ref[:, 0:D_MODEL].astype(F32)) * ya
              + jax.nn.sigmoid(gates_ref[:, D_MODEL:2 * D_MODEL].astype(F32)) * yb)
    y = _dot(merged.astype(BF16), wo_ref[...])
    rows = x_ref.shape[0] // spt
    for j in range(spt):
        sl = slice(j * rows, (j + 1) * rows)
        o_ref[sl, :] = x_ref[sl, :] + _mod_row(mod_ref, 5, spt, j) * y[sl, :]


def _merge(x, oa, z, ob, gates, mod, seq_rows, norm_w, wa, wb, wo):
    n = x.shape[0]
    tm, spt = _tile_rows(n, seq_rows, MERGE_ROWS)
    row_spec = lambda w: pl.BlockSpec((tm, w), lambda i: (i, 0))
    return pl.pallas_call(
        functools.partial(_merge_kernel, spt=spt),
        grid=(n // tm,),
        in_specs=[
            row_spec(D_MODEL), row_spec(D_MODEL), row_spec(D_MODEL), row_spec(D_MODEL), row_spec(2 * D_MODEL),
            _mod_spec(mod.shape[1]),
            _const_spec((1, GDN_DV)),
            _const_spec((D_MODEL, D_MODEL)), _const_spec((D_MODEL, D_MODEL)), _const_spec((D_MODEL, D_MODEL)),
        ],
        out_specs=row_spec(D_MODEL),
        out_shape=jax.ShapeDtypeStruct((n, D_MODEL), F32),
        scratch_shapes=[pltpu.VMEM((tm, GDN_HEADS * GDN_DV), BF16)],
        compiler_params=_cparams(1),
        name="merge",
    )(x, oa, z, ob, gates, mod, norm_w, wa, wb, wo)


def _t5_bucket(rel):
    half = NUM_BUCKETS // 2
    max_exact = half // 2
    n = jnp.abs(rel)
    large = max_exact + (jnp.log(jnp.maximum(n, 1).astype(jnp.float32) / max_exact)
                         / math.log(MAX_DISTANCE / max_exact) * (half - max_exact)).astype(jnp.int32)
    large = jnp.minimum(large, half - 1)
    return jnp.where(rel > 0, half, 0) + jnp.where(n < max_exact, n, large)


def _trunk(x, mod, conv_hist, s_hist, k_hist, v_hist, w):
    bsz, seq, _ = x.shape
    n = bsz * seq
    x2 = x.reshape(n, D_MODEL)
    x1 = _ffn(x2, mod, seq, w["norm_ffn1"], w["ffn1_in"], w["ffn1_out"], w["norm_final"], sub=0, final=False)
    if conv_hist is None:
        hist8 = jnp.zeros((bsz, HIST_ROWS, CONV_DIM), F32)
        s0 = jnp.zeros((bsz, GDN_HEADS, GDN_DK, GDN_DV), F32)
        kv_hist = None
    else:
        hist8 = jnp.concatenate([jnp.zeros((bsz, HIST_ROWS - (CONV_W - 1), CONV_DIM), F32), conv_hist], axis=1)
        s0 = s_hist
        kv_hist = jnp.concatenate([k_hist, v_hist], axis=2).reshape(bsz * WINDOW, KV_COLS)
    qkv, z, qb, kv, gates, gate, tail = _proj(x1, mod, seq, w["norm_mix"], w["w_in"], hist8, w["conv_w"], w["gdn_par"])
    oa, s_new = _gdn(qkv, gate, s0, seq)
    ob = _swa(qb, kv, kv_hist, w["bias"], w["sinks"], seq)
    x3 = _merge(x1, oa, z, ob, gates, mod, seq, w["gdn_norm_w"], w["w_a"], w["w_b"], w["w_out"])
    y = _ffn(x3, mod, seq, w["norm_ffn2"], w["ffn2_in"], w["ffn2_out"], w["norm_final"], sub=2, final=True)

    conv_new = tail[:, HIST_ROWS - (CONV_W - 1):]
    half = SWA_KV_HEADS * SWA_HD
    kv3 = kv.reshape(bsz, seq, KV_COLS)
    if kv_hist is None:
        k_new = kv3[:, seq - WINDOW:, :half]
        v_new = kv3[:, seq - WINDOW:, half:]
    else:
        keep = WINDOW - seq
        k_new = jnp.concatenate([k_hist.reshape(bsz, WINDOW, half)[:, WINDOW - keep:], kv3[:, :, :half]], axis=1)
        v_new = jnp.concatenate([v_hist.reshape(bsz, WINDOW, half)[:, WINDOW - keep:], kv3[:, :, half:]], axis=1)
    k_new = k_new.reshape(bsz, WINDOW, SWA_KV_HEADS, SWA_HD)
    v_new = v_new.reshape(bsz, WINDOW, SWA_KV_HEADS, SWA_HD)
    return (y.reshape(bsz, seq, D_MODEL), conv_new[None], s_new[None], k_new[None], v_new[None])


def kernel(x_prompt, x_sample, state_gdn_conv, state_gdn_s, cache_swa_k, cache_swa_v, c_prompt, c_sample,
           norm_ffn1, w_ffn1_in, w_ffn1_out, norm_mix, w_in, gdn_conv_w, gdn_a_log, gdn_dt_bias, gdn_norm_w,
           swa_sinks, rel_bias, w_branch_a, w_branch_b, w_out, norm_ffn2, w_ffn2_in, w_ffn2_out,
           w_ada, b_ada, norm_final):
    bp = x_prompt.shape[0]
    bs = x_sample.shape[0]
    assert bp == 1 and x_sample.shape[1] == CHUNK and cache_swa_k.shape[2] == WINDOW

    n_c = bp + bs
    pad = -n_c % SUBLANES
    c_all = jnp.concatenate([c_prompt, c_sample, jnp.zeros((pad, D_MODEL), F32)], axis=0)
    mod_all = _modulation(c_all, w_ada[0], b_ada[0][None, :])
    mod_p = mod_all[:, :bp]
    mod_s = mod_all[:, bp:n_c]

    w_packed = _pack_w_in(w_in)
    par = jnp.zeros((SUBLANES, LANES), F32)
    par = par.at[0, GDN_HEADS:2 * GDN_HEADS].set(gdn_a_log[0]).at[1, GDN_HEADS:2 * GDN_HEADS].set(gdn_dt_bias[0])

    rel = jnp.arange(SWA_KEYS)[None, :] - WINDOW - jnp.arange(CHUNK)[:, None]
    bias = _bias_table(_t5_bucket(rel).astype(jnp.int32), rel_bias.reshape(-1))

    w = {
        "norm_ffn1": norm_ffn1, "ffn1_in": w_ffn1_in[0].astype(BF16), "ffn1_out": w_ffn1_out[0].astype(BF16),
        "norm_mix": norm_mix, "w_in": w_packed, "conv_w": gdn_conv_w[0], "gdn_par": par,
        "gdn_norm_w": gdn_norm_w, "bias": bias, "sinks": swa_sinks[0],
        "w_a": w_branch_a[0].astype(BF16), "w_b": w_branch_b[0].astype(BF16), "w_out": w_out[0].astype(BF16),
        "norm_ffn2": norm_ffn2, "ffn2_in": w_ffn2_in[0].astype(BF16), "ffn2_out": w_ffn2_out[0].astype(BF16),
        "norm_final": norm_final[None, :],
    }
    y_p, p_conv, p_s, p_k, p_v = _trunk(x_prompt, mod_p, None, None, None, None, w)
    y_s, s_conv, s_s, s_k, s_v = _trunk(x_sample, mod_s, state_gdn_conv[0], state_gdn_s[0],
                                        cache_swa_k[0], cache_swa_v[0], w)
    return (y_p, y_s, p_conv, p_s, p_k, p_v, s_conv, s_s, s_k, s_v)
```

```python
import functools
import math

import jax
import jax.numpy as jnp
from jax import lax
from jax.experimental import pallas as pl
from jax.experimental.pallas import tpu as pltpu

F32 = jnp.float32
BF16 = jnp.bfloat16

D_MODEL = 1024
CHUNK = 64
GDN_HEADS = 8
GDN_DK = 128
GDN_DV = 128
CONV_W = 4
CONV_DIM = GDN_HEADS * (2 * GDN_DK + GDN_DV)
SWA_HEADS = 16
SWA_KV_HEADS = 2
SWA_GROUP = SWA_HEADS // SWA_KV_HEADS
SWA_HD = 64
WINDOW = 128
NUM_BUCKETS = 32
MAX_DISTANCE = 128
D_FF = 2816
N_MOD = 9
EPS = 1e-6

LANES = 128
SUBLANES = 8
VMEM_LIMIT = 56 * 1024 * 1024

MXU_COLS = 256
FF_CHUNK = MXU_COLS
assert D_FF % FF_CHUNK == 0
FFN_ROWS = 1024
HIST_ROWS = SUBLANES
MERGE_ROWS = 512
PACK_ROWS = 256
GDN_CHUNKS_PER_STEP = 4
SWA_CHUNKS_PER_STEP = 8


def _dot(a, b):
    return jnp.dot(a, b, preferred_element_type=F32)


def _dot_nt(a, b):
    return lax.dot_general(a, b, (((1,), (1,)), ((), ())), preferred_element_type=F32)


def _dot_tn(a, b):
    return lax.dot_general(a, b, (((0,), (0,)), ((), ())), preferred_element_type=F32)


def _cparams(n_grid):
    return pltpu.CompilerParams(dimension_semantics=("arbitrary",) * n_grid, vmem_limit_bytes=VMEM_LIMIT)


def _const_spec(shape):
    nd = len(shape)
    return pl.BlockSpec(shape, lambda i: (0,) * nd, pipeline_mode=pl.Buffered(1))


def _mod_kernel(c_ref, w_ref, b_ref, o_ref):
    c = c_ref[...]
    a = (c * jax.nn.sigmoid(c)).astype(BF16)
    o_ref[0] = _dot(a, w_ref[...].astype(BF16)) + b_ref[...]


def _modulation(c_pad, w_ada, b_ada):
    rows = c_pad.shape[0]
    return pl.pallas_call(
        _mod_kernel,
        grid=(N_MOD,),
        in_specs=[
            pl.BlockSpec((rows, D_MODEL), lambda j: (0, 0)),
            pl.BlockSpec((D_MODEL, D_MODEL), lambda j: (0, j)),
            pl.BlockSpec((1, D_MODEL), lambda j: (0, j)),
        ],
        out_specs=pl.BlockSpec((1, rows, D_MODEL), lambda j: (j, 0, 0)),
        out_shape=jax.ShapeDtypeStruct((N_MOD, rows, D_MODEL), F32),
        compiler_params=_cparams(1),
        name="adaln_mod",
    )(c_pad, w_ada, b_ada)


def _mod_row(mod_ref, idx, spt, j):
    if mod_ref.shape[1] == 1:
        return mod_ref[idx, 0:1, :]
    return mod_ref[idx, pl.ds(pl.program_id(0) * spt + j, 1), :]


def _norm_mod_store(h_ref, x_ref, g_ref, mod_ref, sub, spt):
    rows = x_ref.shape[0] // spt
    g = g_ref[...]
    for j in range(spt):
        xs = x_ref[j * rows:(j + 1) * rows, :]
        ms = jnp.mean(xs * xs, axis=-1, keepdims=True)
        y = xs * lax.rsqrt(ms + EPS) * g
        sh = _mod_row(mod_ref, 3 * sub, spt, j)
        sc = _mod_row(mod_ref, 3 * sub + 1, spt, j)
        h_ref[j * rows:(j + 1) * rows, :] = (y * (1.0 + sc) + sh).astype(h_ref.dtype)


def _tile_rows(n_rows, seq_rows, target):
    if seq_rows >= target:
        assert seq_rows % target == 0
        return target, 1
    assert target % seq_rows == 0
    tm = min(target, n_rows)
    assert n_rows % tm == 0
    return tm, tm // seq_rows


def _mod_spec(n_seq_total):
    return pl.BlockSpec((N_MOD, n_seq_total, D_MODEL), lambda i: (0, 0, 0))


def _ffn_kernel(x_ref, mod_ref, g_ref, w1_ref, w2_ref, gf_ref, o_ref, h_ref, *, sub, final, spt):
    _norm_mod_store(h_ref, x_ref, g_ref, mod_ref, sub, spt)
    h = h_ref[...]
    acc = None
    for c in range(D_FF // FF_CHUNK):
        c0 = c * FF_CHUNK
        gate = _dot(h, w1_ref[:, c0:c0 + FF_CHUNK])
        up = _dot(h, w1_ref[:, D_FF + c0:D_FF + c0 + FF_CHUNK])
        a = (gate * jax.nn.sigmoid(gate) * up).astype(BF16)
        part = _dot(a, w2_ref[c0:c0 + FF_CHUNK, :])
        acc = part if acc is None else acc + part
    rows = x_ref.shape[0] // spt
    for j in range(spt):
        sl = slice(j * rows, (j + 1) * rows)
        ga = _mod_row(mod_ref, 3 * sub + 2, spt, j)
        xn = x_ref[sl, :] + 0.5 * ga * acc[sl, :]
        if final:
            ms = jnp.mean(xn * xn, axis=-1, keepdims=True)
            xn = xn * lax.rsqrt(ms + EPS) * gf_ref[...]
        o_ref[sl, :] = xn


def _ffn(x, mod, seq_rows, g, w1, w2, gf, *, sub, final):
    n = x.shape[0]
    tm, spt = _tile_rows(n, seq_rows, FFN_ROWS)
    kern = functools.partial(_ffn_kernel, sub=sub, final=final, spt=spt)
    return pl.pallas_call(
        kern,
        grid=(n // tm,),
        in_specs=[
            pl.BlockSpec((tm, D_MODEL), lambda i: (i, 0)),
            _mod_spec(mod.shape[1]),
            _const_spec((1, D_MODEL)),
            _const_spec((D_MODEL, 2 * D_FF)),
            _const_spec((D_FF, D_MODEL)),
            _const_spec((1, D_MODEL)),
        ],
        out_specs=pl.BlockSpec((tm, D_MODEL), lambda i: (i, 0)),
        out_shape=jax.ShapeDtypeStruct((n, D_MODEL), F32),
        scratch_shapes=[pltpu.VMEM((tm, D_MODEL), BF16)],
        compiler_params=_cparams(1),
        name="ffn_final" if final else "ffn",
    )(x, mod, g, w1, w2, gf)


PROJ_GROUPS = (("qkv", CONV_DIM), ("z", GDN_HEADS * GDN_DV), ("qb", SWA_HEADS * SWA_HD),
               ("kv", 2 * SWA_KV_HEADS * SWA_HD), ("gates", 2 * D_MODEL), ("ba", LANES))
PROJ_COLS = sum(w for _, w in PROJ_GROUPS)


assert math.log2(SWA_HD) % 2 == 0
W_IN_ALIGNED = CONV_DIM + GDN_HEADS * GDN_DV
W_IN_SMALL = 2 * GDN_HEADS
W_IN_REST = PROJ_COLS - W_IN_ALIGNED - LANES


def _pack_w_in_kernel(w_ref, o_ref):
    o_ref[:, 0:W_IN_ALIGNED] = w_ref[:, 0:W_IN_ALIGNED].astype(BF16)
    hi = W_IN_ALIGNED + W_IN_SMALL
    n_qb = SWA_HEADS * SWA_HD
    o_ref[:, W_IN_ALIGNED:W_IN_ALIGNED + n_qb] = (w_ref[:, hi:hi + n_qb] * (SWA_HD ** -0.5)).astype(BF16)
    o_ref[:, W_IN_ALIGNED + n_qb:W_IN_ALIGNED + W_IN_REST] = w_ref[:, hi + n_qb:hi + W_IN_REST].astype(BF16)
    lane = lax.broadcasted_iota(jnp.int32, (w_ref.shape[0], LANES), 1)
    small = jnp.where(lane < W_IN_SMALL, w_ref[:, W_IN_ALIGNED:W_IN_ALIGNED + LANES], 0.0)
    o_ref[:, W_IN_ALIGNED + W_IN_REST:PROJ_COLS] = small.astype(BF16)


def _pack_w_in(w_in):
    n_cols = W_IN_ALIGNED + W_IN_SMALL + W_IN_REST
    assert w_in.shape == (1, D_MODEL, n_cols)
    rows = PACK_ROWS
    return pl.pallas_call(
        _pack_w_in_kernel,
        grid=(D_MODEL // rows,),
        in_specs=[pl.BlockSpec((None, rows, n_cols), lambda i: (0, i, 0))],
        out_specs=pl.BlockSpec((rows, PROJ_COLS), lambda i: (i, 0)),
        out_shape=jax.ShapeDtypeStruct((D_MODEL, PROJ_COLS), BF16),
        compiler_params=_cparams(1),
        name="pack_w_in",
    )(w_in)


PROJ_OFFSETS = {}
_off = 0
for _name, _width in PROJ_GROUPS:
    PROJ_OFFSETS[_name] = (_off, _width)
    _off += _width
PROJ_ROWS = 512


def _proj_kernel(x_ref, mod_ref, g_ref, w_ref, hist_ref, cw_ref, par_ref,
                 qkv_ref, z_ref, qb_ref, kv_ref, gates_ref, gate_ref, tail_ref,
                 h_ref, xp_ref, *, spt, carry):
    pid = pl.program_id(0)
    tm = x_ref.shape[0]
    n_chunks = tm // CHUNK
    seg = CHUNK if carry else HIST_ROWS + CHUNK

    _norm_mod_store(h_ref, x_ref, g_ref, mod_ref, 1, spt)
    h = h_ref[...]

    def group(name, lo=0, width=None):
        c0, full = PROJ_OFFSETS[name]
        width = full if width is None else width
        return _dot(h, w_ref[:, c0 + lo:c0 + lo + width])

    if carry:
        def _load_hist():
            xp_ref[0:HIST_ROWS, :] = hist_ref[0]

        pl.when(pid == 0)(_load_hist)
    else:
        for c in range(n_chunks):
            xp_ref[c * seg:c * seg + HIST_ROWS, :] = hist_ref[c]

    part = GDN_HEADS * GDN_DK

    def project_part(p):
        raw = group("qkv", p * part, part)
        cols = slice(p * part, (p + 1) * part)
        if carry:
            xp_ref[HIST_ROWS:HIST_ROWS + tm, cols] = raw
        else:
            for c in range(n_chunks):
                xp_ref[c * seg + HIST_ROWS:(c + 1) * seg, cols] = raw[c * CHUNK:(c + 1) * CHUNK, :]

    def conv_part(p):
        for c in range(n_chunks):
            for t in range(p * GDN_HEADS, (p + 1) * GDN_HEADS):
                c0 = t * LANES
                win = xp_ref[c * seg:c * seg + HIST_ROWS + CHUNK, c0:c0 + LANES]
                acc = cw_ref[CONV_W - 1:CONV_W, c0:c0 + LANES] * win[HIST_ROWS:, :]
                for s in range(1, CONV_W):
                    tap = pltpu.roll(win, s, 0)[HIST_ROWS:, :]
                    acc = acc + cw_ref[CONV_W - 1 - s:CONV_W - s, c0:c0 + LANES] * tap
                y = acc * jax.nn.sigmoid(acc)
                if p < 2:
                    y = y * lax.rsqrt(jnp.sum(y * y, axis=-1, keepdims=True) + EPS)
                    if p == 0:
                        y = y * (GDN_DK ** -0.5)
                qkv_ref[c * CHUNK:(c + 1) * CHUNK, c0:c0 + LANES] = y

    def gate_math():
        ba = group("ba")
        ri = lax.broadcasted_iota(jnp.int32, (CHUNK, CHUNK), 0)
        ci = lax.broadcasted_iota(jnp.int32, (CHUNK, CHUNK), 1)
        tri = (ri >= ci).astype(BF16)
        lane = lax.broadcasted_iota(jnp.int32, (CHUNK, LANES), 1)
        a_coef = -jnp.exp(par_ref[0:1, :])
        dt_bias = par_ref[1:2, :]
        for c in range(n_chunks):
            bac = ba[c * CHUNK:(c + 1) * CHUNK, :]
            g_all = a_coef * jax.nn.softplus(bac + dt_bias)
            g_hi = g_all.astype(BF16)
            g_r1 = g_all - g_hi.astype(F32)
            g_mid = g_r1.astype(BF16)
            g_lo = (g_r1 - g_mid.astype(F32)).astype(BF16)
            gc = _dot(tri, g_hi) + _dot(tri, g_mid) + _dot(tri, g_lo)
            gate_ref[c * CHUNK:(c + 1) * CHUNK, :] = jnp.where(lane < GDN_HEADS, jax.nn.sigmoid(bac), gc)

    project_part(0)
    project_part(1)
    conv_part(0)
    project_part(2)
    conv_part(1)
    z_ref[...] = group("z").astype(z_ref.dtype)
    gate_math()
    qb_ref[...] = group("qb").astype(qb_ref.dtype)
    conv_part(2)
    kv_ref[...] = group("kv")
    gates_ref[...] = group("gates").astype(gates_ref.dtype)

    if carry:
        def _store_tail():
            tail_ref[0] = xp_ref[tm:tm + HIST_ROWS, :]

        pl.when(pid == pl.num_programs(0) - 1)(_store_tail)
        xp_ref[0:HIST_ROWS, :] = xp_ref[tm:tm + HIST_ROWS, :]
    else:
        for c in range(n_chunks):
            tail_ref[c] = xp_ref[c * seg + CHUNK:c * seg + CHUNK + HIST_ROWS, :]


def _proj(x, mod, seq_rows, g, w_packed, hist8, conv_w, par):
    n = x.shape[0]
    n_seq = n // seq_rows
    carry = n_seq == 1
    tm, spt = _tile_rows(n, seq_rows, PROJ_ROWS)
    if carry:
        n_hist = 1
        hist_map = lambda i: (0, 0, 0)
        xp_rows = HIST_ROWS + tm
    else:
        assert seq_rows == CHUNK
        n_hist = tm // CHUNK
        hist_map = lambda i: (i, 0, 0)
        xp_rows = n_hist * (HIST_ROWS + CHUNK)
    widths = [CONV_DIM, GDN_HEADS * GDN_DV, SWA_HEADS * SWA_HD, KV_COLS, 2 * D_MODEL, LANES]
    return pl.pallas_call(
        functools.partial(_proj_kernel, spt=spt, carry=carry),
        grid=(n // tm,),
        in_specs=[
            pl.BlockSpec((tm, D_MODEL), lambda i: (i, 0)),
            _mod_spec(mod.shape[1]),
            _const_spec((1, D_MODEL)),
            _const_spec((D_MODEL, PROJ_COLS)),
            pl.BlockSpec((n_hist, HIST_ROWS, CONV_DIM), hist_map),
            _const_spec((CONV_W, CONV_DIM)),
            _const_spec((SUBLANES, LANES)),
        ],
        out_specs=[pl.BlockSpec((tm, w), lambda i: (i, 0)) for w in widths]
        + [pl.BlockSpec((n_hist, HIST_ROWS, CONV_DIM), hist_map)],
        out_shape=[jax.ShapeDtypeStruct((n, w), BF16 if i in (1, 2, 4) else F32) for i, w in enumerate(widths)]
        + [jax.ShapeDtypeStruct((n_seq, HIST_ROWS, CONV_DIM), F32)],
        scratch_shapes=[pltpu.VMEM((tm, D_MODEL), BF16), pltpu.VMEM((xp_rows, CONV_DIM), F32)],
        compiler_params=_cparams(1),
        name="in_proj",
    )(x, mod, g, w_packed, hist8, conv_w, par)


def _unit_lower_inverses(ls, eye, level_masks):
    ts = [eye - jnp.where(level_masks[0], l, 0.0) for l in ls]
    ls16 = [l.astype(BF16) for l in ls]
    for mask in level_masks[1:]:
        lk = [jnp.where(mask, l, jnp.zeros_like(l)) for l in ls16]
        tp = [t.astype(BF16) for t in ts]
        m1 = [_dot(a, b) for a, b in zip(lk, tp)]
        m2 = [_dot(a, b.astype(BF16)) for a, b in zip(tp, m1)]
        ts = [t - m for t, m in zip(ts, m2)]
    return ts


def _gdn_kernel(qkv_ref, gate_ref, s0_ref, o_ref, sout_ref, s_ref, *, n_chunks, carry):
    pid = pl.program_id(0)

    if carry:
        def _load_state():
            s_ref[0] = s0_ref[0]

        pl.when(pid == 0)(_load_state)
    else:
        for c in range(n_chunks):
            s_ref[c] = s0_ref[c]

    ri = lax.broadcasted_iota(jnp.int32, (CHUNK, CHUNK), 0)
    ci = lax.broadcasted_iota(jnp.int32, (CHUNK, CHUNK), 1)
    causal = ri >= ci
    strict = ri > ci
    eye = (ri == ci).astype(F32)
    level_masks = []
    for lvl in range(int(math.log2(CHUNK))):
        same_block = (ri >> (lvl + 1)) == (ci >> (lvl + 1))
        level_masks.append(same_block & (((ri >> lvl) & 1) == 1) & (((ci >> lvl) & 1) == 0))

    items = [(c, h) for c in range(n_chunks) for h in range(GDN_HEADS)]
    gate = [gate_ref[c * CHUNK:(c + 1) * CHUNK, :] for c in range(n_chunks)]
    egc_all = [jnp.exp(g) for g in gate]
    g_last_all = [g[CHUNK - 1:CHUNK, :] for g in gate]
    kdf_all = [jnp.exp(gl - g) for gl, g in zip(g_last_all, gate)]
    eg_last_all = [jnp.exp(gl) for gl in g_last_all]

    def tile(c, t):
        return qkv_ref[c * CHUNK:(c + 1) * CHUNK, t * LANES:(t + 1) * LANES]

    def col(arrs, c, h):
        return arrs[c][:, GDN_HEADS + h:GDN_HEADS + h + 1]

    q = [tile(c, h) for c, h in items]
    k = [tile(c, GDN_HEADS + h) for c, h in items]
    v = [tile(c, 2 * GDN_HEADS + h) for c, h in items]
    beta = [gate[c][:, h:h + 1] for c, h in items]
    gcol = [col(gate, c, h) for c, h in items]
    grow = [jnp.sum(g * eye, axis=0, keepdims=True) for g in gcol]
    decay = [jnp.where(causal, jnp.exp(jnp.where(causal, gc - gr, 0.0)), 0.0) for gc, gr in zip(gcol, grow)]
    kb = [x * b for x, b in zip(k, beta)]
    k16 = [x.astype(BF16) for x in k]
    kq = [_dot_nt(jnp.concatenate([a.astype(BF16), b.astype(BF16)], axis=0), c) for a, b, c in zip(kb, q, k16)]
    kk = [x[0:CHUNK, :] for x in kq]
    qk = [x[CHUNK:, :] for x in kq]
    ls = [jnp.where(strict, a * d, 0.0) for a, d in zip(kk, decay)]
    intra16 = [jnp.where(causal, a * d, 0.0).astype(BF16) for a, d in zip(qk, decay)]
    ts = _unit_lower_inverses(ls, eye, level_masks)
    ys = [(t - eye).astype(BF16) for t in ts]
    egc = [col(egc_all, c, h) for c, h in items]
    vb = [x * b for x, b in zip(v, beta)]
    kbe = [x * e for x, e in zip(kb, egc)]
    rhs = [jnp.concatenate([a, b], axis=1) for a, b in zip(vb, kbe)]
    sol = [x + _dot(y, x.astype(BF16)) for x, y in zip(rhs, ys)]
    u = [x[:, 0:GDN_DV] for x in sol]
    w16 = [x[:, GDN_DV:].astype(BF16) for x in sol]
    qg16 = [(x * e).astype(BF16) for x, e in zip(q, egc)]
    kd16 = [(x * col(kdf_all, c, h)).astype(BF16) for x, (c, h) in zip(k, items)]
    eg_last = [col(eg_last_all, c, h) for c, h in items]
    wq16 = [jnp.concatenate([a, b], axis=0) for a, b in zip(w16, qg16)]

    o = [None] * len(items)
    if carry:
        state = [s_ref[0, h] for h in range(GDN_HEADS)]
    for c in range(n_chunks):
        idx = [c * GDN_HEADS + h for h in range(GDN_HEADS)]
        if not carry:
            state = [s_ref[c, h] for h in range(GDN_HEADS)]
        s16 = [s.astype(BF16) for s in state]
        ws_qs = [_dot(wq16[i], s) for i, s in zip(idx, s16)]
        v_new = [u[i] - x[0:CHUNK, :] for i, x in zip(idx, ws_qs)]
        vn16 = [x.astype(BF16) for x in v_new]
        for i, x, vn in zip(idx, ws_qs, vn16):
            o[i] = x[CHUNK:, :] + _dot(intra16[i], vn)
        state = [s * eg_last[i] + _dot_tn(kd16[i], vn) for i, s, vn in zip(idx, state, vn16)]
        if not carry:
            for h in range(GDN_HEADS):
                sout_ref[c, h] = state[h]

    for (c, h), oo in zip(items, o):
        o_ref[c * CHUNK:(c + 1) * CHUNK, h * GDN_DV:(h + 1) * GDN_DV] = oo.astype(o_ref.dtype)

    if carry:
        for h in range(GDN_HEADS):
            s_ref[0, h] = state[h]

        def _store_state():
            sout_ref[0] = s_ref[0]

        pl.when(pid == pl.num_programs(0) - 1)(_store_state)


def _gdn(qkv, gate, s0, seq_rows):
    n = qkv.shape[0]
    n_seq = n // seq_rows
    carry = n_seq == 1
    n_chunks = GDN_CHUNKS_PER_STEP
    rows = n_chunks * CHUNK
    assert n % rows == 0
    if carry:
        n_state = 1
        s_map = lambda i: (0, 0, 0, 0)
    else:
        assert seq_rows == CHUNK
        n_state = n_chunks
        s_map = lambda i: (i, 0, 0, 0)
    kern = functools.partial(_gdn_kernel, n_chunks=n_chunks, carry=carry)
    return pl.pallas_call(
        kern,
        grid=(n // rows,),
        in_specs=[
            pl.BlockSpec((rows, CONV_DIM), lambda i: (i, 0)),
            pl.BlockSpec((rows, LANES), lambda i: (i, 0)),
            pl.BlockSpec((n_state, GDN_HEADS, GDN_DK, GDN_DV), s_map),
        ],
        out_specs=[
            pl.BlockSpec((rows, GDN_HEADS * GDN_DV), lambda i: (i, 0)),
            pl.BlockSpec((n_state, GDN_HEADS, GDN_DK, GDN_DV), s_map),
        ],
        out_shape=[
            jax.ShapeDtypeStruct((n, GDN_HEADS * GDN_DV), BF16),
            jax.ShapeDtypeStruct((n_seq, GDN_HEADS, GDN_DK, GDN_DV), F32),
        ],
        scratch_shapes=[pltpu.VMEM((n_state, GDN_HEADS, GDN_DK, GDN_DV), F32)],
        compiler_params=_cparams(1),
        name="gdn",
    )(qkv, gate, s0)


SWA_KEYS = WINDOW + CHUNK
SWA_KEYS_PAD = 2 * LANES
KV_COLS = 2 * SWA_KV_HEADS * SWA_HD


def _bias_kernel(bucket_ref, rb_ref, o_ref):
    bucket = bucket_ref[...]
    hits = [bucket == b for b in range(NUM_BUCKETS)]
    for h in range(SWA_HEADS):
        acc = jnp.zeros((CHUNK, SWA_KEYS), F32)
        for b in range(NUM_BUCKETS):
            acc = jnp.where(hits[b], rb_ref[b * SWA_HEADS + h], acc)
        o_ref[h] = acc


def _bias_table(bucket, rel_bias_flat):
    return pl.pallas_call(
        _bias_kernel,
        grid=(1,),
        in_specs=[
            pl.BlockSpec((CHUNK, SWA_KEYS), lambda i: (0, 0)),
            pl.BlockSpec(memory_space=pltpu.SMEM),
        ],
        out_specs=pl.BlockSpec((SWA_HEADS, CHUNK, SWA_KEYS), lambda i: (0, 0, 0)),
        out_shape=jax.ShapeDtypeStruct((SWA_HEADS, CHUNK, SWA_KEYS), F32),
        compiler_params=_cparams(1),
        name="swa_bias",
    )(bucket, rel_bias_flat)


def _swa_kernel(q_ref, kvc_ref, kvp_ref, bias_ref, sink_ref, o_ref, kf_ref, *, n_chunks, one_sequence):
    n_pad = SWA_KEYS_PAD - SWA_KEYS
    kf_ref[kf_ref.shape[0] - n_pad:, :] = jnp.zeros((n_pad, KV_COLS), F32)
    if one_sequence:
        kf_ref[0:WINDOW, :] = kvp_ref[...]
        kf_ref[WINDOW:WINDOW + n_chunks * CHUNK, :] = kvc_ref[...]
        key_start = [c * CHUNK for c in range(n_chunks)]
        key_pos = lax.broadcasted_iota(jnp.int32, (CHUNK, SWA_KEYS), 1)
        tile_start = pl.program_id(0) * (n_chunks * CHUNK)
        valid = [key_pos + (tile_start + c * CHUNK - WINDOW) >= 0 for c in range(n_chunks)]
    else:
        for c in range(n_chunks):
            kf_ref[c * SWA_KEYS:c * SWA_KEYS + WINDOW, :] = kvp_ref[c * WINDOW:(c + 1) * WINDOW, :]
            kf_ref[c * SWA_KEYS + WINDOW:(c + 1) * SWA_KEYS, :] = kvc_ref[c * CHUNK:(c + 1) * CHUNK, :]
        key_start = [c * SWA_KEYS for c in range(n_chunks)]

    pairs = [(c, kh) for c in range(n_chunks) for kh in range(SWA_KV_HEADS)]
    half = SWA_KV_HEADS * SWA_HD
    keys_t = [kf_ref[key_start[c]:key_start[c] + SWA_KEYS_PAD, 0:half].T.astype(BF16) for c in range(n_chunks)]
    keys = {(c, kh): keys_t[c][kh * SWA_HD:(kh + 1) * SWA_HD, :] for c, kh in pairs}
    ones = jnp.ones((SWA_KEYS, LANES - SWA_HD), BF16)
    vals = {(c, kh): jnp.concatenate(
        [kf_ref[key_start[c]:key_start[c] + SWA_KEYS,
                (SWA_KV_HEADS + kh) * SWA_HD:(SWA_KV_HEADS + kh + 1) * SWA_HD].astype(BF16), ones], axis=1)
            for c, kh in pairs}
    items = [(c, hd) for c in range(n_chunks) for hd in range(SWA_HEADS)]
    q = [q_ref[c * CHUNK:(c + 1) * CHUNK, hd * SWA_HD:(hd + 1) * SWA_HD] for c, hd in items]
    def group_rows(c, kh):
        return range(c * SWA_HEADS + kh * SWA_GROUP, c * SWA_HEADS + (kh + 1) * SWA_GROUP)

    qk = {(c, kh): _dot(jnp.concatenate([q[i] for i in group_rows(c, kh)], axis=0), keys[(c, kh)])
          for c, kh in pairs}
    logits = [qk[(c, hd // SWA_GROUP)][(hd % SWA_GROUP) * CHUNK:(hd % SWA_GROUP + 1) * CHUNK, 0:SWA_KEYS] + bias_ref[hd]
              for c, hd in items]
    if one_sequence:
        logits = [jnp.where(valid[c], x, -jnp.inf) if c < WINDOW // CHUNK else x for x, (c, _) in zip(logits, items)]
    sink = [sink_ref[hd] for _, hd in items]
    m = [jnp.maximum(jnp.max(x, axis=-1, keepdims=True), s) for x, s in zip(logits, sink)]
    p16 = [jnp.exp(x - mm).astype(BF16) for x, mm in zip(logits, m)]
    pv = {(c, kh): _dot(jnp.concatenate([p16[i] for i in group_rows(c, kh)], axis=0), vals[(c, kh)])
          for c, kh in pairs}
    pv = [pv[(c, hd // SWA_GROUP)][(hd % SWA_GROUP) * CHUNK:(hd % SWA_GROUP + 1) * CHUNK, :] for c, hd in items]
    denom = [x[:, SWA_HD:SWA_HD + 1] + jnp.exp(s - mm) for x, s, mm in zip(pv, sink, m)]
    out = [x[:, 0:SWA_HD] / d for x, d in zip(pv, denom)]
    for x, (c, hd) in zip(out, items):
        o_ref[c * CHUNK:(c + 1) * CHUNK, hd * SWA_HD:(hd + 1) * SWA_HD] = x.astype(o_ref.dtype)


def _swa(qb, kv, kv_hist, bias, sinks, seq_rows):
    n = qb.shape[0]
    n_chunks = SWA_CHUNKS_PER_STEP
    rows = n_chunks * CHUNK
    assert n % rows == 0
    one_sequence = kv_hist is None
    if one_sequence:
        assert rows % WINDOW == 0
        per = rows // WINDOW
        prev_arr = kv
        prev_spec = pl.BlockSpec((WINDOW, KV_COLS), lambda i: (jnp.maximum(i * per - 1, 0), 0))
        kf_rows = WINDOW + rows
    else:
        assert seq_rows == CHUNK
        prev_arr = kv_hist
        prev_spec = pl.BlockSpec((n_chunks * WINDOW, KV_COLS), lambda i: (i, 0))
        kf_rows = n_chunks * SWA_KEYS
    kern = functools.partial(_swa_kernel, n_chunks=n_chunks, one_sequence=one_sequence)
    return pl.pallas_call(
        kern,
        grid=(n // rows,),
        in_specs=[
            pl.BlockSpec((rows, SWA_HEADS * SWA_HD), lambda i: (i, 0)),
            pl.BlockSpec((rows, KV_COLS), lambda i: (i, 0)),
            prev_spec,
            pl.BlockSpec((SWA_HEADS, CHUNK, SWA_KEYS), lambda i: (0, 0, 0)),
            pl.BlockSpec(memory_space=pltpu.SMEM),
        ],
        out_specs=pl.BlockSpec((rows, SWA_HEADS * SWA_HD), lambda i: (i, 0)),
        out_shape=jax.ShapeDtypeStruct((n, SWA_HEADS * SWA_HD), BF16),
        scratch_shapes=[pltpu.VMEM((kf_rows + SWA_KEYS_PAD - SWA_KEYS, KV_COLS), F32)],
        compiler_params=_cparams(1),
        name="swa",
    )(qb, kv, prev_arr, bias, sinks)


def _merge_kernel(x_ref, oa_ref, z_ref, ob_ref, gates_ref, mod_ref, nw_ref, wa_ref, wb_ref, wo_ref, o_ref, oa16_ref,
                  *, spt):
    for h in range(GDN_HEADS):
        cols = slice(h * GDN_DV, (h + 1) * GDN_DV)
        oo = oa_ref[:, cols].astype(F32)
        zz = z_ref[:, cols].astype(F32)
        on = oo * lax.rsqrt(jnp.mean(oo * oo, axis=-1, keepdims=True) + EPS) * nw_ref[...] * (zz * jax.nn.sigmoid(zz))
        oa16_ref[:, cols] = on.astype(BF16)
    ya = _dot(oa16_ref[...], wa_ref[...])
    yb = _dot(ob_ref[...], wb_ref[...])
    merged = (jax.nn.sigmoid(gates_ref[:, 0:D_MODEL].astype(F32)) * ya
              + jax.nn.sigmoid(gates_ref[:, D_MODEL:2 * D_MODEL].astype(F32)) * yb)
    y = _dot(merged.astype(BF16), wo_ref[...])
    rows = x_ref.shape[0] // spt
    for j in range(spt):
        sl = slice(j * rows, (j + 1) * rows)
        o_ref[sl, :] = x_ref[sl, :] + _mod_row(mod_ref, 5, spt, j) * y[sl, :]


def _merge(x, oa, z, ob, gates, mod, seq_rows, norm_w, wa, wb, wo):
    n = x.shape[0]
    tm, spt = _tile_rows(n, seq_rows, MERGE_ROWS)
    row_spec = lambda w: pl.BlockSpec((tm, w), lambda i: (i, 0))
    return pl.pallas_call(
        functools.partial(_merge_kernel, spt=spt),
        grid=(n // tm,),
        in_specs=[
            row_spec(D_MODEL), row_spec(D_MODEL), row_spec(D_MODEL), row_spec(D_MODEL), row_spec(2 * D_MODEL),
            _mod_spec(mod.shape[1]),
            _const_spec((1, GDN_DV)),
            _const_spec((D_MODEL, D_MODEL)), _const_spec((D_MODEL, D_MODEL)), _const_spec((D_MODEL, D_MODEL)),
        ],
        out_specs=row_spec(D_MODEL),
        out_shape=jax.ShapeDtypeStruct((n, D_MODEL), F32),
        scratch_shapes=[pltpu.VMEM((tm, GDN_HEADS * GDN_DV), BF16)],
        compiler_params=_cparams(1),
        name="merge",
    )(x, oa, z, ob, gates, mod, norm_w, wa, wb, wo)


def _t5_bucket(rel):
    half = NUM_BUCKETS // 2
    max_exact = half // 2
    n = jnp.abs(rel)
    large = max_exact + (jnp.log(jnp.maximum(n, 1).astype(jnp.float32) / max_exact)
                         / math.log(MAX_DISTANCE / max_exact) * (half - max_exact)).astype(jnp.int32)
    large = jnp.minimum(large, half - 1)
    return jnp.where(rel > 0, half, 0) + jnp.where(n < max_exact, n, large)


def _trunk(x, mod, conv_hist, s_hist, k_hist, v_hist, w):
    bsz, seq, _ = x.shape
    n = bsz * seq
    x2 = x.reshape(n, D_MODEL)
    x1 = _ffn(x2, mod, seq, w["norm_ffn1"], w["ffn1_in"], w["ffn1_out"], w["norm_final"], sub=0, final=False)
    if conv_hist is None:
        hist8 = jnp.zeros((bsz, HIST_ROWS, CONV_DIM), F32)
        s0 = jnp.zeros((bsz, GDN_HEADS, GDN_DK, GDN_DV), F32)
        kv_hist = None
    else:
        hist8 = jnp.concatenate([jnp.zeros((bsz, HIST_ROWS - (CONV_W - 1), CONV_DIM), F32), conv_hist], axis=1)
        s0 = s_hist
        kv_hist = jnp.concatenate([k_hist, v_hist], axis=2).reshape(bsz * WINDOW, KV_COLS)
    qkv, z, qb, kv, gates, gate, tail = _proj(x1, mod, seq, w["norm_mix"], w["w_in"], hist8, w["conv_w"], w["gdn_par"])
    oa, s_new = _gdn(qkv, gate, s0, seq)
    ob = _swa(qb, kv, kv_hist, w["bias"], w["sinks"], seq)
    x3 = _merge(x1, oa, z, ob, gates, mod, seq, w["gdn_norm_w"], w["w_a"], w["w_b"], w["w_out"])
    y = _ffn(x3, mod, seq, w["norm_ffn2"], w["ffn2_in"], w["ffn2_out"], w["norm_final"], sub=2, final=True)

    conv_new = tail[:, HIST_ROWS - (CONV_W - 1):]
    half = SWA_KV_HEADS * SWA_HD
    kv3 = kv.reshape(bsz, seq, KV_COLS)
    if kv_hist is None:
        k_new = kv3[:, seq - WINDOW:, :half]
        v_new = kv3[:, seq - WINDOW:, half:]
    else:
        keep = WINDOW - seq
        k_new = jnp.concatenate([k_hist.reshape(bsz, WINDOW, half)[:, WINDOW - keep:], kv3[:, :, :half]], axis=1)
        v_new = jnp.concatenate([v_hist.reshape(bsz, WINDOW, half)[:, WINDOW - keep:], kv3[:, :, half:]], axis=1)
    k_new = k_new.reshape(bsz, WINDOW, SWA_KV_HEADS, SWA_HD)
    v_new = v_new.reshape(bsz, WINDOW, SWA_KV_HEADS, SWA_HD)
    return (y.reshape(bsz, seq, D_MODEL), conv_new[None], s_new[None], k_new[None], v_new[None])


def kernel(x_prompt, x_sample, state_gdn_conv, state_gdn_s, cache_swa_k, cache_swa_v, c_prompt, c_sample,
           norm_ffn1, w_ffn1_in, w_ffn1_out, norm_mix, w_in, gdn_conv_w, gdn_a_log, gdn_dt_bias, gdn_norm_w,
           swa_sinks, rel_bias, w_branch_a, w_branch_b, w_out, norm_ffn2, w_ffn2_in, w_ffn2_out,
           w_ada, b_ada, norm_final):
    bp = x_prompt.shape[0]
    bs = x_sample.shape[0]
    assert bp == 1 and x_sample.shape[1] == CHUNK and cache_swa_k.shape[2] == WINDOW

    n_c = bp + bs
    pad = -n_c % SUBLANES
    c_all = jnp.concatenate([c_prompt, c_sample, jnp.zeros((pad, D_MODEL), F32)], axis=0)
    mod_all = _modulation(c_all, w_ada[0], b_ada[0][None, :])
    mod_p = mod_all[:, :bp]
    mod_s = mod_all[:, bp:n_c]

    w_packed = _pack_w_in(w_in)
    par = jnp.zeros((SUBLANES, LANES), F32)
    par = par.at[0, GDN_HEADS:2 * GDN_HEADS].set(gdn_a_log[0]).at[1, GDN_HEADS:2 * GDN_HEADS].set(gdn_dt_bias[0])

    rel = jnp.arange(SWA_KEYS)[None, :] - WINDOW - jnp.arange(CHUNK)[:, None]
    bias = _bias_table(_t5_bucket(rel).astype(jnp.int32), rel_bias.reshape(-1))

    w = {
        "norm_ffn1": norm_ffn1, "ffn1_in": w_ffn1_in[0].astype(BF16), "ffn1_out": w_ffn1_out[0].astype(BF16),
        "norm_mix": norm_mix, "w_in": w_packed, "conv_w": gdn_conv_w[0], "gdn_par": par,
        "gdn_norm_w": gdn_norm_w, "bias": bias, "sinks": swa_sinks[0],
        "w_a": w_branch_a[0].astype(BF16), "w_b": w_branch_b[0].astype(BF16), "w_out": w_out[0].astype(BF16),
        "norm_ffn2": norm_ffn2, "ffn2_in": w_ffn2_in[0].astype(BF16), "ffn2_out": w_ffn2_out[0].astype(BF16),
        "norm_final": norm_final[None, :],
    }
    y_p, p_conv, p_s, p_k, p_v = _trunk(x_prompt, mod_p, None, None, None, None, w)
    y_s, s_conv, s_s, s_k, s_v = _trunk(x_sample, mod_s, state_gdn_conv[0], state_gdn_s[0],
                                        cache_swa_k[0], cache_swa_v[0], w)
    return (y_p, y_s, p_conv, p_s, p_k, p_v, s_conv, s_s, s_k, s_v)
```

```python
import functools
import math

import jax
import jax.numpy as jnp
from jax import lax
from jax.experimental import pallas as pl
from jax.experimental.pallas import tpu as pltpu

F32 = jnp.float32
BF16 = jnp.bfloat16

D_MODEL = 1024
CHUNK = 64
GDN_HEADS = 8
GDN_DK = 128
GDN_DV = 128
CONV_W = 4
CONV_DIM = GDN_HEADS * (2 * GDN_DK + GDN_DV)
SWA_HEADS = 16
SWA_KV_HEADS = 2
SWA_GROUP = SWA_HEADS // SWA_KV_HEADS
SWA_HD = 64
WINDOW = 128
NUM_BUCKETS = 32
MAX_DISTANCE = 128
D_FF = 2816
N_MOD = 9
EPS = 1e-6

LANES = 128
SUBLANES = 8
VMEM_LIMIT = 56 * 1024 * 1024

MXU_COLS = 256
FF_CHUNK = MXU_COLS
assert D_FF % FF_CHUNK == 0
FFN_ROWS = 1024
HIST_ROWS = SUBLANES
MERGE_ROWS = 512
PACK_ROWS = 256
GDN_CHUNKS_PER_STEP = 4
SWA_CHUNKS_PER_STEP = 8


def _dot(a, b):
    return jnp.dot(a, b, preferred_element_type=F32)


def _dot_nt(a, b):
    return lax.dot_general(a, b, (((1,), (1,)), ((), ())), preferred_element_type=F32)


def _dot_tn(a, b):
    return lax.dot_general(a, b, (((0,), (0,)), ((), ())), preferred_element_type=F32)


def _cparams(n_grid):
    return pltpu.CompilerParams(dimension_semantics=("arbitrary",) * n_grid, vmem_limit_bytes=VMEM_LIMIT)


def _const_spec(shape):
    nd = len(shape)
    return pl.BlockSpec(shape, lambda i: (0,) * nd, pipeline_mode=pl.Buffered(1))


def _mod_kernel(c_ref, w_ref, b_ref, o_ref):
    c = c_ref[...]
    a = (c * jax.nn.sigmoid(c)).astype(BF16)
    o_ref[0] = _dot(a, w_ref[...].astype(BF16)) + b_ref[...]


def _modulation(c_pad, w_ada, b_ada):
    rows = c_pad.shape[0]
    return pl.pallas_call(
        _mod_kernel,
        grid=(N_MOD,),
        in_specs=[
            pl.BlockSpec((rows, D_MODEL), lambda j: (0, 0)),
            pl.BlockSpec((D_MODEL, D_MODEL), lambda j: (0, j)),
            pl.BlockSpec((1, D_MODEL), lambda j: (0, j)),
        ],
        out_specs=pl.BlockSpec((1, rows, D_MODEL), lambda j: (j, 0, 0)),
        out_shape=jax.ShapeDtypeStruct((N_MOD, rows, D_MODEL), F32),
        compiler_params=_cparams(1),
        name="adaln_mod",
    )(c_pad, w_ada, b_ada)


def _mod_row(mod_ref, idx, spt, j):
    if mod_ref.shape[1] == 1:
        return mod_ref[idx, 0:1, :]
    return mod_ref[idx, pl.ds(pl.program_id(0) * spt + j, 1), :]


def _norm_mod_store(h_ref, x_ref, g_ref, mod_ref, sub, spt):
    rows = x_ref.shape[0] // spt
    g = g_ref[...]
    for j in range(spt):
        xs = x_ref[j * rows:(j + 1) * rows, :]
        ms = jnp.mean(xs * xs, axis=-1, keepdims=True)
        y = xs * lax.rsqrt(ms + EPS) * g
        sh = _mod_row(mod_ref, 3 * sub, spt, j)
        sc = _mod_row(mod_ref, 3 * sub + 1, spt, j)
        h_ref[j * rows:(j + 1) * rows, :] = (y * (1.0 + sc) + sh).astype(h_ref.dtype)


def _tile_rows(n_rows, seq_rows, target):
    if seq_rows >= target:
        assert seq_rows % target == 0
        return target, 1
    assert target % seq_rows == 0
    tm = min(target, n_rows)
    assert n_rows % tm == 0
    return tm, tm // seq_rows


def _mod_spec(n_seq_total):
    return pl.BlockSpec((N_MOD, n_seq_total, D_MODEL), lambda i: (0, 0, 0))


def _ffn_kernel(x_ref, mod_ref, g_ref, w1_ref, w2_ref, gf_ref, o_ref, h_ref, *, sub, final, spt):
    _norm_mod_store(h_ref, x_ref, g_ref, mod_ref, sub, spt)
    h = h_ref[...]
    acc = None
    for c in range(D_FF // FF_CHUNK):
        c0 = c * FF_CHUNK
        gate = _dot(h, w1_ref[:, c0:c0 + FF_CHUNK])
        up = _dot(h, w1_ref[:, D_FF + c0:D_FF + c0 + FF_CHUNK])
        a = (gate * jax.nn.sigmoid(gate) * up).astype(BF16)
        part = _dot(a, w2_ref[c0:c0 + FF_CHUNK, :])
        acc = part if acc is None else acc + part
    rows = x_ref.shape[0] // spt
    for j in range(spt):
        sl = slice(j * rows, (j + 1) * rows)
        ga = _mod_row(mod_ref, 3 * sub + 2, spt, j)
        xn = x_ref[sl, :] + 0.5 * ga * acc[sl, :]
        if final:
            ms = jnp.mean(xn * xn, axis=-1, keepdims=True)
            xn = xn * lax.rsqrt(ms + EPS) * gf_ref[...]
        o_ref[sl, :] = xn


def _ffn(x, mod, seq_rows, g, w1, w2, gf, *, sub, final):
    n = x.shape[0]
    tm, spt = _tile_rows(n, seq_rows, FFN_ROWS)
    kern = functools.partial(_ffn_kernel, sub=sub, final=final, spt=spt)
    return pl.pallas_call(
        kern,
        grid=(n // tm,),
        in_specs=[
            pl.BlockSpec((tm, D_MODEL), lambda i: (i, 0)),
            _mod_spec(mod.shape[1]),
            _const_spec((1, D_MODEL)),
            _const_spec((D_MODEL, 2 * D_FF)),
            _const_spec((D_FF, D_MODEL)),
            _const_spec((1, D_MODEL)),
        ],
        out_specs=pl.BlockSpec((tm, D_MODEL), lambda i: (i, 0)),
        out_shape=jax.ShapeDtypeStruct((n, D_MODEL), F32),
        scratch_shapes=[pltpu.VMEM((tm, D_MODEL), BF16)],
        compiler_params=_cparams(1),
        name="ffn_final" if final else "ffn",
    )(x, mod, g, w1, w2, gf)


PROJ_GROUPS = (("qkv", CONV_DIM), ("z", GDN_HEADS * GDN_DV), ("qb", SWA_HEADS * SWA_HD),
               ("kv", 2 * SWA_KV_HEADS * SWA_HD), ("gates", 2 * D_MODEL), ("ba", LANES))
PROJ_COLS = sum(w for _, w in PROJ_GROUPS)


assert math.log2(SWA_HD) % 2 == 0
W_IN_ALIGNED = CONV_DIM + GDN_HEADS * GDN_DV
W_IN_SMALL = 2 * GDN_HEADS
W_IN_REST = PROJ_COLS - W_IN_ALIGNED - LANES


def _pack_w_in_kernel(w_ref, o_ref):
    o_ref[:, 0:W_IN_ALIGNED] = w_ref[:, 0:W_IN_ALIGNED].astype(BF16)
    hi = W_IN_ALIGNED + W_IN_SMALL
    n_qb = SWA_HEADS * SWA_HD
    o_ref[:, W_IN_ALIGNED:W_IN_ALIGNED + n_qb] = (w_ref[:, hi:hi + n_qb] * (SWA_HD ** -0.5)).astype(BF16)
    o_ref[:, W_IN_ALIGNED + n_qb:W_IN_ALIGNED + W_IN_REST] = w_ref[:, hi + n_qb:hi + W_IN_REST].astype(BF16)
    lane = lax.broadcasted_iota(jnp.int32, (w_ref.shape[0], LANES), 1)
    small = jnp.where(lane < W_IN_SMALL, w_ref[:, W_IN_ALIGNED:W_IN_ALIGNED + LANES], 0.0)
    o_ref[:, W_IN_ALIGNED + W_IN_REST:PROJ_COLS] = small.astype(BF16)


def _pack_w_in(w_in):
    n_cols = W_IN_ALIGNED + W_IN_SMALL + W_IN_REST
    assert w_in.shape == (1, D_MODEL, n_cols)
    rows = PACK_ROWS
    return pl.pallas_call(
        _pack_w_in_kernel,
        grid=(D_MODEL // rows,),
        in_specs=[pl.BlockSpec((None, rows, n_cols), lambda i: (0, i, 0))],
        out_specs=pl.BlockSpec((rows, PROJ_COLS), lambda i: (i, 0)),
        out_shape=jax.ShapeDtypeStruct((D_MODEL, PROJ_COLS), BF16),
        compiler_params=_cparams(1),
        name="pack_w_in",
    )(w_in)


PROJ_OFFSETS = {}
_off = 0
for _name, _width in PROJ_GROUPS:
    PROJ_OFFSETS[_name] = (_off, _width)
    _off += _width
PROJ_ROWS = 512


def _proj_kernel(x_ref, mod_ref, g_ref, w_ref, hist_ref, cw_ref, par_ref,
                 qkv_ref, z_ref, qb_ref, kv_ref, gates_ref, gate_ref, tail_ref,
                 h_ref, xp_ref, *, spt, carry):
    pid = pl.program_id(0)
    tm = x_ref.shape[0]
    n_chunks = tm // CHUNK
    seg = CHUNK if carry else HIST_ROWS + CHUNK

    _norm_mod_store(h_ref, x_ref, g_ref, mod_ref, 1, spt)
    h = h_ref[...]

    def group(name, lo=0, width=None):
        c0, full = PROJ_OFFSETS[name]
        width = full if width is None else width
        return _dot(h, w_ref[:, c0 + lo:c0 + lo + width])

    if carry:
        def _load_hist():
            xp_ref[0:HIST_ROWS, :] = hist_ref[0]

        pl.when(pid == 0)(_load_hist)
    else:
        for c in range(n_chunks):
            xp_ref[c * seg:c * seg + HIST_ROWS, :] = hist_ref[c]

    part = GDN_HEADS * GDN_DK

    def project_part(p):
        raw = group("qkv", p * part, part)
        cols = slice(p * part, (p + 1) * part)
        if carry:
            xp_ref[HIST_ROWS:HIST_ROWS + tm, cols] = raw
        else:
            for c in range(n_chunks):
                xp_ref[c * seg + HIST_ROWS:(c + 1) * seg, cols] = raw[c * CHUNK:(c + 1) * CHUNK, :]

    def conv_part(p):
        for c in range(n_chunks):
            for t in range(p * GDN_HEADS, (p + 1) * GDN_HEADS):
                c0 = t * LANES
                win = xp_ref[c * seg:c * seg + HIST_ROWS + CHUNK, c0:c0 + LANES]
                acc = cw_ref[CONV_W - 1:CONV_W, c0:c0 + LANES] * win[HIST_ROWS:, :]
                for s in range(1, CONV_W):
                    tap = pltpu.roll(win, s, 0)[HIST_ROWS:, :]
                    acc = acc + cw_ref[CONV_W - 1 - s:CONV_W - s, c0:c0 + LANES] * tap
                y = acc * jax.nn.sigmoid(acc)
                if p < 2:
                    y = y * lax.rsqrt(jnp.sum(y * y, axis=-1, keepdims=True) + EPS)
                    if p == 0:
                        y = y * (GDN_DK ** -0.5)
                qkv_ref[c * CHUNK:(c + 1) * CHUNK, c0:c0 + LANES] = y

    def gate_math():
        ba = group("ba")
        ri = lax.broadcasted_iota(jnp.int32, (CHUNK, CHUNK), 0)
        ci = lax.broadcasted_iota(jnp.int32, (CHUNK, CHUNK), 1)
        tri = (ri >= ci).astype(BF16)
        lane = lax.broadcasted_iota(jnp.int32, (CHUNK, LANES), 1)
        a_coef = -jnp.exp(par_ref[0:1, :])
        dt_bias = par_ref[1:2, :]
        for c in range(n_chunks):
            bac = ba[c * CHUNK:(c + 1) * CHUNK, :]
            g_all = a_coef * jax.nn.softplus(bac + dt_bias)
            g_hi = g_all.astype(BF16)
            g_r1 = g_all - g_hi.astype(F32)
            g_mid = g_r1.astype(BF16)
            g_lo = (g_r1 - g_mid.astype(F32)).astype(BF16)
            gc = _dot(tri, g_hi) + _dot(tri, g_mid) + _dot(tri, g_lo)
            gate_ref[c * CHUNK:(c + 1) * CHUNK, :] = jnp.where(lane < GDN_HEADS, jax.nn.sigmoid(bac), gc)

    project_part(0)
    project_part(1)
    conv_part(0)
    project_part(2)
    conv_part(1)
    z_ref[...] = group("z").astype(z_ref.dtype)
    gate_math()
    qb_ref[...] = group("qb").astype(qb_ref.dtype)
    conv_part(2)
    kv_ref[...] = group("kv")
    gates_ref[...] = group("gates").astype(gates_ref.dtype)

    if carry:
        def _store_tail():
            tail_ref[0] = xp_ref[tm:tm + HIST_ROWS, :]

        pl.when(pid == pl.num_programs(0) - 1)(_store_tail)
        xp_ref[0:HIST_ROWS, :] = xp_ref[tm:tm + HIST_ROWS, :]
    else:
        for c in range(n_chunks):
            tail_ref[c] = xp_ref[c * seg + CHUNK:c * seg + CHUNK + HIST_ROWS, :]


def _proj(x, mod, seq_rows, g, w_packed, hist8, conv_w, par):
    n = x.shape[0]
    n_seq = n // seq_rows
    carry = n_seq == 1
    tm, spt = _tile_rows(n, seq_rows, PROJ_ROWS)
    if carry:
        n_hist = 1
        hist_map = lambda i: (0, 0, 0)
        xp_rows = HIST_ROWS + tm
    else:
        assert seq_rows == CHUNK
        n_hist = tm // CHUNK
        hist_map = lambda i: (i, 0, 0)
        xp_rows = n_hist * (HIST_ROWS + CHUNK)
    widths = [CONV_DIM, GDN_HEADS * GDN_DV, SWA_HEADS * SWA_HD, KV_COLS, 2 * D_MODEL, LANES]
    return pl.pallas_call(
        functools.partial(_proj_kernel, spt=spt, carry=carry),
        grid=(n // tm,),
        in_specs=[
            pl.BlockSpec((tm, D_MODEL), lambda i: (i, 0)),
            _mod_spec(mod.shape[1]),
            _const_spec((1, D_MODEL)),
            _const_spec((D_MODEL, PROJ_COLS)),
            pl.BlockSpec((n_hist, HIST_ROWS, CONV_DIM), hist_map),
            _const_spec((CONV_W, CONV_DIM)),
            _const_spec((SUBLANES, LANES)),
        ],
        out_specs=[pl.BlockSpec((tm, w), lambda i: (i, 0)) for w in widths]
        + [pl.BlockSpec((n_hist, HIST_ROWS, CONV_DIM), hist_map)],
        out_shape=[jax.ShapeDtypeStruct((n, w), BF16 if i in (1, 2, 4) else F32) for i, w in enumerate(widths)]
        + [jax.ShapeDtypeStruct((n_seq, HIST_ROWS, CONV_DIM), F32)],
        scratch_shapes=[pltpu.VMEM((tm, D_MODEL), BF16), pltpu.VMEM((xp_rows, CONV_DIM), F32)],
        compiler_params=_cparams(1),
        name="in_proj",
    )(x, mod, g, w_packed, hist8, conv_w, par)


def _unit_lower_inverses(ls, eye, level_masks):
    ts = [eye - jnp.where(level_masks[0], l, 0.0) for l in ls]
    ls16 = [l.astype(BF16) for l in ls]
    for mask in level_masks[1:]:
        lk = [jnp.where(mask, l, jnp.zeros_like(l)) for l in ls16]
        tp = [t.astype(BF16) for t in ts]
        m1 = [_dot(a, b) for a, b in zip(lk, tp)]
        m2 = [_dot(a, b.astype(BF16)) for a, b in zip(tp, m1)]
        ts = [t - m for t, m in zip(ts, m2)]
    return ts


def _gdn_kernel(qkv_ref, gate_ref, s0_ref, o_ref, sout_ref, s_ref, *, n_chunks, carry):
    pid = pl.program_id(0)

    if carry:
        def _load_state():
            s_ref[0] = s0_ref[0]

        pl.when(pid == 0)(_load_state)
    else:
        for c in range(n_chunks):
            s_ref[c] = s0_ref[c]

    ri = lax.broadcasted_iota(jnp.int32, (CHUNK, CHUNK), 0)
    ci = lax.broadcasted_iota(jnp.int32, (CHUNK, CHUNK), 1)
    causal = ri >= ci
    strict = ri > ci
    eye = (ri == ci).astype(F32)
    level_masks = []
    for lvl in range(int(math.log2(CHUNK))):
        same_block = (ri >> (lvl + 1)) == (ci >> (lvl + 1))
        level_masks.append(same_block & (((ri >> lvl) & 1) == 1) & (((ci >> lvl) & 1) == 0))

    items = [(c, h) for c in range(n_chunks) for h in range(GDN_HEADS)]
    gate = [gate_ref[c * CHUNK:(c + 1) * CHUNK, :] for c in range(n_chunks)]
    egc_all = [jnp.exp(g) for g in gate]
    g_last_all = [g[CHUNK - 1:CHUNK, :] for g in gate]
    kdf_all = [jnp.exp(gl - g) for gl, g in zip(g_last_all, gate)]
    eg_last_all = [jnp.exp(gl) for gl in g_last_all]

    def tile(c, t):
        return qkv_ref[c * CHUNK:(c + 1) * CHUNK, t * LANES:(t + 1) * LANES]

    def col(arrs, c, h):
        return arrs[c][:, GDN_HEADS + h:GDN_HEADS + h + 1]

    q = [tile(c, h) for c, h in items]
    k = [tile(c, GDN_HEADS + h) for c, h in items]
    v = [tile(c, 2 * GDN_HEADS + h) for c, h in items]
    beta = [gate[c][:, h:h + 1] for c, h in items]
    gcol = [col(gate, c, h) for c, h in items]
    grow = [jnp.sum(g * eye, axis=0, keepdims=True) for g in gcol]
    decay = [jnp.where(causal, jnp.exp(jnp.where(causal, gc - gr, 0.0)), 0.0) for gc, gr in zip(gcol, grow)]
    kb = [x * b for x, b in zip(k, beta)]
    k16 = [x.astype(BF16) for x in k]
    kq = [_dot_nt(jnp.concatenate([a.astype(BF16), b.astype(BF16)], axis=0), c) for a, b, c in zip(kb, q, k16)]
    kk = [x[0:CHUNK, :] for x in kq]
    qk = [x[CHUNK:, :] for x in kq]
    ls = [jnp.where(strict, a * d, 0.0) for a, d in zip(kk, decay)]
    intra16 = [jnp.where(causal, a * d, 0.0).astype(BF16) for a, d in zip(qk, decay)]
    ts = _unit_lower_inverses(ls, eye, level_masks)
    ys = [(t - eye).astype(BF16) for t in ts]
    egc = [col(egc_all, c, h) for c, h in items]
    vb = [x * b for x, b in zip(v, beta)]
    kbe = [x * e for x, e in zip(kb, egc)]
    rhs = [jnp.concatenate([a, b], axis=1) for a, b in zip(vb, kbe)]
    sol = [x + _dot(y, x.astype(BF16)) for x, y in zip(rhs, ys)]
    u = [x[:, 0:GDN_DV] for x in sol]
    w16 = [x[:, GDN_DV:].astype(BF16) for x in sol]
    qg16 = [(x * e).astype(BF16) for x, e in zip(q, egc)]
    kd16 = [(x * col(kdf_all, c, h)).astype(BF16) for x, (c, h) in zip(k, items)]
    eg_last = [col(eg_last_all, c, h) for c, h in items]
    wq16 = [jnp.concatenate([a, b], axis=0) for a, b in zip(w16, qg16)]

    o = [None] * len(items)
    if carry:
        state = [s_ref[0, h] for h in range(GDN_HEADS)]
    for c in range(n_chunks):
        idx = [c * GDN_HEADS + h for h in range(GDN_HEADS)]
        if not carry:
            state = [s_ref[c, h] for h in range(GDN_HEADS)]
        s16 = [s.astype(BF16) for s in state]
        ws_qs = [_dot(wq16[i], s) for i, s in zip(idx, s16)]
        v_new = [u[i] - x[0:CHUNK, :] for i, x in zip(idx, ws_qs)]
        vn16 = [x.astype(BF16) for x in v_new]
        for i, x, vn in zip(idx, ws_qs, vn16):
            o[i] = x[CHUNK:, :] + _dot(intra16[i], vn)
        state = [s * eg_last[i] + _dot_tn(kd16[i], vn) for i, s, vn in zip(idx, state, vn16)]
        if not carry:
            for h in range(GDN_HEADS):
                sout_ref[c, h] = state[h]

    for (c, h), oo in zip(items, o):
        o_ref[c * CHUNK:(c + 1) * CHUNK, h * GDN_DV:(h + 1) * GDN_DV] = oo.astype(o_ref.dtype)

    if carry:
        for h in range(GDN_HEADS):
            s_ref[0, h] = state[h]

        def _store_state():
            sout_ref[0] = s_ref[0]

        pl.when(pid == pl.num_programs(0) - 1)(_store_state)


def _gdn(qkv, gate, s0, seq_rows):
    n = qkv.shape[0]
    n_seq = n // seq_rows
    carry = n_seq == 1
    n_chunks = GDN_CHUNKS_PER_STEP
    rows = n_chunks * CHUNK
    assert n % rows == 0
    if carry:
        n_state = 1
        s_map = lambda i: (0, 0, 0, 0)
    else:
        assert seq_rows == CHUNK
        n_state = n_chunks
        s_map = lambda i: (i, 0, 0, 0)
    kern = functools.partial(_gdn_kernel, n_chunks=n_chunks, carry=carry)
    return pl.pallas_call(
        kern,
        grid=(n // rows,),
        in_specs=[
            pl.BlockSpec((rows, CONV_DIM), lambda i: (i, 0)),
            pl.BlockSpec((rows, LANES), lambda i: (i, 0)),
            pl.BlockSpec((n_state, GDN_HEADS, GDN_DK, GDN_DV), s_map),
        ],
        out_specs=[
            pl.BlockSpec((rows, GDN_HEADS * GDN_DV), lambda i: (i, 0)),
            pl.BlockSpec((n_state, GDN_HEADS, GDN_DK, GDN_DV), s_map),
        ],
        out_shape=[
            jax.ShapeDtypeStruct((n, GDN_HEADS * GDN_DV), BF16),
            jax.ShapeDtypeStruct((n_seq, GDN_HEADS, GDN_DK, GDN_DV), F32),
        ],
        scratch_shapes=[pltpu.VMEM((n_state, GDN_HEADS, GDN_DK, GDN_DV), F32)],
        compiler_params=_cparams(1),
        name="gdn",
    )(qkv, gate, s0)


SWA_KEYS = WINDOW + CHUNK
SWA_KEYS_PAD = 2 * LANES
KV_COLS = 2 * SWA_KV_HEADS * SWA_HD


def _bias_kernel(bucket_ref, rb_ref, o_ref):
    bucket = bucket_ref[...]
    hits = [bucket == b for b in range(NUM_BUCKETS)]
    for h in range(SWA_HEADS):
        acc = jnp.zeros((CHUNK, SWA_KEYS), F32)
        for b in range(NUM_BUCKETS):
            acc = jnp.where(hits[b], rb_ref[b * SWA_HEADS + h], acc)
        o_ref[h] = acc


def _bias_table(bucket, rel_bias_flat):
    return pl.pallas_call(
        _bias_kernel,
        grid=(1,),
        in_specs=[
            pl.BlockSpec((CHUNK, SWA_KEYS), lambda i: (0, 0)),
            pl.BlockSpec(memory_space=pltpu.SMEM),
        ],
        out_specs=pl.BlockSpec((SWA_HEADS, CHUNK, SWA_KEYS), lambda i: (0, 0, 0)),
        out_shape=jax.ShapeDtypeStruct((SWA_HEADS, CHUNK, SWA_KEYS), F32),
        compiler_params=_cparams(1),
        name="swa_bias",
    )(bucket, rel_bias_flat)


def _swa_kernel(q_ref, kvc_ref, kvp_ref, bias_ref, sink_ref, o_ref, kf_ref, *, n_chunks, one_sequence):
    n_pad = SWA_KEYS_PAD - SWA_KEYS
    kf_ref[kf_ref.shape[0] - n_pad:, :] = jnp.zeros((n_pad, KV_COLS), F32)
    if one_sequence:
        kf_ref[0:WINDOW, :] = kvp_ref[...]
        kf_ref[WINDOW:WINDOW + n_chunks * CHUNK, :] = kvc_ref[...]
        key_start = [c * CHUNK for c in range(n_chunks)]
        key_pos = lax.broadcasted_iota(jnp.int32, (CHUNK, SWA_KEYS), 1)
        tile_start = pl.program_id(0) * (n_chunks * CHUNK)
        valid = [key_pos + (tile_start + c * CHUNK - WINDOW) >= 0 for c in range(n_chunks)]
    else:
        for c in range(n_chunks):
            kf_ref[c * SWA_KEYS:c * SWA_KEYS + WINDOW, :] = kvp_ref[c * WINDOW:(c + 1) * WINDOW, :]
            kf_ref[c * SWA_KEYS + WINDOW:(c + 1) * SWA_KEYS, :] = kvc_ref[c * CHUNK:(c + 1) * CHUNK, :]
        key_start = [c * SWA_KEYS for c in range(n_chunks)]

    pairs = [(c, kh) for c in range(n_chunks) for kh in range(SWA_KV_HEADS)]
    half = SWA_KV_HEADS * SWA_HD
    keys_t = [kf_ref[key_start[c]:key_start[c] + SWA_KEYS_PAD, 0:half].T.astype(BF16) for c in range(n_chunks)]
    keys = {(c, kh): keys_t[c][kh * SWA_HD:(kh + 1) * SWA_HD, :] for c, kh in pairs}
    ones = jnp.ones((SWA_KEYS, LANES - SWA_HD), BF16)
    vals = {(c, kh): jnp.concatenate(
        [kf_ref[key_start[c]:key_start[c] + SWA_KEYS,
                (SWA_KV_HEADS + kh) * SWA_HD:(SWA_KV_HEADS + kh + 1) * SWA_HD].astype(BF16), ones], axis=1)
            for c, kh in pairs}
    items = [(c, hd) for c in range(n_chunks) for hd in range(SWA_HEADS)]
    q = [q_ref[c * CHUNK:(c + 1) * CHUNK, hd * SWA_HD:(hd + 1) * SWA_HD] for c, hd in items]
    def group_rows(c, kh):
        return range(c * SWA_HEADS + kh * SWA_GROUP, c * SWA_HEADS + (kh + 1) * SWA_GROUP)

    qk = {(c, kh): _dot(jnp.concatenate([q[i] for i in group_rows(c, kh)], axis=0), keys[(c, kh)])
          for c, kh in pairs}
    logits = [qk[(c, hd // SWA_GROUP)][(hd % SWA_GROUP) * CHUNK:(hd % SWA_GROUP + 1) * CHUNK, 0:SWA_KEYS] + bias_ref[hd]
              for c, hd in items]
    if one_sequence:
        logits = [jnp.where(valid[c], x, -jnp.inf) if c < WINDOW // CHUNK else x for x, (c, _) in zip(logits, items)]
    sink = [sink_ref[hd] for _, hd in items]
    m = [jnp.maximum(jnp.max(x, axis=-1, keepdims=True), s) for x, s in zip(logits, sink)]
    p16 = [jnp.exp(x - mm).astype(BF16) for x, mm in zip(logits, m)]
    pv = {(c, kh): _dot(jnp.concatenate([p16[i] for i in group_rows(c, kh)], axis=0), vals[(c, kh)])
          for c, kh in pairs}
    pv = [pv[(c, hd // SWA_GROUP)][(hd % SWA_GROUP) * CHUNK:(hd % SWA_GROUP + 1) * CHUNK, :] for c, hd in items]
    denom = [x[:, SWA_HD:SWA_HD + 1] + jnp.exp(s - mm) for x, s, mm in zip(pv, sink, m)]
    out = [x[:, 0:SWA_HD] / d for x, d in zip(pv, denom)]
    for x, (c, hd) in zip(out, items):
        o_ref[c * CHUNK:(c + 1) * CHUNK, hd * SWA_HD:(hd + 1) * SWA_HD] = x.astype(o_ref.dtype)


def _swa(qb, kv, kv_hist, bias, sinks, seq_rows):
    n = qb.shape[0]
    n_chunks = SWA_CHUNKS_PER_STEP
    rows = n_chunks * CHUNK
    assert n % rows == 0
    one_sequence = kv_hist is None
    if one_sequence:
        assert rows % WINDOW == 0
        per = rows // WINDOW
        prev_arr = kv
        prev_spec = pl.BlockSpec((WINDOW, KV_COLS), lambda i: (jnp.maximum(i * per - 1, 0), 0))
        kf_rows = WINDOW + rows
    else:
        assert seq_rows == CHUNK
        prev_arr = kv_hist
        prev_spec = pl.BlockSpec((n_chunks * WINDOW, KV_COLS), lambda i: (i, 0))
        kf_rows = n_chunks * SWA_KEYS
    kern = functools.partial(_swa_kernel, n_chunks=n_chunks, one_sequence=one_sequence)
    return pl.pallas_call(
        kern,
        grid=(n // rows,),
        in_specs=[
            pl.BlockSpec((rows, SWA_HEADS * SWA_HD), lambda i: (i, 0)),
            pl.BlockSpec((rows, KV_COLS), lambda i: (i, 0)),
            prev_spec,
            pl.BlockSpec((SWA_HEADS, CHUNK, SWA_KEYS), lambda i: (0, 0, 0)),
            pl.BlockSpec(memory_space=pltpu.SMEM),
        ],
        out_specs=pl.BlockSpec((rows, SWA_HEADS * SWA_HD), lambda i: (i, 0)),
        out_shape=jax.ShapeDtypeStruct((n, SWA_HEADS * SWA_HD), BF16),
        scratch_shapes=[pltpu.VMEM((kf_rows + SWA_KEYS_PAD - SWA_KEYS, KV_COLS), F32)],
        compiler_params=_cparams(1),
        name="swa",
    )(qb, kv, prev_arr, bias, sinks)


def _merge_kernel(x_ref, oa_ref, z_ref, ob_ref, gates_ref, mod_ref, nw_ref, wa_ref, wb_ref, wo_ref, o_ref, oa16_ref,
                  *, spt):
    for h in range(GDN_HEADS):
        cols = slice(h * GDN_DV, (h + 1) * GDN_DV)
        oo = oa_ref[:, cols].astype(F32)
        zz = z_ref[:, cols].astype(F32)
        on = oo * lax.rsqrt(jnp.mean(oo * oo, axis=-1, keepdims=True) + EPS) * nw_ref[...] * (zz * jax.nn.sigmoid(zz))
        oa16_ref[:, cols] = on.astype(BF16)
    ya = _dot(oa16_ref[...], wa_ref[...])
    yb = _dot(ob_ref[...], wb_ref[...])
    merged = (jax.nn.sigmoid(gates_ref[:, 0:D_MODEL].astype(F32)) * ya
              + jax.nn.sigmoid(gates_ref[:, D_MODEL:2 * D_MODEL].astype(F32)) * yb)
    y = _dot(merged.astype(BF16), wo_ref[...])
    rows = x_ref.shape[0] // spt
    for j in range(spt):
        sl = slice(j * rows, (j + 1) * rows)
        o_ref[sl, :] = x_ref[sl, :] + _mod_row(mod_ref, 5, spt, j) * y[sl, :]


def _merge(x, oa, z, ob, gates, mod, seq_rows, norm_w, wa, wb, wo):
    n = x.shape[0]
    tm, spt = _tile_rows(n, seq_rows, MERGE_ROWS)
    row_spec = lambda w: pl.BlockSpec((tm, w), lambda i: (i, 0))
    return pl.pallas_call(
        functools.partial(_merge_kernel, spt=spt),
        grid=(n // tm,),
        in_specs=[
            row_spec(D_MODEL), row_spec(D_MODEL), row_spec(D_MODEL), row_spec(D_MODEL), row_spec(2 * D_MODEL),
            _mod_spec(mod.shape[1]),
            _const_spec((1, GDN_DV)),
            _const_spec((D_MODEL, D_MODEL)), _const_spec((D_MODEL, D_MODEL)), _const_spec((D_MODEL, D_MODEL)),
        ],
        out_specs=row_spec(D_MODEL),
        out_shape=jax.ShapeDtypeStruct((n, D_MODEL), F32),
        scratch_shapes=[pltpu.VMEM((tm, GDN_HEADS * GDN_DV), BF16)],
        compiler_params=_cparams(1),
        name="merge",
    )(x, oa, z, ob, gates, mod, norm_w, wa, wb, wo)


def _merge_ffn_kernel(x_ref, oa_ref, z_ref, ob_ref, gates_ref, mod_ref, nw_ref, wa_ref, wb_ref, wo_ref,
                      g_ref, w1_ref, w2_ref, gf_ref, o_ref, oa16_ref, x3_ref, h_ref, *, spt):
    _merge_kernel(x_ref, oa_ref, z_ref, ob_ref, gates_ref, mod_ref, nw_ref, wa_ref, wb_ref, wo_ref, x3_ref, oa16_ref,
                  spt=spt)
    _ffn_kernel(x3_ref, mod_ref, g_ref, w1_ref, w2_ref, gf_ref, o_ref, h_ref, sub=2, final=True, spt=spt)


def _merge_ffn(x, oa, z, ob, gates, mod, seq_rows, norm_w, wa, wb, wo, g, w1, w2, gf):
    n = x.shape[0]
    tm, spt = _tile_rows(n, seq_rows, MERGE_ROWS)
    row_spec = lambda w: pl.BlockSpec((tm, w), lambda i: (i, 0))
    return pl.pallas_call(
        functools.partial(_merge_ffn_kernel, spt=spt),
        grid=(n // tm,),
        in_specs=[
            row_spec(D_MODEL), row_spec(D_MODEL), row_spec(D_MODEL), row_spec(D_MODEL), row_spec(2 * D_MODEL),
            _mod_spec(mod.shape[1]),
            _const_spec((1, GDN_DV)),
            _const_spec((D_MODEL, D_MODEL)), _const_spec((D_MODEL, D_MODEL)), _const_spec((D_MODEL, D_MODEL)),
            _const_spec((1, D_MODEL)),
            _const_spec((D_MODEL, 2 * D_FF)),
            _const_spec((D_FF, D_MODEL)),
            _const_spec((1, D_MODEL)),
        ],
        out_specs=row_spec(D_MODEL),
        out_shape=jax.ShapeDtypeStruct((n, D_MODEL), F32),
        scratch_shapes=[pltpu.VMEM((tm, GDN_HEADS * GDN_DV), BF16), pltpu.VMEM((tm, D_MODEL), F32),
                        pltpu.VMEM((tm, D_MODEL), BF16)],
        compiler_params=_cparams(1),
        name="merge_ffn",
    )(x, oa, z, ob, gates, mod, norm_w, wa, wb, wo, g, w1, w2, gf)


def _t5_bucket(rel):
    half = NUM_BUCKETS // 2
    max_exact = half // 2
    n = jnp.abs(rel)
    large = max_exact + (jnp.log(jnp.maximum(n, 1).astype(jnp.float32) / max_exact)
                         / math.log(MAX_DISTANCE / max_exact) * (half - max_exact)).astype(jnp.int32)
    large = jnp.minimum(large, half - 1)
    return jnp.where(rel > 0, half, 0) + jnp.where(n < max_exact, n, large)


def _trunk(x, mod, conv_hist, s_hist, k_hist, v_hist, w):
    bsz, seq, _ = x.shape
    n = bsz * seq
    x2 = x.reshape(n, D_MODEL)
    x1 = _ffn(x2, mod, seq, w["norm_ffn1"], w["ffn1_in"], w["ffn1_out"], w["norm_final"], sub=0, final=False)
    if conv_hist is None:
        hist8 = jnp.zeros((bsz, HIST_ROWS, CONV_DIM), F32)
        s0 = jnp.zeros((bsz, GDN_HEADS, GDN_DK, GDN_DV), F32)
        kv_hist = None
    else:
        hist8 = jnp.concatenate([jnp.zeros((bsz, HIST_ROWS - (CONV_W - 1), CONV_DIM), F32), conv_hist], axis=1)
        s0 = s_hist
        kv_hist = jnp.concatenate([k_hist, v_hist], axis=2).reshape(bsz * WINDOW, KV_COLS)
    qkv, z, qb, kv, gates, gate, tail = _proj(x1, mod, seq, w["norm_mix"], w["w_in"], hist8, w["conv_w"], w["gdn_par"])
    oa, s_new = _gdn(qkv, gate, s0, seq)
    ob = _swa(qb, kv, kv_hist, w["bias"], w["sinks"], seq)
    y = _merge_ffn(x1, oa, z, ob, gates, mod, seq, w["gdn_norm_w"], w["w_a"], w["w_b"], w["w_out"],
                   w["norm_ffn2"], w["ffn2_in"], w["ffn2_out"], w["norm_final"])

    conv_new = tail[:, HIST_ROWS - (CONV_W - 1):]
    half = SWA_KV_HEADS * SWA_HD
    kv3 = kv.reshape(bsz, seq, KV_COLS)
    if kv_hist is None:
        k_new = kv3[:, seq - WINDOW:, :half]
        v_new = kv3[:, seq - WINDOW:, half:]
    else:
        keep = WINDOW - seq
        k_new = jnp.concatenate([k_hist.reshape(bsz, WINDOW, half)[:, WINDOW - keep:], kv3[:, :, :half]], axis=1)
        v_new = jnp.concatenate([v_hist.reshape(bsz, WINDOW, half)[:, WINDOW - keep:], kv3[:, :, half:]], axis=1)
    k_new = k_new.reshape(bsz, WINDOW, SWA_KV_HEADS, SWA_HD)
    v_new = v_new.reshape(bsz, WINDOW, SWA_KV_HEADS, SWA_HD)
    return (y.reshape(bsz, seq, D_MODEL), conv_new[None], s_new[None], k_new[None], v_new[None])


def kernel(x_prompt, x_sample, state_gdn_conv, state_gdn_s, cache_swa_k, cache_swa_v, c_prompt, c_sample,
           norm_ffn1, w_ffn1_in, w_ffn1_out, norm_mix, w_in, gdn_conv_w, gdn_a_log, gdn_dt_bias, gdn_norm_w,
           swa_sinks, rel_bias, w_branch_a, w_branch_b, w_out, norm_ffn2, w_ffn2_in, w_ffn2_out,
           w_ada, b_ada, norm_final):
    bp = x_prompt.shape[0]
    bs = x_sample.shape[0]
    assert bp == 1 and x_sample.shape[1] == CHUNK and cache_swa_k.shape[2] == WINDOW

    n_c = bp + bs
    pad = -n_c % SUBLANES
    c_all = jnp.concatenate([c_prompt, c_sample, jnp.zeros((pad, D_MODEL), F32)], axis=0)
    mod_all = _modulation(c_all, w_ada[0], b_ada[0][None, :])
    mod_p = mod_all[:, :bp]
    mod_s = mod_all[:, bp:n_c]

    w_packed = _pack_w_in(w_in)
    par = jnp.zeros((SUBLANES, LANES), F32)
    par = par.at[0, GDN_HEADS:2 * GDN_HEADS].set(gdn_a_log[0]).at[1, GDN_HEADS:2 * GDN_HEADS].set(gdn_dt_bias[0])

    rel = jnp.arange(SWA_KEYS)[None, :] - WINDOW - jnp.arange(CHUNK)[:, None]
    bias = _bias_table(_t5_bucket(rel).astype(jnp.int32), rel_bias.reshape(-1))

    w = {
        "norm_ffn1": norm_ffn1, "ffn1_in": w_ffn1_in[0].astype(BF16), "ffn1_out": w_ffn1_out[0].astype(BF16),
        "norm_mix": norm_mix, "w_in": w_packed, "conv_w": gdn_conv_w[0], "gdn_par": par,
        "gdn_norm_w": gdn_norm_w, "bias": bias, "sinks": swa_sinks[0],
        "w_a": w_branch_a[0].astype(BF16), "w_b": w_branch_b[0].astype(BF16), "w_out": w_out[0].astype(BF16),
        "norm_ffn2": norm_ffn2, "ffn2_in": w_ffn2_in[0].astype(BF16), "ffn2_out": w_ffn2_out[0].astype(BF16),
        "norm_final": norm_final[None, :],
    }
    y_p, p_conv, p_s, p_k, p_v = _trunk(x_prompt, mod_p, None, None, None, None, w)
    y_s, s_conv, s_s, s_k, s_v = _trunk(x_sample, mod_s, state_gdn_conv[0], state_gdn_s[0],
                                        cache_swa_k[0], cache_swa_v[0], w)
    return (y_p, y_s, p_conv, p_s, p_k, p_v, s_conv, s_s, s_k, s_v)
```

```python
import functools
import math

import jax
import jax.numpy as jnp
from jax import lax
from jax.experimental import pallas as pl
from jax.experimental.pallas import tpu as pltpu

F32 = jnp.float32
BF16 = jnp.bfloat16

D_MODEL = 1024
CHUNK = 64
GDN_HEADS = 8
GDN_DK = 128
GDN_DV = 128
CONV_W = 4
CONV_DIM = GDN_HEADS * (2 * GDN_DK + GDN_DV)
SWA_HEADS = 16
SWA_KV_HEADS = 2
SWA_GROUP = SWA_HEADS // SWA_KV_HEADS
SWA_HD = 64
WINDOW = 128
NUM_BUCKETS = 32
MAX_DISTANCE = 128
D_FF = 2816
N_MOD = 9
EPS = 1e-6

LANES = 128
SUBLANES = 8
VMEM_LIMIT = 56 * 1024 * 1024

MXU_COLS = 256
FF_CHUNK = MXU_COLS
assert D_FF % FF_CHUNK == 0
FFN_ROWS = 1024
HIST_ROWS = SUBLANES
MERGE_ROWS = 512
PACK_ROWS = 256
GDN_CHUNKS_PER_STEP = 4
SWA_CHUNKS_PER_STEP = 8


def _dot(a, b):
    return jnp.dot(a, b, preferred_element_type=F32)


def _dot_nt(a, b):
    return lax.dot_general(a, b, (((1,), (1,)), ((), ())), preferred_element_type=F32)


def _dot_tn(a, b):
    return lax.dot_general(a, b, (((0,), (0,)), ((), ())), preferred_element_type=F32)


def _cparams(n_grid):
    return pltpu.CompilerParams(dimension_semantics=("arbitrary",) * n_grid, vmem_limit_bytes=VMEM_LIMIT)


def _const_spec(shape):
    nd = len(shape)
    return pl.BlockSpec(shape, lambda i: (0,) * nd, pipeline_mode=pl.Buffered(1))


def _mod_kernel(c_ref, w_ref, b_ref, o_ref):
    c = c_ref[...]
    a = (c * jax.nn.sigmoid(c)).astype(BF16)
    o_ref[0] = _dot(a, w_ref[...].astype(BF16)) + b_ref[...]


def _modulation(c_pad, w_ada, b_ada):
    rows = c_pad.shape[0]
    return pl.pallas_call(
        _mod_kernel,
        grid=(N_MOD,),
        in_specs=[
            pl.BlockSpec((rows, D_MODEL), lambda j: (0, 0)),
            pl.BlockSpec((D_MODEL, D_MODEL), lambda j: (0, j)),
            pl.BlockSpec((1, D_MODEL), lambda j: (0, j)),
        ],
        out_specs=pl.BlockSpec((1, rows, D_MODEL), lambda j: (j, 0, 0)),
        out_shape=jax.ShapeDtypeStruct((N_MOD, rows, D_MODEL), F32),
        compiler_params=_cparams(1),
        name="adaln_mod",
    )(c_pad, w_ada, b_ada)


def _mod_row(mod_ref, idx, spt, j):
    if mod_ref.shape[1] == 1:
        return mod_ref[idx, 0:1, :]
    return mod_ref[idx, pl.ds(pl.program_id(0) * spt + j, 1), :]


def _norm_mod_store(h_ref, x_ref, g_ref, mod_ref, sub, spt):
    rows = x_ref.shape[0] // spt
    g = g_ref[...]
    for j in range(spt):
        xs = x_ref[j * rows:(j + 1) * rows, :]
        ms = jnp.mean(xs * xs, axis=-1, keepdims=True)
        y = xs * lax.rsqrt(ms + EPS) * g
        sh = _mod_row(mod_ref, 3 * sub, spt, j)
        sc = _mod_row(mod_ref, 3 * sub + 1, spt, j)
        h_ref[j * rows:(j + 1) * rows, :] = (y * (1.0 + sc) + sh).astype(h_ref.dtype)


def _tile_rows(n_rows, seq_rows, target):
    if seq_rows >= target:
        assert seq_rows % target == 0
        return target, 1
    assert target % seq_rows == 0
    tm = min(target, n_rows)
    assert n_rows % tm == 0
    return tm, tm // seq_rows


def _mod_spec(n_seq_total):
    return pl.BlockSpec((N_MOD, n_seq_total, D_MODEL), lambda i: (0, 0, 0))


def _ffn_kernel(x_ref, mod_ref, g_ref, w1_ref, w2_ref, gf_ref, o_ref, h_ref, *, sub, final, spt):
    _norm_mod_store(h_ref, x_ref, g_ref, mod_ref, sub, spt)
    h = h_ref[...]
    acc = None
    for c in range(D_FF // FF_CHUNK):
        c0 = c * FF_CHUNK
        gate = _dot(h, w1_ref[:, c0:c0 + FF_CHUNK])
        up = _dot(h, w1_ref[:, D_FF + c0:D_FF + c0 + FF_CHUNK])
        a = (gate * jax.nn.sigmoid(gate) * up).astype(BF16)
        part = _dot(a, w2_ref[c0:c0 + FF_CHUNK, :])
        acc = part if acc is None else acc + part
    rows = x_ref.shape[0] // spt
    for j in range(spt):
        sl = slice(j * rows, (j + 1) * rows)
        ga = _mod_row(mod_ref, 3 * sub + 2, spt, j)
        xn = x_ref[sl, :] + 0.5 * ga * acc[sl, :]
        if final:
            ms = jnp.mean(xn * xn, axis=-1, keepdims=True)
            xn = xn * lax.rsqrt(ms + EPS) * gf_ref[...]
        o_ref[sl, :] = xn


def _ffn(x, mod, seq_rows, g, w1, w2, gf, *, sub, final):
    n = x.shape[0]
    tm, spt = _tile_rows(n, seq_rows, FFN_ROWS)
    kern = functools.partial(_ffn_kernel, sub=sub, final=final, spt=spt)
    return pl.pallas_call(
        kern,
        grid=(n // tm,),
        in_specs=[
            pl.BlockSpec((tm, D_MODEL), lambda i: (i, 0)),
            _mod_spec(mod.shape[1]),
            _const_spec((1, D_MODEL)),
            _const_spec((D_MODEL, 2 * D_FF)),
            _const_spec((D_FF, D_MODEL)),
            _const_spec((1, D_MODEL)),
        ],
        out_specs=pl.BlockSpec((tm, D_MODEL), lambda i: (i, 0)),
        out_shape=jax.ShapeDtypeStruct((n, D_MODEL), F32),
        scratch_shapes=[pltpu.VMEM((tm, D_MODEL), BF16)],
        compiler_params=_cparams(1),
        name="ffn_final" if final else "ffn",
    )(x, mod, g, w1, w2, gf)


PROJ_GROUPS = (("qkv", CONV_DIM), ("z", GDN_HEADS * GDN_DV), ("qb", SWA_HEADS * SWA_HD),
               ("kv", 2 * SWA_KV_HEADS * SWA_HD), ("gates", 2 * D_MODEL), ("ba", LANES))
PROJ_COLS = sum(w for _, w in PROJ_GROUPS)


assert math.log2(SWA_HD) % 2 == 0
W_IN_ALIGNED = CONV_DIM + GDN_HEADS * GDN_DV
W_IN_SMALL = 2 * GDN_HEADS
W_IN_REST = PROJ_COLS - W_IN_ALIGNED - LANES


def _pack_w_in_kernel(w_ref, o_ref):
    o_ref[:, 0:W_IN_ALIGNED] = w_ref[:, 0:W_IN_ALIGNED].astype(BF16)
    hi = W_IN_ALIGNED + W_IN_SMALL
    n_qb = SWA_HEADS * SWA_HD
    o_ref[:, W_IN_ALIGNED:W_IN_ALIGNED + n_qb] = (w_ref[:, hi:hi + n_qb] * (SWA_HD ** -0.5)).astype(BF16)
    o_ref[:, W_IN_ALIGNED + n_qb:W_IN_ALIGNED + W_IN_REST] = w_ref[:, hi + n_qb:hi + W_IN_REST].astype(BF16)
    lane = lax.broadcasted_iota(jnp.int32, (w_ref.shape[0], LANES), 1)
    small = jnp.where(lane < W_IN_SMALL, w_ref[:, W_IN_ALIGNED:W_IN_ALIGNED + LANES], 0.0)
    o_ref[:, W_IN_ALIGNED + W_IN_REST:PROJ_COLS] = small.astype(BF16)


def _pack_w_in(w_in):
    n_cols = W_IN_ALIGNED + W_IN_SMALL + W_IN_REST
    assert w_in.shape == (1, D_MODEL, n_cols)
    rows = PACK_ROWS
    return pl.pallas_call(
        _pack_w_in_kernel,
        grid=(D_MODEL // rows,),
        in_specs=[pl.BlockSpec((None, rows, n_cols), lambda i: (0, i, 0))],
        out_specs=pl.BlockSpec((rows, PROJ_COLS), lambda i: (i, 0)),
        out_shape=jax.ShapeDtypeStruct((D_MODEL, PROJ_COLS), BF16),
        compiler_params=_cparams(1),
        name="pack_w_in",
    )(w_in)


PROJ_OFFSETS = {}
_off = 0
for _name, _width in PROJ_GROUPS:
    PROJ_OFFSETS[_name] = (_off, _width)
    _off += _width
PROJ_ROWS = 512


def _proj_kernel(x_ref, mod_ref, g_ref, w_ref, hist_ref, cw_ref, par_ref,
                 qkv_ref, z_ref, qb_ref, kv_ref, gates_ref, gate_ref, tail_ref,
                 h_ref, xp_ref, *, spt, carry):
    pid = pl.program_id(0)
    tm = x_ref.shape[0]
    n_chunks = tm // CHUNK
    seg = CHUNK if carry else HIST_ROWS + CHUNK

    _norm_mod_store(h_ref, x_ref, g_ref, mod_ref, 1, spt)
    h = h_ref[...]

    def group(name, lo=0, width=None):
        c0, full = PROJ_OFFSETS[name]
        width = full if width is None else width
        return _dot(h, w_ref[:, c0 + lo:c0 + lo + width])

    if carry:
        def _load_hist():
            xp_ref[0:HIST_ROWS, :] = hist_ref[0]

        pl.when(pid == 0)(_load_hist)
    else:
        for c in range(n_chunks):
            xp_ref[c * seg:c * seg + HIST_ROWS, :] = hist_ref[c]

    part = GDN_HEADS * GDN_DK

    def project_part(p):
        raw = group("qkv", p * part, part)
        cols = slice(p * part, (p + 1) * part)
        if carry:
            xp_ref[HIST_ROWS:HIST_ROWS + tm, cols] = raw
        else:
            for c in range(n_chunks):
                xp_ref[c * seg + HIST_ROWS:(c + 1) * seg, cols] = raw[c * CHUNK:(c + 1) * CHUNK, :]

    def conv_part(p):
        for c in range(n_chunks):
            for t in range(p * GDN_HEADS, (p + 1) * GDN_HEADS):
                c0 = t * LANES
                win = xp_ref[c * seg:c * seg + HIST_ROWS + CHUNK, c0:c0 + LANES]
                acc = cw_ref[CONV_W - 1:CONV_W, c0:c0 + LANES] * win[HIST_ROWS:, :]
                for s in range(1, CONV_W):
                    tap = pltpu.roll(win, s, 0)[HIST_ROWS:, :]
                    acc = acc + cw_ref[CONV_W - 1 - s:CONV_W - s, c0:c0 + LANES] * tap
                y = acc * jax.nn.sigmoid(acc)
                if p < 2:
                    y = y * lax.rsqrt(jnp.sum(y * y, axis=-1, keepdims=True) + EPS)
                    if p == 0:
                        y = y * (GDN_DK ** -0.5)
                qkv_ref[c * CHUNK:(c + 1) * CHUNK, c0:c0 + LANES] = y

    def gate_math():
        ba = group("ba")
        ri = lax.broadcasted_iota(jnp.int32, (CHUNK, CHUNK), 0)
        ci = lax.broadcasted_iota(jnp.int32, (CHUNK, CHUNK), 1)
        tri = (ri >= ci).astype(BF16)
        lane = lax.broadcasted_iota(jnp.int32, (CHUNK, LANES), 1)
        a_coef = -jnp.exp(par_ref[0:1, :])
        dt_bias = par_ref[1:2, :]
        for c in range(n_chunks):
            bac = ba[c * CHUNK:(c + 1) * CHUNK, :]
            g_all = a_coef * jax.nn.softplus(bac + dt_bias)
            g_hi = g_all.astype(BF16)
            g_r1 = g_all - g_hi.astype(F32)
            g_mid = g_r1.astype(BF16)
            g_lo = (g_r1 - g_mid.astype(F32)).astype(BF16)
            gc = _dot(tri, g_hi) + _dot(tri, g_mid) + _dot(tri, g_lo)
            gate_ref[c * CHUNK:(c + 1) * CHUNK, :] = jnp.where(lane < GDN_HEADS, jax.nn.sigmoid(bac), gc)

    project_part(0)
    project_part(1)
    conv_part(0)
    project_part(2)
    conv_part(1)
    z_ref[...] = group("z").astype(z_ref.dtype)
    gate_math()
    qb_ref[...] = group("qb").astype(qb_ref.dtype)
    conv_part(2)
    kv_ref[...] = group("kv")
    gates_ref[...] = group("gates").astype(gates_ref.dtype)

    if carry:
        def _store_tail():
            tail_ref[0] = xp_ref[tm:tm + HIST_ROWS, :]

        pl.when(pid == pl.num_programs(0) - 1)(_store_tail)
        xp_ref[0:HIST_ROWS, :] = xp_ref[tm:tm + HIST_ROWS, :]
    else:
        for c in range(n_chunks):
            tail_ref[c] = xp_ref[c * seg + CHUNK:c * seg + CHUNK + HIST_ROWS, :]


def _proj(x, mod, seq_rows, g, w_packed, hist8, conv_w, par):
    n = x.shape[0]
    n_seq = n // seq_rows
    carry = n_seq == 1
    tm, spt = _tile_rows(n, seq_rows, PROJ_ROWS)
    if carry:
        n_hist = 1
        hist_map = lambda i: (0, 0, 0)
        xp_rows = HIST_ROWS + tm
    else:
        assert seq_rows == CHUNK
        n_hist = tm // CHUNK
        hist_map = lambda i: (i, 0, 0)
        xp_rows = n_hist * (HIST_ROWS + CHUNK)
    widths = [CONV_DIM, GDN_HEADS * GDN_DV, SWA_HEADS * SWA_HD, KV_COLS, 2 * D_MODEL, LANES]
    return pl.pallas_call(
        functools.partial(_proj_kernel, spt=spt, carry=carry),
        grid=(n // tm,),
        in_specs=[
            pl.BlockSpec((tm, D_MODEL), lambda i: (i, 0)),
            _mod_spec(mod.shape[1]),
            _const_spec((1, D_MODEL)),
            _const_spec((D_MODEL, PROJ_COLS)),
            pl.BlockSpec((n_hist, HIST_ROWS, CONV_DIM), hist_map),
            _const_spec((CONV_W, CONV_DIM)),
            _const_spec((SUBLANES, LANES)),
        ],
        out_specs=[pl.BlockSpec((tm, w), lambda i: (i, 0)) for w in widths]
        + [pl.BlockSpec((n_hist, HIST_ROWS, CONV_DIM), hist_map)],
        out_shape=[jax.ShapeDtypeStruct((n, w), BF16 if i in (1, 2, 4) else F32) for i, w in enumerate(widths)]
        + [jax.ShapeDtypeStruct((n_seq, HIST_ROWS, CONV_DIM), F32)],
        scratch_shapes=[pltpu.VMEM((tm, D_MODEL), BF16), pltpu.VMEM((xp_rows, CONV_DIM), F32)],
        compiler_params=_cparams(1),
        name="in_proj",
    )(x, mod, g, w_packed, hist8, conv_w, par)


def _unit_lower_inverses(ls, eye, level_masks):
    ts = [eye - jnp.where(level_masks[0], l, 0.0) for l in ls]
    ls16 = [l.astype(BF16) for l in ls]
    for mask in level_masks[1:]:
        lk = [jnp.where(mask, l, jnp.zeros_like(l)) for l in ls16]
        tp = [t.astype(BF16) for t in ts]
        m1 = [_dot(a, b) for a, b in zip(lk, tp)]
        m2 = [_dot(a, b.astype(BF16)) for a, b in zip(tp, m1)]
        ts = [t - m for t, m in zip(ts, m2)]
    return ts


def _gdn_kernel(qkv_ref, gate_ref, s0_ref, o_ref, sout_ref, s_ref, *, n_chunks, carry):
    pid = pl.program_id(0)

    if carry:
        def _load_state():
            s_ref[0] = s0_ref[0]

        pl.when(pid == 0)(_load_state)
    else:
        for c in range(n_chunks):
            s_ref[c] = s0_ref[c]

    ri = lax.broadcasted_iota(jnp.int32, (CHUNK, CHUNK), 0)
    ci = lax.broadcasted_iota(jnp.int32, (CHUNK, CHUNK), 1)
    causal = ri >= ci
    strict = ri > ci
    eye = (ri == ci).astype(F32)
    level_masks = []
    for lvl in range(int(math.log2(CHUNK))):
        same_block = (ri >> (lvl + 1)) == (ci >> (lvl + 1))
        level_masks.append(same_block & (((ri >> lvl) & 1) == 1) & (((ci >> lvl) & 1) == 0))

    items = [(c, h) for c in range(n_chunks) for h in range(GDN_HEADS)]
    gate = [gate_ref[c * CHUNK:(c + 1) * CHUNK, :] for c in range(n_chunks)]
    egc_all = [jnp.exp(g) for g in gate]
    g_last_all = [g[CHUNK - 1:CHUNK, :] for g in gate]
    kdf_all = [jnp.exp(gl - g) for gl, g in zip(g_last_all, gate)]
    eg_last_all = [jnp.exp(gl) for gl in g_last_all]

    def tile(c, t):
        return qkv_ref[c * CHUNK:(c + 1) * CHUNK, t * LANES:(t + 1) * LANES]

    def col(arrs, c, h):
        return arrs[c][:, GDN_HEADS + h:GDN_HEADS + h + 1]

    q = [tile(c, h) for c, h in items]
    k = [tile(c, GDN_HEADS + h) for c, h in items]
    v = [tile(c, 2 * GDN_HEADS + h) for c, h in items]
    beta = [gate[c][:, h:h + 1] for c, h in items]
    gcol = [col(gate, c, h) for c, h in items]
    grow = [jnp.sum(g * eye, axis=0, keepdims=True) for g in gcol]
    decay = [jnp.where(causal, jnp.exp(jnp.where(causal, gc - gr, 0.0)), 0.0) for gc, gr in zip(gcol, grow)]
    kb = [x * b for x, b in zip(k, beta)]
    k16 = [x.astype(BF16) for x in k]
    kq = [_dot_nt(jnp.concatenate([a.astype(BF16), b.astype(BF16)], axis=0), c) for a, b, c in zip(kb, q, k16)]
    kk = [x[0:CHUNK, :] for x in kq]
    qk = [x[CHUNK:, :] for x in kq]
    ls = [jnp.where(strict, a * d, 0.0) for a, d in zip(kk, decay)]
    intra16 = [jnp.where(causal, a * d, 0.0).astype(BF16) for a, d in zip(qk, decay)]
    ts = _unit_lower_inverses(ls, eye, level_masks)
    ys = [(t - eye).astype(BF16) for t in ts]
    egc = [col(egc_all, c, h) for c, h in items]
    vb = [x * b for x, b in zip(v, beta)]
    kbe = [x * e for x, e in zip(kb, egc)]
    rhs = [jnp.concatenate([a, b], axis=1) for a, b in zip(vb, kbe)]
    sol = [x + _dot(y, x.astype(BF16)) for x, y in zip(rhs, ys)]
    u = [x[:, 0:GDN_DV] for x in sol]
    w16 = [x[:, GDN_DV:].astype(BF16) for x in sol]
    qg16 = [(x * e).astype(BF16) for x, e in zip(q, egc)]
    kd16 = [(x * col(kdf_all, c, h)).astype(BF16) for x, (c, h) in zip(k, items)]
    eg_last = [col(eg_last_all, c, h) for c, h in items]
    wq16 = [jnp.concatenate([a, b], axis=0) for a, b in zip(w16, qg16)]

    o = [None] * len(items)
    if carry:
        state = [s_ref[0, h] for h in range(GDN_HEADS)]
    for c in range(n_chunks):
        idx = [c * GDN_HEADS + h for h in range(GDN_HEADS)]
        if not carry:
            state = [s_ref[c, h] for h in range(GDN_HEADS)]
        s16 = [s.astype(BF16) for s in state]
        ws_qs = [_dot(wq16[i], s) for i, s in zip(idx, s16)]
        v_new = [u[i] - x[0:CHUNK, :] for i, x in zip(idx, ws_qs)]
        vn16 = [x.astype(BF16) for x in v_new]
        for i, x, vn in zip(idx, ws_qs, vn16):
            o[i] = x[CHUNK:, :] + _dot(intra16[i], vn)
        state = [s * eg_last[i] + _dot_tn(kd16[i], vn) for i, s, vn in zip(idx, state, vn16)]
        if not carry:
            for h in range(GDN_HEADS):
                sout_ref[c, h] = state[h]

    for (c, h), oo in zip(items, o):
        o_ref[c * CHUNK:(c + 1) * CHUNK, h * GDN_DV:(h + 1) * GDN_DV] = oo.astype(o_ref.dtype)

    if carry:
        for h in range(GDN_HEADS):
            s_ref[0, h] = state[h]

        def _store_state():
            sout_ref[0] = s_ref[0]

        pl.when(pid == pl.num_programs(0) - 1)(_store_state)


def _gdn(qkv, gate, s0, seq_rows):
    n = qkv.shape[0]
    n_seq = n // seq_rows
    carry = n_seq == 1
    n_chunks = GDN_CHUNKS_PER_STEP
    rows = n_chunks * CHUNK
    assert n % rows == 0
    if carry:
        n_state = 1
        s_map = lambda i: (0, 0, 0, 0)
    else:
        assert seq_rows == CHUNK
        n_state = n_chunks
        s_map = lambda i: (i, 0, 0, 0)
    kern = functools.partial(_gdn_kernel, n_chunks=n_chunks, carry=carry)
    return pl.pallas_call(
        kern,
        grid=(n // rows,),
        in_specs=[
            pl.BlockSpec((rows, CONV_DIM), lambda i: (i, 0)),
            pl.BlockSpec((rows, LANES), lambda i: (i, 0)),
            pl.BlockSpec((n_state, GDN_HEADS, GDN_DK, GDN_DV), s_map),
        ],
        out_specs=[
            pl.BlockSpec((rows, GDN_HEADS * GDN_DV), lambda i: (i, 0)),
            pl.BlockSpec((n_state, GDN_HEADS, GDN_DK, GDN_DV), s_map),
        ],
        out_shape=[
            jax.ShapeDtypeStruct((n, GDN_HEADS * GDN_DV), BF16),
            jax.ShapeDtypeStruct((n_seq, GDN_HEADS, GDN_DK, GDN_DV), F32),
        ],
        scratch_shapes=[pltpu.VMEM((n_state, GDN_HEADS, GDN_DK, GDN_DV), F32)],
        compiler_params=_cparams(1),
        name="gdn",
    )(qkv, gate, s0)


SWA_KEYS = WINDOW + CHUNK
SWA_KEYS_PAD = 2 * LANES
KV_COLS = 2 * SWA_KV_HEADS * SWA_HD


def _bias_kernel(bucket_ref, rb_ref, o_ref):
    bucket = bucket_ref[...]
    hits = [bucket == b for b in range(NUM_BUCKETS)]
    for h in range(SWA_HEADS):
        acc = jnp.zeros((CHUNK, SWA_KEYS), F32)
        for b in range(NUM_BUCKETS):
            acc = jnp.where(hits[b], rb_ref[b * SWA_HEADS + h], acc)
        o_ref[h] = acc


def _bias_table(bucket, rel_bias_flat):
    return pl.pallas_call(
        _bias_kernel,
        grid=(1,),
        in_specs=[
            pl.BlockSpec((CHUNK, SWA_KEYS), lambda i: (0, 0)),
            pl.BlockSpec(memory_space=pltpu.SMEM),
        ],
        out_specs=pl.BlockSpec((SWA_HEADS, CHUNK, SWA_KEYS), lambda i: (0, 0, 0)),
        out_shape=jax.ShapeDtypeStruct((SWA_HEADS, CHUNK, SWA_KEYS), F32),
        compiler_params=_cparams(1),
        name="swa_bias",
    )(bucket, rel_bias_flat)


def _swa_kernel(q_ref, kvc_ref, kvp_ref, bias_ref, sink_ref, o_ref, kf_ref, *, n_chunks, one_sequence):
    n_pad = SWA_KEYS_PAD - SWA_KEYS
    kf_ref[kf_ref.shape[0] - n_pad:, :] = jnp.zeros((n_pad, KV_COLS), F32)
    if one_sequence:
        kf_ref[0:WINDOW, :] = kvp_ref[...]
        kf_ref[WINDOW:WINDOW + n_chunks * CHUNK, :] = kvc_ref[...]
        key_start = [c * CHUNK for c in range(n_chunks)]
        key_pos = lax.broadcasted_iota(jnp.int32, (CHUNK, SWA_KEYS), 1)
        tile_start = pl.program_id(0) * (n_chunks * CHUNK)
        valid = [key_pos + (tile_start + c * CHUNK - WINDOW) >= 0 for c in range(n_chunks)]
    else:
        for c in range(n_chunks):
            kf_ref[c * SWA_KEYS:c * SWA_KEYS + WINDOW, :] = kvp_ref[c * WINDOW:(c + 1) * WINDOW, :]
            kf_ref[c * SWA_KEYS + WINDOW:(c + 1) * SWA_KEYS, :] = kvc_ref[c * CHUNK:(c + 1) * CHUNK, :]
        key_start = [c * SWA_KEYS for c in range(n_chunks)]

    pairs = [(c, kh) for c in range(n_chunks) for kh in range(SWA_KV_HEADS)]
    half = SWA_KV_HEADS * SWA_HD
    keys_t = [kf_ref[key_start[c]:key_start[c] + SWA_KEYS_PAD, 0:half].T.astype(BF16) for c in range(n_chunks)]
    keys = {(c, kh): keys_t[c][kh * SWA_HD:(kh + 1) * SWA_HD, :] for c, kh in pairs}
    ones = jnp.ones((SWA_KEYS, LANES - SWA_HD), BF16)
    vals = {(c, kh): jnp.concatenate(
        [kf_ref[key_start[c]:key_start[c] + SWA_KEYS,
                (SWA_KV_HEADS + kh) * SWA_HD:(SWA_KV_HEADS + kh + 1) * SWA_HD].astype(BF16), ones], axis=1)
            for c, kh in pairs}
    items = [(c, hd) for c in range(n_chunks) for hd in range(SWA_HEADS)]
    q = [q_ref[c * CHUNK:(c + 1) * CHUNK, hd * SWA_HD:(hd + 1) * SWA_HD] for c, hd in items]
    def group_rows(c, kh):
        return range(c * SWA_HEADS + kh * SWA_GROUP, c * SWA_HEADS + (kh + 1) * SWA_GROUP)

    qk = {(c, kh): _dot(jnp.concatenate([q[i] for i in group_rows(c, kh)], axis=0), keys[(c, kh)])
          for c, kh in pairs}
    logits = [qk[(c, hd // SWA_GROUP)][(hd % SWA_GROUP) * CHUNK:(hd % SWA_GROUP + 1) * CHUNK, 0:SWA_KEYS] + bias_ref[hd]
              for c, hd in items]
    if one_sequence:
        logits = [jnp.where(valid[c], x, -jnp.inf) if c < WINDOW // CHUNK else x for x, (c, _) in zip(logits, items)]
    sink = [sink_ref[hd] for _, hd in items]
    m = [jnp.maximum(jnp.max(x, axis=-1, keepdims=True), s) for x, s in zip(logits, sink)]
    p16 = [jnp.exp(x - mm).astype(BF16) for x, mm in zip(logits, m)]
    pv = {(c, kh): _dot(jnp.concatenate([p16[i] for i in group_rows(c, kh)], axis=0), vals[(c, kh)])
          for c, kh in pairs}
    pv = [pv[(c, hd // SWA_GROUP)][(hd % SWA_GROUP) * CHUNK:(hd % SWA_GROUP + 1) * CHUNK, :] for c, hd in items]
    denom = [x[:, SWA_HD:SWA_HD + 1] + jnp.exp(s - mm) for x, s, mm in zip(pv, sink, m)]
    out = [x[:, 0:SWA_HD] / d for x, d in zip(pv, denom)]
    for x, (c, hd) in zip(out, items):
        o_ref[c * CHUNK:(c + 1) * CHUNK, hd * SWA_HD:(hd + 1) * SWA_HD] = x.astype(o_ref.dtype)


def _swa(qb, kv, kv_hist, bias, sinks, seq_rows):
    n = qb.shape[0]
    n_chunks = SWA_CHUNKS_PER_STEP
    rows = n_chunks * CHUNK
    assert n % rows == 0
    one_sequence = kv_hist is None
    if one_sequence:
        assert rows % WINDOW == 0
        per = rows // WINDOW
        prev_arr = kv
        prev_spec = pl.BlockSpec((WINDOW, KV_COLS), lambda i: (jnp.maximum(i * per - 1, 0), 0))
        kf_rows = WINDOW + rows
    else:
        assert seq_rows == CHUNK
        prev_arr = kv_hist
        prev_spec = pl.BlockSpec((n_chunks * WINDOW, KV_COLS), lambda i: (i, 0))
        kf_rows = n_chunks * SWA_KEYS
    kern = functools.partial(_swa_kernel, n_chunks=n_chunks, one_sequence=one_sequence)
    return pl.pallas_call(
        kern,
        grid=(n // rows,),
        in_specs=[
            pl.BlockSpec((rows, SWA_HEADS * SWA_HD), lambda i: (i, 0)),
            pl.BlockSpec((rows, KV_COLS), lambda i: (i, 0)),
            prev_spec,
            pl.BlockSpec((SWA_HEADS, CHUNK, SWA_KEYS), lambda i: (0, 0, 0)),
            pl.BlockSpec(memory_space=pltpu.SMEM),
        ],
        out_specs=pl.BlockSpec((rows, SWA_HEADS * SWA_HD), lambda i: (i, 0)),
        out_shape=jax.ShapeDtypeStruct((n, SWA_HEADS * SWA_HD), BF16),
        scratch_shapes=[pltpu.VMEM((kf_rows + SWA_KEYS_PAD - SWA_KEYS, KV_COLS), F32)],
        compiler_params=_cparams(1),
        name="swa",
    )(qb, kv, prev_arr, bias, sinks)


def _merge_kernel(x_ref, oa_ref, z_ref, ob_ref, gates_ref, mod_ref, nw_ref, wa_ref, wb_ref, wo_ref, o_ref, oa16_ref,
                  *, spt):
    for h in range(GDN_HEADS):
        cols = slice(h * GDN_DV, (h + 1) * GDN_DV)
        oo = oa_ref[:, cols].astype(F32)
        zz = z_ref[:, cols].astype(F32)
        on = oo * lax.rsqrt(jnp.mean(oo * oo, axis=-1, keepdims=True) + EPS) * nw_ref[...] * (zz * jax.nn.sigmoid(zz))
        oa16_ref[:, cols] = on.astype(BF16)
    ya = _dot(oa16_ref[...], wa_ref[...])
    yb = _dot(ob_ref[...], wb_ref[...])
    merged = (jax.nn.sigmoid(gates_ref[:, 0:D_MODEL].astype(F32)) * ya
              + jax.nn.sigmoid(gates_ref[:, D_MODEL:2 * D_MODEL].astype(F32)) * yb)
    y = _dot(merged.astype(BF16), wo_ref[...])
    rows = x_ref.shape[0] // spt
    for j in range(spt):
        sl = slice(j * rows, (j + 1) * rows)
        o_ref[sl, :] = x_ref[sl, :] + _mod_row(mod_ref, 5, spt, j) * y[sl, :]


def _merge_ffn_kernel(x_ref, oa_ref, z_ref, ob_ref, gates_ref, mod_ref, nw_ref, wa_ref, wb_ref, wo_ref,
                      g_ref, w1_ref, w2_ref, gf_ref, o_ref, oa16_ref, x3_ref, h_ref, *, spt):
    _merge_kernel(x_ref, oa_ref, z_ref, ob_ref, gates_ref, mod_ref, nw_ref, wa_ref, wb_ref, wo_ref, x3_ref, oa16_ref,
                  spt=spt)
    _ffn_kernel(x3_ref, mod_ref, g_ref, w1_ref, w2_ref, gf_ref, o_ref, h_ref, sub=2, final=True, spt=spt)


def _merge_ffn(x, oa, z, ob, gates, mod, seq_rows, norm_w, wa, wb, wo, g, w1, w2, gf):
    n = x.shape[0]
    tm, spt = _tile_rows(n, seq_rows, MERGE_ROWS)
    row_spec = lambda w: pl.BlockSpec((tm, w), lambda i: (i, 0))
    return pl.pallas_call(
        functools.partial(_merge_ffn_kernel, spt=spt),
        grid=(n // tm,),
        in_specs=[
            row_spec(D_MODEL), row_spec(D_MODEL), row_spec(D_MODEL), row_spec(D_MODEL), row_spec(2 * D_MODEL),
            _mod_spec(mod.shape[1]),
            _const_spec((1, GDN_DV)),
            _const_spec((D_MODEL, D_MODEL)), _const_spec((D_MODEL, D_MODEL)), _const_spec((D_MODEL, D_MODEL)),
            _const_spec((1, D_MODEL)),
            _const_spec((D_MODEL, 2 * D_FF)),
            _const_spec((D_FF, D_MODEL)),
            _const_spec((1, D_MODEL)),
        ],
        out_specs=row_spec(D_MODEL),
        out_shape=jax.ShapeDtypeStruct((n, D_MODEL), F32),
        scratch_shapes=[pltpu.VMEM((tm, GDN_HEADS * GDN_DV), BF16), pltpu.VMEM((tm, D_MODEL), F32),
                        pltpu.VMEM((tm, D_MODEL), BF16)],
        compiler_params=_cparams(1),
        name="merge_ffn",
    )(x, oa, z, ob, gates, mod, norm_w, wa, wb, wo, g, w1, w2, gf)


def _t5_bucket(rel):
    half = NUM_BUCKETS // 2
    max_exact = half // 2
    n = jnp.abs(rel)
    large = max_exact + (jnp.log(jnp.maximum(n, 1).astype(jnp.float32) / max_exact)
                         / math.log(MAX_DISTANCE / max_exact) * (half - max_exact)).astype(jnp.int32)
    large = jnp.minimum(large, half - 1)
    return jnp.where(rel > 0, half, 0) + jnp.where(n < max_exact, n, large)


def _trunk(x, mod, conv_hist, s_hist, k_hist, v_hist, w):
    bsz, seq, _ = x.shape
    n = bsz * seq
    x2 = x.reshape(n, D_MODEL)
    x1 = _ffn(x2, mod, seq, w["norm_ffn1"], w["ffn1_in"], w["ffn1_out"], w["norm_final"], sub=0, final=False)
    if conv_hist is None:
        hist8 = jnp.zeros((bsz, HIST_ROWS, CONV_DIM), F32)
        s0 = jnp.zeros((bsz, GDN_HEADS, GDN_DK, GDN_DV), F32)
        kv_hist = None
    else:
        hist8 = jnp.concatenate([jnp.zeros((bsz, HIST_ROWS - (CONV_W - 1), CONV_DIM), F32), conv_hist], axis=1)
        s0 = s_hist
        kv_hist = jnp.concatenate([k_hist, v_hist], axis=2).reshape(bsz * WINDOW, KV_COLS)
    qkv, z, qb, kv, gates, gate, tail = _proj(x1, mod, seq, w["norm_mix"], w["w_in"], hist8, w["conv_w"], w["gdn_par"])
    oa, s_new = _gdn(qkv, gate, s0, seq)
    ob = _swa(qb, kv, kv_hist, w["bias"], w["sinks"], seq)
    y = _merge_ffn(x1, oa, z, ob, gates, mod, seq, w["gdn_norm_w"], w["w_a"], w["w_b"], w["w_out"],
                   w["norm_ffn2"], w["ffn2_in"], w["ffn2_out"], w["norm_final"])

    conv_new = tail[:, HIST_ROWS - (CONV_W - 1):]
    half = SWA_KV_HEADS * SWA_HD
    kv3 = kv.reshape(bsz, seq, KV_COLS)
    if kv_hist is None:
        k_new = kv3[:, seq - WINDOW:, :half]
        v_new = kv3[:, seq - WINDOW:, half:]
    else:
        keep = WINDOW - seq
        k_new = jnp.concatenate([k_hist.reshape(bsz, WINDOW, half)[:, WINDOW - keep:], kv3[:, :, :half]], axis=1)
        v_new = jnp.concatenate([v_hist.reshape(bsz, WINDOW, half)[:, WINDOW - keep:], kv3[:, :, half:]], axis=1)
    k_new = k_new.reshape(bsz, WINDOW, SWA_KV_HEADS, SWA_HD)
    v_new = v_new.reshape(bsz, WINDOW, SWA_KV_HEADS, SWA_HD)
    return (y.reshape(bsz, seq, D_MODEL), conv_new[None], s_new[None], k_new[None], v_new[None])


def kernel(x_prompt, x_sample, state_gdn_conv, state_gdn_s, cache_swa_k, cache_swa_v, c_prompt, c_sample,
           norm_ffn1, w_ffn1_in, w_ffn1_out, norm_mix, w_in, gdn_conv_w, gdn_a_log, gdn_dt_bias, gdn_norm_w,
           swa_sinks, rel_bias, w_branch_a, w_branch_b, w_out, norm_ffn2, w_ffn2_in, w_ffn2_out,
           w_ada, b_ada, norm_final):
    bp = x_prompt.shape[0]
    bs = x_sample.shape[0]
    assert bp == 1 and x_sample.shape[1] == CHUNK and cache_swa_k.shape[2] == WINDOW

    n_c = bp + bs
    pad = -n_c % SUBLANES
    c_all = jnp.concatenate([c_prompt, c_sample, jnp.zeros((pad, D_MODEL), F32)], axis=0)
    mod_all = _modulation(c_all, w_ada[0], b_ada[0][None, :])
    mod_p = mod_all[:, :bp]
    mod_s = mod_all[:, bp:n_c]

    w_packed = _pack_w_in(w_in)
    par = jnp.zeros((SUBLANES, LANES), F32)
    par = par.at[0, GDN_HEADS:2 * GDN_HEADS].set(gdn_a_log[0]).at[1, GDN_HEADS:2 * GDN_HEADS].set(gdn_dt_bias[0])

    rel = jnp.arange(SWA_KEYS)[None, :] - WINDOW - jnp.arange(CHUNK)[:, None]
    bias = _bias_table(_t5_bucket(rel).astype(jnp.int32), rel_bias.reshape(-1))

    w = {
        "norm_ffn1": norm_ffn1, "ffn1_in": w_ffn1_in[0].astype(BF16), "ffn1_out": w_ffn1_out[0].astype(BF16),
        "norm_mix": norm_mix, "w_in": w_packed, "conv_w": gdn_conv_w[0], "gdn_par": par,
        "gdn_norm_w": gdn_norm_w, "bias": bias, "sinks": swa_sinks[0],
        "w_a": w_branch_a[0].astype(BF16), "w_b": w_branch_b[0].astype(BF16), "w_out": w_out[0].astype(BF16),
        "norm_ffn2": norm_ffn2, "ffn2_in": w_ffn2_in[0].astype(BF16), "ffn2_out": w_ffn2_out[0].astype(BF16),
        "norm_final": norm_final[None, :],
    }
    y_p, p_conv, p_s, p_k, p_v = _trunk(x_prompt, mod_p, None, None, None, None, w)
    y_s, s_conv, s_s, s_k, s_v = _trunk(x_sample, mod_s, state_gdn_conv[0], state_gdn_s[0],
                                        cache_swa_k[0], cache_swa_v[0], w)
    return (y_p, y_s, p_conv, p_s, p_k, p_v, s_conv, s_s, s_k, s_v)
```
